```python
import jax, jax.numpy as jnp
from jax import lax
import numpy as np

D_MODEL = 1024
BATCH = 16
SEQ = 2048
DEPTH = 2

HEAD_DIM_A = 64
N_HEADS_A = 6
D_A = N_HEADS_A * HEAD_DIM_A
HEAD_DIM_B = 96
N_HEADS_B = 4
D_B = N_HEADS_B * HEAD_DIM_B
POOL_WINDOWS = (2, 4, 8, 16)
N_GROUPS_C = len(POOL_WINDOWS)
GROUP_DIM_C = 64
D_C = N_GROUPS_C * GROUP_DIM_C
D_MIX = D_A + D_B + D_C
D_IN = 2 * D_A + 2 * D_B + D_C
CONV_WIDTH = 31
CHUNK = 128
D_FF = ((8 * D_MODEL // 3 + 255) // 256) * 256
RMS_EPS = 1e-6
LN_EPS = 1e-5

kernel_name = "hybrid_conv_gmlp_pool_block"


def rms_norm(x, g):
    xf = x.astype(jnp.float32)
    y = xf * lax.rsqrt(jnp.mean(xf * xf, axis=-1, keepdims=True) + RMS_EPS)
    return (y * g.astype(jnp.float32)).astype(x.dtype)


def layer_norm(x, g, b):
    xf = x.astype(jnp.float32)
    mu = jnp.mean(xf, axis=-1, keepdims=True)
    xc = xf - mu
    var = jnp.mean(xc * xc, axis=-1, keepdims=True)
    y = xc * lax.rsqrt(var + LN_EPS) * g.astype(jnp.float32) + b.astype(jnp.float32)
    return y.astype(x.dtype)


def conformer_conv(z, conv_w, conv_b, ln_g, ln_b, w_pw):
    a, gate = jnp.split(z, 2, axis=-1)
    y = a * jax.nn.sigmoid(gate)
    y = lax.conv_general_dilated(
        y, conv_w[:, None, :], window_strides=(1,),
        padding=[(CONV_WIDTH - 1, 0)],
        dimension_numbers=("NWC", "WIO", "NWC"),
        feature_group_count=D_A) + conv_b
    y = jax.nn.silu(layer_norm(y, ln_g, ln_b))
    return y @ w_pw


def spatial_gating(z, ln_g, ln_b, w_s, b_s):
    bsz, seq, _ = z.shape
    z = jax.nn.gelu(z)
    u, v = jnp.split(z, 2, axis=-1)
    v = layer_norm(v, ln_g, ln_b)
    v = v.reshape(bsz, seq // CHUNK, CHUNK, N_HEADS_B, HEAD_DIM_B)
    causal = jnp.tril(jnp.ones((CHUNK, CHUNK), dtype=bool))
    w = jnp.where(causal[None], w_s, jnp.zeros_like(w_s))
    s = jnp.einsum('hts,bnshd->bnthd', w, v) + b_s.T[:, :, None]
    return u * s.reshape(bsz, seq, D_B)


def multiscale_pool(z, w_pool, pool_scale):
    bsz, seq, _ = z.shape
    zf = z.astype(jnp.float32)
    cs0 = jnp.concatenate([jnp.zeros((bsz, 1, D_C), jnp.float32), jnp.cumsum(zf, axis=1)], axis=1)
    t = jnp.arange(seq, dtype=jnp.float32)[:, None]
    outs = []
    for g, w in enumerate(POOL_WINDOWS):
        sl = slice(g * GROUP_DIM_C, (g + 1) * GROUP_DIM_C)
        c = cs0[..., sl]
        upper = c[:, 1:]
        lower = jnp.concatenate([jnp.zeros((bsz, w - 1, GROUP_DIM_C), jnp.float32),
                                 c[:, :seq - w + 1]], axis=1)
        cnt = jnp.minimum(t + 1.0, float(w))
        outs.append((upper - lower) / cnt - zf[..., sl])
    p = jnp.stack(outs, axis=2).astype(z.dtype)
    y = jnp.einsum('bsgi,gio->bsgo', p, w_pool).reshape(bsz, seq, D_C)
    return y * pool_scale


def _fwd_setup_inputs(seed: int = 0) -> dict:
    key = jax.random.key(seed)
    ks = jax.random.split(key, 20)
    n = lambda k, shape, s: jax.random.normal(k, shape, jnp.float32) * s
    L = DEPTH
    return {
        "x": n(ks[0], (BATCH, SEQ, D_MODEL), 1.0),
        "norm1_g": 1.0 + n(ks[1], (L, D_MODEL), 0.02),
        "w_in": n(ks[2], (L, D_MODEL, D_IN), D_MODEL ** -0.5),
        "conv_w": n(ks[3], (L, CONV_WIDTH, D_A), CONV_WIDTH ** -0.5),
        "conv_b": n(ks[4], (L, D_A), 0.01),
        "conv_ln_g": 1.0 + n(ks[5], (L, D_A), 0.02),
        "conv_ln_b": n(ks[6], (L, D_A), 0.01),
        "w_pw": n(ks[7], (L, D_A, D_A), D_A ** -0.5),
        "sg_ln_g": 1.0 + n(ks[8], (L, D_B), 0.02),
        "sg_ln_b": n(ks[9], (L, D_B), 0.01),
        "w_s": n(ks[10], (L, N_HEADS_B, CHUNK, CHUNK), CHUNK ** -0.5),
        "b_s": 1.0 + n(ks[11], (L, N_HEADS_B, CHUNK), 0.02),
        "w_pool": n(ks[12], (L, N_GROUPS_C, GROUP_DIM_C, GROUP_DIM_C), GROUP_DIM_C ** -0.5),
        "pool_scale": 1.0 + n(ks[13], (L, D_C), 0.02),
        "w_out": n(ks[14], (L, D_MIX, D_MODEL), D_MIX ** -0.5),
        "norm2_g": 1.0 + n(ks[15], (L, D_MODEL), 0.02),
        "w_gate_up": n(ks[16], (L, D_MODEL, 2 * D_FF), D_MODEL ** -0.5),
        "w_down": n(ks[17], (L, D_FF, D_MODEL), D_FF ** -0.5),
        "final_g": 1.0 + n(ks[18], (D_MODEL,), 0.02),
    }


def _fwd_reference(x, norm1_g, w_in, conv_w, conv_b, conv_ln_g, conv_ln_b, w_pw,
              sg_ln_g, sg_ln_b, w_s, b_s, w_pool, pool_scale, w_out,
              norm2_g, w_gate_up, w_down, final_g):
    for l in range(DEPTH):
        h = rms_norm(x, norm1_g[l])
        z = h @ w_in[l]
        za = z[..., :2 * D_A]
        zb = z[..., 2 * D_A:2 * D_A + 2 * D_B]
        zc = z[..., 2 * D_A + 2 * D_B:]
        ya = conformer_conv(za, conv_w[l], conv_b[l], conv_ln_g[l], conv_ln_b[l], w_pw[l])
        yb = spatial_gating(zb, sg_ln_g[l], sg_ln_b[l], w_s[l], b_s[l])
        yc = multiscale_pool(zc, w_pool[l], pool_scale[l])
        x = x + jnp.concatenate([ya, yb, yc], axis=-1) @ w_out[l]
        h = rms_norm(x, norm2_g[l])
        gate, up = jnp.split(h @ w_gate_up[l], 2, axis=-1)
        x = x + (jax.nn.silu(gate) * up) @ w_down[l]
    return rms_norm(x, final_g)


import jax as _jax
import jax.numpy as _jnp

TWIN_FORMAT = 'train_step'
FWD_PARAMS = ['x', 'norm1_g', 'w_in', 'conv_w', 'conv_b', 'conv_ln_g', 'conv_ln_b', 'w_pw', 'sg_ln_g', 'sg_ln_b', 'w_s', 'b_s', 'w_pool', 'pool_scale', 'w_out', 'norm2_g', 'w_gate_up', 'w_down', 'final_g']
TWIN_WEIGHTS = ['norm1_g', 'w_in', 'conv_w', 'conv_b', 'conv_ln_g', 'conv_ln_b', 'w_pw', 'sg_ln_g', 'sg_ln_b', 'w_s', 'b_s', 'w_pool', 'pool_scale', 'w_out', 'norm2_g', 'w_gate_up', 'w_down', 'final_g']
TWIN_DIFF_INPUT = 'x'
TWIN_INPUTS = ['x', 'norm1_g', 'w_in', 'conv_w', 'conv_b', 'conv_ln_g', 'conv_ln_b', 'w_pw', 'sg_ln_g', 'sg_ln_b', 'w_s', 'b_s', 'w_pool', 'pool_scale', 'w_out', 'norm2_g', 'w_gate_up', 'w_down', 'final_g', 'loss_target', 'm_norm1_g', 'm_w_in', 'm_conv_w', 'm_conv_b', 'm_conv_ln_g', 'm_conv_ln_b', 'm_w_pw', 'm_sg_ln_g', 'm_sg_ln_b', 'm_w_s', 'm_b_s', 'm_w_pool', 'm_pool_scale', 'm_w_out', 'm_norm2_g', 'm_w_gate_up', 'm_w_down', 'm_final_g', 'v_norm1_g', 'v_w_in', 'v_conv_w', 'v_conv_b', 'v_conv_ln_g', 'v_conv_ln_b', 'v_w_pw', 'v_sg_ln_g', 'v_sg_ln_b', 'v_w_s', 'v_b_s', 'v_w_pool', 'v_pool_scale', 'v_w_out', 'v_norm2_g', 'v_w_gate_up', 'v_w_down', 'v_final_g']
TWIN_OUTPUTS = ['loss', 'grad_x', 'grad_norm1_g', 'grad_w_in', 'grad_conv_w', 'grad_conv_b', 'grad_conv_ln_g', 'grad_conv_ln_b', 'grad_w_pw', 'grad_sg_ln_g', 'grad_sg_ln_b', 'grad_w_s', 'grad_b_s', 'grad_w_pool', 'grad_pool_scale', 'grad_w_out', 'grad_norm2_g', 'grad_w_gate_up', 'grad_w_down', 'grad_final_g', 'delta_norm1_g', 'delta_w_in', 'delta_conv_w', 'delta_conv_b', 'delta_conv_ln_g', 'delta_conv_ln_b', 'delta_w_pw', 'delta_sg_ln_g', 'delta_sg_ln_b', 'delta_w_s', 'delta_b_s', 'delta_w_pool', 'delta_pool_scale', 'delta_w_out', 'delta_norm2_g', 'delta_w_gate_up', 'delta_w_down', 'delta_final_g', 'new_m_norm1_g', 'new_m_w_in', 'new_m_conv_w', 'new_m_conv_b', 'new_m_conv_ln_g', 'new_m_conv_ln_b', 'new_m_w_pw', 'new_m_sg_ln_g', 'new_m_sg_ln_b', 'new_m_w_s', 'new_m_b_s', 'new_m_w_pool', 'new_m_pool_scale', 'new_m_w_out', 'new_m_norm2_g', 'new_m_w_gate_up', 'new_m_w_down', 'new_m_final_g', 'new_v_norm1_g', 'new_v_w_in', 'new_v_conv_w', 'new_v_conv_b', 'new_v_conv_ln_g', 'new_v_conv_ln_b', 'new_v_w_pw', 'new_v_sg_ln_g', 'new_v_sg_ln_b', 'new_v_w_s', 'new_v_b_s', 'new_v_w_pool', 'new_v_pool_scale', 'new_v_w_out', 'new_v_norm2_g', 'new_v_w_gate_up', 'new_v_w_down', 'new_v_final_g']
TWIN_LEAF_KINDS = {'loss': 'loss', 'grad_x': 'grad_x', 'grad_norm1_g': 'grad_w', 'grad_w_in': 'grad_w', 'grad_conv_w': 'grad_w', 'grad_conv_b': 'grad_w', 'grad_conv_ln_g': 'grad_w', 'grad_conv_ln_b': 'grad_w', 'grad_w_pw': 'grad_w', 'grad_sg_ln_g': 'grad_w', 'grad_sg_ln_b': 'grad_w', 'grad_w_s': 'grad_w', 'grad_b_s': 'grad_w', 'grad_w_pool': 'grad_w', 'grad_pool_scale': 'grad_w', 'grad_w_out': 'grad_w', 'grad_norm2_g': 'grad_w', 'grad_w_gate_up': 'grad_w', 'grad_w_down': 'grad_w', 'grad_final_g': 'grad_w', 'delta_norm1_g': 'delta_w', 'delta_w_in': 'delta_w', 'delta_conv_w': 'delta_w', 'delta_conv_b': 'delta_w', 'delta_conv_ln_g': 'delta_w', 'delta_conv_ln_b': 'delta_w', 'delta_w_pw': 'delta_w', 'delta_sg_ln_g': 'delta_w', 'delta_sg_ln_b': 'delta_w', 'delta_w_s': 'delta_w', 'delta_b_s': 'delta_w', 'delta_w_pool': 'delta_w', 'delta_pool_scale': 'delta_w', 'delta_w_out': 'delta_w', 'delta_norm2_g': 'delta_w', 'delta_w_gate_up': 'delta_w', 'delta_w_down': 'delta_w', 'delta_final_g': 'delta_w', 'new_m_norm1_g': 'new_m', 'new_m_w_in': 'new_m', 'new_m_conv_w': 'new_m', 'new_m_conv_b': 'new_m', 'new_m_conv_ln_g': 'new_m', 'new_m_conv_ln_b': 'new_m', 'new_m_w_pw': 'new_m', 'new_m_sg_ln_g': 'new_m', 'new_m_sg_ln_b': 'new_m', 'new_m_w_s': 'new_m', 'new_m_b_s': 'new_m', 'new_m_w_pool': 'new_m', 'new_m_pool_scale': 'new_m', 'new_m_w_out': 'new_m', 'new_m_norm2_g': 'new_m', 'new_m_w_gate_up': 'new_m', 'new_m_w_down': 'new_m', 'new_m_final_g': 'new_m', 'new_v_norm1_g': 'new_v', 'new_v_w_in': 'new_v', 'new_v_conv_w': 'new_v', 'new_v_conv_b': 'new_v', 'new_v_conv_ln_g': 'new_v', 'new_v_conv_ln_b': 'new_v', 'new_v_w_pw': 'new_v', 'new_v_sg_ln_g': 'new_v', 'new_v_sg_ln_b': 'new_v', 'new_v_w_s': 'new_v', 'new_v_b_s': 'new_v', 'new_v_w_pool': 'new_v', 'new_v_pool_scale': 'new_v', 'new_v_w_out': 'new_v', 'new_v_norm2_g': 'new_v', 'new_v_w_gate_up': 'new_v', 'new_v_w_down': 'new_v', 'new_v_final_g': 'new_v'}


def _forward(args):
    return _fwd_reference(*[args[k] for k in FWD_PARAMS])


def _output_shape():
    out = _jax.eval_shape(lambda: _forward(_fwd_setup_inputs(0)))
    return out.shape, out.dtype

N_MICROBATCH = 1
ADAM_LR = 0.001
ADAM_B1 = 0.9
ADAM_B2 = 0.999
ADAM_EPS = 1e-08
ADAM_WD = 0.01
ADAM_STEP = 10
PER_EXAMPLE_BATCH_AXIS = {'x': 0, 'loss_target': 0}
SHARED_INPUTS = []
_WEIGHT_DTYPES = {'norm1_g': _jnp.float32, 'w_in': _jnp.float32, 'conv_w': _jnp.float32, 'conv_b': _jnp.float32, 'conv_ln_g': _jnp.float32, 'conv_ln_b': _jnp.float32, 'w_pw': _jnp.float32, 'sg_ln_g': _jnp.float32, 'sg_ln_b': _jnp.float32, 'w_s': _jnp.float32, 'b_s': _jnp.float32, 'w_pool': _jnp.float32, 'pool_scale': _jnp.float32, 'w_out': _jnp.float32, 'norm2_g': _jnp.float32, 'w_gate_up': _jnp.float32, 'w_down': _jnp.float32, 'final_g': _jnp.float32}
MOMENT_SCALE = {'norm1_g': 1.273104e-01, 'w_in': 9.465173e-02, 'conv_w': 9.167690e-02, 'conv_b': 2.001961e-01, 'conv_ln_g': 1.149539e-01, 'conv_ln_b': 9.610781e-02, 'w_pw': 8.880963e-02, 'sg_ln_g': 6.604354e-02, 'sg_ln_b': 8.047438e-02, 'w_s': 5.780756e-02, 'b_s': 8.519987e-02, 'w_pool': 1.275436e-01, 'pool_scale': 1.294115e-01, 'w_out': 1.117275e-01, 'norm2_g': 1.085271e-01, 'w_gate_up': 4.526149e-02, 'w_down': 7.380149e-02, 'final_g': 3.202188e+01}


def _to_microbatches(a, axis):
    t = _jnp.moveaxis(a, axis, 0)
    t = t.reshape((N_MICROBATCH, t.shape[0] // N_MICROBATCH) + t.shape[1:])
    return _jnp.moveaxis(t, 1, axis + 1)


def setup_inputs(seed: int = 0) -> dict:
    inp = _fwd_setup_inputs(seed)
    key = _jax.random.fold_in(_jax.random.key(seed), 7919)
    shape, _ = _output_shape()
    out = dict(inp)
    out["loss_target"] = _jax.random.normal(_jax.random.fold_in(key, 0), shape, _jnp.float32)
    for i, name in enumerate(TWIN_WEIGHTS):
        w = inp[name].astype(_jnp.float32)
        if MOMENT_SCALE is None:
            s = _jnp.sqrt(_jnp.mean(_jnp.square(w)) + 1e-30)
        else:
            s = MOMENT_SCALE[name]
        km, kv = _jax.random.split(_jax.random.fold_in(key, i + 1))
        out[name] = w
        out["m_" + name] = s * _jax.random.normal(km, w.shape, _jnp.float32)
        out["v_" + name] = (s * s) * _jax.random.uniform(kv, w.shape, _jnp.float32, 0.5, 1.5)
    if N_MICROBATCH > 1:
        for name, axis in PER_EXAMPLE_BATCH_AXIS.items():
            out[name] = _to_microbatches(out[name], axis)
    return {'x': out['x'], 'norm1_g': out['norm1_g'], 'w_in': out['w_in'], 'conv_w': out['conv_w'], 'conv_b': out['conv_b'], 'conv_ln_g': out['conv_ln_g'], 'conv_ln_b': out['conv_ln_b'], 'w_pw': out['w_pw'], 'sg_ln_g': out['sg_ln_g'], 'sg_ln_b': out['sg_ln_b'], 'w_s': out['w_s'], 'b_s': out['b_s'], 'w_pool': out['w_pool'], 'pool_scale': out['pool_scale'], 'w_out': out['w_out'], 'norm2_g': out['norm2_g'], 'w_gate_up': out['w_gate_up'], 'w_down': out['w_down'], 'final_g': out['final_g'], 'loss_target': out['loss_target'], 'm_norm1_g': out['m_norm1_g'], 'm_w_in': out['m_w_in'], 'm_conv_w': out['m_conv_w'], 'm_conv_b': out['m_conv_b'], 'm_conv_ln_g': out['m_conv_ln_g'], 'm_conv_ln_b': out['m_conv_ln_b'], 'm_w_pw': out['m_w_pw'], 'm_sg_ln_g': out['m_sg_ln_g'], 'm_sg_ln_b': out['m_sg_ln_b'], 'm_w_s': out['m_w_s'], 'm_b_s': out['m_b_s'], 'm_w_pool': out['m_w_pool'], 'm_pool_scale': out['m_pool_scale'], 'm_w_out': out['m_w_out'], 'm_norm2_g': out['m_norm2_g'], 'm_w_gate_up': out['m_w_gate_up'], 'm_w_down': out['m_w_down'], 'm_final_g': out['m_final_g'], 'v_norm1_g': out['v_norm1_g'], 'v_w_in': out['v_w_in'], 'v_conv_w': out['v_conv_w'], 'v_conv_b': out['v_conv_b'], 'v_conv_ln_g': out['v_conv_ln_g'], 'v_conv_ln_b': out['v_conv_ln_b'], 'v_w_pw': out['v_w_pw'], 'v_sg_ln_g': out['v_sg_ln_g'], 'v_sg_ln_b': out['v_sg_ln_b'], 'v_w_s': out['v_w_s'], 'v_b_s': out['v_b_s'], 'v_w_pool': out['v_w_pool'], 'v_pool_scale': out['v_pool_scale'], 'v_w_out': out['v_w_out'], 'v_norm2_g': out['v_norm2_g'], 'v_w_gate_up': out['v_w_gate_up'], 'v_w_down': out['v_w_down'], 'v_final_g': out['v_final_g']}


def _loss(weights, diff, rest, loss_target):
    with _jax.named_scope("forward"):
        args = {**rest, TWIN_DIFF_INPUT: diff, **{k: w.astype(_WEIGHT_DTYPES[k]) for k, w in weights.items()}}
        y = _forward(args)
    with _jax.named_scope("loss_head"):
        err = _jnp.square(y.astype(_jnp.float32) - loss_target)
        return 0.5 * _jnp.sum(_jnp.mean(err, axis=-1)) if err.ndim else 0.5 * err


def _adamw(w, g, m, v):
    m = ADAM_B1 * m + (1.0 - ADAM_B1) * g
    v = ADAM_B2 * v + (1.0 - ADAM_B2) * _jnp.square(g)
    m_hat = m / (1.0 - ADAM_B1 ** ADAM_STEP)
    v_hat = v / (1.0 - ADAM_B2 ** ADAM_STEP)
    delta = -ADAM_LR * (m_hat / (_jnp.sqrt(v_hat) + ADAM_EPS) + ADAM_WD * w)
    return delta, m, v


def reference(x, norm1_g, w_in, conv_w, conv_b, conv_ln_g, conv_ln_b, w_pw, sg_ln_g, sg_ln_b, w_s, b_s, w_pool, pool_scale, w_out, norm2_g, w_gate_up, w_down, final_g, loss_target, m_norm1_g, m_w_in, m_conv_w, m_conv_b, m_conv_ln_g, m_conv_ln_b, m_w_pw, m_sg_ln_g, m_sg_ln_b, m_w_s, m_b_s, m_w_pool, m_pool_scale, m_w_out, m_norm2_g, m_w_gate_up, m_w_down, m_final_g, v_norm1_g, v_w_in, v_conv_w, v_conv_b, v_conv_ln_g, v_conv_ln_b, v_w_pw, v_sg_ln_g, v_sg_ln_b, v_w_s, v_b_s, v_w_pool, v_pool_scale, v_w_out, v_norm2_g, v_w_gate_up, v_w_down, v_final_g):
    given = dict(x=x, norm1_g=norm1_g, w_in=w_in, conv_w=conv_w, conv_b=conv_b, conv_ln_g=conv_ln_g, conv_ln_b=conv_ln_b, w_pw=w_pw, sg_ln_g=sg_ln_g, sg_ln_b=sg_ln_b, w_s=w_s, b_s=b_s, w_pool=w_pool, pool_scale=pool_scale, w_out=w_out, norm2_g=norm2_g, w_gate_up=w_gate_up, w_down=w_down, final_g=final_g, loss_target=loss_target, m_norm1_g=m_norm1_g, m_w_in=m_w_in, m_conv_w=m_conv_w, m_conv_b=m_conv_b, m_conv_ln_g=m_conv_ln_g, m_conv_ln_b=m_conv_ln_b, m_w_pw=m_w_pw, m_sg_ln_g=m_sg_ln_g, m_sg_ln_b=m_sg_ln_b, m_w_s=m_w_s, m_b_s=m_b_s, m_w_pool=m_w_pool, m_pool_scale=m_pool_scale, m_w_out=m_w_out, m_norm2_g=m_norm2_g, m_w_gate_up=m_w_gate_up, m_w_down=m_w_down, m_final_g=m_final_g, v_norm1_g=v_norm1_g, v_w_in=v_w_in, v_conv_w=v_conv_w, v_conv_b=v_conv_b, v_conv_ln_g=v_conv_ln_g, v_conv_ln_b=v_conv_ln_b, v_w_pw=v_w_pw, v_sg_ln_g=v_sg_ln_g, v_sg_ln_b=v_sg_ln_b, v_w_s=v_w_s, v_b_s=v_b_s, v_w_pool=v_w_pool, v_pool_scale=v_pool_scale, v_w_out=v_w_out, v_norm2_g=v_norm2_g, v_w_gate_up=v_w_gate_up, v_w_down=v_w_down, v_final_g=v_final_g)
    weights = {n: given[n] for n in TWIN_WEIGHTS}
    shared = {n: given[n] for n in SHARED_INPUTS}
    per_example = {n: given[n] for n in ['x']}
    grad_fn = _jax.value_and_grad(_loss, argnums=(0, 1))

    def one_microbatch(ex, loss_target):
        ex = dict(ex)
        diff = ex.pop(TWIN_DIFF_INPUT)
        return grad_fn(weights, diff, {**shared, **ex}, loss_target)

    if N_MICROBATCH == 1:
        loss, (grad_w, grad_x) = one_microbatch(per_example, given["loss_target"])
    else:
        def body(carry, xs):
            loss_sum, grad_sum = carry
            l_k, (gw_k, gx_k) = one_microbatch(xs[0], xs[1])
            with _jax.named_scope("update"):
                return (loss_sum + l_k, _jax.tree.map(_jnp.add, grad_sum, gw_k)), gx_k

        init = (_jnp.zeros((), _jnp.float32), _jax.tree.map(_jnp.zeros_like, weights))
        (loss, grad_w), grad_x = _jax.lax.scan(body, init, (per_example, given["loss_target"]))
    with _jax.named_scope("update"):
        delta_w, new_m, new_v = {}, {}, {}
        for n in TWIN_WEIGHTS:
            delta_w[n], new_m[n], new_v[n] = _adamw(weights[n], grad_w[n], given["m_" + n], given["v_" + n])
    return (loss, grad_x, *[grad_w[n] for n in TWIN_WEIGHTS], *[delta_w[n] for n in TWIN_WEIGHTS],
            *[new_m[n] for n in TWIN_WEIGHTS], *[new_v[n] for n in TWIN_WEIGHTS])
```

```python
import functools
import math

import jax
import jax.numpy as jnp
from jax import lax
from jax.experimental import pallas as pl
from jax.experimental.pallas import tpu as pltpu

F32 = jnp.float32
BF16 = jnp.bfloat16

D_MODEL = 1024
DEPTH = 2
D_A = 384
D_B = 384
D_C = 256
D_IN = 2 * D_A + 2 * D_B + D_C
N_HEADS_B = 4
HEAD_DIM_B = 96
GROUP_DIM_C = 64
CONV_WIDTH = 31
CHUNK = 128
D_FF = 2816
RMS_EPS = 1e-6
LN_EPS = 1e-5
HALO = 32

ADAM_LR = 0.001
ADAM_B1 = 0.9
ADAM_B2 = 0.999
ADAM_EPS = 1e-08
ADAM_WD = 0.01
ADAM_STEP = 10

VMEM_LIMIT = 56 * 1024 * 1024
_INTERPRET = False

MESH = pl.DeviceIdType.MESH
ANY = pl.BlockSpec(memory_space=pl.ANY)


def _params(sem=None):
    return pltpu.CompilerParams(dimension_semantics=sem, vmem_limit_bytes=VMEM_LIMIT)


def _sigmoid(x):
    return 1.0 / (1.0 + jnp.exp(-x))


_GELU_C = math.sqrt(2.0 / math.pi)


def _gelu(x):
    return 0.5 * x * (1.0 + jnp.tanh(_GELU_C * (x + 0.044715 * x * x * x)))


def _gelu_grad(x):
    t = jnp.tanh(_GELU_C * (x + 0.044715 * x * x * x))
    return 0.5 * (1.0 + t) + 0.5 * x * (1.0 - t * t) * _GELU_C * (1.0 + 3 * 0.044715 * x * x)


def _dot(a, b):
    return jnp.dot(a, b, preferred_element_type=F32)


def _dot_nt(a, b):
    return lax.dot_general(a, b, (((1,), (1,)), ((), ())), preferred_element_type=F32)


def _dot_tn(a, b):
    return lax.dot_general(a, b, (((0,), (0,)), ((), ())), preferred_element_type=F32)


def rmsnorm_fwd(x, g, *, name):
    t, d = x.shape
    tm = min(512, t)

    def body(x_ref, g_ref, o_ref):
        xv = x_ref[...]
        rstd = lax.rsqrt(jnp.mean(xv * xv, axis=-1, keepdims=True) + RMS_EPS)
        o_ref[...] = (xv * rstd * g_ref[...]).astype(BF16)

    return pl.pallas_call(
        body, name=name, grid=(t // tm,),
        in_specs=[pl.BlockSpec((tm, d), lambda i: (i, 0)), pl.BlockSpec((1, d), lambda i: (0, 0))],
        out_specs=pl.BlockSpec((tm, d), lambda i: (i, 0)),
        out_shape=jax.ShapeDtypeStruct((t, d), BF16),
        compiler_params=_params(("arbitrary",)), interpret=_INTERPRET,
    )(x, g)


def rmsnorm_bwd(dh, x, g, dres, *, name):
    t, d = x.shape
    tm = min(512, t)

    def body(dh_ref, x_ref, g_ref, dres_ref, dx_ref, dxb_ref, dg_ref):
        i = pl.program_id(0)
        xv = x_ref[...]
        rstd = lax.rsqrt(jnp.mean(xv * xv, axis=-1, keepdims=True) + RMS_EPS)
        xhat = xv * rstd
        dhv = dh_ref[...]
        dxhat = dhv * g_ref[...]
        dx = dres_ref[...] + rstd * (dxhat - xhat * jnp.mean(dxhat * xhat, axis=-1, keepdims=True))
        dx_ref[...] = dx
        dxb_ref[...] = dx.astype(BF16)

        @pl.when(i == 0)
        def _():
            dg_ref[...] = jnp.zeros_like(dg_ref)

        dg_ref[...] += jnp.sum(dhv * xhat, axis=0, keepdims=True)

    row = pl.BlockSpec((tm, d), lambda i: (i, 0))
    vec = pl.BlockSpec((1, d), lambda i: (0, 0))
    return pl.pallas_call(
        body, name=name, grid=(t // tm,),
        in_specs=[row, row, vec, row], out_specs=[row, row, vec],
        out_shape=[jax.ShapeDtypeStruct((t, d), F32), jax.ShapeDtypeStruct((t, d), BF16),
                   jax.ShapeDtypeStruct((1, d), F32)],
        compiler_params=_params(("arbitrary",)), interpret=_INTERPRET,
    )(dh, x, g, dres)


def loss_head(x, g, target, *, name):
    t, d = x.shape
    tm = min(512, t)

    def body(x_ref, g_ref, t_ref, loss_ref, dx_ref, dxb_ref, dg_ref):
        i = pl.program_id(0)
        xv = x_ref[...]
        gv = g_ref[...]
        rstd = lax.rsqrt(jnp.mean(xv * xv, axis=-1, keepdims=True) + RMS_EPS)
        xhat = xv * rstd
        err = xhat * gv - t_ref[...]
        dy = err * (1.0 / d)
        dxhat = dy * gv
        dx = rstd * (dxhat - xhat * jnp.mean(dxhat * xhat, axis=-1, keepdims=True))
        dx_ref[...] = dx
        dxb_ref[...] = dx.astype(BF16)

        @pl.when(i == 0)
        def _():
            dg_ref[...] = jnp.zeros_like(dg_ref)
            loss_ref[...] = jnp.zeros_like(loss_ref)

        dg_ref[...] += jnp.sum(dy * xhat, axis=0, keepdims=True)
        per_tok = jnp.sum(err * err, axis=-1, keepdims=True) * (0.5 / d)
        loss_ref[...] += jnp.sum(per_tok, axis=0, keepdims=True)

    row = pl.BlockSpec((tm, d), lambda i: (i, 0))
    vec = pl.BlockSpec((1, d), lambda i: (0, 0))
    one = pl.BlockSpec((1, 1), lambda i: (0, 0))
    return pl.pallas_call(
        body, name=name, grid=(t // tm,),
        in_specs=[row, vec, row], out_specs=[one, row, row, vec],
        out_shape=[jax.ShapeDtypeStruct((1, 1), F32), jax.ShapeDtypeStruct((t, d), F32),
                   jax.ShapeDtypeStruct((t, d), BF16), jax.ShapeDtypeStruct((1, d), F32)],
        compiler_params=_params(("arbitrary",)), interpret=_INTERPRET,
    )(x, g, target)


def mm_nn(a, b, *, name, tm, tn, out_dtype, residual=None):
    m, k = a.shape
    n = b.shape[1]
    tm, tn = min(tm, m), min(tn, n)
    has_res = residual is not None

    def body(a_ref, b_ref, *rest):
        o_ref = rest[-1]
        acc = _dot(a_ref[...], b_ref[...])
        if has_res:
            acc = acc + rest[0][...]
        o_ref[...] = acc.astype(o_ref.dtype)

    in_specs = [pl.BlockSpec((tm, k), lambda j, i: (i, 0)), pl.BlockSpec((k, tn), lambda j, i: (0, j))]
    args = [a, b]
    if has_res:
        in_specs.append(pl.BlockSpec((tm, tn), lambda j, i: (i, j)))
        args.append(residual)
    return pl.pallas_call(
        body, name=name, grid=(n // tn, m // tm), in_specs=in_specs,
        out_specs=pl.BlockSpec((tm, tn), lambda j, i: (i, j)),
        out_shape=jax.ShapeDtypeStruct((m, n), out_dtype),
        compiler_params=_params(("arbitrary", "arbitrary")), interpret=_INTERPRET,
    )(*args)


def mm_nt(a, b, *, name, tm, tn, out_dtype):
    m, k = a.shape
    n = b.shape[0]
    tm, tn = min(tm, m), min(tn, n)

    def body(a_ref, b_ref, o_ref):
        o_ref[...] = _dot_nt(a_ref[...], b_ref[...]).astype(o_ref.dtype)

    return pl.pallas_call(
        body, name=name, grid=(n // tn, m // tm),
        in_specs=[pl.BlockSpec((tm, k), lambda j, i: (i, 0)), pl.BlockSpec((tn, k), lambda j, i: (j, 0))],
        out_specs=pl.BlockSpec((tm, tn), lambda j, i: (i, j)),
        out_shape=jax.ShapeDtypeStruct((m, n), out_dtype),
        compiler_params=_params(("arbitrary", "arbitrary")), interpret=_INTERPRET,
    )(a, b)


def mm_tn(a, b, *, name, tk, tn, tt, out_dtype):
    t, k = a.shape
    n = b.shape[1]
    tk, tn, tt = min(tk, k), min(tn, n), min(tt, t)
    nt = t // tt

    def body(a_ref, b_ref, o_ref, acc_ref):
        s = pl.program_id(2)

        @pl.when(s == 0)
        def _():
            acc_ref[...] = jnp.zeros_like(acc_ref)

        acc_ref[...] += _dot_tn(a_ref[...], b_ref[...])

        @pl.when(s == nt - 1)
        def _():
            o_ref[...] = acc_ref[...].astype(o_ref.dtype)

    return pl.pallas_call(
        body, name=name, grid=(k // tk, n // tn, nt),
        in_specs=[pl.BlockSpec((tt, tk), lambda i, j, s: (s, i)), pl.BlockSpec((tt, tn), lambda i, j, s: (s, j))],
        out_specs=pl.BlockSpec((tk, tn), lambda i, j, s: (i, j)),
        out_shape=jax.ShapeDtypeStruct((k, n), out_dtype),
        scratch_shapes=[pltpu.VMEM((tk, tn), F32)],
        compiler_params=_params(("arbitrary", "arbitrary", "arbitrary")), interpret=_INTERPRET,
    )(a, b)


def swiglu_fwd(h, w, *, name, tm, tn):
    t, d = h.shape
    ff = w.shape[1] // 2
    tm, tn = min(tm, t), min(tn, ff)
    nb = ff // tn

    def body(h_ref, wg_ref, wu_ref, gu_ref, act_ref):
        hv = h_ref[...]
        gate = _dot(hv, wg_ref[...])
        up = _dot(hv, wu_ref[...])
        gu_ref[0] = gate.astype(BF16)
        gu_ref[1] = up.astype(BF16)
        act_ref[...] = (gate * _sigmoid(gate) * up).astype(BF16)

    return pl.pallas_call(
        body, name=name, grid=(nb, t // tm),
        in_specs=[pl.BlockSpec((tm, d), lambda j, i: (i, 0)),
                  pl.BlockSpec((d, tn), lambda j, i: (0, j)),
                  pl.BlockSpec((d, tn), lambda j, i: (0, j + nb))],
        out_specs=[pl.BlockSpec((2, tm, tn), lambda j, i: (0, i, j)), pl.BlockSpec((tm, tn), lambda j, i: (i, j))],
        out_shape=[jax.ShapeDtypeStruct((2, t, ff), BF16), jax.ShapeDtypeStruct((t, ff), BF16)],
        compiler_params=_params(("arbitrary", "arbitrary")), interpret=_INTERPRET,
    )(h, w, w)


def swiglu_bwd(dx, w_down, gu, *, name, tm, tn):
    t, d = dx.shape
    ff = w_down.shape[0]
    tm, tn = min(tm, t), min(tn, ff)

    def body(dx_ref, w_ref, gu_ref, o_ref):
        dact = _dot_nt(dx_ref[...], w_ref[...])
        gate = gu_ref[0].astype(F32)
        up = gu_ref[1].astype(F32)
        sg = _sigmoid(gate)
        o_ref[0] = (dact * up * sg * (1.0 + gate * (1.0 - sg))).astype(BF16)
        o_ref[1] = (dact * gate * sg).astype(BF16)

    return pl.pallas_call(
        body, name=name, grid=(ff // tn, t // tm),
        in_specs=[pl.BlockSpec((tm, d), lambda j, i: (i, 0)), pl.BlockSpec((tn, d), lambda j, i: (j, 0)),
                  pl.BlockSpec((2, tm, tn), lambda j, i: (0, i, j))],
        out_specs=pl.BlockSpec((2, tm, tn), lambda j, i: (0, i, j)),
        out_shape=jax.ShapeDtypeStruct((2, t, ff), BF16),
        compiler_params=_params(("arbitrary", "arbitrary")), interpret=_INTERPRET,
    )(dx, w_down, gu)


def mm_gu_nt(dgu, w, *, name, tm):
    _, t, ff = dgu.shape
    d = w.shape[0]
    tm = min(tm, t)

    def body(a_ref, w_ref, o_ref):
        acc = _dot_nt(a_ref[0], w_ref[:, :ff])
        acc = acc + _dot_nt(a_ref[1], w_ref[:, ff:])
        o_ref[...] = acc

    return pl.pallas_call(
        body, name=name, grid=(t // tm,),
        in_specs=[pl.BlockSpec((2, tm, ff), lambda i: (0, i, 0)), pl.BlockSpec((d, 2 * ff), lambda i: (0, 0))],
        out_specs=pl.BlockSpec((tm, d), lambda i: (i, 0)),
        out_shape=jax.ShapeDtypeStruct((t, d), F32),
        compiler_params=_params(("arbitrary",)), interpret=_INTERPRET,
    )(dgu, w)


def mm_gu_tn(h, dgu, *, name, tn, tt):
    t, d = h.shape
    ff = dgu.shape[2]
    tn, tt = min(tn, ff), min(tt, t)
    nb = ff // tn
    nt = t // tt

    def body(h_ref, b_ref, o_ref, acc_ref):
        s = pl.program_id(1)

        @pl.when(s == 0)
        def _():
            acc_ref[...] = jnp.zeros_like(acc_ref)

        acc_ref[...] += _dot_tn(h_ref[...], b_ref[...])

        @pl.when(s == nt - 1)
        def _():
            o_ref[...] = acc_ref[...].astype(o_ref.dtype)

    return pl.pallas_call(
        body, name=name, grid=(2 * nb, nt),
        in_specs=[pl.BlockSpec((tt, d), lambda j, s: (s, 0)),
                  pl.BlockSpec((None, tt, tn), lambda j, s: (j // nb, s, j % nb))],
        out_specs=pl.BlockSpec((d, tn), lambda j, s: (0, j)),
        out_shape=jax.ShapeDtypeStruct((d, 2 * ff), BF16),
        scratch_shapes=[pltpu.VMEM((d, tn), F32)],
        compiler_params=_params(("arbitrary", "arbitrary")), interpret=_INTERPRET,
    )(h, dgu)


def _head_masks(shape):
    lane = lax.broadcasted_iota(jnp.int32, shape, 1)
    return [(lane >= h * HEAD_DIM_B) & (lane < (h + 1) * HEAD_DIM_B) for h in range(N_HEADS_B)]


def _causal_rows(ws):
    r = lax.broadcasted_iota(jnp.int32, ws.shape, 0) % CHUNK
    c = lax.broadcasted_iota(jnp.int32, ws.shape, 1)
    return jnp.where(c <= r, ws, jnp.zeros_like(ws))


def _pool_window(shape):
    lane = lax.broadcasted_iota(jnp.int32, shape, 1)
    return jnp.left_shift(2, lane // GROUP_DIM_C)


def _layer_norm_fwd(x, g, b):
    mu = jnp.mean(x, axis=-1, keepdims=True)
    xc = x - mu
    rstd = lax.rsqrt(jnp.mean(xc * xc, axis=-1, keepdims=True) + LN_EPS)
    xhat = xc * rstd
    return xhat * g + b, xhat, rstd


def _layer_norm_bwd(dy, xhat, rstd, g):
    dxhat = dy * g
    return rstd * (dxhat - jnp.mean(dxhat, axis=-1, keepdims=True)
                   - xhat * jnp.mean(dxhat * xhat, axis=-1, keepdims=True))


def _gate_mix(ws_masked, vl_chunk, masks):
    out = _dot(ws_masked, vl_chunk.astype(BF16))
    s = jnp.zeros((CHUNK, D_B), F32)
    for h in range(N_HEADS_B):
        s = s + jnp.where(masks[h], out[h * CHUNK:(h + 1) * CHUNK], 0.0)
    return s


def _mixer_specs(tm, nt, seq):
    hb = tm // HALO

    def cur(c):
        return pl.BlockSpec((tm, c), lambda b, i: (b * nt + i, 0))

    def prev(c):
        return pl.BlockSpec((HALO, c), lambda b, i: (jnp.maximum((b * nt + i) * hb - 1, 0), 0))

    def nxt(c):
        last = (2 * seq) // HALO - 1
        return pl.BlockSpec((HALO, c), lambda b, i: (jnp.minimum((b * nt + i + 1) * hb, last), 0))

    def full(shape):
        return pl.BlockSpec(shape, lambda b, i: tuple(0 for _ in shape))

    return cur, prev, nxt, full


_MIX_PARAM_SHAPES = [(32, D_A), (1, D_A), (1, D_A), (1, D_A), (D_A, D_A), (1, D_B), (1, D_B),
                     (N_HEADS_B * CHUNK, CHUNK), (CHUNK, D_B), (D_C, D_C), (1, D_C)]


def mixer_fwd(z, mp, *, seq, name, tm=512):
    t = z.shape[0]
    tm = min(tm, seq)
    nt = seq // tm
    cur, prev, _, full = _mixer_specs(tm, nt, seq)

    def body(zc_ref, zp_ref, cw_ref, cb_ref, clg_ref, clb_ref, wpw_ref, slg_ref, slb_ref, ws_ref, bias_ref,
             wp_ref, ps_ref, o_ref, ys_ref, zs_ref):
        i = pl.program_id(1)
        has_prev = i > 0
        yp = zp_ref[:, 0:D_A].astype(F32) * _sigmoid(zp_ref[:, D_A:2 * D_A].astype(F32))
        ys_ref[0:HALO, :] = jnp.where(has_prev, yp, 0.0)
        ys_ref[HALO:HALO + tm, :] = zc_ref[:, 0:D_A].astype(F32) * _sigmoid(zc_ref[:, D_A:2 * D_A].astype(F32))
        acc = jnp.zeros((tm, D_A), F32) + cb_ref[...]
        for k in range(CONV_WIDTH):
            acc = acc + cw_ref[k:k + 1, :] * ys_ref[pl.ds(HALO - (CONV_WIDTH - 1) + k, tm), :]
        ln, _, _ = _layer_norm_fwd(acc, clg_ref[...], clb_ref[...])
        sl = ln * _sigmoid(ln)
        o_ref[:, 0:D_A] = _dot(sl.astype(BF16), wpw_ref[...]).astype(BF16)
        gz = _gelu(zc_ref[:, 2 * D_A:2 * D_A + 2 * D_B].astype(F32))
        u = gz[:, :D_B]
        vl, _, _ = _layer_norm_fwd(gz[:, D_B:], slg_ref[...], slb_ref[...])
        wsm = _causal_rows(ws_ref[...])
        masks = _head_masks((CHUNK, D_B))
        for c in range(tm // CHUNK):
            rows = slice(c * CHUNK, (c + 1) * CHUNK)
            s = _gate_mix(wsm, vl[rows], masks) + bias_ref[...]
            o_ref[rows, D_A:D_A + D_B] = (u[rows] * s).astype(BF16)
        c0 = 2 * D_A + 2 * D_B
        zs_ref[0:HALO, :] = jnp.where(has_prev, zp_ref[:, c0:c0 + D_C].astype(F32), 0.0)
        zcur = zc_ref[:, c0:c0 + D_C].astype(F32)
        zs_ref[HALO:HALO + tm, :] = zcur
        win = _pool_window((tm, D_C))
        wsum = jnp.zeros((tm, D_C), F32)
        for j in range(16):
            wsum = wsum + jnp.where(j < win, zs_ref[pl.ds(HALO - j, tm), :], 0.0)
        pos = i * tm + lax.broadcasted_iota(jnp.int32, (tm, D_C), 0)
        cnt = jnp.minimum(pos + 1, win).astype(F32)
        p = wsum / cnt - zcur
        y = _dot(p.astype(BF16), wp_ref[...])
        o_ref[:, D_A + D_B:D_A + D_B + D_C] = (y * ps_ref[...]).astype(BF16)

    in_specs = [cur(D_IN), prev(D_IN)] + [full(s) for s in _MIX_PARAM_SHAPES]
    return pl.pallas_call(
        body, name=name, grid=(2, nt), in_specs=in_specs, out_specs=cur(D_MODEL),
        out_shape=jax.ShapeDtypeStruct((t, D_MODEL), BF16),
        scratch_shapes=[pltpu.VMEM((HALO + tm, D_A), F32), pltpu.VMEM((HALO + tm, D_C), F32)],
        compiler_params=_params(("arbitrary", "arbitrary")), interpret=_INTERPRET,
    )(z, z, *mp)


def mixer_bwd(z, dm, mp, *, seq, name, tm=256):
    t = z.shape[0]
    tm = min(tm, seq)
    nt = seq // tm
    ext = tm + HALO
    cur, prev, nxt, full = _mixer_specs(tm, nt, seq)
    grad_shapes = [(32, D_A), (1, D_A), (1, D_A), (1, D_A), (D_A, D_A), (1, D_B), (1, D_B),
                   (N_HEADS_B * CHUNK, CHUNK), (CHUNK, CHUNK), (D_C, D_C), (1, D_C)]

    def body(zc_ref, zp_ref, zn_ref, dmc_ref, dmn_ref, cw_ref, cb_ref, clg_ref, clb_ref, wpw_ref, slg_ref, slb_ref,
             ws_ref, bias_ref, wp_ref, ps_ref,
             dz_ref, dcw_ref, dcb_ref, dclg_ref, dclb_ref, dwpw_ref, dslg_ref, dslb_ref, dws_ref, dbs_ref, dwp_ref,
             dps_ref, ys_ref, dcs_ref, zs_ref, qs_ref):
        b = pl.program_id(0)
        i = pl.program_id(1)
        has_prev = i > 0
        has_next = i < nt - 1
        grads = [dcw_ref, dcb_ref, dclg_ref, dclb_ref, dwpw_ref, dslg_ref, dslb_ref, dws_ref, dbs_ref, dwp_ref, dps_ref]

        @pl.when((b == 0) & (i == 0))
        def _():
            for r in grads:
                r[...] = jnp.zeros_like(r)

        ext_row = lax.broadcasted_iota(jnp.int32, (ext, 1), 0)
        live = (ext_row < tm) | has_next

        yp = zp_ref[:, 0:D_A].astype(F32) * _sigmoid(zp_ref[:, D_A:2 * D_A].astype(F32))
        ys_ref[0:HALO, :] = jnp.where(has_prev, yp, 0.0)
        a_cur = zc_ref[:, 0:D_A].astype(F32)
        sig_cur = _sigmoid(zc_ref[:, D_A:2 * D_A].astype(F32))
        ys_ref[HALO:HALO + tm, :] = a_cur * sig_cur
        yn = zn_ref[:, 0:D_A].astype(F32) * _sigmoid(zn_ref[:, D_A:2 * D_A].astype(F32))
        ys_ref[HALO + tm:HALO + tm + HALO, :] = jnp.where(has_next, yn, 0.0)
        acc = jnp.zeros((ext, D_A), F32) + cb_ref[...]
        for k in range(CONV_WIDTH):
            acc = acc + cw_ref[k:k + 1, :] * ys_ref[pl.ds(HALO - (CONV_WIDTH - 1) + k, ext), :]
        ln, xhat, rstd = _layer_norm_fwd(acc, clg_ref[...], clb_ref[...])
        sg = _sigmoid(ln)
        sl = ln * sg
        dya = jnp.concatenate([dmc_ref[:, 0:D_A], dmn_ref[:, 0:D_A]], axis=0)
        dsl = _dot_nt(dya, wpw_ref[...])
        dln = dsl * sg * (1.0 + ln * (1.0 - sg))
        dc = _layer_norm_bwd(dln, xhat, rstd, clg_ref[...])
        dc = jnp.where(live, dc, 0.0)
        dcs_ref[...] = dc
        dwpw_ref[...] += _dot_tn(sl[:tm].astype(BF16), dya[:tm])
        dclg_ref[...] += jnp.sum(dln[:tm] * xhat[:tm], axis=0, keepdims=True)
        dclb_ref[...] += jnp.sum(dln[:tm], axis=0, keepdims=True)
        dcb_ref[...] += jnp.sum(dc[:tm], axis=0, keepdims=True)
        dy = jnp.zeros((tm, D_A), F32)
        for k in range(CONV_WIDTH):
            off = HALO - (CONV_WIDTH - 1) + k
            dcw_ref[k:k + 1, :] += jnp.sum(dcs_ref[0:tm, :] * ys_ref[pl.ds(off, tm), :], axis=0, keepdims=True)
            dy = dy + cw_ref[k:k + 1, :] * dcs_ref[pl.ds(CONV_WIDTH - 1 - k, tm), :]
        dz_ref[:, 0:D_A] = (dy * sig_cur).astype(BF16)
        dz_ref[:, D_A:2 * D_A] = (dy * a_cur * sig_cur * (1.0 - sig_cur)).astype(BF16)

        zb = zc_ref[:, 2 * D_A:2 * D_A + 2 * D_B].astype(F32)
        gz = _gelu(zb)
        u = gz[:, :D_B]
        vl, vhat, vrstd = _layer_norm_fwd(gz[:, D_B:], slg_ref[...], slb_ref[...])
        dyb = dmc_ref[:, D_A:D_A + D_B].astype(F32)
        wsm = _causal_rows(ws_ref[...])
        masks = _head_masks((CHUNK, D_B))
        ds_all = dyb * u
        du_parts, dvl_parts = [], []
        for c in range(tm // CHUNK):
            rows = slice(c * CHUNK, (c + 1) * CHUNK)
            vlc = vl[rows].astype(BF16)
            s = _gate_mix(wsm, vl[rows], masks) + bias_ref[...]
            du_parts.append(dyb[rows] * s)
            ds = ds_all[rows]
            stack = jnp.concatenate([jnp.where(masks[h], ds, 0.0) for h in range(N_HEADS_B)], axis=0).astype(BF16)
            dvl_parts.append(_dot_tn(wsm, stack))
            dws_ref[...] += _dot_nt(stack, vlc)
        du = jnp.concatenate(du_parts, axis=0)
        dvl = jnp.concatenate(dvl_parts, axis=0)
        dbias = jnp.zeros((CHUNK, D_B), F32)
        for c in range(tm // CHUNK):
            dbias = dbias + ds_all[c * CHUNK:(c + 1) * CHUNK]
        lane = lax.broadcasted_iota(jnp.int32, (CHUNK, CHUNK), 1)
        dbs = jnp.zeros((CHUNK, CHUNK), F32)
        for h in range(N_HEADS_B):
            col = jnp.sum(jnp.where(masks[h], dbias, 0.0), axis=1, keepdims=True)
            dbs = dbs + jnp.where(lane == h, col, 0.0)
        dbs_ref[...] += dbs
        dslg_ref[...] += jnp.sum(dvl * vhat, axis=0, keepdims=True)
        dslb_ref[...] += jnp.sum(dvl, axis=0, keepdims=True)
        dv = _layer_norm_bwd(dvl, vhat, vrstd, slg_ref[...])
        gg = _gelu_grad(zb)
        dz_ref[:, 2 * D_A:2 * D_A + D_B] = (du * gg[:, :D_B]).astype(BF16)
        dz_ref[:, 2 * D_A + D_B:2 * D_A + 2 * D_B] = (dv * gg[:, D_B:]).astype(BF16)

        c0 = 2 * D_A + 2 * D_B
        m0 = D_A + D_B
        zs_ref[0:HALO, :] = jnp.where(has_prev, zp_ref[:, c0:c0 + D_C].astype(F32), 0.0)
        zcur = zc_ref[:, c0:c0 + D_C].astype(F32)
        zs_ref[HALO:HALO + tm, :] = zcur
        win = _pool_window((tm, D_C))
        wsum = jnp.zeros((tm, D_C), F32)
        for j in range(16):
            wsum = wsum + jnp.where(j < win, zs_ref[pl.ds(HALO - j, tm), :], 0.0)
        pos = i * tm + lax.broadcasted_iota(jnp.int32, (tm, D_C), 0)
        cnt = jnp.minimum(pos + 1, win).astype(F32)
        pb = (wsum / cnt - zcur).astype(BF16)
        y = _dot(pb, wp_ref[...])
        dyc = jnp.concatenate([dmc_ref[:, m0:m0 + D_C], dmn_ref[:, m0:m0 + D_C]], axis=0).astype(F32)
        dps_ref[...] += jnp.sum(dyc[:tm] * y, axis=0, keepdims=True)
        dyv = (dyc * ps_ref[...]).astype(BF16)
        dwp_ref[...] += _dot_tn(pb, dyv[:tm])
        dp = _dot_nt(dyv, wp_ref[...])
        win_e = _pool_window((ext, D_C))
        pos_e = i * tm + lax.broadcasted_iota(jnp.int32, (ext, D_C), 0)
        cnt_e = jnp.minimum(pos_e + 1, win_e).astype(F32)
        qs_ref[...] = jnp.where(live, dp / cnt_e, 0.0)
        dzc = -dp[:tm]
        for j in range(16):
            dzc = dzc + jnp.where(j < win, qs_ref[pl.ds(j, tm), :], 0.0)
        dz_ref[:, c0:c0 + D_C] = dzc.astype(BF16)

        @pl.when((b == 1) & (i == nt - 1))
        def _():
            dws_ref[...] = _causal_rows(dws_ref[...])

    in_specs = ([cur(D_IN), prev(D_IN), nxt(D_IN), cur(D_MODEL), nxt(D_MODEL)]
                + [full(s) for s in _MIX_PARAM_SHAPES])
    out_specs = [cur(D_IN)] + [full(s) for s in grad_shapes]
    out_shape = [jax.ShapeDtypeStruct((t, D_IN), BF16)] + [jax.ShapeDtypeStruct(s, F32) for s in grad_shapes]
    return pl.pallas_call(
        body, name=name, grid=(2, nt), in_specs=in_specs, out_specs=out_specs, out_shape=out_shape,
        scratch_shapes=[pltpu.VMEM((HALO + tm + HALO, D_A), F32), pltpu.VMEM((ext, D_A), F32),
                        pltpu.VMEM((HALO + tm, D_C), F32), pltpu.VMEM((ext, D_C), F32)],
        compiler_params=_params(("arbitrary", "arbitrary")), interpret=_INTERPRET,
    )(z, z, z, dm, dm, *mp)


MIXER_SMALL = ["conv_w", "conv_b", "conv_ln_g", "conv_ln_b", "w_pw", "sg_ln_g", "sg_ln_b", "w_s", "b_s", "w_pool",
               "pool_scale"]


def _mixer_params(p, l):
    wp_bd = jnp.zeros((D_C, D_C), F32)
    for g in range(D_C // GROUP_DIM_C):
        sl = slice(g * GROUP_DIM_C, (g + 1) * GROUP_DIM_C)
        wp_bd = wp_bd.at[sl, sl].set(p["w_pool"][l, g])
    return [
        jnp.pad(p["conv_w"][l], ((0, 32 - CONV_WIDTH), (0, 0))),
        p["conv_b"][l][None], p["conv_ln_g"][l][None], p["conv_ln_b"][l][None],
        p["w_pw"][l].astype(BF16),
        p["sg_ln_g"][l][None], p["sg_ln_b"][l][None],
        p["w_s"][l].reshape(N_HEADS_B * CHUNK, CHUNK).astype(BF16),
        jnp.repeat(p["b_s"][l].T, HEAD_DIM_B, axis=1),
        wp_bd.astype(BF16),
        p["pool_scale"][l][None],
    ]


def _mixer_grads(g):
    dcw, dcb, dclg, dclb, dwpw, dslg, dslb, dws, dbs, dwp, dps = g
    blocks = [dwp[i * GROUP_DIM_C:(i + 1) * GROUP_DIM_C, i * GROUP_DIM_C:(i + 1) * GROUP_DIM_C]
              for i in range(D_C // GROUP_DIM_C)]
    return [dcw[:CONV_WIDTH], dcb[0], dclg[0], dclb[0], dwpw, dslg[0], dslb[0],
            dws.reshape(N_HEADS_B, CHUNK, CHUNK), dbs[:, :N_HEADS_B].T, jnp.stack(blocks), dps[0]]


def local_step(x, target, p, wb, *, mix_tm=(512, 256)):
    bsz, seq, d = x.shape
    t = bsz * seq
    xs = [x.reshape(t, d)]
    saved = []
    for l in range(DEPTH):
        x0 = xs[-1]
        mp = _mixer_params(p, l)
        h1 = rmsnorm_fwd(x0, p["norm1_g"][l][None], name=f"norm1_fwd_{l}")
        z = mm_nn(h1, wb["w_in"][l], name=f"in_proj_{l}", tm=512, tn=D_IN, out_dtype=BF16)
        mc = mixer_fwd(z, mp, seq=seq, name=f"mixer_fwd_{l}", tm=mix_tm[0])
        x1 = mm_nn(mc, wb["w_out"][l], name=f"out_proj_{l}", tm=512, tn=D_MODEL, out_dtype=F32, residual=x0)
        h2 = rmsnorm_fwd(x1, p["norm2_g"][l][None], name=f"norm2_fwd_{l}")
        gu, act = swiglu_fwd(h2, wb["w_gate_up"][l], name=f"swiglu_fwd_{l}", tm=512, tn=1408)
        x2 = mm_nn(act, wb["w_down"][l], name=f"down_proj_{l}", tm=512, tn=D_MODEL, out_dtype=F32, residual=x1)
        saved.append((x0, h1, z, mc, x1, h2, gu, act, mp))
        xs.append(x2)
    loss, dx, dxb, d_final_g = loss_head(xs[-1], p["final_g"][None], target.reshape(t, d), name="loss_head")
    grads = {k: [None] * DEPTH for k in
             ["norm1_g", "w_in", "w_out", "norm2_g", "w_gate_up", "w_down"] + MIXER_SMALL}
    for l in reversed(range(DEPTH)):
        x0, h1, z, mc, x1, h2, gu, act, mp = saved[l]
        grads["w_down"][l] = mm_tn(act, dxb, name=f"down_proj_dw_{l}", tk=D_FF // 2, tn=D_MODEL, tt=512, out_dtype=BF16)
        dgu = swiglu_bwd(dxb, wb["w_down"][l], gu, name=f"swiglu_bwd_{l}", tm=512, tn=1408)
        grads["w_gate_up"][l] = mm_gu_tn(h2, dgu, name=f"gate_up_dw_{l}", tn=1408, tt=512)
        dh2 = mm_gu_nt(dgu, wb["w_gate_up"][l], name=f"gate_up_dx_{l}", tm=512)
        dx, dxb, dn2 = rmsnorm_bwd(dh2, x1, p["norm2_g"][l][None], dx, name=f"norm2_bwd_{l}")
        grads["norm2_g"][l] = dn2[0]
        grads["w_out"][l] = mm_tn(mc, dxb, name=f"out_proj_dw_{l}", tk=D_MODEL, tn=D_MODEL, tt=512, out_dtype=BF16)
        dmc = mm_nt(dxb, wb["w_out"][l], name=f"out_proj_dx_{l}", tm=512, tn=D_MODEL, out_dtype=BF16)
        dz, *mg = mixer_bwd(z, dmc, mp, seq=seq, name=f"mixer_bwd_{l}", tm=mix_tm[1])
        for k, v in zip(MIXER_SMALL, _mixer_grads(mg)):
            grads[k][l] = v
        grads["w_in"][l] = mm_tn(h1, dz, name=f"in_proj_dw_{l}", tk=D_MODEL, tn=D_IN, tt=512, out_dtype=BF16)
        dh1 = mm_nt(dz, wb["w_in"][l], name=f"in_proj_dx_{l}", tm=512, tn=D_MODEL, out_dtype=F32)
        dx, dxb, dn1 = rmsnorm_bwd(dh1, x0, p["norm1_g"][l][None], dx, name=f"norm1_bwd_{l}")
        grads["norm1_g"][l] = dn1[0]
    grads = {k: jnp.stack(v) for k, v in grads.items()}
    grads["final_g"] = d_final_g[0]
    return loss, dx.reshape(bsz, seq, d), grads


def _place():
    x, y, c = lax.axis_index("x"), lax.axis_index("y"), lax.axis_index("c")
    chips = [(1 - x, y), (x, 1 - y), (1 - x, 1 - y)]
    return x, y, c, chips


def _remote(src, dst, send_sem, recv_sem, to):
    return pltpu.make_async_remote_copy(src_ref=src, dst_ref=dst, send_sem=send_sem, recv_sem=recv_sem,
                                        device_id=to, device_id_type=MESH)


def all_gather_xy(tensors, *, name):
    n = len(tensors)

    def body(*refs):
        ins, outs = refs[:n], refs[n:2 * n]
        local_sems, send_sems, recv_sems, fsend_sems, frecv_sems = refs[2 * n:]
        x, y, c, chips = _place()
        j = 2 * x + y
        sibling = (x, y, 1 - c)
        local = [pltpu.make_async_copy(ins[i], outs[i].at[j], local_sems.at[i]) for i in range(n)]
        for cp in local:
            cp.start()
        sends = []
        for i in range(n):
            for k, (px, py) in enumerate(chips):
                cp = _remote(ins[i].at[c], outs[i].at[j, c], send_sems.at[3 * i + k], recv_sems.at[3 * i + k],
                             (px, py, c))
                cp.start()
                sends.append(cp)
        for i in range(n):
            for k, (px, py) in enumerate(chips):
                blk = outs[i].at[2 * px + py, c]
                _remote(blk, blk, send_sems.at[3 * i + k], recv_sems.at[3 * i + k], (px, py, c)).wait_recv()
                cp = _remote(blk, blk, fsend_sems.at[3 * i + k], frecv_sems.at[3 * i + k], sibling)
                cp.start()
                sends.append(cp)
        for i in range(n):
            for k, (px, py) in enumerate(chips):
                blk = outs[i].at[2 * px + py, 1 - c]
                _remote(blk, blk, fsend_sems.at[3 * i + k], frecv_sems.at[3 * i + k], sibling).wait_recv()
        for cp in sends:
            cp.wait_send()
        for cp in local:
            cp.wait()

    return pl.pallas_call(
        body, name=name, in_specs=[ANY] * n, out_specs=[ANY] * n,
        out_shape=[jax.ShapeDtypeStruct((4,) + t.shape, t.dtype) for t in tensors],
        scratch_shapes=[pltpu.SemaphoreType.DMA((n,))] + [pltpu.SemaphoreType.DMA((3 * n,))] * 4,
        interpret=_INTERPRET,
    )(*tensors)


def sibling_send(tensors, *, name):
    n = len(tensors)

    def body(*refs):
        ins, outs = refs[:n], refs[n:2 * n]
        send_sems, recv_sems = refs[2 * n:]
        x, y, c, _ = _place()
        cps = [_remote(ins[i].at[1 - c], outs[i], send_sems.at[i], recv_sems.at[i], (x, y, 1 - c)) for i in range(n)]
        for cp in cps:
            cp.start()
        for cp in cps:
            cp.wait_recv()
        for cp in cps:
            cp.wait_send()

    return pl.pallas_call(
        body, name=name, in_specs=[ANY] * n, out_specs=[ANY] * n,
        out_shape=[jax.ShapeDtypeStruct(t.shape[1:], t.dtype) for t in tensors],
        scratch_shapes=[pltpu.SemaphoreType.DMA((n,))] * 2,
        interpret=_INTERPRET,
    )(*tensors)


def scatter_xy(tensors, *, name):
    n = len(tensors)

    def body(*refs):
        ins, outs = refs[:n], refs[n:2 * n]
        local_sems, send_sems, recv_sems = refs[2 * n:]
        x, y, c, chips = _place()
        j = 2 * x + y
        local = [pltpu.make_async_copy(ins[i].at[j], outs[i].at[j], local_sems.at[i]) for i in range(n)]
        for cp in local:
            cp.start()
        sends = []
        for i in range(n):
            for k, (px, py) in enumerate(chips):
                cp = _remote(ins[i].at[2 * px + py], outs[i].at[j], send_sems.at[3 * i + k], recv_sems.at[3 * i + k],
                             (px, py, c))
                cp.start()
                sends.append(cp)
        for i in range(n):
            for k, (px, py) in enumerate(chips):
                blk = outs[i].at[2 * px + py]
                _remote(blk, blk, send_sems.at[3 * i + k], recv_sems.at[3 * i + k], (px, py, c)).wait_recv()
        for cp in sends:
            cp.wait_send()
        for cp in local:
            cp.wait()

    return pl.pallas_call(
        body, name=name, in_specs=[ANY] * n, out_specs=[ANY] * n,
        out_shape=[jax.ShapeDtypeStruct(t.shape, t.dtype) for t in tensors],
        scratch_shapes=[pltpu.SemaphoreType.DMA((n,))] + [pltpu.SemaphoreType.DMA((3 * n,))] * 2,
        interpret=_INTERPRET,
    )(*tensors)


def sibling_exchange(tensors, *, name):
    n = len(tensors)

    def body(*refs):
        ins, outs = refs[:n], refs[n:2 * n]
        local_sems, send_sems, recv_sems = refs[2 * n:]
        x, y, c, _ = _place()
        local = [pltpu.make_async_copy(ins[i], outs[i].at[c], local_sems.at[i]) for i in range(n)]
        for cp in local:
            cp.start()
        cps = [_remote(ins[i], outs[i].at[c], send_sems.at[i], recv_sems.at[i], (x, y, 1 - c)) for i in range(n)]
        for cp in cps:
            cp.start()
        for i in range(n):
            blk = outs[i].at[1 - c]
            _remote(blk, blk, send_sems.at[i], recv_sems.at[i], (x, y, 1 - c)).wait_recv()
        for cp in cps:
            cp.wait_send()
        for cp in local:
            cp.wait()

    return pl.pallas_call(
        body, name=name, in_specs=[ANY] * n, out_specs=[ANY] * n,
        out_shape=[jax.ShapeDtypeStruct((2,) + t.shape, t.dtype) for t in tensors],
        scratch_shapes=[pltpu.SemaphoreType.DMA((n,))] * 3,
        interpret=_INTERPRET,
    )(*tensors)


def _row_tile(r, cap=512):
    best = r
    for d in range(16, min(r, cap) + 1, 16):
        if r % d == 0:
            best = d
    return best if best <= cap else r


def add_core_halves(g, r1, core, *, name):
    _, _, r, c = g.shape
    tr = _row_tile(r)

    def body(core_ref, g_ref, r_ref, o_ref):
        o_ref[...] = (g_ref[...].astype(F32) + r_ref[...].astype(F32)).astype(o_ref.dtype)

    return pl.pallas_call(
        body, name=name,
        grid_spec=pltpu.PrefetchScalarGridSpec(
            num_scalar_prefetch=1, grid=(4, r // tr),
            in_specs=[pl.BlockSpec((None, None, tr, c), lambda j, i, core_ref: (core_ref[0], j, i, 0)),
                      pl.BlockSpec((None, tr, c), lambda j, i, core_ref: (j, i, 0))],
            out_specs=pl.BlockSpec((None, tr, c), lambda j, i, core_ref: (j, i, 0))),
        out_shape=jax.ShapeDtypeStruct((4, r, c), g.dtype),
        compiler_params=_params(("arbitrary", "arbitrary")), interpret=_INTERPRET,
    )(core, g, r1)


def sum_chips(r2, *, name):
    _, r, c = r2.shape
    tr = _row_tile(r, 256)

    def body(r_ref, o_ref):
        acc = r_ref[0].astype(F32) + r_ref[1].astype(F32)
        acc = acc + r_ref[2].astype(F32)
        o_ref[...] = acc + r_ref[3].astype(F32)

    return pl.pallas_call(
        body, name=name, grid=(r // tr,),
        in_specs=[pl.BlockSpec((4, tr, c), lambda i: (0, i, 0))],
        out_specs=pl.BlockSpec((tr, c), lambda i: (i, 0)),
        out_shape=jax.ShapeDtypeStruct((r, c), F32),
        compiler_params=_params(("arbitrary",)), interpret=_INTERPRET,
    )(r2)


def adamw(w, g, m, v, *, name):
    nl, r, c = w.shape
    tr = _row_tile(r, 256)

    def body(w_ref, g_ref, m_ref, v_ref, d_ref, mo_ref, vo_ref):
        gv = g_ref[...]
        m_new = ADAM_B1 * m_ref[...] + (1.0 - ADAM_B1) * gv
        v_new = ADAM_B2 * v_ref[...] + (1.0 - ADAM_B2) * (gv * gv)
        m_hat = m_new / (1.0 - ADAM_B1 ** ADAM_STEP)
        v_hat = v_new / (1.0 - ADAM_B2 ** ADAM_STEP)
        d_ref[...] = -ADAM_LR * (m_hat / (jnp.sqrt(v_hat) + ADAM_EPS) + ADAM_WD * w_ref[...])
        mo_ref[...] = m_new
        vo_ref[...] = v_new

    blk = pl.BlockSpec((None, tr, c), lambda l, i: (l, i, 0))
    return pl.pallas_call(
        body, name=name, grid=(nl, r // tr), in_specs=[blk] * 4, out_specs=[blk] * 3,
        out_shape=[jax.ShapeDtypeStruct(w.shape, F32)] * 3,
        compiler_params=_params(("arbitrary", "arbitrary")), interpret=_INTERPRET,
    )(w, g, m, v)


BIG = ["w_in", "w_out", "w_gate_up", "w_down"]
COL_SHARDED = {"w_in": True, "w_out": False, "w_gate_up": True, "w_down": False}
WEIGHTS = ["norm1_g", "w_in", "conv_w", "conv_b", "conv_ln_g", "conv_ln_b", "w_pw", "sg_ln_g", "sg_ln_b", "w_s",
           "b_s", "w_pool", "pool_scale", "w_out", "norm2_g", "w_gate_up", "w_down", "final_g"]
SMALL = [k for k in WEIGHTS if k not in BIG]
LANES = 128
PACK_ROWS = 8 * 16


def _pack(arrays):
    flat = jnp.concatenate([a.reshape(-1) for a in arrays])
    rows = -(-flat.shape[0] // (LANES * PACK_ROWS)) * PACK_ROWS
    return jnp.pad(flat, (0, rows * LANES - flat.shape[0])).reshape(rows, LANES)


def _unpack(packed, shapes):
    flat = packed.reshape(-1)
    out, off = [], 0
    for s in shapes:
        size = math.prod(s)
        out.append(flat[off:off + size].reshape(s))
        off += size
    return out


def _to_full(g, col_sharded):
    _, nl, r, c = g.shape
    if col_sharded:
        return g.transpose(1, 2, 0, 3).reshape(nl, r, 4 * c)
    return g.transpose(1, 0, 2, 3).reshape(nl, 4 * r, c)


def _to_shards(g, col_sharded):
    nl, r, c = g.shape
    if col_sharded:
        return g.reshape(nl, r, 4, c // 4).transpose(0, 2, 1, 3)
    return g.reshape(nl, 4, r // 4, c)


def kernel(x, norm1_g, w_in, conv_w, conv_b, conv_ln_g, conv_ln_b, w_pw, sg_ln_g, sg_ln_b, w_s, b_s, w_pool, pool_scale, w_out, norm2_g, w_gate_up, w_down, final_g, loss_target, m_norm1_g, m_w_in, m_conv_w, m_conv_b, m_conv_ln_g, m_conv_ln_b, m_w_pw, m_sg_ln_g, m_sg_ln_b, m_w_s, m_b_s, m_w_pool, m_pool_scale, m_w_out, m_norm2_g, m_w_gate_up, m_w_down, m_final_g, v_norm1_g, v_w_in, v_conv_w, v_conv_b, v_conv_ln_g, v_conv_ln_b, v_w_pw, v_sg_ln_g, v_sg_ln_b, v_w_s, v_b_s, v_w_pool, v_pool_scale, v_w_out, v_norm2_g, v_w_gate_up, v_w_down, v_final_g):
    w = dict(norm1_g=norm1_g, w_in=w_in, conv_w=conv_w, conv_b=conv_b, conv_ln_g=conv_ln_g, conv_ln_b=conv_ln_b,
             w_pw=w_pw, sg_ln_g=sg_ln_g, sg_ln_b=sg_ln_b, w_s=w_s, b_s=b_s, w_pool=w_pool, pool_scale=pool_scale,
             w_out=w_out, norm2_g=norm2_g, w_gate_up=w_gate_up, w_down=w_down, final_g=final_g)
    m = dict(norm1_g=m_norm1_g, w_in=m_w_in, conv_w=m_conv_w, conv_b=m_conv_b, conv_ln_g=m_conv_ln_g,
             conv_ln_b=m_conv_ln_b, w_pw=m_w_pw, sg_ln_g=m_sg_ln_g, sg_ln_b=m_sg_ln_b, w_s=m_w_s, b_s=m_b_s,
             w_pool=m_w_pool, pool_scale=m_pool_scale, w_out=m_w_out, norm2_g=m_norm2_g, w_gate_up=m_w_gate_up,
             w_down=m_w_down, final_g=m_final_g)
    v = dict(norm1_g=v_norm1_g, w_in=v_w_in, conv_w=v_conv_w, conv_b=v_conv_b, conv_ln_g=v_conv_ln_g,
             conv_ln_b=v_conv_ln_b, w_pw=v_w_pw, sg_ln_g=v_sg_ln_g, sg_ln_b=v_sg_ln_b, w_s=v_w_s, b_s=v_b_s,
             w_pool=v_w_pool, pool_scale=v_pool_scale, w_out=v_w_out, norm2_g=v_norm2_g, w_gate_up=v_w_gate_up,
             w_down=v_w_down, final_g=v_final_g)
    chip = 2 * lax.axis_index("x") + lax.axis_index("y")
    core = lax.axis_index("c")
    core_arr = jnp.reshape(core, (1,)).astype(jnp.int32)

    pw_rows, cw_cols = w_pw.shape[1], conv_w.shape[2]
    side = jnp.zeros((DEPTH, 128, D_A), F32)
    side = side.at[:, :pw_rows, :].set(w_pw).at[:, pw_rows:pw_rows + CONV_WIDTH, :cw_cols].set(conv_w)
    gathered = all_gather_xy([w[k].astype(BF16) for k in BIG] + [side], name="gather_weights")
    wb = {k: _to_full(g, COL_SHARDED[k]) for k, g in zip(BIG, gathered)}
    side_all = gathered[-1]
    p = dict(w)
    p["w_pw"] = side_all[:, :, :pw_rows, :].transpose(1, 0, 2, 3).reshape(DEPTH, 4 * pw_rows, D_A)
    p["conv_w"] = side_all[:, :, pw_rows:pw_rows + CONV_WIDTH, :cw_cols].transpose(1, 2, 0, 3).reshape(
        DEPTH, CONV_WIDTH, 4 * cw_cols)

    loss, grad_x, g_local = local_step(x, loss_target, p, wb)
    loss = lax.psum(loss[0, 0], ("x", "y", "c"))

    small_shapes = [g_local[k].shape for k in SMALL]
    g_small = _pack([g_local[k] for k in SMALL])
    parts = [_to_shards(g_local[k], COL_SHARDED[k]) for k in BIG]
    parts.append(g_small.reshape(2, 4, g_small.shape[0] // 8, LANES))
    names = BIG + ["small"]
    from_sibling = sibling_send(parts, name="reduce_cores")
    halves = [add_core_halves(g, r1, core_arr, name=f"add_cores_{k}") for k, g, r1 in zip(names, parts, from_sibling)]
    from_chips = scatter_xy(halves, name="reduce_chips")
    reduced = [sum_chips(r2, name=f"sum_chips_{k}") for k, r2 in zip(names, from_chips)]
    both = sibling_exchange(reduced, name="exchange_cores")
    grad = dict(zip(BIG, both[:-1]))
    small_all = all_gather_xy([both[-1]], name="gather_small")[0]
    small_all = small_all.transpose(1, 0, 2, 3).reshape(g_small.shape)
    grad.update(zip(SMALL, _unpack(small_all, small_shapes)))
    grad["w_pw"] = lax.dynamic_slice_in_dim(grad["w_pw"], chip * pw_rows, pw_rows, axis=1)
    grad["conv_w"] = lax.dynamic_slice_in_dim(grad["conv_w"], chip * cw_cols, cw_cols, axis=2)

    delta, new_m, new_v = {}, {}, {}
    for k in BIG:
        delta[k], new_m[k], new_v[k] = adamw(w[k], grad[k], m[k], v[k], name=f"adamw_{k}")
    shapes = [w[k].shape for k in SMALL]
    packed = [_pack([t[k] for k in SMALL])[None] for t in (w, grad, m, v)]
    for out, arr in zip((delta, new_m, new_v), adamw(*packed, name="adamw_small")):
        out.update(zip(SMALL, _unpack(arr[0], shapes)))
    return (loss, grad_x, *[grad[k] for k in WEIGHTS], *[delta[k] for k in WEIGHTS],
            *[new_m[k] for k in WEIGHTS], *[new_v[k] for k in WEIGHTS])
```

```python
import functools
import math

import jax
import jax.numpy as jnp
from jax import lax
from jax.experimental import pallas as pl
from jax.experimental.pallas import tpu as pltpu

F32 = jnp.float32
BF16 = jnp.bfloat16

D_MODEL = 1024
DEPTH = 2
D_A = 384
D_B = 384
D_C = 256
D_IN = 2 * D_A + 2 * D_B + D_C
N_HEADS_B = 4
HEAD_DIM_B = 96
GROUP_DIM_C = 64
CONV_WIDTH = 31
CHUNK = 128
D_FF = 2816
RMS_EPS = 1e-6
LN_EPS = 1e-5
HALO = 32
N_CHIPS = 4
N_DEV = 8
LANES = 128

ADAM_LR = 0.001
ADAM_B1 = 0.9
ADAM_B2 = 0.999
ADAM_EPS = 1e-08
ADAM_WD = 0.01
ADAM_STEP = 10

VMEM_LIMIT = 56 * 1024 * 1024
_INTERPRET = False

MESH = pl.DeviceIdType.MESH
ANY = pl.BlockSpec(memory_space=pl.ANY)
SDS = jax.ShapeDtypeStruct


def _sigmoid(x):
    return 1.0 / (1.0 + jnp.exp(-x))


_GELU_C = math.sqrt(2.0 / math.pi)


def _gelu(x):
    return 0.5 * x * (1.0 + jnp.tanh(_GELU_C * (x + 0.044715 * x * x * x)))


def _gelu_grad(x):
    t = jnp.tanh(_GELU_C * (x + 0.044715 * x * x * x))
    return 0.5 * (1.0 + t) + 0.5 * x * (1.0 - t * t) * _GELU_C * (1.0 + 3 * 0.044715 * x * x)


def _dot(a, b):
    return jnp.dot(a, b, preferred_element_type=F32)


def _dot_nt(a, b):
    return lax.dot_general(a, b, (((1,), (1,)), ((), ())), preferred_element_type=F32)


def _dot_tn(a, b):
    return lax.dot_general(a, b, (((0,), (0,)), ((), ())), preferred_element_type=F32)


class Piece:
    def __init__(self, operands, out_shapes, aliases, n_sems, start, finish):
        self.operands, self.out_shapes, self.aliases, self.n_sems = operands, out_shapes, aliases, n_sems
        self.start, self.finish = start, finish


def _place():
    x, y, c = lax.axis_index("x"), lax.axis_index("y"), lax.axis_index("c")
    chips = [(1 - x, y), (x, 1 - y), (1 - x, 1 - y)]
    return x, y, c, chips


def _remote(src, dst, send_sem, recv_sem, to):
    return pltpu.make_async_remote_copy(src_ref=src, dst_ref=dst, send_sem=send_sem, recv_sem=recv_sem,
                                        device_id=to, device_id_type=MESH)


def gather_ici(src):
    def copies(ins, outs, sem):
        x, y, c, chips = _place()
        j = 2 * x + y
        return [(_remote(ins[0].at[c], outs[0].at[j, c], sem(k), sem(3 + k), (px, py, c)),
                 outs[0].at[2 * px + py, c], (px, py, c)) for k, (px, py) in enumerate(chips)]

    def start(ins, outs, sem):
        for cp, _, _ in copies(ins, outs, sem):
            cp.start()

    def finish(ins, outs, sem):
        for k, (cp, landed, frm) in enumerate(copies(ins, outs, sem)):
            _remote(landed, landed, sem(k), sem(3 + k), frm).wait_recv()
        for cp, _, _ in copies(ins, outs, sem):
            cp.wait_send()

    return Piece([src], [SDS((N_CHIPS,) + src.shape, src.dtype)], {}, 6, start, finish)


def gather_d2d(g):
    def copies(outs, sem):
        x, y, c, chips = _place()
        return [(_remote(outs[0].at[2 * px + py, c], outs[0].at[2 * px + py, c], sem(k), sem(3 + k), (x, y, 1 - c)),
                 outs[0].at[2 * px + py, 1 - c], (x, y, 1 - c)) for k, (px, py) in enumerate(chips)]

    def start(ins, outs, sem):
        for cp, _, _ in copies(outs, sem):
            cp.start()

    def finish(ins, outs, sem):
        for k, (cp, landed, frm) in enumerate(copies(outs, sem)):
            _remote(landed, landed, sem(k), sem(3 + k), frm).wait_recv()
        for cp, _, _ in copies(outs, sem):
            cp.wait_send()

    return Piece([g], [SDS(g.shape, g.dtype)], {0: 0}, 6, start, finish)


def gather_both(src):
    a = gather_ici(src)

    def finish(ins, outs, sem):
        a.finish(ins, outs, lambda k: sem(k))
        x, y, c, chips = _place()
        cps = [_remote(outs[0].at[2 * px + py, c], outs[0].at[2 * px + py, c], sem(6 + k), sem(9 + k), (x, y, 1 - c))
               for k, (px, py) in enumerate(chips)]
        for cp in cps:
            cp.start()
        for k, (px, py) in enumerate(chips):
            blk = outs[0].at[2 * px + py, 1 - c]
            _remote(blk, blk, sem(6 + k), sem(9 + k), (x, y, 1 - c)).wait_recv()
        for cp in cps:
            cp.wait_send()

    return Piece(a.operands, a.out_shapes, {}, 12, a.start, finish)


def reduce_d2d(g):
    def copies(ins, outs, sem):
        x, y, c, _ = _place()
        return [_remote(ins[0].at[j, 1 - c], outs[0].at[j], sem(j), sem(4 + j), (x, y, 1 - c)) for j in range(N_CHIPS)]

    def start(ins, outs, sem):
        for cp in copies(ins, outs, sem):
            cp.start()

    def finish(ins, outs, sem):
        for cp in copies(ins, outs, sem):
            cp.wait_recv()
        for cp in copies(ins, outs, sem):
            cp.wait_send()

    return Piece([g], [SDS((N_CHIPS,) + g.shape[2:], g.dtype)], {}, 8, start, finish)


def reduce_ici(h):
    def copies(ins, outs, sem):
        x, y, c, chips = _place()
        j = 2 * x + y
        return [(_remote(ins[0].at[2 * px + py], outs[0].at[j], sem(k), sem(3 + k), (px, py, c)),
                 outs[0].at[2 * px + py], (px, py, c)) for k, (px, py) in enumerate(chips)]

    def start(ins, outs, sem):
        for cp, _, _ in copies(ins, outs, sem):
            cp.start()

    def finish(ins, outs, sem):
        for k, (cp, landed, frm) in enumerate(copies(ins, outs, sem)):
            _remote(landed, landed, sem(k), sem(3 + k), frm).wait_recv()
        for cp, _, _ in copies(ins, outs, sem):
            cp.wait_send()

    return Piece([h], [SDS(h.shape, h.dtype)], {}, 6, start, finish)


def exchange_d2d(g):
    def copy(outs, sem):
        x, y, c, _ = _place()
        return _remote(outs[0].at[c], outs[0].at[c], sem(0), sem(1), (x, y, 1 - c)), outs[0].at[1 - c], (x, y, 1 - c)

    def start(ins, outs, sem):
        copy(outs, sem)[0].start()

    def finish(ins, outs, sem):
        cp, landed, frm = copy(outs, sem)
        _remote(landed, landed, sem(0), sem(1), frm).wait_recv()
        cp.wait_send()

    return Piece([g], [SDS(g.shape, g.dtype)], {0: 0}, 2, start, finish)


def _call(body, *, name, grid, in_specs, out_specs, out_shape, args, scratch_shapes=(), pieces=(), prefetch=0):
    n_in, n_out, n_scr = len(in_specs), len(out_specs), len(scratch_shapes)
    c_ops = [a for p in pieces for a in p.operands]
    c_outs = [s for p in pieces for s in p.out_shapes]
    n_sems = sum(p.n_sems for p in pieces)
    aliases = {}
    op_off, out_off = prefetch + n_in, n_out
    for p in pieces:
        for i, o in p.aliases.items():
            aliases[op_off + i] = out_off + o
        op_off += len(p.operands)
        out_off += len(p.out_shapes)

    def wrapped(*refs):
        pre, refs = refs[:prefetch], refs[prefetch:]
        ins, cin = refs[:n_in], refs[n_in:n_in + len(c_ops)]
        o0 = n_in + len(c_ops)
        outs, cout = refs[o0:o0 + n_out], refs[o0 + n_out:o0 + n_out + len(c_outs)]
        s0 = o0 + n_out + len(c_outs)
        scr = refs[s0:s0 + n_scr]

        def each(method):
            sems = refs[s0 + n_scr]
            i_off = o_off = s_off = 0
            for p in pieces:
                getattr(p, method)(cin[i_off:i_off + len(p.operands)], cout[o_off:o_off + len(p.out_shapes)],
                                   functools.partial(lambda k, base: sems.at[base + k], base=s_off))
                i_off, o_off, s_off = i_off + len(p.operands), o_off + len(p.out_shapes), s_off + p.n_sems

        if pieces and grid:
            ids = [pl.program_id(a) for a in range(len(grid))]
            first = functools.reduce(jnp.logical_and, [i == 0 for i in ids])
            last = functools.reduce(jnp.logical_and, [i == g - 1 for i, g in zip(ids, grid)])
            pl.when(first)(lambda: each("start"))
        elif pieces:
            each("start")
        if body is not None:
            body(*pre, *ins, *outs, *scr)
        if pieces and grid:
            pl.when(last)(lambda: each("finish"))
        elif pieces:
            each("finish")

    scratch = list(scratch_shapes) + ([pltpu.SemaphoreType.DMA((n_sems,))] if pieces else [])
    all_in = list(in_specs) + [ANY] * len(c_ops)
    all_out = list(out_specs) + [ANY] * len(c_outs)
    shapes = list(out_shape) + c_outs
    kw = dict(name=name, out_shape=shapes, input_output_aliases=aliases, interpret=_INTERPRET)
    if grid:
        kw["compiler_params"] = pltpu.CompilerParams(dimension_semantics=("arbitrary",) * len(grid),
                                                     vmem_limit_bytes=VMEM_LIMIT)
    if prefetch:
        kw["grid_spec"] = pltpu.PrefetchScalarGridSpec(num_scalar_prefetch=prefetch, grid=grid, in_specs=all_in,
                                                       out_specs=all_out, scratch_shapes=scratch)
    else:
        kw.update(in_specs=all_in, out_specs=all_out, scratch_shapes=scratch)
        if grid:
            kw["grid"] = grid
    res = pl.pallas_call(wrapped, **kw)(*args, *c_ops)
    outs, rest = list(res[:n_out]), list(res[n_out:])
    couts = []
    for p in pieces:
        couts.append(rest[:len(p.out_shapes)])
        rest = rest[len(p.out_shapes):]
    return outs, couts


def comm_only(pieces, *, name):
    return _call(None, name=name, grid=(), in_specs=[], out_specs=[], out_shape=[], args=[], pieces=pieces)[1]


def rmsnorm_fwd(x, g, *, name, pieces=()):
    t, d = x.shape
    tm = min(512, t)

    def body(x_ref, g_ref, o_ref):
        xv = x_ref[...]
        rstd = lax.rsqrt(jnp.mean(xv * xv, axis=-1, keepdims=True) + RMS_EPS)
        o_ref[...] = (xv * rstd * g_ref[...]).astype(BF16)

    outs, couts = _call(
        body, name=name, grid=(t // tm,),
        in_specs=[pl.BlockSpec((tm, d), lambda i: (i, 0)), pl.BlockSpec((1, d), lambda i: (0, 0))],
        out_specs=[pl.BlockSpec((tm, d), lambda i: (i, 0))], out_shape=[SDS((t, d), BF16)],
        args=[x, g], pieces=pieces)
    return outs[0], couts


def rmsnorm_bwd(dh, x, g, dres, *, name, pieces=()):
    t, d = x.shape
    tm = min(512, t)

    def body(dh_ref, x_ref, g_ref, dres_ref, dx_ref, dxb_ref, dg_ref):
        i = pl.program_id(0)
        xv = x_ref[...]
        rstd = lax.rsqrt(jnp.mean(xv * xv, axis=-1, keepdims=True) + RMS_EPS)
        xhat = xv * rstd
        dhv = dh_ref[...]
        dxhat = dhv * g_ref[...]
        dx = dres_ref[...] + rstd * (dxhat - xhat * jnp.mean(dxhat * xhat, axis=-1, keepdims=True))
        dx_ref[...] = dx
        dxb_ref[...] = dx.astype(BF16)

        @pl.when(i == 0)
        def _():
            dg_ref[...] = jnp.zeros_like(dg_ref)

        dg_ref[...] += jnp.sum(dhv * xhat, axis=0, keepdims=True)

    row = pl.BlockSpec((tm, d), lambda i: (i, 0))
    vec = pl.BlockSpec((1, d), lambda i: (0, 0))
    outs, couts = _call(
        body, name=name, grid=(t // tm,), in_specs=[row, row, vec, row], out_specs=[row, row, vec],
        out_shape=[SDS((t, d), F32), SDS((t, d), BF16), SDS((1, d), F32)],
        args=[dh, x, g, dres], pieces=pieces)
    return outs, couts


def loss_head(x, g, target, *, name):
    t, d = x.shape
    tm = min(512, t)

    def body(x_ref, g_ref, t_ref, loss_ref, dx_ref, dxb_ref, dg_ref):
        i = pl.program_id(0)
        xv = x_ref[...]
        gv = g_ref[...]
        rstd = lax.rsqrt(jnp.mean(xv * xv, axis=-1, keepdims=True) + RMS_EPS)
        xhat = xv * rstd
        err = xhat * gv - t_ref[...]
        dy = err * (1.0 / d)
        dxhat = dy * gv
        dx = rstd * (dxhat - xhat * jnp.mean(dxhat * xhat, axis=-1, keepdims=True))
        dx_ref[...] = dx
        dxb_ref[...] = dx.astype(BF16)

        @pl.when(i == 0)
        def _():
            dg_ref[...] = jnp.zeros_like(dg_ref)
            loss_ref[...] = jnp.zeros_like(loss_ref)

        dg_ref[...] += jnp.sum(dy * xhat, axis=0, keepdims=True)
        per_tok = jnp.sum(err * err, axis=-1, keepdims=True) * (0.5 / d)
        loss_ref[...] += jnp.sum(per_tok, axis=0, keepdims=True)

    row = pl.BlockSpec((tm, d), lambda i: (i, 0))
    vec = pl.BlockSpec((1, d), lambda i: (0, 0))
    one = pl.BlockSpec((1, 1), lambda i: (0, 0))
    outs, _ = _call(
        body, name=name, grid=(t // tm,), in_specs=[row, vec, row], out_specs=[one, row, row, vec],
        out_shape=[SDS((1, 1), F32), SDS((t, d), F32), SDS((t, d), BF16), SDS((1, d), F32)],
        args=[x, g, target])
    return outs


def mm_nn(a, b, *, name, tm, tn, out_dtype, residual=None, pieces=()):
    m, k = a.shape
    n = b.shape[1]
    tm, tn = min(tm, m), min(tn, n)
    has_res = residual is not None

    def body(a_ref, b_ref, *rest):
        o_ref = rest[-1]
        acc = _dot(a_ref[...], b_ref[...])
        if has_res:
            acc = acc + rest[0][...]
        o_ref[...] = acc.astype(o_ref.dtype)

    in_specs = [pl.BlockSpec((tm, k), lambda j, i: (i, 0)), pl.BlockSpec((k, tn), lambda j, i: (0, j))]
    args = [a, b]
    if has_res:
        in_specs.append(pl.BlockSpec((tm, tn), lambda j, i: (i, j)))
        args.append(residual)
    outs, couts = _call(
        body, name=name, grid=(n // tn, m // tm), in_specs=in_specs,
        out_specs=[pl.BlockSpec((tm, tn), lambda j, i: (i, j))], out_shape=[SDS((m, n), out_dtype)],
        args=args, pieces=pieces)
    return outs[0], couts


def mm_nt(a, b, *, name, tm, tn, out_dtype, pieces=()):
    m, k = a.shape
    n = b.shape[0]
    tm, tn = min(tm, m), min(tn, n)

    def body(a_ref, b_ref, o_ref):
        o_ref[...] = _dot_nt(a_ref[...], b_ref[...]).astype(o_ref.dtype)

    outs, couts = _call(
        body, name=name, grid=(n // tn, m // tm),
        in_specs=[pl.BlockSpec((tm, k), lambda j, i: (i, 0)), pl.BlockSpec((tn, k), lambda j, i: (j, 0))],
        out_specs=[pl.BlockSpec((tm, tn), lambda j, i: (i, j))], out_shape=[SDS((m, n), out_dtype)],
        args=[a, b], pieces=pieces)
    return outs[0], couts


def mm_tn(a, b, *, name, tk, tn, tt, out_dtype, pieces=()):
    t, k = a.shape
    n = b.shape[1]
    tk, tn, tt = min(tk, k), min(tn, n), min(tt, t)
    nt = t // tt

    def body(a_ref, b_ref, o_ref, acc_ref):
        s = pl.program_id(2)

        @pl.when(s == 0)
        def _():
            acc_ref[...] = jnp.zeros_like(acc_ref)

        acc_ref[...] += _dot_tn(a_ref[...], b_ref[...])

        @pl.when(s == nt - 1)
        def _():
            o_ref[...] = acc_ref[...].astype(o_ref.dtype)

    outs, couts = _call(
        body, name=name, grid=(k // tk, n // tn, nt),
        in_specs=[pl.BlockSpec((tt, tk), lambda i, j, s: (s, i)), pl.BlockSpec((tt, tn), lambda i, j, s: (s, j))],
        out_specs=[pl.BlockSpec((tk, tn), lambda i, j, s: (i, j))], out_shape=[SDS((k, n), out_dtype)],
        scratch_shapes=[pltpu.VMEM((tk, tn), F32)], args=[a, b], pieces=pieces)
    return outs[0], couts


def swiglu_fwd(h, wt, *, name, tm, tn, pieces=()):
    t, d = h.shape
    ff = wt.shape[0] // 2
    tm, tn = min(tm, t), min(tn, ff)
    nb = ff // tn

    def body(h_ref, wg_ref, wu_ref, gu_ref, act_ref):
        hv = h_ref[...]
        gate = _dot_nt(hv, wg_ref[...])
        up = _dot_nt(hv, wu_ref[...])
        gu_ref[0] = gate.astype(BF16)
        gu_ref[1] = up.astype(BF16)
        act_ref[...] = (gate * _sigmoid(gate) * up).astype(BF16)

    outs, couts = _call(
        body, name=name, grid=(nb, t // tm),
        in_specs=[pl.BlockSpec((tm, d), lambda j, i: (i, 0)),
                  pl.BlockSpec((tn, d), lambda j, i: (j, 0)),
                  pl.BlockSpec((tn, d), lambda j, i: (j + nb, 0))],
        out_specs=[pl.BlockSpec((2, tm, tn), lambda j, i: (0, i, j)), pl.BlockSpec((tm, tn), lambda j, i: (i, j))],
        out_shape=[SDS((2, t, ff), BF16), SDS((t, ff), BF16)], args=[h, wt, wt], pieces=pieces)
    return outs, couts


def swiglu_bwd(dx, w_down, gu, *, name, tm, tn, pieces=()):
    t, d = dx.shape
    ff = w_down.shape[0]
    tm, tn = min(tm, t), min(tn, ff)

    def body(dx_ref, w_ref, gu_ref, o_ref):
        dact = _dot_nt(dx_ref[...], w_ref[...])
        gate = gu_ref[0].astype(F32)
        up = gu_ref[1].astype(F32)
        sg = _sigmoid(gate)
        o_ref[0] = (dact * up * sg * (1.0 + gate * (1.0 - sg))).astype(BF16)
        o_ref[1] = (dact * gate * sg).astype(BF16)

    outs, couts = _call(
        body, name=name, grid=(ff // tn, t // tm),
        in_specs=[pl.BlockSpec((tm, d), lambda j, i: (i, 0)), pl.BlockSpec((tn, d), lambda j, i: (j, 0)),
                  pl.BlockSpec((2, tm, tn), lambda j, i: (0, i, j))],
        out_specs=[pl.BlockSpec((2, tm, tn), lambda j, i: (0, i, j))], out_shape=[SDS((2, t, ff), BF16)],
        args=[dx, w_down, gu], pieces=pieces)
    return outs[0], couts


def mm_gu_nn(dgu, wt, *, name, tm, pieces=()):
    _, t, ff = dgu.shape
    d = wt.shape[1]
    tm = min(tm, t)

    def body(a_ref, w_ref, o_ref):
        acc = _dot(a_ref[0], w_ref[:ff, :])
        o_ref[...] = acc + _dot(a_ref[1], w_ref[ff:, :])

    outs, couts = _call(
        body, name=name, grid=(t // tm,),
        in_specs=[pl.BlockSpec((2, tm, ff), lambda i: (0, i, 0)), pl.BlockSpec((2 * ff, d), lambda i: (0, 0))],
        out_specs=[pl.BlockSpec((tm, d), lambda i: (i, 0))], out_shape=[SDS((t, d), F32)],
        args=[dgu, wt], pieces=pieces)
    return outs[0], couts


def mm_gu_tn(dgu, h, *, name, tn, tt, pieces=()):
    t, d = h.shape
    ff = dgu.shape[2]
    tn, tt = min(tn, ff), min(tt, t)
    nb = ff // tn
    nt = t // tt

    def body(a_ref, h_ref, o_ref, acc_ref):
        s = pl.program_id(1)

        @pl.when(s == 0)
        def _():
            acc_ref[...] = jnp.zeros_like(acc_ref)

        acc_ref[...] += _dot_tn(a_ref[...], h_ref[...])

        @pl.when(s == nt - 1)
        def _():
            o_ref[...] = acc_ref[...].astype(o_ref.dtype)

    outs, couts = _call(
        body, name=name, grid=(2 * nb, nt),
        in_specs=[pl.BlockSpec((None, tt, tn), lambda j, s: (j // nb, s, j % nb)),
                  pl.BlockSpec((tt, d), lambda j, s: (s, 0))],
        out_specs=[pl.BlockSpec((tn, d), lambda j, s: (j, 0))], out_shape=[SDS((2 * ff, d), BF16)],
        scratch_shapes=[pltpu.VMEM((tn, d), F32)], args=[dgu, h], pieces=pieces)
    return outs[0], couts


def _head_masks(shape):
    lane = lax.broadcasted_iota(jnp.int32, shape, 1)
    return [(lane >= h * HEAD_DIM_B) & (lane < (h + 1) * HEAD_DIM_B) for h in range(N_HEADS_B)]


def _causal_rows(ws):
    r = lax.broadcasted_iota(jnp.int32, ws.shape, 0) % CHUNK
    c = lax.broadcasted_iota(jnp.int32, ws.shape, 1)
    return jnp.where(c <= r, ws, jnp.zeros_like(ws))


def _pool_window(shape):
    lane = lax.broadcasted_iota(jnp.int32, shape, 1)
    return jnp.left_shift(2, lane // GROUP_DIM_C)


def _layer_norm_fwd(x, g, b):
    mu = jnp.mean(x, axis=-1, keepdims=True)
    xc = x - mu
    rstd = lax.rsqrt(jnp.mean(xc * xc, axis=-1, keepdims=True) + LN_EPS)
    xhat = xc * rstd
    return xhat * g + b, xhat, rstd


def _layer_norm_bwd(dy, xhat, rstd, g):
    dxhat = dy * g
    return rstd * (dxhat - jnp.mean(dxhat, axis=-1, keepdims=True)
                   - xhat * jnp.mean(dxhat * xhat, axis=-1, keepdims=True))


def _gate_mix(ws_masked, vl_chunk, masks):
    out = _dot(ws_masked, vl_chunk.astype(BF16))
    s = jnp.zeros((CHUNK, D_B), F32)
    for h in range(N_HEADS_B):
        s = s + jnp.where(masks[h], out[h * CHUNK:(h + 1) * CHUNK], 0.0)
    return s


def _mixer_specs(tm, nt, seq):
    hb = tm // HALO

    def cur(c):
        return pl.BlockSpec((tm, c), lambda b, i: (b * nt + i, 0))

    def prev(c):
        return pl.BlockSpec((HALO, c), lambda b, i: (jnp.maximum((b * nt + i) * hb - 1, 0), 0))

    def nxt(c):
        last = (2 * seq) // HALO - 1
        return pl.BlockSpec((HALO, c), lambda b, i: (jnp.minimum((b * nt + i + 1) * hb, last), 0))

    def full(shape):
        return pl.BlockSpec(shape, lambda b, i: tuple(0 for _ in shape))

    return cur, prev, nxt, full


_MIX_PARAM_SHAPES = [(32, D_A), (1, D_A), (1, D_A), (1, D_A), (D_A, D_A), (1, D_B), (1, D_B),
                     (N_HEADS_B * CHUNK, CHUNK), (CHUNK, D_B), (D_C, D_C), (1, D_C)]


def mixer_fwd(z, mp, *, seq, name, tm=512, pieces=()):
    t = z.shape[0]
    tm = min(tm, seq)
    nt = seq // tm
    cur, prev, _, full = _mixer_specs(tm, nt, seq)

    def body(zc_ref, zp_ref, cw_ref, cb_ref, clg_ref, clb_ref, wpw_ref, slg_ref, slb_ref, ws_ref, bias_ref,
             wp_ref, ps_ref, o_ref, ys_ref, zs_ref):
        i = pl.program_id(1)
        has_prev = i > 0
        yp = zp_ref[:, 0:D_A].astype(F32) * _sigmoid(zp_ref[:, D_A:2 * D_A].astype(F32))
        ys_ref[0:HALO, :] = jnp.where(has_prev, yp, 0.0)
        ys_ref[HALO:HALO + tm, :] = zc_ref[:, 0:D_A].astype(F32) * _sigmoid(zc_ref[:, D_A:2 * D_A].astype(F32))
        acc = jnp.zeros((tm, D_A), F32) + cb_ref[...]
        for k in range(CONV_WIDTH):
            acc = acc + cw_ref[k:k + 1, :] * ys_ref[pl.ds(HALO - (CONV_WIDTH - 1) + k, tm), :]
        ln, _, _ = _layer_norm_fwd(acc, clg_ref[...], clb_ref[...])
        sl = ln * _sigmoid(ln)
        o_ref[:, 0:D_A] = _dot(sl.astype(BF16), wpw_ref[...]).astype(BF16)
        gz = _gelu(zc_ref[:, 2 * D_A:2 * D_A + 2 * D_B].astype(F32))
        u = gz[:, :D_B]
        vl, _, _ = _layer_norm_fwd(gz[:, D_B:], slg_ref[...], slb_ref[...])
        wsm = _causal_rows(ws_ref[...])
        masks = _head_masks((CHUNK, D_B))
        for c in range(tm // CHUNK):
            rows = slice(c * CHUNK, (c + 1) * CHUNK)
            s = _gate_mix(wsm, vl[rows], masks) + bias_ref[...]
            o_ref[rows, D_A:D_A + D_B] = (u[rows] * s).astype(BF16)
        c0 = 2 * D_A + 2 * D_B
        zs_ref[0:HALO, :] = jnp.where(has_prev, zp_ref[:, c0:c0 + D_C].astype(F32), 0.0)
        zcur = zc_ref[:, c0:c0 + D_C].astype(F32)
        zs_ref[HALO:HALO + tm, :] = zcur
        win = _pool_window((tm, D_C))
        wsum = jnp.zeros((tm, D_C), F32)
        for j in range(16):
            wsum = wsum + jnp.where(j < win, zs_ref[pl.ds(HALO - j, tm), :], 0.0)
        pos = i * tm + lax.broadcasted_iota(jnp.int32, (tm, D_C), 0)
        cnt = jnp.minimum(pos + 1, win).astype(F32)
        p = wsum / cnt - zcur
        y = _dot(p.astype(BF16), wp_ref[...])
        o_ref[:, D_A + D_B:D_A + D_B + D_C] = (y * ps_ref[...]).astype(BF16)

    in_specs = [cur(D_IN), prev(D_IN)] + [full(s) for s in _MIX_PARAM_SHAPES]
    outs, couts = _call(
        body, name=name, grid=(2, nt), in_specs=in_specs, out_specs=[cur(D_MODEL)],
        out_shape=[SDS((t, D_MODEL), BF16)],
        scratch_shapes=[pltpu.VMEM((HALO + tm, D_A), F32), pltpu.VMEM((HALO + tm, D_C), F32)],
        args=[z, z, *mp], pieces=pieces)
    return outs[0], couts


def mixer_bwd(z, dm, mp, *, seq, name, tm=256, pieces=()):
    t = z.shape[0]
    tm = min(tm, seq)
    nt = seq // tm
    ext = tm + HALO
    cur, prev, nxt, full = _mixer_specs(tm, nt, seq)
    grad_shapes = [(32, D_A), (1, D_A), (1, D_A), (1, D_A), (D_A, D_A), (1, D_B), (1, D_B),
                   (N_HEADS_B * CHUNK, CHUNK), (CHUNK, CHUNK), (D_C, D_C), (1, D_C)]

    def body(zc_ref, zp_ref, zn_ref, dmc_ref, dmn_ref, cw_ref, cb_ref, clg_ref, clb_ref, wpw_ref, slg_ref, slb_ref,
             ws_ref, bias_ref, wp_ref, ps_ref,
             dz_ref, dcw_ref, dcb_ref, dclg_ref, dclb_ref, dwpw_ref, dslg_ref, dslb_ref, dws_ref, dbs_ref, dwp_ref,
             dps_ref, ys_ref, dcs_ref, zs_ref, qs_ref):
        b = pl.program_id(0)
        i = pl.program_id(1)
        has_prev = i > 0
        has_next = i < nt - 1
        grads = [dcw_ref, dcb_ref, dclg_ref, dclb_ref, dwpw_ref, dslg_ref, dslb_ref, dws_ref, dbs_ref, dwp_ref, dps_ref]

        @pl.when((b == 0) & (i == 0))
        def _():
            for r in grads:
                r[...] = jnp.zeros_like(r)

        ext_row = lax.broadcasted_iota(jnp.int32, (ext, 1), 0)
        live = (ext_row < tm) | has_next

        yp = zp_ref[:, 0:D_A].astype(F32) * _sigmoid(zp_ref[:, D_A:2 * D_A].astype(F32))
        ys_ref[0:HALO, :] = jnp.where(has_prev, yp, 0.0)
        a_cur = zc_ref[:, 0:D_A].astype(F32)
        sig_cur = _sigmoid(zc_ref[:, D_A:2 * D_A].astype(F32))
        ys_ref[HALO:HALO + tm, :] = a_cur * sig_cur
        yn = zn_ref[:, 0:D_A].astype(F32) * _sigmoid(zn_ref[:, D_A:2 * D_A].astype(F32))
        ys_ref[HALO + tm:HALO + tm + HALO, :] = jnp.where(has_next, yn, 0.0)
        acc = jnp.zeros((ext, D_A), F32) + cb_ref[...]
        for k in range(CONV_WIDTH):
            acc = acc + cw_ref[k:k + 1, :] * ys_ref[pl.ds(HALO - (CONV_WIDTH - 1) + k, ext), :]
        ln, xhat, rstd = _layer_norm_fwd(acc, clg_ref[...], clb_ref[...])
        sg = _sigmoid(ln)
        sl = ln * sg
        dya = jnp.concatenate([dmc_ref[:, 0:D_A], dmn_ref[:, 0:D_A]], axis=0)
        dsl = _dot_nt(dya, wpw_ref[...])
        dln = dsl * sg * (1.0 + ln * (1.0 - sg))
        dc = _layer_norm_bwd(dln, xhat, rstd, clg_ref[...])
        dc = jnp.where(live, dc, 0.0)
        dcs_ref[...] = dc
        dwpw_ref[...] += _dot_tn(sl[:tm].astype(BF16), dya[:tm])
        dclg_ref[...] += jnp.sum(dln[:tm] * xhat[:tm], axis=0, keepdims=True)
        dclb_ref[...] += jnp.sum(dln[:tm], axis=0, keepdims=True)
        dcb_ref[...] += jnp.sum(dc[:tm], axis=0, keepdims=True)
        dy = jnp.zeros((tm, D_A), F32)
        for k in range(CONV_WIDTH):
            off = HALO - (CONV_WIDTH - 1) + k
            dcw_ref[k:k + 1, :] += jnp.sum(dcs_ref[0:tm, :] * ys_ref[pl.ds(off, tm), :], axis=0, keepdims=True)
            dy = dy + cw_ref[k:k + 1, :] * dcs_ref[pl.ds(CONV_WIDTH - 1 - k, tm), :]
        dz_ref[:, 0:D_A] = (dy * sig_cur).astype(BF16)
        dz_ref[:, D_A:2 * D_A] = (dy * a_cur * sig_cur * (1.0 - sig_cur)).astype(BF16)

        zb = zc_ref[:, 2 * D_A:2 * D_A + 2 * D_B].astype(F32)
        gz = _gelu(zb)
        u = gz[:, :D_B]
        vl, vhat, vrstd = _layer_norm_fwd(gz[:, D_B:], slg_ref[...], slb_ref[...])
        dyb = dmc_ref[:, D_A:D_A + D_B].astype(F32)
        wsm = _causal_rows(ws_ref[...])
        masks = _head_masks((CHUNK, D_B))
        ds_all = dyb * u
        du_parts, dvl_parts = [], []
        for c in range(tm // CHUNK):
            rows = slice(c * CHUNK, (c + 1) * CHUNK)
            vlc = vl[rows].astype(BF16)
            s = _gate_mix(wsm, vl[rows], masks) + bias_ref[...]
            du_parts.append(dyb[rows] * s)
            ds = ds_all[rows]
            stack = jnp.concatenate([jnp.where(masks[h], ds, 0.0) for h in range(N_HEADS_B)], axis=0).astype(BF16)
            dvl_parts.append(_dot_tn(wsm, stack))
            dws_ref[...] += _dot_nt(stack, vlc)
        du = jnp.concatenate(du_parts, axis=0)
        dvl = jnp.concatenate(dvl_parts, axis=0)
        dbias = jnp.zeros((CHUNK, D_B), F32)
        for c in range(tm // CHUNK):
            dbias = dbias + ds_all[c * CHUNK:(c + 1) * CHUNK]
        lane = lax.broadcasted_iota(jnp.int32, (CHUNK, CHUNK), 1)
        dbs = jnp.zeros((CHUNK, CHUNK), F32)
        for h in range(N_HEADS_B):
            col = jnp.sum(jnp.where(masks[h], dbias, 0.0), axis=1, keepdims=True)
            dbs = dbs + jnp.where(lane == h, col, 0.0)
        dbs_ref[...] += dbs
        dslg_ref[...] += jnp.sum(dvl * vhat, axis=0, keepdims=True)
        dslb_ref[...] += jnp.sum(dvl, axis=0, keepdims=True)
        dv = _layer_norm_bwd(dvl, vhat, vrstd, slg_ref[...])
        gg = _gelu_grad(zb)
        dz_ref[:, 2 * D_A:2 * D_A + D_B] = (du * gg[:, :D_B]).astype(BF16)
        dz_ref[:, 2 * D_A + D_B:2 * D_A + 2 * D_B] = (dv * gg[:, D_B:]).astype(BF16)

        c0 = 2 * D_A + 2 * D_B
        m0 = D_A + D_B
        zs_ref[0:HALO, :] = jnp.where(has_prev, zp_ref[:, c0:c0 + D_C].astype(F32), 0.0)
        zcur = zc_ref[:, c0:c0 + D_C].astype(F32)
        zs_ref[HALO:HALO + tm, :] = zcur
        win = _pool_window((tm, D_C))
        wsum = jnp.zeros((tm, D_C), F32)
        for j in range(16):
            wsum = wsum + jnp.where(j < win, zs_ref[pl.ds(HALO - j, tm), :], 0.0)
        pos = i * tm + lax.broadcasted_iota(jnp.int32, (tm, D_C), 0)
        cnt = jnp.minimum(pos + 1, win).astype(F32)
        pb = (wsum / cnt - zcur).astype(BF16)
        y = _dot(pb, wp_ref[...])
        dyc = jnp.concatenate([dmc_ref[:, m0:m0 + D_C], dmn_ref[:, m0:m0 + D_C]], axis=0).astype(F32)
        dps_ref[...] += jnp.sum(dyc[:tm] * y, axis=0, keepdims=True)
        dyv = (dyc * ps_ref[...]).astype(BF16)
        dwp_ref[...] += _dot_tn(pb, dyv[:tm])
        dp = _dot_nt(dyv, wp_ref[...])
        win_e = _pool_window((ext, D_C))
        pos_e = i * tm + lax.broadcasted_iota(jnp.int32, (ext, D_C), 0)
        cnt_e = jnp.minimum(pos_e + 1, win_e).astype(F32)
        qs_ref[...] = jnp.where(live, dp / cnt_e, 0.0)
        dzc = -dp[:tm]
        for j in range(16):
            dzc = dzc + jnp.where(j < win, qs_ref[pl.ds(j, tm), :], 0.0)
        dz_ref[:, c0:c0 + D_C] = dzc.astype(BF16)

        @pl.when((b == 1) & (i == nt - 1))
        def _():
            dws_ref[...] = _causal_rows(dws_ref[...])

    in_specs = ([cur(D_IN), prev(D_IN), nxt(D_IN), cur(D_MODEL), nxt(D_MODEL)]
                + [full(s) for s in _MIX_PARAM_SHAPES])
    out_specs = [cur(D_IN)] + [full(s) for s in grad_shapes]
    out_shape = [SDS((t, D_IN), BF16)] + [SDS(s, F32) for s in grad_shapes]
    outs, couts = _call(
        body, name=name, grid=(2, nt), in_specs=in_specs, out_specs=out_specs, out_shape=out_shape,
        scratch_shapes=[pltpu.VMEM((HALO + tm + HALO, D_A), F32), pltpu.VMEM((ext, D_A), F32),
                        pltpu.VMEM((HALO + tm, D_C), F32), pltpu.VMEM((ext, D_C), F32)],
        args=[z, z, z, dm, dm, *mp], pieces=pieces)
    return outs, couts


MIXER_SMALL = ["conv_w", "conv_b", "conv_ln_g", "conv_ln_b", "w_pw", "sg_ln_g", "sg_ln_b", "w_s", "b_s", "w_pool",
               "pool_scale"]


def _mixer_params(p, w_pw_bf16, l):
    wp_bd = jnp.zeros((D_C, D_C), F32)
    for g in range(D_C // GROUP_DIM_C):
        sl = slice(g * GROUP_DIM_C, (g + 1) * GROUP_DIM_C)
        wp_bd = wp_bd.at[sl, sl].set(p["w_pool"][l, g])
    return [
        jnp.pad(p["conv_w"][l], ((0, 32 - CONV_WIDTH), (0, 0))),
        p["conv_b"][l][None], p["conv_ln_g"][l][None], p["conv_ln_b"][l][None],
        w_pw_bf16,
        p["sg_ln_g"][l][None], p["sg_ln_b"][l][None],
        p["w_s"][l].reshape(N_HEADS_B * CHUNK, CHUNK).astype(BF16),
        jnp.repeat(p["b_s"][l].T, HEAD_DIM_B, axis=1),
        wp_bd.astype(BF16),
        p["pool_scale"][l][None],
    ]


def _mixer_grads(g):
    dcw, dcb, dclg, dclb, dwpw, dslg, dslb, dws, dbs, dwp, dps = g
    blocks = [dwp[i * GROUP_DIM_C:(i + 1) * GROUP_DIM_C, i * GROUP_DIM_C:(i + 1) * GROUP_DIM_C]
              for i in range(D_C // GROUP_DIM_C)]
    return [dcw[:CONV_WIDTH], dcb[0], dclg[0], dclb[0], dwpw, dslg[0], dslb[0],
            dws.reshape(N_HEADS_B, CHUNK, CHUNK), dbs[:, :N_HEADS_B].T, jnp.stack(blocks), dps[0]]


def _row_tile(r, cap=512):
    best = r
    for d in range(16, min(r, cap) + 1, 16):
        if r % d == 0:
            best = d
    return best if best <= cap else r


def add_core_halves(g, r1, core, *, name):
    _, _, r, c = g.shape
    tr = _row_tile(r)

    def body(core_ref, g_ref, r_ref, o_ref):
        o_ref[...] = (g_ref[...].astype(F32) + r_ref[...].astype(F32)).astype(o_ref.dtype)

    outs, _ = _call(
        body, name=name, grid=(N_CHIPS, r // tr), prefetch=1,
        in_specs=[pl.BlockSpec((None, None, tr, c), lambda j, i, s: (j, s[0], i, 0)),
                  pl.BlockSpec((None, tr, c), lambda j, i, s: (j, i, 0))],
        out_specs=[pl.BlockSpec((None, tr, c), lambda j, i, s: (j, i, 0))],
        out_shape=[SDS((N_CHIPS, r, c), g.dtype)], args=[core, g, r1])
    return outs[0]


def sum_chips(h, r2, place, *, name):
    _, r, c = h.shape
    tr = _row_tile(r, 256)

    def body(place_ref, h_ref, a_ref, b_ref, c_ref, o_ref):
        acc = h_ref[...].astype(F32) + a_ref[...].astype(F32)
        acc = acc + b_ref[...].astype(F32)
        o_ref[...] = acc + c_ref[...].astype(F32)

    def blk(k):
        return pl.BlockSpec((None, tr, c), lambda i, s: (jnp.bitwise_xor(s[0], k), i, 0))

    outs, _ = _call(
        body, name=name, grid=(r // tr,), prefetch=1, in_specs=[blk(0), blk(1), blk(2), blk(3)],
        out_specs=[pl.BlockSpec((None, tr, c), lambda i, s: (s[1], i, 0))],
        out_shape=[SDS((2, r, c), F32)], args=[place, h, r2, r2, r2])
    return outs[0]


def allreduce_small(p, *, name):
    _, n, _ = p.shape

    def body(p_ref, o_ref, land_ref, send1, recv1, send2, recv2):
        x, y, c = lax.axis_index("x"), lax.axis_index("y"), lax.axis_index("c")
        me = 4 * x + 2 * y + c

        def peer(r):
            return ((1 - x) if r & 4 else x, (1 - y) if r & 2 else y, (1 - c) if r & 1 else c)

        def index(r):
            px, py, pc = peer(r)
            return 4 * px + 2 * py + pc

        land_ref[me] = p_ref[me]
        first = [_remote(p_ref.at[index(r)], land_ref.at[me], send1.at[r - 1], recv1.at[r - 1], peer(r))
                 for r in range(1, N_DEV)]
        for cp in first:
            cp.start()
        for r in range(1, N_DEV):
            blk = land_ref.at[index(r)]
            _remote(blk, blk, send1.at[r - 1], recv1.at[r - 1], peer(r)).wait_recv()
        acc = land_ref[0]
        for d in range(1, N_DEV):
            acc = acc + land_ref[d]
        o_ref[me] = acc
        second = [_remote(o_ref.at[me], o_ref.at[me], send2.at[r - 1], recv2.at[r - 1], peer(r))
                  for r in range(1, N_DEV)]
        for cp in second:
            cp.start()
        for r in range(1, N_DEV):
            blk = o_ref.at[index(r)]
            _remote(blk, blk, send2.at[r - 1], recv2.at[r - 1], peer(r)).wait_recv()
        for cp in first + second:
            cp.wait_send()

    return pl.pallas_call(
        body, name=name, out_shape=SDS(p.shape, F32),
        in_specs=[pl.BlockSpec(memory_space=pltpu.VMEM)], out_specs=pl.BlockSpec(memory_space=pltpu.VMEM),
        scratch_shapes=[pltpu.VMEM(p.shape, F32)] + [pltpu.SemaphoreType.DMA((N_DEV - 1,))] * 4,
        interpret=_INTERPRET,
    )(p)


def _adam_update(w, g, m, v):
    m_new = ADAM_B1 * m + (1.0 - ADAM_B1) * g
    v_new = ADAM_B2 * v + (1.0 - ADAM_B2) * (g * g)
    m_hat = m_new / (1.0 - ADAM_B1 ** ADAM_STEP)
    v_hat = v_new / (1.0 - ADAM_B2 ** ADAM_STEP)
    return -ADAM_LR * (m_hat / (jnp.sqrt(v_hat) + ADAM_EPS) + ADAM_WD * w), m_new, v_new


def adamw(w, g, m, v, *, name):
    nl, r, c = w.shape
    tr = _row_tile(r, 256)

    def body(w_ref, g_ref, m_ref, v_ref, d_ref, mo_ref, vo_ref):
        d_ref[...], mo_ref[...], vo_ref[...] = _adam_update(w_ref[...], g_ref[...], m_ref[...], v_ref[...])

    blk = pl.BlockSpec((None, tr, c), lambda l, i: (l, i, 0))
    outs, _ = _call(body, name=name, grid=(nl, r // tr), in_specs=[blk] * 4, out_specs=[blk] * 3,
                    out_shape=[SDS(w.shape, F32)] * 3, args=[w, g, m, v])
    return outs


def adamw_small(ws, gs, ms, vs, *, name):
    n = len(ws)

    def body(*refs):
        for i in range(n):
            w_ref, g_ref, m_ref, v_ref = (refs[k * n + i] for k in range(4))
            d, mn, vn = _adam_update(w_ref[...], g_ref[...], m_ref[...], v_ref[...])
            refs[4 * n + i][...] = d
            refs[5 * n + i][...] = mn
            refs[6 * n + i][...] = vn

    vm = pl.BlockSpec(memory_space=pltpu.VMEM)
    res = pl.pallas_call(
        body, name=name, in_specs=[vm] * (4 * n), out_specs=[vm] * (3 * n),
        out_shape=[SDS(w.shape, F32) for w in ws] * 3, interpret=_INTERPRET,
    )(*ws, *gs, *ms, *vs)
    return res[:n], res[n:2 * n], res[2 * n:]


WEIGHTS = ["norm1_g", "w_in", "conv_w", "conv_b", "conv_ln_g", "conv_ln_b", "w_pw", "sg_ln_g", "sg_ln_b", "w_s",
           "b_s", "w_pool", "pool_scale", "w_out", "norm2_g", "w_gate_up", "w_down", "final_g"]
BIG = ["w_in", "w_pw", "w_out", "w_gate_up", "w_down"]
TRANSPOSED = {"w_in": True, "w_pw": False, "w_out": False, "w_gate_up": True, "w_down": False}
SMALL = [k for k in WEIGHTS if k not in BIG]


def _wire(a, transposed):
    if transposed:
        a = a.transpose(0, 2, 1)
    return [a[l].reshape(2, a.shape[1] // 2, a.shape[2]) for l in range(a.shape[0])]


def _pack(arrays):
    flat = jnp.concatenate([a.reshape(-1) for a in arrays])
    n = -(-flat.shape[0] // (N_DEV * LANES * 8)) * 8
    return jnp.pad(flat, (0, N_DEV * n * LANES - flat.shape[0])).reshape(N_DEV, n, LANES)


def _unpack(packed, shapes):
    flat = packed.reshape(-1)
    out, off = [], 0
    for s in shapes:
        size = math.prod(s)
        out.append(flat[off:off + size].reshape(s))
        off += size
    return out


def kernel(x, norm1_g, w_in, conv_w, conv_b, conv_ln_g, conv_ln_b, w_pw, sg_ln_g, sg_ln_b, w_s, b_s, w_pool, pool_scale, w_out, norm2_g, w_gate_up, w_down, final_g, loss_target, m_norm1_g, m_w_in, m_conv_w, m_conv_b, m_conv_ln_g, m_conv_ln_b, m_w_pw, m_sg_ln_g, m_sg_ln_b, m_w_s, m_b_s, m_w_pool, m_pool_scale, m_w_out, m_norm2_g, m_w_gate_up, m_w_down, m_final_g, v_norm1_g, v_w_in, v_conv_w, v_conv_b, v_conv_ln_g, v_conv_ln_b, v_w_pw, v_sg_ln_g, v_sg_ln_b, v_w_s, v_b_s, v_w_pool, v_pool_scale, v_w_out, v_norm2_g, v_w_gate_up, v_w_down, v_final_g):
    w = dict(norm1_g=norm1_g, w_in=w_in, conv_w=conv_w, conv_b=conv_b, conv_ln_g=conv_ln_g, conv_ln_b=conv_ln_b,
             w_pw=w_pw, sg_ln_g=sg_ln_g, sg_ln_b=sg_ln_b, w_s=w_s, b_s=b_s, w_pool=w_pool, pool_scale=pool_scale,
             w_out=w_out, norm2_g=norm2_g, w_gate_up=w_gate_up, w_down=w_down, final_g=final_g)
    m = dict(norm1_g=m_norm1_g, w_in=m_w_in, conv_w=m_conv_w, conv_b=m_conv_b, conv_ln_g=m_conv_ln_g,
             conv_ln_b=m_conv_ln_b, w_pw=m_w_pw, sg_ln_g=m_sg_ln_g, sg_ln_b=m_sg_ln_b, w_s=m_w_s, b_s=m_b_s,
             w_pool=m_w_pool, pool_scale=m_pool_scale, w_out=m_w_out, norm2_g=m_norm2_g, w_gate_up=m_w_gate_up,
             w_down=m_w_down, final_g=m_final_g)
    v = dict(norm1_g=v_norm1_g, w_in=v_w_in, conv_w=v_conv_w, conv_b=v_conv_b, conv_ln_g=v_conv_ln_g,
             conv_ln_b=v_conv_ln_b, w_pw=v_w_pw, sg_ln_g=v_sg_ln_g, sg_ln_b=v_sg_ln_b, w_s=v_w_s, b_s=v_b_s,
             w_pool=v_w_pool, pool_scale=v_pool_scale, w_out=v_w_out, norm2_g=v_norm2_g, w_gate_up=v_w_gate_up,
             w_down=v_w_down, final_g=v_final_g)
    bsz, seq, d = x.shape
    t = bsz * seq
    chip = 2 * lax.axis_index("x") + lax.axis_index("y")
    core = lax.axis_index("c")
    core_arr = jnp.reshape(core, (1,)).astype(jnp.int32)
    place_arr = jnp.stack([chip, core]).astype(jnp.int32)

    own = {k: _wire(w[k].astype(BF16), TRANSPOSED[k]) for k in BIG}
    cw_cols = conv_w.shape[2]
    side = jnp.pad(conv_w, ((0, 0), (0, 32 - CONV_WIDTH), (0, 0)))

    def complete(g, mine):
        g = lax.dynamic_update_index_in_dim(g, mine, chip, 0)
        return g.reshape(-1, g.shape[-1])

    first = comm_only([gather_both(own["w_in"][0]), gather_both(own["w_pw"][0]), gather_both(own["w_pw"][1]),
                       gather_both(side)], name="gather_first")
    full = {("w_in", 0): complete(first[0][0], own["w_in"][0]),
            ("w_pw", 0): complete(first[1][0], own["w_pw"][0]),
            ("w_pw", 1): complete(first[2][0], own["w_pw"][1])}
    side_all = lax.dynamic_update_index_in_dim(first[3][0], side, chip, 0)
    p = dict(w)
    p["conv_w"] = side_all[:, :, :CONV_WIDTH, :].transpose(1, 2, 0, 3).reshape(DEPTH, CONV_WIDTH, N_CHIPS * cw_cols)

    xs = [x.reshape(t, d)]
    saved = []
    pend = {}
    for l in range(DEPTH):
        x0 = xs[-1]
        mp = _mixer_params(p, full[("w_pw", l)], l)
        if l == 0:
            h1, (a,) = rmsnorm_fwd(x0, p["norm1_g"][l][None], name=f"norm1_fwd_{l}",
                                   pieces=[gather_ici(own["w_out"][0])])
            pend["w_out", 0] = a[0]
            z, (b, a) = mm_nt(h1, full["w_in", l], name=f"in_proj_{l}", tm=512, tn=D_IN, out_dtype=BF16,
                              pieces=[gather_d2d(pend.pop(("w_out", 0))), gather_ici(own["w_down"][0])])
            full["w_out", 0] = complete(b[0], own["w_out"][0])
            pend["w_down", 0] = a[0]
            mc, (b, a) = mixer_fwd(z, mp, seq=seq, name=f"mixer_fwd_{l}",
                                   pieces=[gather_d2d(pend.pop(("w_down", 0))), gather_ici(own["w_gate_up"][0])])
            full["w_down", 0] = complete(b[0], own["w_down"][0])
            pend["w_gate_up", 0] = a[0]
            x1, (b, a) = mm_nn(mc, full["w_out", l], name=f"out_proj_{l}", tm=512, tn=D_MODEL, out_dtype=F32,
                               residual=x0,
                               pieces=[gather_d2d(pend.pop(("w_gate_up", 0))), gather_ici(own["w_in"][1])])
            full["w_gate_up", 0] = complete(b[0], own["w_gate_up"][0])
            pend["w_in", 1] = a[0]
            h2, (b, a) = rmsnorm_fwd(x1, p["norm2_g"][l][None], name=f"norm2_fwd_{l}",
                                     pieces=[gather_d2d(pend.pop(("w_in", 1))), gather_ici(own["w_out"][1])])
            full["w_in", 1] = complete(b[0], own["w_in"][1])
            pend["w_out", 1] = a[0]
            (gu, act), (b, a) = swiglu_fwd(h2, full["w_gate_up", l], name=f"swiglu_fwd_{l}", tm=512, tn=1408,
                                           pieces=[gather_d2d(pend.pop(("w_out", 1))),
                                                   gather_ici(own["w_gate_up"][1])])
            full["w_out", 1] = complete(b[0], own["w_out"][1])
            pend["w_gate_up", 1] = a[0]
            x2, (b, a) = mm_nn(act, full["w_down", l], name=f"down_proj_{l}", tm=512, tn=D_MODEL, out_dtype=F32,
                               residual=x1,
                               pieces=[gather_d2d(pend.pop(("w_gate_up", 1))), gather_ici(own["w_down"][1])])
            full["w_gate_up", 1] = complete(b[0], own["w_gate_up"][1])
            pend["w_down", 1] = a[0]
        else:
            h1, (b,) = rmsnorm_fwd(x0, p["norm1_g"][l][None], name=f"norm1_fwd_{l}",
                                   pieces=[gather_d2d(pend.pop(("w_down", 1)))])
            full["w_down", 1] = complete(b[0], own["w_down"][1])
            z, _ = mm_nt(h1, full["w_in", l], name=f"in_proj_{l}", tm=512, tn=D_IN, out_dtype=BF16)
            mc, _ = mixer_fwd(z, mp, seq=seq, name=f"mixer_fwd_{l}")
            x1, _ = mm_nn(mc, full["w_out", l], name=f"out_proj_{l}", tm=512, tn=D_MODEL, out_dtype=F32, residual=x0)
            h2, _ = rmsnorm_fwd(x1, p["norm2_g"][l][None], name=f"norm2_fwd_{l}")
            (gu, act), _ = swiglu_fwd(h2, full["w_gate_up", l], name=f"swiglu_fwd_{l}", tm=512, tn=1408)
            x2, _ = mm_nn(act, full["w_down", l], name=f"down_proj_{l}", tm=512, tn=D_MODEL, out_dtype=F32,
                          residual=x1)
        saved.append((x0, h1, z, mc, x1, h2, gu, act, mp))
        xs.append(x2)

    loss, dx, dxb, d_final_g = loss_head(xs[-1], p["final_g"][None], loss_target.reshape(t, d), name="loss_head")

    small = {k: [None] * DEPTH for k in SMALL if k != "final_g"}
    reduced = {}
    carry = None

    def halves(g):
        return g.reshape(N_CHIPS, 2, g.shape[0] // (2 * N_CHIPS), g.shape[1])

    for l in reversed(range(DEPTH)):
        x0, h1, z, mc, x1, h2, gu, act, mp = saved[l]
        pieces = [exchange_d2d(carry)] if carry is not None else []
        g_down, co = mm_tn(act, dxb, name=f"down_proj_dw_{l}", tk=D_FF // 2, tn=D_MODEL, tt=512, out_dtype=BF16,
                           pieces=pieces)
        if carry is not None:
            reduced["w_in", l + 1] = co[0][0]
        g_down = halves(g_down)
        dgu, (r1,) = swiglu_bwd(dxb, full["w_down", l], gu, name=f"swiglu_bwd_{l}", tm=512, tn=1408,
                                pieces=[reduce_d2d(g_down)])
        h_down = add_core_halves(g_down, r1[0], core_arr, name=f"add_cores_w_down_{l}")
        g_gu, (r2,) = mm_gu_tn(dgu, h2, name=f"gate_up_dw_{l}", tn=1408, tt=512, pieces=[reduce_ici(h_down)])
        s_down = sum_chips(h_down, r2[0], place_arr, name=f"sum_chips_w_down_{l}")
        g_gu = halves(g_gu)
        dh2, (e, r1) = mm_gu_nn(dgu, full["w_gate_up", l], name=f"gate_up_dx_{l}", tm=512,
                                pieces=[exchange_d2d(s_down), reduce_d2d(g_gu)])
        reduced["w_down", l] = e[0]
        h_gu = add_core_halves(g_gu, r1[0], core_arr, name=f"add_cores_w_gate_up_{l}")
        (dx, dxb, dn2), _ = rmsnorm_bwd(dh2, x1, p["norm2_g"][l][None], dx, name=f"norm2_bwd_{l}")
        small["norm2_g"][l] = dn2[0]
        g_out, _ = mm_tn(mc, dxb, name=f"out_proj_dw_{l}", tk=D_MODEL, tn=D_MODEL, tt=512, out_dtype=BF16)
        g_out = halves(g_out)
        dmc, (r1,) = mm_nt(dxb, full["w_out", l], name=f"out_proj_dx_{l}", tm=512, tn=D_MODEL, out_dtype=BF16,
                           pieces=[reduce_d2d(g_out)])
        h_out = add_core_halves(g_out, r1[0], core_arr, name=f"add_cores_w_out_{l}")
        (dz, *mg), (r2a, r2b) = mixer_bwd(z, dmc, mp, seq=seq, name=f"mixer_bwd_{l}",
                                          pieces=[reduce_ici(h_gu), reduce_ici(h_out)])
        s_gu = sum_chips(h_gu, r2a[0], place_arr, name=f"sum_chips_w_gate_up_{l}")
        s_out = sum_chips(h_out, r2b[0], place_arr, name=f"sum_chips_w_out_{l}")
        mgrads = dict(zip(MIXER_SMALL, _mixer_grads(mg)))
        for k in MIXER_SMALL:
            if k != "w_pw":
                small[k][l] = mgrads[k]
        g_pw = halves(mgrads["w_pw"].astype(BF16))
        g_in, (ea, eb, r1) = mm_tn(dz, h1, name=f"in_proj_dw_{l}", tk=D_IN // 2, tn=D_MODEL, tt=512, out_dtype=BF16,
                                   pieces=[exchange_d2d(s_gu), exchange_d2d(s_out), reduce_d2d(g_pw)])
        reduced["w_gate_up", l], reduced["w_out", l] = ea[0], eb[0]
        h_pw = add_core_halves(g_pw, r1[0], core_arr, name=f"add_cores_w_pw_{l}")
        g_in = halves(g_in)
        dh1, (r2, r1) = mm_nn(dz, full["w_in", l], name=f"in_proj_dx_{l}", tm=512, tn=D_MODEL, out_dtype=F32,
                              pieces=[reduce_ici(h_pw), reduce_d2d(g_in)])
        s_pw = sum_chips(h_pw, r2[0], place_arr, name=f"sum_chips_w_pw_{l}")
        h_in = add_core_halves(g_in, r1[0], core_arr, name=f"add_cores_w_in_{l}")
        (dx, dxb, dn1), (e, r2) = rmsnorm_bwd(dh1, x0, p["norm1_g"][l][None], dx, name=f"norm1_bwd_{l}",
                                              pieces=[exchange_d2d(s_pw), reduce_ici(h_in)])
        reduced["w_pw", l] = e[0]
        small["norm1_g"][l] = dn1[0]
        carry = sum_chips(h_in, r2[0], place_arr, name=f"sum_chips_w_in_{l}")
    reduced["w_in", 0] = comm_only([exchange_d2d(carry)], name="exchange_last")[0][0]
    grad_x = dx.reshape(bsz, seq, d)

    g_small = [jnp.stack(small[k]) if k != "final_g" else d_final_g[0] for k in SMALL]
    small_shapes = [g.shape for g in g_small] + [(1,)]
    summed = _unpack(allreduce_small(_pack(g_small + [loss.reshape(1)]), name="allreduce_small"), small_shapes)
    loss = summed[-1][0]
    grad = dict(zip(SMALL, summed[:-1]))
    grad["conv_w"] = lax.dynamic_slice_in_dim(grad["conv_w"], chip * cw_cols, cw_cols, axis=2)

    for k in BIG:
        g = jnp.stack([reduced[k, l].reshape(-1, reduced[k, l].shape[-1]) for l in range(DEPTH)])
        grad[k] = g.transpose(0, 2, 1) if TRANSPOSED[k] else g
    delta, new_m, new_v = {}, {}, {}
    for k in BIG:
        delta[k], new_m[k], new_v[k] = adamw(w[k], grad[k], m[k], v[k], name=f"adamw_{k}")

    def flat2(a):
        return a.reshape(-1, a.shape[-1])

    res = adamw_small(*[[flat2(tt[k]) for k in SMALL] for tt in (w, grad, m, v)], name="adamw_small")
    for out, arrs in zip((delta, new_m, new_v), res):
        out.update({k: a.reshape(w[k].shape) for k, a in zip(SMALL, arrs)})
    return (loss, grad_x, *[grad[k] for k in WEIGHTS], *[delta[k] for k in WEIGHTS],
            *[new_m[k] for k in WEIGHTS], *[new_v[k] for k in WEIGHTS])
```

```python
import functools
import math

import jax
import jax.numpy as jnp
from jax import lax
from jax.experimental import pallas as pl
from jax.experimental.pallas import tpu as pltpu

F32 = jnp.float32
BF16 = jnp.bfloat16

D_MODEL = 1024
DEPTH = 2
D_A = 384
D_B = 384
D_C = 256
D_IN = 2 * D_A + 2 * D_B + D_C
N_HEADS_B = 4
HEAD_DIM_B = 96
GROUP_DIM_C = 64
CONV_WIDTH = 31
CHUNK = 128
D_FF = 2816
RMS_EPS = 1e-6
LN_EPS = 1e-5
HALO = 32
TT = 2048
N_CHIPS = 4
N_DEV = 8
LANES = 128

ADAM_LR = 0.001
ADAM_B1 = 0.9
ADAM_B2 = 0.999
ADAM_EPS = 1e-08
ADAM_WD = 0.01
ADAM_STEP = 10

VMEM_LIMIT = 56 * 1024 * 1024
_INTERPRET = False

MESH = pl.DeviceIdType.MESH
ANY = pl.BlockSpec(memory_space=pl.ANY)
SDS = jax.ShapeDtypeStruct


def _sigmoid(x):
    return 1.0 / (1.0 + jnp.exp(-x))


_GELU_C = math.sqrt(2.0 / math.pi)


def _gelu(x):
    return 0.5 * x * (1.0 + jnp.tanh(_GELU_C * (x + 0.044715 * x * x * x)))


def _gelu_grad(x):
    t = jnp.tanh(_GELU_C * (x + 0.044715 * x * x * x))
    return 0.5 * (1.0 + t) + 0.5 * x * (1.0 - t * t) * _GELU_C * (1.0 + 3 * 0.044715 * x * x)


def _dot(a, b):
    return jnp.dot(a, b, preferred_element_type=F32)


def _dot_nt(a, b):
    return lax.dot_general(a, b, (((1,), (1,)), ((), ())), preferred_element_type=F32)


def _dot_tn(a, b):
    return lax.dot_general(a, b, (((0,), (0,)), ((), ())), preferred_element_type=F32)


class Piece:
    def __init__(self, operands, out_shapes, aliases, n_sems, start, finish):
        self.operands, self.out_shapes, self.aliases, self.n_sems = operands, out_shapes, aliases, n_sems
        self.start, self.finish = start, finish


def _place():
    x, y, c = lax.axis_index("x"), lax.axis_index("y"), lax.axis_index("c")
    chips = [(1 - x, y), (x, 1 - y), (1 - x, 1 - y)]
    return x, y, c, chips


def _remote(src, dst, send_sem, recv_sem, to):
    return pltpu.make_async_remote_copy(src_ref=src, dst_ref=dst, send_sem=send_sem, recv_sem=recv_sem,
                                        device_id=to, device_id_type=MESH)


def gather_ici(src):
    def copies(ins, outs, sem):
        x, y, c, chips = _place()
        j = 2 * x + y
        return [(_remote(ins[0].at[c], outs[0].at[j, c], sem(k), sem(3 + k), (px, py, c)),
                 outs[0].at[2 * px + py, c], (px, py, c)) for k, (px, py) in enumerate(chips)]

    def start(ins, outs, sem):
        for cp, _, _ in copies(ins, outs, sem):
            cp.start()

    def finish(ins, outs, sem):
        cps = copies(ins, outs, sem)
        for k, (_, landed, frm) in enumerate(cps):
            _remote(landed, landed, sem(k), sem(3 + k), frm).wait_recv()
        for cp, _, _ in cps:
            cp.wait_send()

    return Piece([src], [SDS((N_CHIPS,) + src.shape, src.dtype)], {}, 6, start, finish)


def gather_d2d(g):
    def copies(outs, sem):
        x, y, c, chips = _place()
        return [(_remote(outs[0].at[2 * px + py, c], outs[0].at[2 * px + py, c], sem(k), sem(3 + k), (x, y, 1 - c)),
                 outs[0].at[2 * px + py, 1 - c], (x, y, 1 - c)) for k, (px, py) in enumerate(chips)]

    def start(ins, outs, sem):
        for cp, _, _ in copies(outs, sem):
            cp.start()

    def finish(ins, outs, sem):
        cps = copies(outs, sem)
        for k, (_, landed, frm) in enumerate(cps):
            _remote(landed, landed, sem(k), sem(3 + k), frm).wait_recv()
        for cp, _, _ in cps:
            cp.wait_send()

    return Piece([g], [SDS(g.shape, g.dtype)], {0: 0}, 6, start, finish)


def gather_both(src):
    a = gather_ici(src)

    def finish(ins, outs, sem):
        a.finish(ins, outs, lambda k: sem(k))
        x, y, c, chips = _place()
        cps = [_remote(outs[0].at[2 * px + py, c], outs[0].at[2 * px + py, c], sem(6 + k), sem(9 + k), (x, y, 1 - c))
               for k, (px, py) in enumerate(chips)]
        for cp in cps:
            cp.start()
        for k, (px, py) in enumerate(chips):
            blk = outs[0].at[2 * px + py, 1 - c]
            _remote(blk, blk, sem(6 + k), sem(9 + k), (x, y, 1 - c)).wait_recv()
        for cp in cps:
            cp.wait_send()

    return Piece(a.operands, a.out_shapes, {}, 12, a.start, finish)


def reduce_d2d(g):
    def copies(ins, outs, sem):
        x, y, c, _ = _place()
        return [_remote(ins[0].at[j, 1 - c], outs[0].at[j], sem(j), sem(4 + j), (x, y, 1 - c)) for j in range(N_CHIPS)]

    def start(ins, outs, sem):
        for cp in copies(ins, outs, sem):
            cp.start()

    def finish(ins, outs, sem):
        cps = copies(ins, outs, sem)
        for cp in cps:
            cp.wait_recv()
        for cp in cps:
            cp.wait_send()

    return Piece([g], [SDS((N_CHIPS,) + g.shape[2:], g.dtype)], {}, 8, start, finish)


def reduce_ici(h):
    def copies(ins, outs, sem):
        x, y, c, chips = _place()
        j = 2 * x + y
        return [(_remote(ins[0].at[2 * px + py], outs[0].at[j], sem(k), sem(3 + k), (px, py, c)),
                 outs[0].at[2 * px + py], (px, py, c)) for k, (px, py) in enumerate(chips)]

    def start(ins, outs, sem):
        for cp, _, _ in copies(ins, outs, sem):
            cp.start()

    def finish(ins, outs, sem):
        cps = copies(ins, outs, sem)
        for k, (_, landed, frm) in enumerate(cps):
            _remote(landed, landed, sem(k), sem(3 + k), frm).wait_recv()
        for cp, _, _ in cps:
            cp.wait_send()

    return Piece([h], [SDS(h.shape, h.dtype)], {}, 6, start, finish)


def exchange_d2d(g):
    def copy(outs, sem):
        x, y, c, _ = _place()
        return _remote(outs[0].at[c], outs[0].at[c], sem(0), sem(1), (x, y, 1 - c)), outs[0].at[1 - c], (x, y, 1 - c)

    def start(ins, outs, sem):
        copy(outs, sem)[0].start()

    def finish(ins, outs, sem):
        cp, landed, frm = copy(outs, sem)
        _remote(landed, landed, sem(0), sem(1), frm).wait_recv()
        cp.wait_send()

    return Piece([g], [SDS(g.shape, g.dtype)], {0: 0}, 2, start, finish)


def _call(body, *, name, grid, in_specs, out_specs, out_shape, args, scratch_shapes=(), pieces=(), prefetch=0):
    n_in, n_out, n_scr = len(in_specs), len(out_specs), len(scratch_shapes)
    c_ops = [a for p in pieces for a in p.operands]
    c_outs = [s for p in pieces for s in p.out_shapes]
    n_sems = sum(p.n_sems for p in pieces)
    aliases = {}
    op_off, out_off = prefetch + n_in, n_out
    for p in pieces:
        for i, o in p.aliases.items():
            aliases[op_off + i] = out_off + o
        op_off += len(p.operands)
        out_off += len(p.out_shapes)

    def wrapped(*refs):
        pre, refs = refs[:prefetch], refs[prefetch:]
        ins, cin = refs[:n_in], refs[n_in:n_in + len(c_ops)]
        o0 = n_in + len(c_ops)
        outs, cout = refs[o0:o0 + n_out], refs[o0 + n_out:o0 + n_out + len(c_outs)]
        s0 = o0 + n_out + len(c_outs)
        scr = refs[s0:s0 + n_scr]

        def each(method):
            sems = refs[s0 + n_scr]
            i_off = o_off = s_off = 0
            for p in pieces:
                getattr(p, method)(cin[i_off:i_off + len(p.operands)], cout[o_off:o_off + len(p.out_shapes)],
                                   functools.partial(lambda k, base: sems.at[base + k], base=s_off))
                i_off, o_off, s_off = i_off + len(p.operands), o_off + len(p.out_shapes), s_off + p.n_sems

        if pieces and grid:
            ids = [pl.program_id(a) for a in range(len(grid))]
            first = functools.reduce(jnp.logical_and, [i == 0 for i in ids])
            last = functools.reduce(jnp.logical_and, [i == g - 1 for i, g in zip(ids, grid)])
            pl.when(first)(lambda: each("start"))
        elif pieces:
            each("start")
        if body is not None:
            body(*pre, *ins, *outs, *scr)
        if pieces and grid:
            pl.when(last)(lambda: each("finish"))
        elif pieces:
            each("finish")

    scratch = list(scratch_shapes) + ([pltpu.SemaphoreType.DMA((n_sems,))] if pieces else [])
    all_in = list(in_specs) + [ANY] * len(c_ops)
    all_out = list(out_specs) + [ANY] * len(c_outs)
    shapes = list(out_shape) + c_outs
    kw = dict(name=name, out_shape=shapes, input_output_aliases=aliases, interpret=_INTERPRET)
    if grid:
        kw["compiler_params"] = pltpu.CompilerParams(dimension_semantics=("arbitrary",) * len(grid),
                                                     vmem_limit_bytes=VMEM_LIMIT)
    if prefetch:
        kw["grid_spec"] = pltpu.PrefetchScalarGridSpec(num_scalar_prefetch=prefetch, grid=grid, in_specs=all_in,
                                                       out_specs=all_out, scratch_shapes=scratch)
    else:
        kw.update(in_specs=all_in, out_specs=all_out, scratch_shapes=scratch)
        if grid:
            kw["grid"] = grid
    res = pl.pallas_call(wrapped, **kw)(*args, *c_ops)
    outs, rest = list(res[:n_out]), list(res[n_out:])
    couts = []
    for p in pieces:
        couts.append(rest[:len(p.out_shapes)])
        rest = rest[len(p.out_shapes):]
    return outs, couts


def comm_only(pieces, *, name):
    return _call(None, name=name, grid=(), in_specs=[], out_specs=[], out_shape=[], args=[], pieces=pieces)[1]


def rmsnorm_fwd(x, g, *, name, pieces=()):
    t, d = x.shape
    tm = min(512, t)

    def body(x_ref, g_ref, o_ref):
        xv = x_ref[...]
        rstd = lax.rsqrt(jnp.mean(xv * xv, axis=-1, keepdims=True) + RMS_EPS)
        o_ref[...] = (xv * rstd * g_ref[...]).astype(BF16)

    outs, couts = _call(
        body, name=name, grid=(t // tm,),
        in_specs=[pl.BlockSpec((tm, d), lambda i: (i, 0)), pl.BlockSpec((1, d), lambda i: (0, 0))],
        out_specs=[pl.BlockSpec((tm, d), lambda i: (i, 0))], out_shape=[SDS((t, d), BF16)],
        args=[x, g], pieces=pieces)
    return outs[0], couts


def loss_head(x, g, target, *, name):
    t, d = x.shape
    tm = min(512, t)

    def body(x_ref, g_ref, t_ref, loss_ref, dx_ref, dxb_ref, dg_ref):
        i = pl.program_id(0)
        xv = x_ref[...]
        gv = g_ref[...]
        rstd = lax.rsqrt(jnp.mean(xv * xv, axis=-1, keepdims=True) + RMS_EPS)
        xhat = xv * rstd
        err = xhat * gv - t_ref[...]
        dy = err * (1.0 / d)
        dxhat = dy * gv
        dx = rstd * (dxhat - xhat * jnp.mean(dxhat * xhat, axis=-1, keepdims=True))
        dx_ref[...] = dx
        dxb_ref[...] = dx.astype(BF16)

        @pl.when(i == 0)
        def _():
            dg_ref[...] = jnp.zeros_like(dg_ref)
            loss_ref[...] = jnp.zeros_like(loss_ref)

        dg_ref[...] += jnp.sum(dy * xhat, axis=0, keepdims=True)
        per_tok = jnp.sum(err * err, axis=-1, keepdims=True) * (0.5 / d)
        loss_ref[...] += jnp.sum(per_tok, axis=0, keepdims=True)

    row = pl.BlockSpec((tm, d), lambda i: (i, 0))
    vec = pl.BlockSpec((1, d), lambda i: (0, 0))
    one = pl.BlockSpec((1, 1), lambda i: (0, 0))
    outs, _ = _call(
        body, name=name, grid=(t // tm,), in_specs=[row, vec, row], out_specs=[one, row, row, vec],
        out_shape=[SDS((1, 1), F32), SDS((t, d), F32), SDS((t, d), BF16), SDS((1, d), F32)],
        args=[x, g, target])
    return outs


def mm_nn(a, b, *, name, tm, tn, out_dtype, residual=None, pieces=()):
    m, k = a.shape
    n = b.shape[1]
    tm, tn = min(tm, m), min(tn, n)
    has_res = residual is not None

    def body(a_ref, b_ref, *rest):
        o_ref = rest[-1]
        acc = _dot(a_ref[...], b_ref[...])
        if has_res:
            acc = acc + rest[0][...]
        o_ref[...] = acc.astype(o_ref.dtype)

    in_specs = [pl.BlockSpec((tm, k), lambda j, i: (i, 0)), pl.BlockSpec((k, tn), lambda j, i: (0, j))]
    args = [a, b]
    if has_res:
        in_specs.append(pl.BlockSpec((tm, tn), lambda j, i: (i, j)))
        args.append(residual)
    outs, couts = _call(
        body, name=name, grid=(n // tn, m // tm), in_specs=in_specs,
        out_specs=[pl.BlockSpec((tm, tn), lambda j, i: (i, j))], out_shape=[SDS((m, n), out_dtype)],
        args=args, pieces=pieces)
    return outs[0], couts


def mm_nt(a, b, *, name, tm, tn, out_dtype, pieces=()):
    m, k = a.shape
    n = b.shape[0]
    tm, tn = min(tm, m), min(tn, n)

    def body(a_ref, b_ref, o_ref):
        o_ref[...] = _dot_nt(a_ref[...], b_ref[...]).astype(o_ref.dtype)

    outs, couts = _call(
        body, name=name, grid=(n // tn, m // tm),
        in_specs=[pl.BlockSpec((tm, k), lambda j, i: (i, 0)), pl.BlockSpec((tn, k), lambda j, i: (j, 0))],
        out_specs=[pl.BlockSpec((tm, tn), lambda j, i: (i, j))], out_shape=[SDS((m, n), out_dtype)],
        args=[a, b], pieces=pieces)
    return outs[0], couts


def mm_tn(a, b, *, name, tk, tn, tt, out_dtype, pieces=()):
    t, k = a.shape
    n = b.shape[1]
    tk, tn, tt = min(tk, k), min(tn, n), min(tt, t)
    nt = t // tt

    def body(a_ref, b_ref, o_ref, acc_ref):
        s = pl.program_id(2)

        @pl.when(s == 0)
        def _():
            acc_ref[...] = jnp.zeros_like(acc_ref)

        acc_ref[...] += _dot_tn(a_ref[...], b_ref[...])

        @pl.when(s == nt - 1)
        def _():
            o_ref[...] = acc_ref[...].astype(o_ref.dtype)

    outs, couts = _call(
        body, name=name, grid=(k // tk, n // tn, nt),
        in_specs=[pl.BlockSpec((tt, tk), lambda i, j, s: (s, i)), pl.BlockSpec((tt, tn), lambda i, j, s: (s, j))],
        out_specs=[pl.BlockSpec((tk, tn), lambda i, j, s: (i, j))], out_shape=[SDS((k, n), out_dtype)],
        scratch_shapes=[pltpu.VMEM((tk, tn), F32)], args=[a, b], pieces=pieces)
    return outs[0], couts


def swiglu_fwd(h, wt, *, name, tm, tn, pieces=()):
    t, d = h.shape
    ff = wt.shape[0] // 2
    tm, tn = min(tm, t), min(tn, ff)
    nb = ff // tn

    def body(h_ref, wg_ref, wu_ref, gu_ref, act_ref):
        hv = h_ref[...]
        gate = _dot_nt(hv, wg_ref[...])
        up = _dot_nt(hv, wu_ref[...])
        gu_ref[0] = gate.astype(BF16)
        gu_ref[1] = up.astype(BF16)
        act_ref[...] = (gate * _sigmoid(gate) * up).astype(BF16)

    outs, couts = _call(
        body, name=name, grid=(nb, t // tm),
        in_specs=[pl.BlockSpec((tm, d), lambda j, i: (i, 0)),
                  pl.BlockSpec((tn, d), lambda j, i: (j, 0)),
                  pl.BlockSpec((tn, d), lambda j, i: (j + nb, 0))],
        out_specs=[pl.BlockSpec((2, tm, tn), lambda j, i: (0, i, j)), pl.BlockSpec((tm, tn), lambda j, i: (i, j))],
        out_shape=[SDS((2, t, ff), BF16), SDS((t, ff), BF16)], args=[h, wt, wt], pieces=pieces)
    return outs, couts


def swiglu_bwd(dx, w_down, gu, *, name, tm, tn, pieces=()):
    t, d = dx.shape
    ff = w_down.shape[0]
    tm, tn = min(tm, t), min(tn, ff)

    def body(dx_ref, w_ref, gu_ref, o_ref):
        dact = _dot_nt(dx_ref[...], w_ref[...])
        gate = gu_ref[0].astype(F32)
        up = gu_ref[1].astype(F32)
        sg = _sigmoid(gate)
        o_ref[0] = (dact * up * sg * (1.0 + gate * (1.0 - sg))).astype(BF16)
        o_ref[1] = (dact * gate * sg).astype(BF16)

    outs, couts = _call(
        body, name=name, grid=(ff // tn, t // tm),
        in_specs=[pl.BlockSpec((tm, d), lambda j, i: (i, 0)), pl.BlockSpec((tn, d), lambda j, i: (j, 0)),
                  pl.BlockSpec((2, tm, tn), lambda j, i: (0, i, j))],
        out_specs=[pl.BlockSpec((2, tm, tn), lambda j, i: (0, i, j))], out_shape=[SDS((2, t, ff), BF16)],
        args=[dx, w_down, gu], pieces=pieces)
    return outs[0], couts


def mm_norm_bwd(a, wt, x, g, dres, *, name, tm, pieces=()):
    parts = a.shape[0] if a.ndim == 3 else 1
    t, kp = a.shape[-2:]
    d = wt.shape[1]
    tm = min(tm, t)

    def body(a_ref, w_ref, x_ref, g_ref, dres_ref, dx_ref, dxb_ref, dg_ref):
        i = pl.program_id(0)
        if parts == 1:
            dh = _dot(a_ref[...], w_ref[...])
        else:
            dh = _dot(a_ref[0], w_ref[0:kp, :])
            for q in range(1, parts):
                dh = dh + _dot(a_ref[q], w_ref[q * kp:(q + 1) * kp, :])
        xv = x_ref[...]
        rstd = lax.rsqrt(jnp.mean(xv * xv, axis=-1, keepdims=True) + RMS_EPS)
        xhat = xv * rstd
        dxhat = dh * g_ref[...]
        dx = dres_ref[...] + rstd * (dxhat - xhat * jnp.mean(dxhat * xhat, axis=-1, keepdims=True))
        dx_ref[...] = dx
        dxb_ref[...] = dx.astype(BF16)

        @pl.when(i == 0)
        def _():
            dg_ref[...] = jnp.zeros_like(dg_ref)

        dg_ref[...] += jnp.sum(dh * xhat, axis=0, keepdims=True)

    a_spec = (pl.BlockSpec((tm, kp), lambda i: (i, 0)) if parts == 1
              else pl.BlockSpec((parts, tm, kp), lambda i: (0, i, 0)))
    row = pl.BlockSpec((tm, d), lambda i: (i, 0))
    vec = pl.BlockSpec((1, d), lambda i: (0, 0))
    outs, couts = _call(
        body, name=name, grid=(t // tm,),
        in_specs=[a_spec, pl.BlockSpec((parts * kp, d), lambda i: (0, 0)), row, vec, row],
        out_specs=[row, row, vec], out_shape=[SDS((t, d), F32), SDS((t, d), BF16), SDS((1, d), F32)],
        args=[a, wt, x, g, dres], pieces=pieces)
    return outs, couts


def mm_gu_tn(dgu, h, *, name, tn, tt, pieces=()):
    t, d = h.shape
    ff = dgu.shape[2]
    tn, tt = min(tn, ff), min(tt, t)
    nb = ff // tn
    nt = t // tt

    def body(a_ref, h_ref, o_ref, acc_ref):
        s = pl.program_id(1)

        @pl.when(s == 0)
        def _():
            acc_ref[...] = jnp.zeros_like(acc_ref)

        acc_ref[...] += _dot_tn(a_ref[...], h_ref[...])

        @pl.when(s == nt - 1)
        def _():
            o_ref[...] = acc_ref[...].astype(o_ref.dtype)

    outs, couts = _call(
        body, name=name, grid=(2 * nb, nt),
        in_specs=[pl.BlockSpec((None, tt, tn), lambda j, s: (j // nb, s, j % nb)),
                  pl.BlockSpec((tt, d), lambda j, s: (s, 0))],
        out_specs=[pl.BlockSpec((tn, d), lambda j, s: (j, 0))], out_shape=[SDS((2 * ff, d), BF16)],
        scratch_shapes=[pltpu.VMEM((tn, d), F32)], args=[dgu, h], pieces=pieces)
    return outs[0], couts


def _head_masks(shape):
    lane = lax.broadcasted_iota(jnp.int32, shape, 1)
    return [(lane >= h * HEAD_DIM_B) & (lane < (h + 1) * HEAD_DIM_B) for h in range(N_HEADS_B)]


def _causal_rows(ws):
    r = lax.broadcasted_iota(jnp.int32, ws.shape, 0) % CHUNK
    c = lax.broadcasted_iota(jnp.int32, ws.shape, 1)
    return jnp.where(c <= r, ws, jnp.zeros_like(ws))


def _pool_window(shape):
    lane = lax.broadcasted_iota(jnp.int32, shape, 1)
    return jnp.left_shift(2, lane // GROUP_DIM_C)


def _layer_norm_fwd(x, g, b):
    mu = jnp.mean(x, axis=-1, keepdims=True)
    xc = x - mu
    rstd = lax.rsqrt(jnp.mean(xc * xc, axis=-1, keepdims=True) + LN_EPS)
    xhat = xc * rstd
    return xhat * g + b, xhat, rstd


def _layer_norm_bwd(dy, xhat, rstd, g):
    dxhat = dy * g
    return rstd * (dxhat - jnp.mean(dxhat, axis=-1, keepdims=True)
                   - xhat * jnp.mean(dxhat * xhat, axis=-1, keepdims=True))


def _gate_mix(ws_masked, vl_chunk, masks):
    out = _dot(ws_masked, vl_chunk.astype(BF16))
    s = jnp.zeros((CHUNK, D_B), F32)
    for h in range(N_HEADS_B):
        s = s + jnp.where(masks[h], out[h * CHUNK:(h + 1) * CHUNK], 0.0)
    return s


SUB = 8


def _shifted_copies(ref, n):
    for b in range(1, SUB):
        ref[b, 0:n, :] = ref[0, pl.ds(b, n), :]


def _tap(ref, off, n):
    a, b = divmod(off, SUB)
    return ref[b, pl.ds(SUB * a, n), :]


def _mixer_specs(tm, nt, seq):
    hb = tm // HALO

    def cur(c):
        return pl.BlockSpec((tm, c), lambda b, i: (b * nt + i, 0))

    def prev(c):
        return pl.BlockSpec((HALO, c), lambda b, i: (jnp.maximum((b * nt + i) * hb - 1, 0), 0))

    def nxt(c):
        last = (2 * seq) // HALO - 1
        return pl.BlockSpec((HALO, c), lambda b, i: (jnp.minimum((b * nt + i + 1) * hb, last), 0))

    def full(shape):
        return pl.BlockSpec(shape, lambda b, i: tuple(0 for _ in shape))

    return cur, prev, nxt, full


_MIX_PARAM_SHAPES = [(32, D_A), (1, D_A), (1, D_A), (1, D_A), (D_A, D_A), (1, D_B), (1, D_B),
                     (N_HEADS_B * CHUNK, CHUNK), (CHUNK, D_B), (D_C, D_C), (1, D_C)]


def mixer_fwd(z, mp, *, seq, name, tm=512, pieces=()):
    t = z.shape[0]
    tm = min(tm, seq)
    nt = seq // tm
    cur, prev, _, full = _mixer_specs(tm, nt, seq)

    def body(zc_ref, zp_ref, cw_ref, cb_ref, clg_ref, clb_ref, wpw_ref, slg_ref, slb_ref, ws_ref, bias_ref,
             wp_ref, ps_ref, o_ref, ys_ref, zs_ref):
        i = pl.program_id(1)
        has_prev = i > 0
        yp = zp_ref[:, 0:D_A].astype(F32) * _sigmoid(zp_ref[:, D_A:2 * D_A].astype(F32))
        ys_ref[0, 0:HALO, :] = jnp.where(has_prev, yp, 0.0)
        ys_ref[0, HALO:HALO + tm, :] = zc_ref[:, 0:D_A].astype(F32) * _sigmoid(zc_ref[:, D_A:2 * D_A].astype(F32))
        _shifted_copies(ys_ref, tm + HALO - SUB)
        acc = jnp.zeros((tm, D_A), F32) + cb_ref[...]
        for k in range(CONV_WIDTH):
            acc = acc + cw_ref[k:k + 1, :] * _tap(ys_ref, HALO - (CONV_WIDTH - 1) + k, tm)
        ln, _, _ = _layer_norm_fwd(acc, clg_ref[...], clb_ref[...])
        sl = ln * _sigmoid(ln)
        o_ref[:, 0:D_A] = _dot(sl.astype(BF16), wpw_ref[...]).astype(BF16)
        gz = _gelu(zc_ref[:, 2 * D_A:2 * D_A + 2 * D_B].astype(F32))
        u = gz[:, :D_B]
        vl, _, _ = _layer_norm_fwd(gz[:, D_B:], slg_ref[...], slb_ref[...])
        wsm = _causal_rows(ws_ref[...])
        masks = _head_masks((CHUNK, D_B))
        for c in range(tm // CHUNK):
            rows = slice(c * CHUNK, (c + 1) * CHUNK)
            s = _gate_mix(wsm, vl[rows], masks) + bias_ref[...]
            o_ref[rows, D_A:D_A + D_B] = (u[rows] * s).astype(BF16)
        c0 = 2 * D_A + 2 * D_B
        zs_ref[0:HALO, :] = jnp.where(has_prev, zp_ref[:, c0:c0 + D_C].astype(F32), 0.0)
        zcur = zc_ref[:, c0:c0 + D_C].astype(F32)
        zs_ref[HALO:HALO + tm, :] = zcur
        win = _pool_window((tm, D_C))
        wsum = jnp.zeros((tm, D_C), F32)
        for j in range(16):
            wsum = wsum + jnp.where(j < win, zs_ref[pl.ds(HALO - j, tm), :], 0.0)
        pos = i * tm + lax.broadcasted_iota(jnp.int32, (tm, D_C), 0)
        cnt = jnp.minimum(pos + 1, win).astype(F32)
        p = wsum / cnt - zcur
        y = _dot(p.astype(BF16), wp_ref[...])
        o_ref[:, D_A + D_B:D_A + D_B + D_C] = (y * ps_ref[...]).astype(BF16)

    in_specs = [cur(D_IN), prev(D_IN)] + [full(s) for s in _MIX_PARAM_SHAPES]
    outs, couts = _call(
        body, name=name, grid=(2, nt), in_specs=in_specs, out_specs=[cur(D_MODEL)],
        out_shape=[SDS((t, D_MODEL), BF16)],
        scratch_shapes=[pltpu.VMEM((SUB, HALO + tm, D_A), F32), pltpu.VMEM((HALO + tm, D_C), F32)],
        args=[z, z, *mp], pieces=pieces)
    return outs[0], couts


def mixer_bwd(z, dm, mp, *, seq, name, tm=256, pieces=()):
    t = z.shape[0]
    tm = min(tm, seq)
    nt = seq // tm
    ext = tm + HALO
    cur, prev, nxt, full = _mixer_specs(tm, nt, seq)
    grad_shapes = [(32, D_A), (1, D_A), (1, D_A), (1, D_A), (D_A, D_A), (1, D_B), (1, D_B),
                   (N_HEADS_B * CHUNK, CHUNK), (CHUNK, CHUNK), (D_C, D_C), (1, D_C)]

    def body(zc_ref, zp_ref, zn_ref, dmc_ref, dmn_ref, cw_ref, cb_ref, clg_ref, clb_ref, wpw_ref, slg_ref, slb_ref,
             ws_ref, bias_ref, wp_ref, ps_ref,
             dz_ref, dcw_ref, dcb_ref, dclg_ref, dclb_ref, dwpw_ref, dslg_ref, dslb_ref, dws_ref, dbs_ref, dwp_ref,
             dps_ref, ys_ref, dcs_ref, zs_ref, qs_ref):
        b = pl.program_id(0)
        i = pl.program_id(1)
        has_prev = i > 0
        has_next = i < nt - 1
        grads = [dcw_ref, dcb_ref, dclg_ref, dclb_ref, dwpw_ref, dslg_ref, dslb_ref, dws_ref, dbs_ref, dwp_ref, dps_ref]

        @pl.when((b == 0) & (i == 0))
        def _():
            for r in grads:
                r[...] = jnp.zeros_like(r)

        ext_row = lax.broadcasted_iota(jnp.int32, (ext, 1), 0)
        live = (ext_row < tm) | has_next

        yp = zp_ref[:, 0:D_A].astype(F32) * _sigmoid(zp_ref[:, D_A:2 * D_A].astype(F32))
        ys_ref[0, 0:HALO, :] = jnp.where(has_prev, yp, 0.0)
        a_cur = zc_ref[:, 0:D_A].astype(F32)
        sig_cur = _sigmoid(zc_ref[:, D_A:2 * D_A].astype(F32))
        ys_ref[0, HALO:HALO + tm, :] = a_cur * sig_cur
        yn = zn_ref[:, 0:D_A].astype(F32) * _sigmoid(zn_ref[:, D_A:2 * D_A].astype(F32))
        ys_ref[0, HALO + tm:HALO + tm + HALO, :] = jnp.where(has_next, yn, 0.0)
        _shifted_copies(ys_ref, ext + HALO - SUB)
        acc = jnp.zeros((ext, D_A), F32) + cb_ref[...]
        for k in range(CONV_WIDTH):
            acc = acc + cw_ref[k:k + 1, :] * _tap(ys_ref, HALO - (CONV_WIDTH - 1) + k, ext)
        ln, xhat, rstd = _layer_norm_fwd(acc, clg_ref[...], clb_ref[...])
        sg = _sigmoid(ln)
        sl = ln * sg
        dya = jnp.concatenate([dmc_ref[:, 0:D_A], dmn_ref[:, 0:D_A]], axis=0)
        dsl = _dot_nt(dya, wpw_ref[...])
        dln = dsl * sg * (1.0 + ln * (1.0 - sg))
        dc = _layer_norm_bwd(dln, xhat, rstd, clg_ref[...])
        dc = jnp.where(live, dc, 0.0)
        dcs_ref[0] = dc
        _shifted_copies(dcs_ref, ext - SUB)
        dwpw_ref[...] += _dot_tn(sl[:tm].astype(BF16), dya[:tm])
        dclg_ref[...] += jnp.sum(dln[:tm] * xhat[:tm], axis=0, keepdims=True)
        dclb_ref[...] += jnp.sum(dln[:tm], axis=0, keepdims=True)
        dcb_ref[...] += jnp.sum(dc[:tm], axis=0, keepdims=True)
        dy = jnp.zeros((tm, D_A), F32)
        for k in range(CONV_WIDTH):
            off = HALO - (CONV_WIDTH - 1) + k
            dcw_ref[k:k + 1, :] += jnp.sum(dc[:tm] * _tap(ys_ref, off, tm), axis=0, keepdims=True)
            dy = dy + cw_ref[k:k + 1, :] * _tap(dcs_ref, CONV_WIDTH - 1 - k, tm)
        dz_ref[:, 0:D_A] = (dy * sig_cur).astype(BF16)
        dz_ref[:, D_A:2 * D_A] = (dy * a_cur * sig_cur * (1.0 - sig_cur)).astype(BF16)

        zb = zc_ref[:, 2 * D_A:2 * D_A + 2 * D_B].astype(F32)
        gz = _gelu(zb)
        u = gz[:, :D_B]
        vl, vhat, vrstd = _layer_norm_fwd(gz[:, D_B:], slg_ref[...], slb_ref[...])
        dyb = dmc_ref[:, D_A:D_A + D_B].astype(F32)
        wsm = _causal_rows(ws_ref[...])
        masks = _head_masks((CHUNK, D_B))
        ds_all = dyb * u
        du_parts, dvl_parts = [], []
        for c in range(tm // CHUNK):
            rows = slice(c * CHUNK, (c + 1) * CHUNK)
            vlc = vl[rows].astype(BF16)
            s = _gate_mix(wsm, vl[rows], masks) + bias_ref[...]
            du_parts.append(dyb[rows] * s)
            ds = ds_all[rows]
            stack = jnp.concatenate([jnp.where(masks[h], ds, 0.0) for h in range(N_HEADS_B)], axis=0).astype(BF16)
            dvl_parts.append(_dot_tn(wsm, stack))
            dws_ref[...] += _dot_nt(stack, vlc)
        du = jnp.concatenate(du_parts, axis=0)
        dvl = jnp.concatenate(dvl_parts, axis=0)
        dbias = jnp.zeros((CHUNK, D_B), F32)
        for c in range(tm // CHUNK):
            dbias = dbias + ds_all[c * CHUNK:(c + 1) * CHUNK]
        lane = lax.broadcasted_iota(jnp.int32, (CHUNK, CHUNK), 1)
        dbs = jnp.zeros((CHUNK, CHUNK), F32)
        for h in range(N_HEADS_B):
            col = jnp.sum(jnp.where(masks[h], dbias, 0.0), axis=1, keepdims=True)
            dbs = dbs + jnp.where(lane == h, col, 0.0)
        dbs_ref[...] += dbs
        dslg_ref[...] += jnp.sum(dvl * vhat, axis=0, keepdims=True)
        dslb_ref[...] += jnp.sum(dvl, axis=0, keepdims=True)
        dv = _layer_norm_bwd(dvl, vhat, vrstd, slg_ref[...])
        gg = _gelu_grad(zb)
        dz_ref[:, 2 * D_A:2 * D_A + D_B] = (du * gg[:, :D_B]).astype(BF16)
        dz_ref[:, 2 * D_A + D_B:2 * D_A + 2 * D_B] = (dv * gg[:, D_B:]).astype(BF16)

        c0 = 2 * D_A + 2 * D_B
        m0 = D_A + D_B
        zs_ref[0:HALO, :] = jnp.where(has_prev, zp_ref[:, c0:c0 + D_C].astype(F32), 0.0)
        zcur = zc_ref[:, c0:c0 + D_C].astype(F32)
        zs_ref[HALO:HALO + tm, :] = zcur
        win = _pool_window((tm, D_C))
        wsum = jnp.zeros((tm, D_C), F32)
        for j in range(16):
            wsum = wsum + jnp.where(j < win, zs_ref[pl.ds(HALO - j, tm), :], 0.0)
        pos = i * tm + lax.broadcasted_iota(jnp.int32, (tm, D_C), 0)
        cnt = jnp.minimum(pos + 1, win).astype(F32)
        pb = (wsum / cnt - zcur).astype(BF16)
        y = _dot(pb, wp_ref[...])
        dyc = jnp.concatenate([dmc_ref[:, m0:m0 + D_C], dmn_ref[:, m0:m0 + D_C]], axis=0).astype(F32)
        dps_ref[...] += jnp.sum(dyc[:tm] * y, axis=0, keepdims=True)
        dyv = (dyc * ps_ref[...]).astype(BF16)
        dwp_ref[...] += _dot_tn(pb, dyv[:tm])
        dp = _dot_nt(dyv, wp_ref[...])
        win_e = _pool_window((ext, D_C))
        pos_e = i * tm + lax.broadcasted_iota(jnp.int32, (ext, D_C), 0)
        cnt_e = jnp.minimum(pos_e + 1, win_e).astype(F32)
        qs_ref[...] = jnp.where(live, dp / cnt_e, 0.0)
        dzc = -dp[:tm]
        for j in range(16):
            dzc = dzc + jnp.where(j < win, qs_ref[pl.ds(j, tm), :], 0.0)
        dz_ref[:, c0:c0 + D_C] = dzc.astype(BF16)

        @pl.when((b == 1) & (i == nt - 1))
        def _():
            dws_ref[...] = _causal_rows(dws_ref[...])

    in_specs = ([cur(D_IN), prev(D_IN), nxt(D_IN), cur(D_MODEL), nxt(D_MODEL)]
                + [full(s) for s in _MIX_PARAM_SHAPES])
    out_specs = [cur(D_IN)] + [full(s) for s in grad_shapes]
    out_shape = [SDS((t, D_IN), BF16)] + [SDS(s, F32) for s in grad_shapes]
    outs, couts = _call(
        body, name=name, grid=(2, nt), in_specs=in_specs, out_specs=out_specs, out_shape=out_shape,
        scratch_shapes=[pltpu.VMEM((SUB, HALO + tm + HALO, D_A), F32), pltpu.VMEM((SUB, ext, D_A), F32),
                        pltpu.VMEM((HALO + tm, D_C), F32), pltpu.VMEM((ext, D_C), F32)],
        args=[z, z, z, dm, dm, *mp], pieces=pieces)
    return outs, couts


MIXER_SMALL = ["conv_w", "conv_b", "conv_ln_g", "conv_ln_b", "w_pw", "sg_ln_g", "sg_ln_b", "w_s", "b_s", "w_pool",
               "pool_scale"]


def _mixer_params(p, w_pw_bf16, l):
    wp_bd = jnp.zeros((D_C, D_C), F32)
    for g in range(D_C // GROUP_DIM_C):
        sl = slice(g * GROUP_DIM_C, (g + 1) * GROUP_DIM_C)
        wp_bd = wp_bd.at[sl, sl].set(p["w_pool"][l, g])
    return [
        jnp.pad(p["conv_w"][l], ((0, 32 - CONV_WIDTH), (0, 0))),
        p["conv_b"][l][None], p["conv_ln_g"][l][None], p["conv_ln_b"][l][None],
        w_pw_bf16,
        p["sg_ln_g"][l][None], p["sg_ln_b"][l][None],
        p["w_s"][l].reshape(N_HEADS_B * CHUNK, CHUNK).astype(BF16),
        jnp.repeat(p["b_s"][l].T, HEAD_DIM_B, axis=1),
        wp_bd.astype(BF16),
        p["pool_scale"][l][None],
    ]


def _mixer_grads(g):
    dcw, dcb, dclg, dclb, dwpw, dslg, dslb, dws, dbs, dwp, dps = g
    blocks = [dwp[i * GROUP_DIM_C:(i + 1) * GROUP_DIM_C, i * GROUP_DIM_C:(i + 1) * GROUP_DIM_C]
              for i in range(D_C // GROUP_DIM_C)]
    return [dcw[:CONV_WIDTH], dcb[0], dclg[0], dclb[0], dwpw, dslg[0], dslb[0],
            dws.reshape(N_HEADS_B, CHUNK, CHUNK), dbs[:, :N_HEADS_B].T, jnp.stack(blocks), dps[0]]


def _row_tile(r, cap=512):
    best = r
    for d in range(16, min(r, cap) + 1, 16):
        if r % d == 0:
            best = d
    return best if best <= cap else r


def add_core_halves(g, r1, core, *, name):
    _, _, r, c = g.shape
    tr = _row_tile(r)

    def body(core_ref, g_ref, r_ref, o_ref):
        o_ref[...] = (g_ref[...].astype(F32) + r_ref[...].astype(F32)).astype(o_ref.dtype)

    outs, _ = _call(
        body, name=name, grid=(N_CHIPS, r // tr), prefetch=1,
        in_specs=[pl.BlockSpec((None, None, tr, c), lambda j, i, s: (j, s[0], i, 0)),
                  pl.BlockSpec((None, tr, c), lambda j, i, s: (j, i, 0))],
        out_specs=[pl.BlockSpec((None, tr, c), lambda j, i, s: (j, i, 0))],
        out_shape=[SDS((N_CHIPS, r, c), g.dtype)], args=[core, g, r1])
    return outs[0]


def sum_chips(h, r2, place, *, name):
    _, r, c = h.shape
    tr = _row_tile(r, 256)

    def body(place_ref, h_ref, a_ref, b_ref, c_ref, o_ref):
        acc = h_ref[...].astype(F32) + a_ref[...].astype(F32)
        acc = acc + b_ref[...].astype(F32)
        o_ref[...] = acc + c_ref[...].astype(F32)

    def blk(k):
        return pl.BlockSpec((None, tr, c), lambda i, s: (jnp.bitwise_xor(s[0], k), i, 0))

    outs, _ = _call(
        body, name=name, grid=(r // tr,), prefetch=1, in_specs=[blk(0), blk(1), blk(2), blk(3)],
        out_specs=[pl.BlockSpec((None, tr, c), lambda i, s: (s[1], i, 0))],
        out_shape=[SDS((2, r, c), F32)], args=[place, h, r2, r2, r2])
    return outs[0]


def allreduce_small(p, *, name):
    _, n, _ = p.shape

    def body(p_ref, o_ref, land_ref, send1, recv1, send2, recv2):
        x, y, c = lax.axis_index("x"), lax.axis_index("y"), lax.axis_index("c")
        me = 4 * x + 2 * y + c

        def peer(r):
            return ((1 - x) if r & 4 else x, (1 - y) if r & 2 else y, (1 - c) if r & 1 else c)

        def index(r):
            px, py, pc = peer(r)
            return 4 * px + 2 * py + pc

        land_ref[me] = p_ref[me]
        first = [_remote(p_ref.at[index(r)], land_ref.at[me], send1.at[r - 1], recv1.at[r - 1], peer(r))
                 for r in range(1, N_DEV)]
        for cp in first:
            cp.start()
        for r in range(1, N_DEV):
            blk = land_ref.at[index(r)]
            _remote(blk, blk, send1.at[r - 1], recv1.at[r - 1], peer(r)).wait_recv()
        acc = land_ref[0]
        for d in range(1, N_DEV):
            acc = acc + land_ref[d]
        o_ref[me] = acc
        second = [_remote(o_ref.at[me], o_ref.at[me], send2.at[r - 1], recv2.at[r - 1], peer(r))
                  for r in range(1, N_DEV)]
        for cp in second:
            cp.start()
        for r in range(1, N_DEV):
            blk = o_ref.at[index(r)]
            _remote(blk, blk, send2.at[r - 1], recv2.at[r - 1], peer(r)).wait_recv()
        for cp in first + second:
            cp.wait_send()

    return pl.pallas_call(
        body, name=name, out_shape=SDS(p.shape, F32),
        in_specs=[pl.BlockSpec(memory_space=pltpu.VMEM)], out_specs=pl.BlockSpec(memory_space=pltpu.VMEM),
        scratch_shapes=[pltpu.VMEM(p.shape, F32)] + [pltpu.SemaphoreType.DMA((N_DEV - 1,))] * 4,
        interpret=_INTERPRET,
    )(p)


def _adam_update(w, g, m, v):
    m_new = ADAM_B1 * m + (1.0 - ADAM_B1) * g
    v_new = ADAM_B2 * v + (1.0 - ADAM_B2) * (g * g)
    m_hat = m_new / (1.0 - ADAM_B1 ** ADAM_STEP)
    v_hat = v_new / (1.0 - ADAM_B2 ** ADAM_STEP)
    return -ADAM_LR * (m_hat / (jnp.sqrt(v_hat) + ADAM_EPS) + ADAM_WD * w), m_new, v_new


def adamw(w, g, m, v, *, name, pieces=()):
    nl, r, c = w.shape
    tr = _row_tile(r, 256)

    def body(w_ref, g_ref, m_ref, v_ref, d_ref, mo_ref, vo_ref):
        d_ref[...], mo_ref[...], vo_ref[...] = _adam_update(w_ref[...], g_ref[...], m_ref[...], v_ref[...])

    blk = pl.BlockSpec((None, tr, c), lambda l, i: (l, i, 0))
    return _call(body, name=name, grid=(nl, r // tr), in_specs=[blk] * 4, out_specs=[blk] * 3,
                 out_shape=[SDS(w.shape, F32)] * 3, args=[w, g, m, v], pieces=pieces)


def adamw_small(ws, gs, ms, vs, *, name):
    n = len(ws)

    def body(*refs):
        for i in range(n):
            w_ref, g_ref, m_ref, v_ref = (refs[k * n + i] for k in range(4))
            d, mn, vn = _adam_update(w_ref[...], g_ref[...], m_ref[...], v_ref[...])
            refs[4 * n + i][...] = d
            refs[5 * n + i][...] = mn
            refs[6 * n + i][...] = vn

    vm = pl.BlockSpec(memory_space=pltpu.VMEM)
    res = pl.pallas_call(
        body, name=name, in_specs=[vm] * (4 * n), out_specs=[vm] * (3 * n),
        out_shape=[SDS(w.shape, F32) for w in ws] * 3, interpret=_INTERPRET,
    )(*ws, *gs, *ms, *vs)
    return res[:n], res[n:2 * n], res[2 * n:]


WEIGHTS = ["norm1_g", "w_in", "conv_w", "conv_b", "conv_ln_g", "conv_ln_b", "w_pw", "sg_ln_g", "sg_ln_b", "w_s",
           "b_s", "w_pool", "pool_scale", "w_out", "norm2_g", "w_gate_up", "w_down", "final_g"]
BIG = ["w_in", "w_pw", "w_out", "w_gate_up", "w_down"]
TRANSPOSED = {"w_in": True, "w_pw": False, "w_out": False, "w_gate_up": True, "w_down": False}
SMALL = [k for k in WEIGHTS if k not in BIG]


def _wire(a, transposed):
    if transposed:
        a = a.transpose(0, 2, 1)
    return [a[l].reshape(2, a.shape[1] // 2, a.shape[2]) for l in range(a.shape[0])]


def _pack(arrays):
    flat = jnp.concatenate([a.reshape(-1) for a in arrays])
    n = -(-flat.shape[0] // (N_DEV * LANES * 8)) * 8
    return jnp.pad(flat, (0, N_DEV * n * LANES - flat.shape[0])).reshape(N_DEV, n, LANES)


def _unpack(packed, shapes):
    flat = packed.reshape(-1)
    out, off = [], 0
    for s in shapes:
        size = math.prod(s)
        out.append(flat[off:off + size].reshape(s))
        off += size
    return out


def kernel(x, norm1_g, w_in, conv_w, conv_b, conv_ln_g, conv_ln_b, w_pw, sg_ln_g, sg_ln_b, w_s, b_s, w_pool, pool_scale, w_out, norm2_g, w_gate_up, w_down, final_g, loss_target, m_norm1_g, m_w_in, m_conv_w, m_conv_b, m_conv_ln_g, m_conv_ln_b, m_w_pw, m_sg_ln_g, m_sg_ln_b, m_w_s, m_b_s, m_w_pool, m_pool_scale, m_w_out, m_norm2_g, m_w_gate_up, m_w_down, m_final_g, v_norm1_g, v_w_in, v_conv_w, v_conv_b, v_conv_ln_g, v_conv_ln_b, v_w_pw, v_sg_ln_g, v_sg_ln_b, v_w_s, v_b_s, v_w_pool, v_pool_scale, v_w_out, v_norm2_g, v_w_gate_up, v_w_down, v_final_g):
    w = dict(norm1_g=norm1_g, w_in=w_in, conv_w=conv_w, conv_b=conv_b, conv_ln_g=conv_ln_g, conv_ln_b=conv_ln_b,
             w_pw=w_pw, sg_ln_g=sg_ln_g, sg_ln_b=sg_ln_b, w_s=w_s, b_s=b_s, w_pool=w_pool, pool_scale=pool_scale,
             w_out=w_out, norm2_g=norm2_g, w_gate_up=w_gate_up, w_down=w_down, final_g=final_g)
    m = dict(norm1_g=m_norm1_g, w_in=m_w_in, conv_w=m_conv_w, conv_b=m_conv_b, conv_ln_g=m_conv_ln_g,
             conv_ln_b=m_conv_ln_b, w_pw=m_w_pw, sg_ln_g=m_sg_ln_g, sg_ln_b=m_sg_ln_b, w_s=m_w_s, b_s=m_b_s,
             w_pool=m_w_pool, pool_scale=m_pool_scale, w_out=m_w_out, norm2_g=m_norm2_g, w_gate_up=m_w_gate_up,
             w_down=m_w_down, final_g=m_final_g)
    v = dict(norm1_g=v_norm1_g, w_in=v_w_in, conv_w=v_conv_w, conv_b=v_conv_b, conv_ln_g=v_conv_ln_g,
             conv_ln_b=v_conv_ln_b, w_pw=v_w_pw, sg_ln_g=v_sg_ln_g, sg_ln_b=v_sg_ln_b, w_s=v_w_s, b_s=v_b_s,
             w_pool=v_w_pool, pool_scale=v_pool_scale, w_out=v_w_out, norm2_g=v_norm2_g, w_gate_up=v_w_gate_up,
             w_down=v_w_down, final_g=v_final_g)
    bsz, seq, d = x.shape
    t = bsz * seq
    chip = 2 * lax.axis_index("x") + lax.axis_index("y")
    core = lax.axis_index("c")
    core_arr = jnp.reshape(core, (1,)).astype(jnp.int32)
    place_arr = jnp.stack([chip, core]).astype(jnp.int32)

    own = {k: _wire(w[k].astype(BF16), TRANSPOSED[k]) for k in BIG}
    cw_cols = conv_w.shape[2]
    side = jnp.pad(conv_w, ((0, 0), (0, 32 - CONV_WIDTH), (0, 0)))

    def complete(g, mine):
        g = lax.dynamic_update_index_in_dim(g, mine, chip, 0)
        return g.reshape(-1, g.shape[-1])

    first = comm_only([gather_both(own["w_in"][0]), gather_both(own["w_pw"][0]), gather_both(own["w_pw"][1]),
                       gather_both(side)], name="gather_first")
    full = {("w_in", 0): complete(first[0][0], own["w_in"][0]),
            ("w_pw", 0): complete(first[1][0], own["w_pw"][0]),
            ("w_pw", 1): complete(first[2][0], own["w_pw"][1])}
    side_all = lax.dynamic_update_index_in_dim(first[3][0], side, chip, 0)
    p = dict(w)
    p["conv_w"] = side_all[:, :, :CONV_WIDTH, :].transpose(1, 2, 0, 3).reshape(DEPTH, CONV_WIDTH, N_CHIPS * cw_cols)

    xs = [x.reshape(t, d)]
    saved = []
    pend = {}
    for l in range(DEPTH):
        x0 = xs[-1]
        mp = _mixer_params(p, full[("w_pw", l)], l)
        if l == 0:
            h1, (a,) = rmsnorm_fwd(x0, p["norm1_g"][l][None], name=f"norm1_fwd_{l}",
                                   pieces=[gather_ici(own["w_out"][0])])
            pend["w_out", 0] = a[0]
            z, (b, a) = mm_nt(h1, full["w_in", l], name=f"in_proj_{l}", tm=512, tn=D_IN, out_dtype=BF16,
                              pieces=[gather_d2d(pend.pop(("w_out", 0))), gather_ici(own["w_down"][0])])
            full["w_out", 0] = complete(b[0], own["w_out"][0])
            pend["w_down", 0] = a[0]
            mc, (b, a) = mixer_fwd(z, mp, seq=seq, name=f"mixer_fwd_{l}",
                                   pieces=[gather_d2d(pend.pop(("w_down", 0))), gather_ici(own["w_gate_up"][0])])
            full["w_down", 0] = complete(b[0], own["w_down"][0])
            pend["w_gate_up", 0] = a[0]
            x1, (b, a) = mm_nn(mc, full["w_out", l], name=f"out_proj_{l}", tm=512, tn=D_MODEL, out_dtype=F32,
                               residual=x0,
                               pieces=[gather_d2d(pend.pop(("w_gate_up", 0))), gather_ici(own["w_in"][1])])
            full["w_gate_up", 0] = complete(b[0], own["w_gate_up"][0])
            pend["w_in", 1] = a[0]
            h2, (b, a) = rmsnorm_fwd(x1, p["norm2_g"][l][None], name=f"norm2_fwd_{l}",
                                     pieces=[gather_d2d(pend.pop(("w_in", 1))), gather_ici(own["w_out"][1])])
            full["w_in", 1] = complete(b[0], own["w_in"][1])
            pend["w_out", 1] = a[0]
            (gu, act), (b, a) = swiglu_fwd(h2, full["w_gate_up", l], name=f"swiglu_fwd_{l}", tm=512, tn=1408,
                                           pieces=[gather_d2d(pend.pop(("w_out", 1))),
                                                   gather_ici(own["w_gate_up"][1])])
            full["w_out", 1] = complete(b[0], own["w_out"][1])
            pend["w_gate_up", 1] = a[0]
            x2, (b, a) = mm_nn(act, full["w_down", l], name=f"down_proj_{l}", tm=512, tn=D_MODEL, out_dtype=F32,
                               residual=x1,
                               pieces=[gather_d2d(pend.pop(("w_gate_up", 1))), gather_ici(own["w_down"][1])])
            full["w_gate_up", 1] = complete(b[0], own["w_gate_up"][1])
            pend["w_down", 1] = a[0]
        else:
            h1, (b,) = rmsnorm_fwd(x0, p["norm1_g"][l][None], name=f"norm1_fwd_{l}",
                                   pieces=[gather_d2d(pend.pop(("w_down", 1)))])
            full["w_down", 1] = complete(b[0], own["w_down"][1])
            z, _ = mm_nt(h1, full["w_in", l], name=f"in_proj_{l}", tm=512, tn=D_IN, out_dtype=BF16)
            mc, _ = mixer_fwd(z, mp, seq=seq, name=f"mixer_fwd_{l}")
            x1, _ = mm_nn(mc, full["w_out", l], name=f"out_proj_{l}", tm=512, tn=D_MODEL, out_dtype=F32, residual=x0)
            h2, _ = rmsnorm_fwd(x1, p["norm2_g"][l][None], name=f"norm2_fwd_{l}")
            (gu, act), _ = swiglu_fwd(h2, full["w_gate_up", l], name=f"swiglu_fwd_{l}", tm=512, tn=1408)
            x2, _ = mm_nn(act, full["w_down", l], name=f"down_proj_{l}", tm=512, tn=D_MODEL, out_dtype=F32,
                          residual=x1)
        saved.append((x0, h1, z, mc, x1, h2, gu, act, mp))
        xs.append(x2)

    loss, dx, dxb, d_final_g = loss_head(xs[-1], p["final_g"][None], loss_target.reshape(t, d), name="loss_head")

    small = {k: [None] * DEPTH for k in SMALL if k != "final_g"}
    reduced = {}
    carry = None

    def halves(g):
        return g.reshape(N_CHIPS, 2, g.shape[0] // (2 * N_CHIPS), g.shape[1])

    for l in reversed(range(DEPTH)):
        x0, h1, z, mc, x1, h2, gu, act, mp = saved[l]
        pieces = [exchange_d2d(carry[0]), reduce_ici(carry[1])] if carry else []
        g_down, co = mm_tn(act, dxb, name=f"down_proj_dw_{l}", tk=D_FF // 2, tn=D_MODEL, tt=TT, out_dtype=BF16,
                           pieces=pieces)
        g_down = halves(g_down)
        pieces = [reduce_d2d(g_down)]
        if carry:
            reduced["w_pw", l + 1] = co[0][0]
            s_in = sum_chips(carry[1], co[1][0], place_arr, name=f"sum_chips_w_in_{l + 1}")
            pieces.append(exchange_d2d(s_in))
        dgu, co = swiglu_bwd(dxb, full["w_down", l], gu, name=f"swiglu_bwd_{l}", tm=512, tn=1408, pieces=pieces)
        if carry:
            reduced["w_in", l + 1] = co[1][0]
        h_down = add_core_halves(g_down, co[0][0], core_arr, name=f"add_cores_w_down_{l}")
        g_gu, (r2,) = mm_gu_tn(dgu, h2, name=f"gate_up_dw_{l}", tn=1408, tt=TT, pieces=[reduce_ici(h_down)])
        s_down = sum_chips(h_down, r2[0], place_arr, name=f"sum_chips_w_down_{l}")
        g_gu = halves(g_gu)
        (dx, dxb, dn2), (e, r1) = mm_norm_bwd(dgu, full["w_gate_up", l], x1, p["norm2_g"][l][None], dx,
                                              name=f"gate_up_dx_{l}", tm=256,
                                              pieces=[exchange_d2d(s_down), reduce_d2d(g_gu)])
        reduced["w_down", l] = e[0]
        small["norm2_g"][l] = dn2[0]
        h_gu = add_core_halves(g_gu, r1[0], core_arr, name=f"add_cores_w_gate_up_{l}")
        g_out, _ = mm_tn(mc, dxb, name=f"out_proj_dw_{l}", tk=D_MODEL, tn=D_MODEL, tt=TT, out_dtype=BF16)
        g_out = halves(g_out)
        dmc, (r1,) = mm_nt(dxb, full["w_out", l], name=f"out_proj_dx_{l}", tm=512, tn=D_MODEL, out_dtype=BF16,
                           pieces=[reduce_d2d(g_out)])
        h_out = add_core_halves(g_out, r1[0], core_arr, name=f"add_cores_w_out_{l}")
        (dz, *mg), (r2a, r2b) = mixer_bwd(z, dmc, mp, seq=seq, name=f"mixer_bwd_{l}",
                                          pieces=[reduce_ici(h_gu), reduce_ici(h_out)])
        s_gu = sum_chips(h_gu, r2a[0], place_arr, name=f"sum_chips_w_gate_up_{l}")
        s_out = sum_chips(h_out, r2b[0], place_arr, name=f"sum_chips_w_out_{l}")
        mgrads = dict(zip(MIXER_SMALL, _mixer_grads(mg)))
        for k in MIXER_SMALL:
            if k != "w_pw":
                small[k][l] = mgrads[k]
        g_pw = halves(mgrads["w_pw"].astype(BF16))
        g_in, (ea, eb, r1) = mm_tn(dz, h1, name=f"in_proj_dw_{l}", tk=D_IN // 2, tn=D_MODEL, tt=TT, out_dtype=BF16,
                                   pieces=[exchange_d2d(s_gu), exchange_d2d(s_out), reduce_d2d(g_pw)])
        reduced["w_gate_up", l], reduced["w_out", l] = ea[0], eb[0]
        h_pw = add_core_halves(g_pw, r1[0], core_arr, name=f"add_cores_w_pw_{l}")
        g_in = halves(g_in)
        (dx, dxb, dn1), (r2, r1) = mm_norm_bwd(dz, full["w_in", l], x0, p["norm1_g"][l][None], dx,
                                               name=f"in_proj_dx_{l}", tm=512,
                                               pieces=[reduce_ici(h_pw), reduce_d2d(g_in)])
        small["norm1_g"][l] = dn1[0]
        s_pw = sum_chips(h_pw, r2[0], place_arr, name=f"sum_chips_w_pw_{l}")
        h_in = add_core_halves(g_in, r1[0], core_arr, name=f"add_cores_w_in_{l}")
        carry = (s_pw, h_in)
    grad_x = dx.reshape(bsz, seq, d)

    g_small = [jnp.stack(small[k]) if k != "final_g" else d_final_g[0] for k in SMALL]
    small_shapes = [g.shape for g in g_small] + [(1,)]
    summed = _unpack(allreduce_small(_pack(g_small + [loss.reshape(1)]), name="allreduce_small"), small_shapes)
    loss = summed[-1][0]
    grad = dict(zip(SMALL, summed[:-1]))
    grad["conv_w"] = lax.dynamic_slice_in_dim(grad["conv_w"], chip * cw_cols, cw_cols, axis=2)

    delta, new_m, new_v = {}, {}, {}

    def big_step(k, pieces=()):
        g = jnp.stack([reduced[k, l].reshape(-1, reduced[k, l].shape[-1]) for l in range(DEPTH)])
        grad[k] = g.transpose(0, 2, 1) if TRANSPOSED[k] else g
        (delta[k], new_m[k], new_v[k]), co = adamw(w[k], grad[k], m[k], v[k], name=f"adamw_{k}", pieces=pieces)
        return co

    s_pw, h_in = carry
    co = big_step("w_gate_up", [exchange_d2d(s_pw), reduce_ici(h_in)])
    reduced["w_pw", 0] = co[0][0]
    s_in = sum_chips(h_in, co[1][0], place_arr, name="sum_chips_w_in_0")
    reduced["w_in", 0] = big_step("w_down", [exchange_d2d(s_in)])[0][0]
    for k in ["w_out", "w_pw", "w_in"]:
        big_step(k)

    def flat2(a):
        return a.reshape(-1, a.shape[-1])

    res = adamw_small(*[[flat2(tt[k]) for k in SMALL] for tt in (w, grad, m, v)], name="adamw_small")
    for out, arrs in zip((delta, new_m, new_v), res):
        out.update({k: a.reshape(w[k].shape) for k, a in zip(SMALL, arrs)})
    return (loss, grad_x, *[grad[k] for k in WEIGHTS], *[delta[k] for k in WEIGHTS],
            *[new_m[k] for k in WEIGHTS], *[new_v[k] for k in WEIGHTS])
```

```python
import functools
import math

import jax
import jax.numpy as jnp
from jax import lax
from jax.experimental import pallas as pl
from jax.experimental.pallas import tpu as pltpu

F32 = jnp.float32
BF16 = jnp.bfloat16

D_MODEL = 1024
DEPTH = 2
D_A = 384
D_B = 384
D_C = 256
D_IN = 2 * D_A + 2 * D_B + D_C
N_HEADS_B = 4
HEAD_DIM_B = 96
GROUP_DIM_C = 64
CONV_WIDTH = 31
CHUNK = 128
D_FF = 2816
RMS_EPS = 1e-6
LN_EPS = 1e-5
HALO = 32
TT = 2048
N_CHIPS = 4
N_DEV = 8
LANES = 128

ADAM_LR = 0.001
ADAM_B1 = 0.9
ADAM_B2 = 0.999
ADAM_EPS = 1e-08
ADAM_WD = 0.01
ADAM_STEP = 10

VMEM_LIMIT = 56 * 1024 * 1024
_INTERPRET = False

MESH = pl.DeviceIdType.MESH
ANY = pl.BlockSpec(memory_space=pl.ANY)
SDS = jax.ShapeDtypeStruct


def _sigmoid(x):
    return 0.5 * jnp.tanh(0.5 * x) + 0.5


_GELU_C = math.sqrt(2.0 / math.pi)


def _gelu(x):
    return 0.5 * x * (1.0 + jnp.tanh(_GELU_C * (x + 0.044715 * x * x * x)))


def _gelu_grad(x):
    t = jnp.tanh(_GELU_C * (x + 0.044715 * x * x * x))
    return 0.5 * (1.0 + t) + 0.5 * x * (1.0 - t * t) * _GELU_C * (1.0 + 3 * 0.044715 * x * x)


def _dot(a, b):
    return jnp.dot(a, b, preferred_element_type=F32)


def _dot_nt(a, b):
    return lax.dot_general(a, b, (((1,), (1,)), ((), ())), preferred_element_type=F32)


def _dot_tn(a, b):
    return lax.dot_general(a, b, (((0,), (0,)), ((), ())), preferred_element_type=F32)


class Piece:
    def __init__(self, operands, out_shapes, aliases, n_sems, start, finish):
        self.operands, self.out_shapes, self.aliases, self.n_sems = operands, out_shapes, aliases, n_sems
        self.start, self.finish = start, finish


def _place():
    x, y, c = lax.axis_index("x"), lax.axis_index("y"), lax.axis_index("c")
    chips = [(1 - x, y), (x, 1 - y), (1 - x, 1 - y)]
    return x, y, c, chips


def _remote(src, dst, send_sem, recv_sem, to):
    return pltpu.make_async_remote_copy(src_ref=src, dst_ref=dst, send_sem=send_sem, recv_sem=recv_sem,
                                        device_id=to, device_id_type=MESH)


def gather_ici(src):
    def copies(ins, outs, sem):
        x, y, c, chips = _place()
        j = 2 * x + y
        return [(_remote(ins[0].at[c], outs[0].at[j, c], sem(k), sem(3 + k), (px, py, c)),
                 outs[0].at[2 * px + py, c], (px, py, c)) for k, (px, py) in enumerate(chips)]

    def start(ins, outs, sem):
        for cp, _, _ in copies(ins, outs, sem):
            cp.start()

    def finish(ins, outs, sem):
        cps = copies(ins, outs, sem)
        for k, (_, landed, frm) in enumerate(cps):
            _remote(landed, landed, sem(k), sem(3 + k), frm).wait_recv()
        for cp, _, _ in cps:
            cp.wait_send()

    return Piece([src], [SDS((N_CHIPS,) + src.shape, src.dtype)], {}, 6, start, finish)


def gather_d2d(g):
    def copies(outs, sem):
        x, y, c, chips = _place()
        return [(_remote(outs[0].at[2 * px + py, c], outs[0].at[2 * px + py, c], sem(k), sem(3 + k), (x, y, 1 - c)),
                 outs[0].at[2 * px + py, 1 - c], (x, y, 1 - c)) for k, (px, py) in enumerate(chips)]

    def start(ins, outs, sem):
        for cp, _, _ in copies(outs, sem):
            cp.start()

    def finish(ins, outs, sem):
        cps = copies(outs, sem)
        for k, (_, landed, frm) in enumerate(cps):
            _remote(landed, landed, sem(k), sem(3 + k), frm).wait_recv()
        for cp, _, _ in cps:
            cp.wait_send()

    return Piece([g], [SDS(g.shape, g.dtype)], {0: 0}, 6, start, finish)


def gather_both(src):
    a = gather_ici(src)

    def finish(ins, outs, sem):
        a.finish(ins, outs, lambda k: sem(k))
        x, y, c, chips = _place()
        cps = [_remote(outs[0].at[2 * px + py, c], outs[0].at[2 * px + py, c], sem(6 + k), sem(9 + k), (x, y, 1 - c))
               for k, (px, py) in enumerate(chips)]
        for cp in cps:
            cp.start()
        for k, (px, py) in enumerate(chips):
            blk = outs[0].at[2 * px + py, 1 - c]
            _remote(blk, blk, sem(6 + k), sem(9 + k), (x, y, 1 - c)).wait_recv()
        for cp in cps:
            cp.wait_send()

    return Piece(a.operands, a.out_shapes, {}, 12, a.start, finish)


def reduce_d2d(g):
    def copies(ins, outs, sem):
        x, y, c, _ = _place()
        return [_remote(ins[0].at[j, 1 - c], outs[0].at[j], sem(j), sem(4 + j), (x, y, 1 - c)) for j in range(N_CHIPS)]

    def start(ins, outs, sem):
        for cp in copies(ins, outs, sem):
            cp.start()

    def finish(ins, outs, sem):
        cps = copies(ins, outs, sem)
        for cp in cps:
            cp.wait_recv()
        for cp in cps:
            cp.wait_send()

    return Piece([g], [SDS((N_CHIPS,) + g.shape[2:], g.dtype)], {}, 8, start, finish)


def reduce_ici(h):
    def copies(ins, outs, sem):
        x, y, c, chips = _place()
        j = 2 * x + y
        return [(_remote(ins[0].at[2 * px + py], outs[0].at[j], sem(k), sem(3 + k), (px, py, c)),
                 outs[0].at[2 * px + py], (px, py, c)) for k, (px, py) in enumerate(chips)]

    def start(ins, outs, sem):
        for cp, _, _ in copies(ins, outs, sem):
            cp.start()

    def finish(ins, outs, sem):
        cps = copies(ins, outs, sem)
        for k, (_, landed, frm) in enumerate(cps):
            _remote(landed, landed, sem(k), sem(3 + k), frm).wait_recv()
        for cp, _, _ in cps:
            cp.wait_send()

    return Piece([h], [SDS(h.shape, h.dtype)], {}, 6, start, finish)


def exchange_d2d(g):
    def copy(outs, sem):
        x, y, c, _ = _place()
        return _remote(outs[0].at[c], outs[0].at[c], sem(0), sem(1), (x, y, 1 - c)), outs[0].at[1 - c], (x, y, 1 - c)

    def start(ins, outs, sem):
        copy(outs, sem)[0].start()

    def finish(ins, outs, sem):
        cp, landed, frm = copy(outs, sem)
        _remote(landed, landed, sem(0), sem(1), frm).wait_recv()
        cp.wait_send()

    return Piece([g], [SDS(g.shape, g.dtype)], {0: 0}, 2, start, finish)


def _call(body, *, name, grid, in_specs, out_specs, out_shape, args, scratch_shapes=(), pieces=(), prefetch=0):
    n_in, n_out, n_scr = len(in_specs), len(out_specs), len(scratch_shapes)
    c_ops = [a for p in pieces for a in p.operands]
    c_outs = [s for p in pieces for s in p.out_shapes]
    n_sems = sum(p.n_sems for p in pieces)
    aliases = {}
    op_off, out_off = prefetch + n_in, n_out
    for p in pieces:
        for i, o in p.aliases.items():
            aliases[op_off + i] = out_off + o
        op_off += len(p.operands)
        out_off += len(p.out_shapes)

    def wrapped(*refs):
        pre, refs = refs[:prefetch], refs[prefetch:]
        ins, cin = refs[:n_in], refs[n_in:n_in + len(c_ops)]
        o0 = n_in + len(c_ops)
        outs, cout = refs[o0:o0 + n_out], refs[o0 + n_out:o0 + n_out + len(c_outs)]
        s0 = o0 + n_out + len(c_outs)
        scr = refs[s0:s0 + n_scr]

        def each(method):
            sems = refs[s0 + n_scr]
            i_off = o_off = s_off = 0
            for p in pieces:
                getattr(p, method)(cin[i_off:i_off + len(p.operands)], cout[o_off:o_off + len(p.out_shapes)],
                                   functools.partial(lambda k, base: sems.at[base + k], base=s_off))
                i_off, o_off, s_off = i_off + len(p.operands), o_off + len(p.out_shapes), s_off + p.n_sems

        if pieces and grid:
            ids = [pl.program_id(a) for a in range(len(grid))]
            first = functools.reduce(jnp.logical_and, [i == 0 for i in ids])
            last = functools.reduce(jnp.logical_and, [i == g - 1 for i, g in zip(ids, grid)])
            pl.when(first)(lambda: each("start"))
        elif pieces:
            each("start")
        if body is not None:
            body(*pre, *ins, *outs, *scr)
        if pieces and grid:
            pl.when(last)(lambda: each("finish"))
        elif pieces:
            each("finish")

    scratch = list(scratch_shapes) + ([pltpu.SemaphoreType.DMA((n_sems,))] if pieces else [])
    all_in = list(in_specs) + [ANY] * len(c_ops)
    all_out = list(out_specs) + [ANY] * len(c_outs)
    shapes = list(out_shape) + c_outs
    kw = dict(name=name, out_shape=shapes, input_output_aliases=aliases, interpret=_INTERPRET)
    if grid:
        kw["compiler_params"] = pltpu.CompilerParams(dimension_semantics=("arbitrary",) * len(grid),
                                                     vmem_limit_bytes=VMEM_LIMIT)
    if prefetch:
        kw["grid_spec"] = pltpu.PrefetchScalarGridSpec(num_scalar_prefetch=prefetch, grid=grid, in_specs=all_in,
                                                       out_specs=all_out, scratch_shapes=scratch)
    else:
        kw.update(in_specs=all_in, out_specs=all_out, scratch_shapes=scratch)
        if grid:
            kw["grid"] = grid
    res = pl.pallas_call(wrapped, **kw)(*args, *c_ops)
    outs, rest = list(res[:n_out]), list(res[n_out:])
    couts = []
    for p in pieces:
        couts.append(rest[:len(p.out_shapes)])
        rest = rest[len(p.out_shapes):]
    return outs, couts


def comm_only(pieces, *, name):
    return _call(None, name=name, grid=(), in_specs=[], out_specs=[], out_shape=[], args=[], pieces=pieces)[1]


def rmsnorm_fwd(x, g, *, name, pieces=()):
    t, d = x.shape
    tm = min(512, t)

    def body(x_ref, g_ref, o_ref):
        xv = x_ref[...]
        rstd = lax.rsqrt(jnp.mean(xv * xv, axis=-1, keepdims=True) + RMS_EPS)
        o_ref[...] = (xv * rstd * g_ref[...]).astype(BF16)

    outs, couts = _call(
        body, name=name, grid=(t // tm,),
        in_specs=[pl.BlockSpec((tm, d), lambda i: (i, 0)), pl.BlockSpec((1, d), lambda i: (0, 0))],
        out_specs=[pl.BlockSpec((tm, d), lambda i: (i, 0))], out_shape=[SDS((t, d), BF16)],
        args=[x, g], pieces=pieces)
    return outs[0], couts


def loss_head(x, g, target, *, name):
    t, d = x.shape
    tm = min(512, t)

    def body(x_ref, g_ref, t_ref, loss_ref, dx_ref, dxb_ref, dg_ref):
        i = pl.program_id(0)
        xv = x_ref[...]
        gv = g_ref[...]
        rstd = lax.rsqrt(jnp.mean(xv * xv, axis=-1, keepdims=True) + RMS_EPS)
        xhat = xv * rstd
        err = xhat * gv - t_ref[...]
        dy = err * (1.0 / d)
        dxhat = dy * gv
        dx = rstd * (dxhat - xhat * jnp.mean(dxhat * xhat, axis=-1, keepdims=True))
        dx_ref[...] = dx
        dxb_ref[...] = dx.astype(BF16)

        @pl.when(i == 0)
        def _():
            dg_ref[...] = jnp.zeros_like(dg_ref)
            loss_ref[...] = jnp.zeros_like(loss_ref)

        dg_ref[...] += jnp.sum(dy * xhat, axis=0, keepdims=True)
        per_tok = jnp.sum(err * err, axis=-1, keepdims=True) * (0.5 / d)
        loss_ref[...] += jnp.sum(per_tok, axis=0, keepdims=True)

    row = pl.BlockSpec((tm, d), lambda i: (i, 0))
    vec = pl.BlockSpec((1, d), lambda i: (0, 0))
    one = pl.BlockSpec((1, 1), lambda i: (0, 0))
    outs, _ = _call(
        body, name=name, grid=(t // tm,), in_specs=[row, vec, row], out_specs=[one, row, row, vec],
        out_shape=[SDS((1, 1), F32), SDS((t, d), F32), SDS((t, d), BF16), SDS((1, d), F32)],
        args=[x, g, target])
    return outs


def mm_nn(a, b, *, name, tm, tn, out_dtype, residual=None, pieces=()):
    m, k = a.shape
    n = b.shape[1]
    tm, tn = min(tm, m), min(tn, n)
    has_res = residual is not None

    def body(a_ref, b_ref, *rest):
        o_ref = rest[-1]
        acc = _dot(a_ref[...], b_ref[...])
        if has_res:
            acc = acc + rest[0][...]
        o_ref[...] = acc.astype(o_ref.dtype)

    in_specs = [pl.BlockSpec((tm, k), lambda j, i: (i, 0)), pl.BlockSpec((k, tn), lambda j, i: (0, j))]
    args = [a, b]
    if has_res:
        in_specs.append(pl.BlockSpec((tm, tn), lambda j, i: (i, j)))
        args.append(residual)
    outs, couts = _call(
        body, name=name, grid=(n // tn, m // tm), in_specs=in_specs,
        out_specs=[pl.BlockSpec((tm, tn), lambda j, i: (i, j))], out_shape=[SDS((m, n), out_dtype)],
        args=args, pieces=pieces)
    return outs[0], couts


def mm_nt(a, b, *, name, tm, tn, out_dtype, pieces=()):
    m, k = a.shape
    n = b.shape[0]
    tm, tn = min(tm, m), min(tn, n)

    def body(a_ref, b_ref, o_ref):
        o_ref[...] = _dot_nt(a_ref[...], b_ref[...]).astype(o_ref.dtype)

    outs, couts = _call(
        body, name=name, grid=(n // tn, m // tm),
        in_specs=[pl.BlockSpec((tm, k), lambda j, i: (i, 0)), pl.BlockSpec((tn, k), lambda j, i: (j, 0))],
        out_specs=[pl.BlockSpec((tm, tn), lambda j, i: (i, j))], out_shape=[SDS((m, n), out_dtype)],
        args=[a, b], pieces=pieces)
    return outs[0], couts


def mm_tn(a, b, *, name, tk, tn, tt, out_dtype, pieces=()):
    t, k = a.shape
    n = b.shape[1]
    tk, tn, tt = min(tk, k), min(tn, n), min(tt, t)
    nt = t // tt

    def body(a_ref, b_ref, o_ref, acc_ref):
        s = pl.program_id(2)

        @pl.when(s == 0)
        def _():
            acc_ref[...] = jnp.zeros_like(acc_ref)

        acc_ref[...] += _dot_tn(a_ref[...], b_ref[...])

        @pl.when(s == nt - 1)
        def _():
            o_ref[...] = acc_ref[...].astype(o_ref.dtype)

    outs, couts = _call(
        body, name=name, grid=(k // tk, n // tn, nt),
        in_specs=[pl.BlockSpec((tt, tk), lambda i, j, s: (s, i)), pl.BlockSpec((tt, tn), lambda i, j, s: (s, j))],
        out_specs=[pl.BlockSpec((tk, tn), lambda i, j, s: (i, j))], out_shape=[SDS((k, n), out_dtype)],
        scratch_shapes=[pltpu.VMEM((tk, tn), F32)], args=[a, b], pieces=pieces)
    return outs[0], couts


def swiglu_fwd(h, wt, *, name, tm, tn, pieces=()):
    t, d = h.shape
    ff = wt.shape[0] // 2
    tm, tn = min(tm, t), min(tn, ff)
    nb = ff // tn

    def body(h_ref, wg_ref, wu_ref, gu_ref, act_ref):
        hv = h_ref[...]
        gate = _dot_nt(hv, wg_ref[...])
        up = _dot_nt(hv, wu_ref[...])
        gu_ref[0] = gate.astype(BF16)
        gu_ref[1] = up.astype(BF16)
        act_ref[...] = (gate * _sigmoid(gate) * up).astype(BF16)

    outs, couts = _call(
        body, name=name, grid=(nb, t // tm),
        in_specs=[pl.BlockSpec((tm, d), lambda j, i: (i, 0)),
                  pl.BlockSpec((tn, d), lambda j, i: (j, 0)),
                  pl.BlockSpec((tn, d), lambda j, i: (j + nb, 0))],
        out_specs=[pl.BlockSpec((2, tm, tn), lambda j, i: (0, i, j)), pl.BlockSpec((tm, tn), lambda j, i: (i, j))],
        out_shape=[SDS((2, t, ff), BF16), SDS((t, ff), BF16)], args=[h, wt, wt], pieces=pieces)
    return outs, couts


def swiglu_bwd(dx, w_down, gu, *, name, tm, tn, pieces=()):
    t, d = dx.shape
    ff = w_down.shape[0]
    tm, tn = min(tm, t), min(tn, ff)

    def body(dx_ref, w_ref, gu_ref, o_ref):
        dact = _dot_nt(dx_ref[...], w_ref[...])
        gate = gu_ref[0].astype(F32)
        up = gu_ref[1].astype(F32)
        sg = _sigmoid(gate)
        o_ref[0] = (dact * up * sg * (1.0 + gate * (1.0 - sg))).astype(BF16)
        o_ref[1] = (dact * gate * sg).astype(BF16)

    outs, couts = _call(
        body, name=name, grid=(ff // tn, t // tm),
        in_specs=[pl.BlockSpec((tm, d), lambda j, i: (i, 0)), pl.BlockSpec((tn, d), lambda j, i: (j, 0)),
                  pl.BlockSpec((2, tm, tn), lambda j, i: (0, i, j))],
        out_specs=[pl.BlockSpec((2, tm, tn), lambda j, i: (0, i, j))], out_shape=[SDS((2, t, ff), BF16)],
        args=[dx, w_down, gu], pieces=pieces)
    return outs[0], couts


def mm_norm_bwd(a, wt, x, g, dres, *, name, tm, pieces=()):
    parts = a.shape[0] if a.ndim == 3 else 1
    t, kp = a.shape[-2:]
    d = wt.shape[1]
    tm = min(tm, t)

    def body(a_ref, w_ref, x_ref, g_ref, dres_ref, dx_ref, dxb_ref, dg_ref):
        i = pl.program_id(0)
        if parts == 1:
            dh = _dot(a_ref[...], w_ref[...])
        else:
            dh = _dot(a_ref[0], w_ref[0:kp, :])
            for q in range(1, parts):
                dh = dh + _dot(a_ref[q], w_ref[q * kp:(q + 1) * kp, :])
        xv = x_ref[...]
        rstd = lax.rsqrt(jnp.mean(xv * xv, axis=-1, keepdims=True) + RMS_EPS)
        xhat = xv * rstd
        dxhat = dh * g_ref[...]
        dx = dres_ref[...] + rstd * (dxhat - xhat * jnp.mean(dxhat * xhat, axis=-1, keepdims=True))
        dx_ref[...] = dx
        dxb_ref[...] = dx.astype(BF16)

        @pl.when(i == 0)
        def _():
            dg_ref[...] = jnp.zeros_like(dg_ref)

        dg_ref[...] += jnp.sum(dh * xhat, axis=0, keepdims=True)

    a_spec = (pl.BlockSpec((tm, kp), lambda i: (i, 0)) if parts == 1
              else pl.BlockSpec((parts, tm, kp), lambda i: (0, i, 0)))
    row = pl.BlockSpec((tm, d), lambda i: (i, 0))
    vec = pl.BlockSpec((1, d), lambda i: (0, 0))
    outs, couts = _call(
        body, name=name, grid=(t // tm,),
        in_specs=[a_spec, pl.BlockSpec((parts * kp, d), lambda i: (0, 0)), row, vec, row],
        out_specs=[row, row, vec], out_shape=[SDS((t, d), F32), SDS((t, d), BF16), SDS((1, d), F32)],
        args=[a, wt, x, g, dres], pieces=pieces)
    return outs, couts


def mm_gu_tn(dgu, h, *, name, tn, tt, pieces=()):
    t, d = h.shape
    ff = dgu.shape[2]
    tn, tt = min(tn, ff), min(tt, t)
    nb = ff // tn
    nt = t // tt

    def body(a_ref, h_ref, o_ref, acc_ref):
        s = pl.program_id(1)

        @pl.when(s == 0)
        def _():
            acc_ref[...] = jnp.zeros_like(acc_ref)

        acc_ref[...] += _dot_tn(a_ref[...], h_ref[...])

        @pl.when(s == nt - 1)
        def _():
            o_ref[...] = acc_ref[...].astype(o_ref.dtype)

    outs, couts = _call(
        body, name=name, grid=(2 * nb, nt),
        in_specs=[pl.BlockSpec((None, tt, tn), lambda j, s: (j // nb, s, j % nb)),
                  pl.BlockSpec((tt, d), lambda j, s: (s, 0))],
        out_specs=[pl.BlockSpec((tn, d), lambda j, s: (j, 0))], out_shape=[SDS((2 * ff, d), BF16)],
        scratch_shapes=[pltpu.VMEM((tn, d), F32)], args=[dgu, h], pieces=pieces)
    return outs[0], couts


def _head_masks(shape):
    lane = lax.broadcasted_iota(jnp.int32, shape, 1)
    return [(lane >= h * HEAD_DIM_B) & (lane < (h + 1) * HEAD_DIM_B) for h in range(N_HEADS_B)]


def _causal_rows(ws):
    r = lax.broadcasted_iota(jnp.int32, ws.shape, 0) % CHUNK
    c = lax.broadcasted_iota(jnp.int32, ws.shape, 1)
    return jnp.where(c <= r, ws, jnp.zeros_like(ws))


def _pool_window(shape):
    lane = lax.broadcasted_iota(jnp.int32, shape, 1)
    return jnp.left_shift(2, lane // GROUP_DIM_C)


def _layer_norm_fwd(x, g, b):
    mu = jnp.mean(x, axis=-1, keepdims=True)
    xc = x - mu
    rstd = lax.rsqrt(jnp.mean(xc * xc, axis=-1, keepdims=True) + LN_EPS)
    xhat = xc * rstd
    return xhat * g + b, xhat, rstd


def _layer_norm_bwd(dy, xhat, rstd, g):
    dxhat = dy * g
    return rstd * (dxhat - jnp.mean(dxhat, axis=-1, keepdims=True)
                   - xhat * jnp.mean(dxhat * xhat, axis=-1, keepdims=True))


def _gate_mix(ws_masked, vl_chunk, masks):
    out = _dot(ws_masked, vl_chunk.astype(BF16))
    s = jnp.zeros((CHUNK, D_B), F32)
    for h in range(N_HEADS_B):
        s = s + jnp.where(masks[h], out[h * CHUNK:(h + 1) * CHUNK], 0.0)
    return s


SUB = 8


def _shifted_copies(ref, n):
    for b in range(1, SUB):
        ref[b, 0:n, :] = ref[0, pl.ds(b, n), :]


def _tap(ref, off, n):
    a, b = divmod(off, SUB)
    return ref[b, pl.ds(SUB * a, n), :]


def _pick_window(s2, s4, s8, s16):
    grp = lax.broadcasted_iota(jnp.int32, s2.shape, 1) // GROUP_DIM_C
    return jnp.where(grp == 0, s2, jnp.where(grp == 1, s4, jnp.where(grp == 2, s8, s16)))


def _trailing_window_sums(src_ref, l2_ref, l4_ref, l8_ref, tm):
    n = HALO + tm
    l2_ref[8:n, :] = src_ref[8:n, :] + src_ref[pl.ds(7, n - 8), :]
    l4_ref[16:n, :] = l2_ref[16:n, :] + l2_ref[pl.ds(14, n - 16), :]
    l8_ref[24:n, :] = l4_ref[24:n, :] + l4_ref[pl.ds(20, n - 24), :]
    s16 = l8_ref[HALO:n, :] + l8_ref[HALO - 8:n - 8, :]
    return _pick_window(l2_ref[HALO:n, :], l4_ref[HALO:n, :], l8_ref[HALO:n, :], s16)


def _leading_window_sums(src_ref, l2_ref, l4_ref, l8_ref, tm):
    n = HALO + tm
    l2_ref[0:n - 8, :] = src_ref[0:n - 8, :] + src_ref[pl.ds(1, n - 8), :]
    l4_ref[0:n - 16, :] = l2_ref[0:n - 16, :] + l2_ref[pl.ds(2, n - 16), :]
    l8_ref[0:n - 24, :] = l4_ref[0:n - 24, :] + l4_ref[pl.ds(4, n - 24), :]
    s16 = l8_ref[0:tm, :] + l8_ref[8:tm + 8, :]
    return _pick_window(l2_ref[0:tm, :], l4_ref[0:tm, :], l8_ref[0:tm, :], s16)


def _mixer_specs(tm, nt, seq):
    hb = tm // HALO

    def cur(c):
        return pl.BlockSpec((tm, c), lambda b, i: (b * nt + i, 0))

    def prev(c):
        return pl.BlockSpec((HALO, c), lambda b, i: (jnp.maximum((b * nt + i) * hb - 1, 0), 0))

    def nxt(c):
        last = (2 * seq) // HALO - 1
        return pl.BlockSpec((HALO, c), lambda b, i: (jnp.minimum((b * nt + i + 1) * hb, last), 0))

    def full(shape):
        return pl.BlockSpec(shape, lambda b, i: tuple(0 for _ in shape))

    return cur, prev, nxt, full


_MIX_PARAM_SHAPES = [(32, D_A), (1, D_A), (1, D_A), (1, D_A), (D_A, D_A), (1, D_B), (1, D_B),
                     (N_HEADS_B * CHUNK, CHUNK), (CHUNK, D_B), (D_C, D_C), (1, D_C)]


def mixer_fwd(z, mp, *, seq, name, tm=512, pieces=()):
    t = z.shape[0]
    tm = min(tm, seq)
    nt = seq // tm
    cur, prev, _, full = _mixer_specs(tm, nt, seq)

    def body(zc_ref, zp_ref, cw_ref, cb_ref, clg_ref, clb_ref, wpw_ref, slg_ref, slb_ref, ws_ref, bias_ref,
             wp_ref, ps_ref, o_ref, cv_ref, ys_ref, zs_ref, l2_ref, l4_ref, l8_ref):
        i = pl.program_id(1)
        has_prev = i > 0
        yp = zp_ref[:, 0:D_A].astype(F32) * _sigmoid(zp_ref[:, D_A:2 * D_A].astype(F32))
        ys_ref[0, 0:HALO, :] = jnp.where(has_prev, yp, 0.0)
        ys_ref[0, HALO:HALO + tm, :] = zc_ref[:, 0:D_A].astype(F32) * _sigmoid(zc_ref[:, D_A:2 * D_A].astype(F32))
        _shifted_copies(ys_ref, tm + HALO - SUB)
        acc = jnp.zeros((tm, D_A), F32) + cb_ref[...]
        for k in range(CONV_WIDTH):
            acc = acc + cw_ref[k:k + 1, :] * _tap(ys_ref, HALO - (CONV_WIDTH - 1) + k, tm)
        cv_ref[...] = acc
        ln, _, _ = _layer_norm_fwd(acc, clg_ref[...], clb_ref[...])
        sl = ln * _sigmoid(ln)
        o_ref[:, 0:D_A] = _dot(sl.astype(BF16), wpw_ref[...]).astype(BF16)
        gz = _gelu(zc_ref[:, 2 * D_A:2 * D_A + 2 * D_B].astype(F32))
        u = gz[:, :D_B]
        vl, _, _ = _layer_norm_fwd(gz[:, D_B:], slg_ref[...], slb_ref[...])
        wsm = _causal_rows(ws_ref[...])
        masks = _head_masks((CHUNK, D_B))
        for c in range(tm // CHUNK):
            rows = slice(c * CHUNK, (c + 1) * CHUNK)
            s = _gate_mix(wsm, vl[rows], masks) + bias_ref[...]
            o_ref[rows, D_A:D_A + D_B] = (u[rows] * s).astype(BF16)
        c0 = 2 * D_A + 2 * D_B
        zs_ref[0:HALO, :] = jnp.where(has_prev, zp_ref[:, c0:c0 + D_C].astype(F32), 0.0)
        zcur = zc_ref[:, c0:c0 + D_C].astype(F32)
        zs_ref[HALO:HALO + tm, :] = zcur
        win = _pool_window((tm, D_C))
        wsum = _trailing_window_sums(zs_ref, l2_ref, l4_ref, l8_ref, tm)
        pos = i * tm + lax.broadcasted_iota(jnp.int32, (tm, D_C), 0)
        cnt = jnp.minimum(pos + 1, win).astype(F32)
        p = wsum / cnt - zcur
        y = _dot(p.astype(BF16), wp_ref[...])
        o_ref[:, D_A + D_B:D_A + D_B + D_C] = (y * ps_ref[...]).astype(BF16)

    in_specs = [cur(D_IN), prev(D_IN)] + [full(s) for s in _MIX_PARAM_SHAPES]
    outs, couts = _call(
        body, name=name, grid=(2, nt), in_specs=in_specs, out_specs=[cur(D_MODEL), cur(D_A)],
        out_shape=[SDS((t, D_MODEL), BF16), SDS((t, D_A), F32)],
        scratch_shapes=[pltpu.VMEM((SUB, HALO + tm, D_A), F32)] + [pltpu.VMEM((HALO + tm, D_C), F32)] * 4,
        args=[z, z, *mp], pieces=pieces)
    return outs, couts


def mixer_bwd(z, conv, dm, mp, *, seq, name, tm=256, pieces=()):
    t = z.shape[0]
    tm = min(tm, seq)
    nt = seq // tm
    ext = tm + HALO
    cur, prev, nxt, full = _mixer_specs(tm, nt, seq)
    grad_shapes = [(32, D_A), (1, D_A), (1, D_A), (1, D_A), (D_A, D_A), (1, D_B), (1, D_B),
                   (N_HEADS_B * CHUNK, CHUNK), (CHUNK, CHUNK), (D_C, D_C), (1, D_C)]

    def body(zc_ref, zp_ref, cvc_ref, cvn_ref, dmc_ref, dmn_ref, cw_ref, cb_ref, clg_ref, clb_ref, wpw_ref, slg_ref,
             slb_ref, ws_ref, bias_ref, wp_ref, ps_ref,
             dz_ref, dcw_ref, dcb_ref, dclg_ref, dclb_ref, dwpw_ref, dslg_ref, dslb_ref, dws_ref, dbs_ref, dwp_ref,
             dps_ref, ys_ref, dcs_ref, zs_ref, qs_ref, l2_ref, l4_ref, l8_ref):
        b = pl.program_id(0)
        i = pl.program_id(1)
        has_prev = i > 0
        has_next = i < nt - 1
        grads = [dcw_ref, dcb_ref, dclg_ref, dclb_ref, dwpw_ref, dslg_ref, dslb_ref, dws_ref, dbs_ref, dwp_ref, dps_ref]

        @pl.when((b == 0) & (i == 0))
        def _():
            for r in grads:
                r[...] = jnp.zeros_like(r)

        ext_row = lax.broadcasted_iota(jnp.int32, (ext, 1), 0)
        live = (ext_row < tm) | has_next

        yp = zp_ref[:, 0:D_A].astype(F32) * _sigmoid(zp_ref[:, D_A:2 * D_A].astype(F32))
        ys_ref[0, 0:HALO, :] = jnp.where(has_prev, yp, 0.0)
        a_cur = zc_ref[:, 0:D_A].astype(F32)
        sig_cur = _sigmoid(zc_ref[:, D_A:2 * D_A].astype(F32))
        ys_ref[0, HALO:HALO + tm, :] = a_cur * sig_cur
        _shifted_copies(ys_ref, tm + HALO - SUB)
        acc = jnp.concatenate([cvc_ref[...], cvn_ref[...]], axis=0)
        ln, xhat, rstd = _layer_norm_fwd(acc, clg_ref[...], clb_ref[...])
        sg = _sigmoid(ln)
        sl = ln * sg
        dya = jnp.concatenate([dmc_ref[:, 0:D_A], dmn_ref[:, 0:D_A]], axis=0)
        dsl = _dot_nt(dya, wpw_ref[...])
        dln = dsl * sg * (1.0 + ln * (1.0 - sg))
        dc = _layer_norm_bwd(dln, xhat, rstd, clg_ref[...])
        dc = jnp.where(live, dc, 0.0)
        dcs_ref[0] = dc
        _shifted_copies(dcs_ref, ext - SUB)
        dwpw_ref[...] += _dot_tn(sl[:tm].astype(BF16), dya[:tm])
        dclg_ref[...] += jnp.sum(dln[:tm] * xhat[:tm], axis=0, keepdims=True)
        dclb_ref[...] += jnp.sum(dln[:tm], axis=0, keepdims=True)
        dcb_ref[...] += jnp.sum(dc[:tm], axis=0, keepdims=True)
        dy = jnp.zeros((tm, D_A), F32)
        for k in range(CONV_WIDTH):
            off = HALO - (CONV_WIDTH - 1) + k
            dcw_ref[k:k + 1, :] += jnp.sum(dc[:tm] * _tap(ys_ref, off, tm), axis=0, keepdims=True)
            dy = dy + cw_ref[k:k + 1, :] * _tap(dcs_ref, CONV_WIDTH - 1 - k, tm)
        dz_ref[:, 0:D_A] = (dy * sig_cur).astype(BF16)
        dz_ref[:, D_A:2 * D_A] = (dy * a_cur * sig_cur * (1.0 - sig_cur)).astype(BF16)

        zb = zc_ref[:, 2 * D_A:2 * D_A + 2 * D_B].astype(F32)
        gz = _gelu(zb)
        u = gz[:, :D_B]
        vl, vhat, vrstd = _layer_norm_fwd(gz[:, D_B:], slg_ref[...], slb_ref[...])
        dyb = dmc_ref[:, D_A:D_A + D_B].astype(F32)
        wsm = _causal_rows(ws_ref[...])
        masks = _head_masks((CHUNK, D_B))
        ds_all = dyb * u
        du_parts, dvl_parts = [], []
        for c in range(tm // CHUNK):
            rows = slice(c * CHUNK, (c + 1) * CHUNK)
            vlc = vl[rows].astype(BF16)
            s = _gate_mix(wsm, vl[rows], masks) + bias_ref[...]
            du_parts.append(dyb[rows] * s)
            ds = ds_all[rows]
            stack = jnp.concatenate([jnp.where(masks[h], ds, 0.0) for h in range(N_HEADS_B)], axis=0).astype(BF16)
            dvl_parts.append(_dot_tn(wsm, stack))
            dws_ref[...] += _dot_nt(stack, vlc)
        du = jnp.concatenate(du_parts, axis=0)
        dvl = jnp.concatenate(dvl_parts, axis=0)
        dbias = jnp.zeros((CHUNK, D_B), F32)
        for c in range(tm // CHUNK):
            dbias = dbias + ds_all[c * CHUNK:(c + 1) * CHUNK]
        lane = lax.broadcasted_iota(jnp.int32, (CHUNK, CHUNK), 1)
        dbs = jnp.zeros((CHUNK, CHUNK), F32)
        for h in range(N_HEADS_B):
            col = jnp.sum(jnp.where(masks[h], dbias, 0.0), axis=1, keepdims=True)
            dbs = dbs + jnp.where(lane == h, col, 0.0)
        dbs_ref[...] += dbs
        dslg_ref[...] += jnp.sum(dvl * vhat, axis=0, keepdims=True)
        dslb_ref[...] += jnp.sum(dvl, axis=0, keepdims=True)
        dv = _layer_norm_bwd(dvl, vhat, vrstd, slg_ref[...])
        gg = _gelu_grad(zb)
        dz_ref[:, 2 * D_A:2 * D_A + D_B] = (du * gg[:, :D_B]).astype(BF16)
        dz_ref[:, 2 * D_A + D_B:2 * D_A + 2 * D_B] = (dv * gg[:, D_B:]).astype(BF16)

        c0 = 2 * D_A + 2 * D_B
        m0 = D_A + D_B
        zs_ref[0:HALO, :] = jnp.where(has_prev, zp_ref[:, c0:c0 + D_C].astype(F32), 0.0)
        zcur = zc_ref[:, c0:c0 + D_C].astype(F32)
        zs_ref[HALO:HALO + tm, :] = zcur
        win = _pool_window((tm, D_C))
        wsum = _trailing_window_sums(zs_ref, l2_ref, l4_ref, l8_ref, tm)
        pos = i * tm + lax.broadcasted_iota(jnp.int32, (tm, D_C), 0)
        cnt = jnp.minimum(pos + 1, win).astype(F32)
        pb = (wsum / cnt - zcur).astype(BF16)
        y = _dot(pb, wp_ref[...])
        dyc = jnp.concatenate([dmc_ref[:, m0:m0 + D_C], dmn_ref[:, m0:m0 + D_C]], axis=0).astype(F32)
        dps_ref[...] += jnp.sum(dyc[:tm] * y, axis=0, keepdims=True)
        dyv = (dyc * ps_ref[...]).astype(BF16)
        dwp_ref[...] += _dot_tn(pb, dyv[:tm])
        dp = _dot_nt(dyv, wp_ref[...])
        win_e = _pool_window((ext, D_C))
        pos_e = i * tm + lax.broadcasted_iota(jnp.int32, (ext, D_C), 0)
        cnt_e = jnp.minimum(pos_e + 1, win_e).astype(F32)
        qs_ref[...] = jnp.where(live, dp / cnt_e, 0.0)
        dzc = _leading_window_sums(qs_ref, l2_ref, l4_ref, l8_ref, tm) - dp[:tm]
        dz_ref[:, c0:c0 + D_C] = dzc.astype(BF16)

        @pl.when((b == 1) & (i == nt - 1))
        def _():
            dws_ref[...] = _causal_rows(dws_ref[...])

    in_specs = ([cur(D_IN), prev(D_IN), cur(D_A), nxt(D_A), cur(D_MODEL), nxt(D_MODEL)]
                + [full(s) for s in _MIX_PARAM_SHAPES])
    out_specs = [cur(D_IN)] + [full(s) for s in grad_shapes]
    out_shape = [SDS((t, D_IN), BF16)] + [SDS(s, F32) for s in grad_shapes]
    outs, couts = _call(
        body, name=name, grid=(2, nt), in_specs=in_specs, out_specs=out_specs, out_shape=out_shape,
        scratch_shapes=[pltpu.VMEM((SUB, HALO + tm, D_A), F32), pltpu.VMEM((SUB, ext, D_A), F32)]
        + [pltpu.VMEM((ext, D_C), F32)] * 5,
        args=[z, z, conv, conv, dm, dm, *mp], pieces=pieces)
    return outs, couts


MIXER_SMALL = ["conv_w", "conv_b", "conv_ln_g", "conv_ln_b", "w_pw", "sg_ln_g", "sg_ln_b", "w_s", "b_s", "w_pool",
               "pool_scale"]


def _mixer_params(p, w_pw_bf16, l):
    wp_bd = jnp.zeros((D_C, D_C), F32)
    for g in range(D_C // GROUP_DIM_C):
        sl = slice(g * GROUP_DIM_C, (g + 1) * GROUP_DIM_C)
        wp_bd = wp_bd.at[sl, sl].set(p["w_pool"][l, g])
    return [
        jnp.pad(p["conv_w"][l], ((0, 32 - CONV_WIDTH), (0, 0))),
        p["conv_b"][l][None], p["conv_ln_g"][l][None], p["conv_ln_b"][l][None],
        w_pw_bf16,
        p["sg_ln_g"][l][None], p["sg_ln_b"][l][None],
        p["w_s"][l].reshape(N_HEADS_B * CHUNK, CHUNK).astype(BF16),
        jnp.repeat(p["b_s"][l].T, HEAD_DIM_B, axis=1),
        wp_bd.astype(BF16),
        p["pool_scale"][l][None],
    ]


def _mixer_grads(g):
    dcw, dcb, dclg, dclb, dwpw, dslg, dslb, dws, dbs, dwp, dps = g
    blocks = [dwp[i * GROUP_DIM_C:(i + 1) * GROUP_DIM_C, i * GROUP_DIM_C:(i + 1) * GROUP_DIM_C]
              for i in range(D_C // GROUP_DIM_C)]
    return [dcw[:CONV_WIDTH], dcb[0], dclg[0], dclb[0], dwpw, dslg[0], dslb[0],
            dws.reshape(N_HEADS_B, CHUNK, CHUNK), dbs[:, :N_HEADS_B].T, jnp.stack(blocks), dps[0]]


def _row_tile(r, cap=512):
    best = r
    for d in range(16, min(r, cap) + 1, 16):
        if r % d == 0:
            best = d
    return best if best <= cap else r


def add_core_halves(g, r1, core, *, name):
    _, _, r, c = g.shape
    tr = _row_tile(r)

    def body(core_ref, g_ref, r_ref, o_ref):
        o_ref[...] = (g_ref[...].astype(F32) + r_ref[...].astype(F32)).astype(o_ref.dtype)

    outs, _ = _call(
        body, name=name, grid=(N_CHIPS, r // tr), prefetch=1,
        in_specs=[pl.BlockSpec((None, None, tr, c), lambda j, i, s: (j, s[0], i, 0)),
                  pl.BlockSpec((None, tr, c), lambda j, i, s: (j, i, 0))],
        out_specs=[pl.BlockSpec((None, tr, c), lambda j, i, s: (j, i, 0))],
        out_shape=[SDS((N_CHIPS, r, c), g.dtype)], args=[core, g, r1])
    return outs[0]


def sum_chips(h, r2, place, *, name):
    _, r, c = h.shape
    tr = _row_tile(r, 256)

    def body(place_ref, h_ref, a_ref, b_ref, c_ref, o_ref):
        acc = h_ref[...].astype(F32) + a_ref[...].astype(F32)
        acc = acc + b_ref[...].astype(F32)
        o_ref[...] = acc + c_ref[...].astype(F32)

    def blk(k):
        return pl.BlockSpec((None, tr, c), lambda i, s: (jnp.bitwise_xor(s[0], k), i, 0))

    outs, _ = _call(
        body, name=name, grid=(r // tr,), prefetch=1, in_specs=[blk(0), blk(1), blk(2), blk(3)],
        out_specs=[pl.BlockSpec((None, tr, c), lambda i, s: (s[1], i, 0))],
        out_shape=[SDS((2, r, c), F32)], args=[place, h, r2, r2, r2])
    return outs[0]


def allreduce_small(p, *, name, pieces=()):
    _, n, _ = p.shape

    def body(p_ref, o_ref, land_ref, send1, recv1, send2, recv2):
        x, y, c = lax.axis_index("x"), lax.axis_index("y"), lax.axis_index("c")
        me = 4 * x + 2 * y + c

        def peer(r):
            return ((1 - x) if r & 4 else x, (1 - y) if r & 2 else y, (1 - c) if r & 1 else c)

        def index(r):
            px, py, pc = peer(r)
            return 4 * px + 2 * py + pc

        land_ref[me] = p_ref[me]
        first = [_remote(p_ref.at[index(r)], land_ref.at[me], send1.at[r - 1], recv1.at[r - 1], peer(r))
                 for r in range(1, N_DEV)]
        for cp in first:
            cp.start()
        for r in range(1, N_DEV):
            blk = land_ref.at[index(r)]
            _remote(blk, blk, send1.at[r - 1], recv1.at[r - 1], peer(r)).wait_recv()
        acc = land_ref[0]
        for d in range(1, N_DEV):
            acc = acc + land_ref[d]
        o_ref[me] = acc
        second = [_remote(o_ref.at[me], o_ref.at[me], send2.at[r - 1], recv2.at[r - 1], peer(r))
                  for r in range(1, N_DEV)]
        for cp in second:
            cp.start()
        for r in range(1, N_DEV):
            blk = o_ref.at[index(r)]
            _remote(blk, blk, send2.at[r - 1], recv2.at[r - 1], peer(r)).wait_recv()
        for cp in first + second:
            cp.wait_send()

    vm = pl.BlockSpec(memory_space=pltpu.VMEM)
    outs, couts = _call(
        body, name=name, grid=(), in_specs=[vm], out_specs=[vm], out_shape=[SDS(p.shape, F32)],
        scratch_shapes=[pltpu.VMEM(p.shape, F32)] + [pltpu.SemaphoreType.DMA((N_DEV - 1,))] * 4,
        args=[p], pieces=pieces)
    return outs[0], couts


def _adam_update(w, g, m, v):
    m_new = ADAM_B1 * m + (1.0 - ADAM_B1) * g
    v_new = ADAM_B2 * v + (1.0 - ADAM_B2) * (g * g)
    m_hat = m_new / (1.0 - ADAM_B1 ** ADAM_STEP)
    v_hat = v_new / (1.0 - ADAM_B2 ** ADAM_STEP)
    return -ADAM_LR * (m_hat / (jnp.sqrt(v_hat) + ADAM_EPS) + ADAM_WD * w), m_new, v_new


def adamw(w, g, m, v, *, name, pieces=()):
    nl, r, c = w.shape
    tr = _row_tile(r, 256)

    def body(w_ref, g_ref, m_ref, v_ref, d_ref, mo_ref, vo_ref):
        d_ref[...], mo_ref[...], vo_ref[...] = _adam_update(w_ref[...], g_ref[...], m_ref[...], v_ref[...])

    blk = pl.BlockSpec((None, tr, c), lambda l, i: (l, i, 0))
    return _call(body, name=name, grid=(nl, r // tr), in_specs=[blk] * 4, out_specs=[blk] * 3,
                 out_shape=[SDS(w.shape, F32)] * 3, args=[w, g, m, v], pieces=pieces)


def adamw_small(ws, gs, ms, vs, *, name, pieces=()):
    n = len(ws)

    def body(*refs):
        for i in range(n):
            w_ref, g_ref, m_ref, v_ref = (refs[k * n + i] for k in range(4))
            d, mn, vn = _adam_update(w_ref[...], g_ref[...], m_ref[...], v_ref[...])
            refs[4 * n + i][...] = d
            refs[5 * n + i][...] = mn
            refs[6 * n + i][...] = vn

    vm = pl.BlockSpec(memory_space=pltpu.VMEM)
    res, couts = _call(body, name=name, grid=(), in_specs=[vm] * (4 * n), out_specs=[vm] * (3 * n),
                       out_shape=[SDS(w.shape, F32) for w in ws] * 3, args=[*ws, *gs, *ms, *vs], pieces=pieces)
    return (res[:n], res[n:2 * n], res[2 * n:]), couts


WEIGHTS = ["norm1_g", "w_in", "conv_w", "conv_b", "conv_ln_g", "conv_ln_b", "w_pw", "sg_ln_g", "sg_ln_b", "w_s",
           "b_s", "w_pool", "pool_scale", "w_out", "norm2_g", "w_gate_up", "w_down", "final_g"]
BIG = ["w_in", "w_pw", "w_out", "w_gate_up", "w_down"]
TRANSPOSED = {"w_in": True, "w_pw": False, "w_out": False, "w_gate_up": True, "w_down": False}
SMALL = [k for k in WEIGHTS if k not in BIG]


def _wire(a, transposed):
    if transposed:
        a = a.transpose(0, 2, 1)
    return [a[l].reshape(2, a.shape[1] // 2, a.shape[2]) for l in range(a.shape[0])]


def _pack(arrays):
    flat = jnp.concatenate([a.reshape(-1) for a in arrays])
    n = -(-flat.shape[0] // (N_DEV * LANES * 8)) * 8
    return jnp.pad(flat, (0, N_DEV * n * LANES - flat.shape[0])).reshape(N_DEV, n, LANES)


def _unpack(packed, shapes):
    flat = packed.reshape(-1)
    out, off = [], 0
    for s in shapes:
        size = math.prod(s)
        out.append(flat[off:off + size].reshape(s))
        off += size
    return out


def kernel(x, norm1_g, w_in, conv_w, conv_b, conv_ln_g, conv_ln_b, w_pw, sg_ln_g, sg_ln_b, w_s, b_s, w_pool, pool_scale, w_out, norm2_g, w_gate_up, w_down, final_g, loss_target, m_norm1_g, m_w_in, m_conv_w, m_conv_b, m_conv_ln_g, m_conv_ln_b, m_w_pw, m_sg_ln_g, m_sg_ln_b, m_w_s, m_b_s, m_w_pool, m_pool_scale, m_w_out, m_norm2_g, m_w_gate_up, m_w_down, m_final_g, v_norm1_g, v_w_in, v_conv_w, v_conv_b, v_conv_ln_g, v_conv_ln_b, v_w_pw, v_sg_ln_g, v_sg_ln_b, v_w_s, v_b_s, v_w_pool, v_pool_scale, v_w_out, v_norm2_g, v_w_gate_up, v_w_down, v_final_g):
    w = dict(norm1_g=norm1_g, w_in=w_in, conv_w=conv_w, conv_b=conv_b, conv_ln_g=conv_ln_g, conv_ln_b=conv_ln_b,
             w_pw=w_pw, sg_ln_g=sg_ln_g, sg_ln_b=sg_ln_b, w_s=w_s, b_s=b_s, w_pool=w_pool, pool_scale=pool_scale,
             w_out=w_out, norm2_g=norm2_g, w_gate_up=w_gate_up, w_down=w_down, final_g=final_g)
    m = dict(norm1_g=m_norm1_g, w_in=m_w_in, conv_w=m_conv_w, conv_b=m_conv_b, conv_ln_g=m_conv_ln_g,
             conv_ln_b=m_conv_ln_b, w_pw=m_w_pw, sg_ln_g=m_sg_ln_g, sg_ln_b=m_sg_ln_b, w_s=m_w_s, b_s=m_b_s,
             w_pool=m_w_pool, pool_scale=m_pool_scale, w_out=m_w_out, norm2_g=m_norm2_g, w_gate_up=m_w_gate_up,
             w_down=m_w_down, final_g=m_final_g)
    v = dict(norm1_g=v_norm1_g, w_in=v_w_in, conv_w=v_conv_w, conv_b=v_conv_b, conv_ln_g=v_conv_ln_g,
             conv_ln_b=v_conv_ln_b, w_pw=v_w_pw, sg_ln_g=v_sg_ln_g, sg_ln_b=v_sg_ln_b, w_s=v_w_s, b_s=v_b_s,
             w_pool=v_w_pool, pool_scale=v_pool_scale, w_out=v_w_out, norm2_g=v_norm2_g, w_gate_up=v_w_gate_up,
             w_down=v_w_down, final_g=v_final_g)
    bsz, seq, d = x.shape
    t = bsz * seq
    chip = 2 * lax.axis_index("x") + lax.axis_index("y")
    core = lax.axis_index("c")
    core_arr = jnp.reshape(core, (1,)).astype(jnp.int32)
    place_arr = jnp.stack([chip, core]).astype(jnp.int32)

    own = {k: _wire(w[k].astype(BF16), TRANSPOSED[k]) for k in BIG}
    cw_cols = conv_w.shape[2]
    side = jnp.pad(conv_w, ((0, 0), (0, 32 - CONV_WIDTH), (0, 0)))

    def complete(g, mine):
        g = lax.dynamic_update_index_in_dim(g, mine, chip, 0)
        return g.reshape(-1, g.shape[-1])

    first = comm_only([gather_both(own["w_in"][0]), gather_both(own["w_pw"][0]), gather_both(own["w_pw"][1]),
                       gather_both(side)], name="gather_first")
    full = {("w_in", 0): complete(first[0][0], own["w_in"][0]),
            ("w_pw", 0): complete(first[1][0], own["w_pw"][0]),
            ("w_pw", 1): complete(first[2][0], own["w_pw"][1])}
    side_all = lax.dynamic_update_index_in_dim(first[3][0], side, chip, 0)
    p = dict(w)
    p["conv_w"] = side_all[:, :, :CONV_WIDTH, :].transpose(1, 2, 0, 3).reshape(DEPTH, CONV_WIDTH, N_CHIPS * cw_cols)

    xs = [x.reshape(t, d)]
    saved = []
    pend = {}
    for l in range(DEPTH):
        x0 = xs[-1]
        mp = _mixer_params(p, full[("w_pw", l)], l)
        if l == 0:
            h1, (a,) = rmsnorm_fwd(x0, p["norm1_g"][l][None], name=f"norm1_fwd_{l}",
                                   pieces=[gather_ici(own["w_out"][0])])
            pend["w_out", 0] = a[0]
            z, (b, a) = mm_nt(h1, full["w_in", l], name=f"in_proj_{l}", tm=512, tn=D_IN, out_dtype=BF16,
                              pieces=[gather_d2d(pend.pop(("w_out", 0))), gather_ici(own["w_down"][0])])
            full["w_out", 0] = complete(b[0], own["w_out"][0])
            pend["w_down", 0] = a[0]
            (mc, cv), (b, a) = mixer_fwd(z, mp, seq=seq, name=f"mixer_fwd_{l}",
                                   pieces=[gather_d2d(pend.pop(("w_down", 0))), gather_ici(own["w_gate_up"][0])])
            full["w_down", 0] = complete(b[0], own["w_down"][0])
            pend["w_gate_up", 0] = a[0]
            x1, (b, a) = mm_nn(mc, full["w_out", l], name=f"out_proj_{l}", tm=512, tn=D_MODEL, out_dtype=F32,
                               residual=x0,
                               pieces=[gather_d2d(pend.pop(("w_gate_up", 0))), gather_ici(own["w_in"][1])])
            full["w_gate_up", 0] = complete(b[0], own["w_gate_up"][0])
            pend["w_in", 1] = a[0]
            h2, (b, a) = rmsnorm_fwd(x1, p["norm2_g"][l][None], name=f"norm2_fwd_{l}",
                                     pieces=[gather_d2d(pend.pop(("w_in", 1))), gather_ici(own["w_out"][1])])
            full["w_in", 1] = complete(b[0], own["w_in"][1])
            pend["w_out", 1] = a[0]
            (gu, act), (b, a) = swiglu_fwd(h2, full["w_gate_up", l], name=f"swiglu_fwd_{l}", tm=512, tn=1408,
                                           pieces=[gather_d2d(pend.pop(("w_out", 1))),
                                                   gather_ici(own["w_gate_up"][1])])
            full["w_out", 1] = complete(b[0], own["w_out"][1])
            pend["w_gate_up", 1] = a[0]
            x2, (b, a) = mm_nn(act, full["w_down", l], name=f"down_proj_{l}", tm=512, tn=D_MODEL, out_dtype=F32,
                               residual=x1,
                               pieces=[gather_d2d(pend.pop(("w_gate_up", 1))), gather_ici(own["w_down"][1])])
            full["w_gate_up", 1] = complete(b[0], own["w_gate_up"][1])
            pend["w_down", 1] = a[0]
        else:
            h1, (b,) = rmsnorm_fwd(x0, p["norm1_g"][l][None], name=f"norm1_fwd_{l}",
                                   pieces=[gather_d2d(pend.pop(("w_down", 1)))])
            full["w_down", 1] = complete(b[0], own["w_down"][1])
            z, _ = mm_nt(h1, full["w_in", l], name=f"in_proj_{l}", tm=512, tn=D_IN, out_dtype=BF16)
            (mc, cv), _ = mixer_fwd(z, mp, seq=seq, name=f"mixer_fwd_{l}")
            x1, _ = mm_nn(mc, full["w_out", l], name=f"out_proj_{l}", tm=512, tn=D_MODEL, out_dtype=F32, residual=x0)
            h2, _ = rmsnorm_fwd(x1, p["norm2_g"][l][None], name=f"norm2_fwd_{l}")
            (gu, act), _ = swiglu_fwd(h2, full["w_gate_up", l], name=f"swiglu_fwd_{l}", tm=512, tn=1408)
            x2, _ = mm_nn(act, full["w_down", l], name=f"down_proj_{l}", tm=512, tn=D_MODEL, out_dtype=F32,
                          residual=x1)
        saved.append((x0, h1, z, cv, mc, x1, h2, gu, act, mp))
        xs.append(x2)

    loss, dx, dxb, d_final_g = loss_head(xs[-1], p["final_g"][None], loss_target.reshape(t, d), name="loss_head")

    small = {k: [None] * DEPTH for k in SMALL if k != "final_g"}
    reduced = {}
    carry = None

    def halves(g):
        return g.reshape(N_CHIPS, 2, g.shape[0] // (2 * N_CHIPS), g.shape[1])

    for l in reversed(range(DEPTH)):
        x0, h1, z, cv, mc, x1, h2, gu, act, mp = saved[l]
        pieces = [exchange_d2d(carry[0]), reduce_ici(carry[1])] if carry else []
        g_down, co = mm_tn(act, dxb, name=f"down_proj_dw_{l}", tk=D_FF // 2, tn=D_MODEL, tt=TT, out_dtype=BF16,
                           pieces=pieces)
        g_down = halves(g_down)
        pieces = [reduce_d2d(g_down)]
        if carry:
            reduced["w_pw", l + 1] = co[0][0]
            s_in = sum_chips(carry[1], co[1][0], place_arr, name=f"sum_chips_w_in_{l + 1}")
            pieces.append(exchange_d2d(s_in))
        dgu, co = swiglu_bwd(dxb, full["w_down", l], gu, name=f"swiglu_bwd_{l}", tm=512, tn=1408, pieces=pieces)
        if carry:
            reduced["w_in", l + 1] = co[1][0]
        h_down = add_core_halves(g_down, co[0][0], core_arr, name=f"add_cores_w_down_{l}")
        g_gu, (r2,) = mm_gu_tn(dgu, h2, name=f"gate_up_dw_{l}", tn=1408, tt=TT, pieces=[reduce_ici(h_down)])
        s_down = sum_chips(h_down, r2[0], place_arr, name=f"sum_chips_w_down_{l}")
        g_gu = halves(g_gu)
        (dx, dxb, dn2), (e, r1) = mm_norm_bwd(dgu, full["w_gate_up", l], x1, p["norm2_g"][l][None], dx,
                                              name=f"gate_up_dx_{l}", tm=256,
                                              pieces=[exchange_d2d(s_down), reduce_d2d(g_gu)])
        reduced["w_down", l] = e[0]
        small["norm2_g"][l] = dn2[0]
        h_gu = add_core_halves(g_gu, r1[0], core_arr, name=f"add_cores_w_gate_up_{l}")
        g_out, _ = mm_tn(mc, dxb, name=f"out_proj_dw_{l}", tk=D_MODEL, tn=D_MODEL, tt=TT, out_dtype=BF16)
        g_out = halves(g_out)
        dmc, (r1,) = mm_nt(dxb, full["w_out", l], name=f"out_proj_dx_{l}", tm=512, tn=D_MODEL, out_dtype=BF16,
                           pieces=[reduce_d2d(g_out)])
        h_out = add_core_halves(g_out, r1[0], core_arr, name=f"add_cores_w_out_{l}")
        (dz, *mg), (r2a, r2b) = mixer_bwd(z, cv, dmc, mp, seq=seq, name=f"mixer_bwd_{l}",
                                          pieces=[reduce_ici(h_gu), reduce_ici(h_out)])
        s_gu = sum_chips(h_gu, r2a[0], place_arr, name=f"sum_chips_w_gate_up_{l}")
        s_out = sum_chips(h_out, r2b[0], place_arr, name=f"sum_chips_w_out_{l}")
        mgrads = dict(zip(MIXER_SMALL, _mixer_grads(mg)))
        for k in MIXER_SMALL:
            if k != "w_pw":
                small[k][l] = mgrads[k]
        g_pw = halves(mgrads["w_pw"].astype(BF16))
        g_in, (ea, eb, r1) = mm_tn(dz, h1, name=f"in_proj_dw_{l}", tk=D_IN // 2, tn=D_MODEL, tt=TT, out_dtype=BF16,
                                   pieces=[exchange_d2d(s_gu), exchange_d2d(s_out), reduce_d2d(g_pw)])
        reduced["w_gate_up", l], reduced["w_out", l] = ea[0], eb[0]
        h_pw = add_core_halves(g_pw, r1[0], core_arr, name=f"add_cores_w_pw_{l}")
        g_in = halves(g_in)
        (dx, dxb, dn1), (r2, r1) = mm_norm_bwd(dz, full["w_in", l], x0, p["norm1_g"][l][None], dx,
                                               name=f"in_proj_dx_{l}", tm=512,
                                               pieces=[reduce_ici(h_pw), reduce_d2d(g_in)])
        small["norm1_g"][l] = dn1[0]
        s_pw = sum_chips(h_pw, r2[0], place_arr, name=f"sum_chips_w_pw_{l}")
        h_in = add_core_halves(g_in, r1[0], core_arr, name=f"add_cores_w_in_{l}")
        carry = (s_pw, h_in)
    grad_x = dx.reshape(bsz, seq, d)

    g_small = [jnp.stack(small[k]) if k != "final_g" else d_final_g[0] for k in SMALL]
    small_shapes = [g.shape for g in g_small] + [(1,)]
    s_pw, h_in = carry
    summed, co = allreduce_small(_pack(g_small + [loss.reshape(1)]), name="allreduce_small",
                                 pieces=[exchange_d2d(s_pw), reduce_ici(h_in)])
    reduced["w_pw", 0] = co[0][0]
    s_in = sum_chips(h_in, co[1][0], place_arr, name="sum_chips_w_in_0")
    summed = _unpack(summed, small_shapes)
    loss = summed[-1][0]
    grad = dict(zip(SMALL, summed[:-1]))
    grad["conv_w"] = lax.dynamic_slice_in_dim(grad["conv_w"], chip * cw_cols, cw_cols, axis=2)

    delta, new_m, new_v = {}, {}, {}

    def flat2(a):
        return a.reshape(-1, a.shape[-1])

    res, co = adamw_small(*[[flat2(tt[k]) for k in SMALL] for tt in (w, grad, m, v)], name="adamw_small",
                          pieces=[exchange_d2d(s_in)])
    reduced["w_in", 0] = co[0][0]
    for out, arrs in zip((delta, new_m, new_v), res):
        out.update({k: a.reshape(w[k].shape) for k, a in zip(SMALL, arrs)})
    for k in BIG:
        g = jnp.stack([reduced[k, l].reshape(-1, reduced[k, l].shape[-1]) for l in range(DEPTH)])
        grad[k] = g.transpose(0, 2, 1) if TRANSPOSED[k] else g
        (delta[k], new_m[k], new_v[k]), _ = adamw(w[k], grad[k], m[k], v[k], name=f"adamw_{k}")
    return (loss, grad_x, *[grad[k] for k in WEIGHTS], *[delta[k] for k in WEIGHTS],
            *[new_m[k] for k in WEIGHTS], *[new_v[k] for k in WEIGHTS])
```

```python
import functools
import math

import jax
import jax.numpy as jnp
from jax import lax
from jax.experimental import pallas as pl
from jax.experimental.pallas import tpu as pltpu

F32 = jnp.float32
BF16 = jnp.bfloat16

D_MODEL = 1024
DEPTH = 2
D_A = 384
D_B = 384
D_C = 256
D_IN = 2 * D_A + 2 * D_B + D_C
N_HEADS_B = 4
HEAD_DIM_B = 96
GROUP_DIM_C = 64
CONV_WIDTH = 31
CHUNK = 128
D_FF = 2816
RMS_EPS = 1e-6
LN_EPS = 1e-5
HALO = 32
TT = 2048
N_CHIPS = 4
N_DEV = 8
LANES = 128

ADAM_LR = 0.001
ADAM_B1 = 0.9
ADAM_B2 = 0.999
ADAM_EPS = 1e-08
ADAM_WD = 0.01
ADAM_STEP = 10

VMEM_LIMIT = 56 * 1024 * 1024
_INTERPRET = False

MESH = pl.DeviceIdType.MESH
ANY = pl.BlockSpec(memory_space=pl.ANY)
SDS = jax.ShapeDtypeStruct


def _sigmoid(x):
    return 0.5 * jnp.tanh(0.5 * x) + 0.5


_GELU_C = math.sqrt(2.0 / math.pi)


def _gelu(x):
    return 0.5 * x * (1.0 + jnp.tanh(_GELU_C * (x + 0.044715 * x * x * x)))


def _gelu_grad(x):
    t = jnp.tanh(_GELU_C * (x + 0.044715 * x * x * x))
    return 0.5 * (1.0 + t) + 0.5 * x * (1.0 - t * t) * _GELU_C * (1.0 + 3 * 0.044715 * x * x)


def _dot(a, b):
    return jnp.dot(a, b, preferred_element_type=F32)


def _dot_nt(a, b):
    return lax.dot_general(a, b, (((1,), (1,)), ((), ())), preferred_element_type=F32)


def _dot_tn(a, b):
    return lax.dot_general(a, b, (((0,), (0,)), ((), ())), preferred_element_type=F32)


class Piece:
    def __init__(self, operands, out_shapes, aliases, n_sems, start, finish):
        self.operands, self.out_shapes, self.aliases, self.n_sems = operands, out_shapes, aliases, n_sems
        self.start, self.finish = start, finish


def _place():
    x, y, c = lax.axis_index("x"), lax.axis_index("y"), lax.axis_index("c")
    chips = [(1 - x, y), (x, 1 - y), (1 - x, 1 - y)]
    return x, y, c, chips


def _remote(src, dst, send_sem, recv_sem, to):
    return pltpu.make_async_remote_copy(src_ref=src, dst_ref=dst, send_sem=send_sem, recv_sem=recv_sem,
                                        device_id=to, device_id_type=MESH)


def _rows(r, q, nq):
    return pl.ds(q * (r // nq), r // nq)


def gather_ici(src, g=None, q=0, nq=1):
    rows = _rows(src.shape[1], q, nq)

    def copies(ins, outs, sem):
        x, y, c, chips = _place()
        j = 2 * x + y
        return [(_remote(ins[0].at[c, rows], outs[0].at[j, c, rows], sem(k), sem(3 + k), (px, py, c)),
                 outs[0].at[2 * px + py, c, rows], (px, py, c)) for k, (px, py) in enumerate(chips)]

    def start(ins, outs, sem):
        for cp, _, _ in copies(ins, outs, sem):
            cp.start()

    def finish(ins, outs, sem):
        cps = copies(ins, outs, sem)
        for k, (_, landed, frm) in enumerate(cps):
            _remote(landed, landed, sem(k), sem(3 + k), frm).wait_recv()
        for cp, _, _ in cps:
            cp.wait_send()

    shape = SDS((N_CHIPS,) + src.shape, src.dtype)
    if g is None:
        return Piece([src], [shape], {}, 6, start, finish)
    return Piece([src, g], [shape], {1: 0}, 6, start, finish)


def gather_d2d(g, q=0, nq=1, sem0=0):
    rows = _rows(g.shape[2], q, nq)

    def copies(outs, sem):
        x, y, c, chips = _place()
        return [(_remote(outs[0].at[2 * px + py, c, rows], outs[0].at[2 * px + py, c, rows],
                         sem(sem0 + k), sem(sem0 + 3 + k), (x, y, 1 - c)),
                 outs[0].at[2 * px + py, 1 - c, rows], (x, y, 1 - c)) for k, (px, py) in enumerate(chips)]

    def start(ins, outs, sem):
        for cp, _, _ in copies(outs, sem):
            cp.start()

    def finish(ins, outs, sem):
        cps = copies(outs, sem)
        for k, (_, landed, frm) in enumerate(cps):
            _remote(landed, landed, sem(sem0 + k), sem(sem0 + 3 + k), frm).wait_recv()
        for cp, _, _ in cps:
            cp.wait_send()

    return Piece([g], [SDS(g.shape, g.dtype)], {0: 0}, 6, start, finish)


def gather_both(src, g=None, q=0, nq=1):
    a = gather_ici(src, g, q, nq)
    b = gather_d2d(jax.ShapeDtypeStruct((N_CHIPS,) + src.shape, src.dtype), q, nq, sem0=6)

    def finish(ins, outs, sem):
        a.finish(ins, outs, sem)
        b.start(ins, outs, sem)
        b.finish(ins, outs, sem)

    return Piece(a.operands, a.out_shapes, a.aliases, 12, a.start, finish)


def gather_stages(src, g, stages, nq):
    shape = SDS((N_CHIPS,) + src.shape, src.dtype)
    make = {"ici": lambda q: gather_ici(src, None, q, nq), "d2d": lambda q: gather_d2d(shape, q, nq),
            "both": lambda q: gather_both(src, None, q, nq)}
    subs, n_sems = [], 0
    for stage, q in stages:
        sub = make[stage](q)
        subs.append((sub, n_sems))
        n_sems += sub.n_sems

    def run(method):
        def go(ins, outs, sem):
            for sub, base in subs:
                getattr(sub, method)(ins, outs, functools.partial(lambda k, base: sem(base + k), base=base))
        return go

    if g is None:
        return Piece([src], [shape], {}, n_sems, run("start"), run("finish"))
    return Piece([src, g], [shape], {1: 0}, n_sems, run("start"), run("finish"))


def reduce_d2d(g):
    def copies(ins, outs, sem):
        x, y, c, _ = _place()
        return [_remote(ins[0].at[j, 1 - c], outs[0].at[j], sem(j), sem(4 + j), (x, y, 1 - c)) for j in range(N_CHIPS)]

    def start(ins, outs, sem):
        for cp in copies(ins, outs, sem):
            cp.start()

    def finish(ins, outs, sem):
        cps = copies(ins, outs, sem)
        for cp in cps:
            cp.wait_recv()
        for cp in cps:
            cp.wait_send()

    return Piece([g], [SDS((N_CHIPS,) + g.shape[2:], g.dtype)], {}, 8, start, finish)


def reduce_ici(h):
    def copies(ins, outs, sem):
        x, y, c, chips = _place()
        j = 2 * x + y
        return [(_remote(ins[0].at[2 * px + py], outs[0].at[j], sem(k), sem(3 + k), (px, py, c)),
                 outs[0].at[2 * px + py], (px, py, c)) for k, (px, py) in enumerate(chips)]

    def start(ins, outs, sem):
        for cp, _, _ in copies(ins, outs, sem):
            cp.start()

    def finish(ins, outs, sem):
        cps = copies(ins, outs, sem)
        for k, (_, landed, frm) in enumerate(cps):
            _remote(landed, landed, sem(k), sem(3 + k), frm).wait_recv()
        for cp, _, _ in cps:
            cp.wait_send()

    return Piece([h], [SDS(h.shape, h.dtype)], {}, 6, start, finish)


def exchange_d2d(g):
    def copy(outs, sem):
        x, y, c, _ = _place()
        return _remote(outs[0].at[c], outs[0].at[c], sem(0), sem(1), (x, y, 1 - c)), outs[0].at[1 - c], (x, y, 1 - c)

    def start(ins, outs, sem):
        copy(outs, sem)[0].start()

    def finish(ins, outs, sem):
        cp, landed, frm = copy(outs, sem)
        _remote(landed, landed, sem(0), sem(1), frm).wait_recv()
        cp.wait_send()

    return Piece([g], [SDS(g.shape, g.dtype)], {0: 0}, 2, start, finish)


def _call(body, *, name, grid, in_specs, out_specs, out_shape, args, scratch_shapes=(), pieces=(), prefetch=0):
    n_in, n_out, n_scr = len(in_specs), len(out_specs), len(scratch_shapes)
    c_ops = [a for p in pieces for a in p.operands]
    c_outs = [s for p in pieces for s in p.out_shapes]
    n_sems = sum(p.n_sems for p in pieces)
    aliases = {}
    op_off, out_off = prefetch + n_in, n_out
    for p in pieces:
        for i, o in p.aliases.items():
            aliases[op_off + i] = out_off + o
        op_off += len(p.operands)
        out_off += len(p.out_shapes)

    def wrapped(*refs):
        pre, refs = refs[:prefetch], refs[prefetch:]
        ins, cin = refs[:n_in], refs[n_in:n_in + len(c_ops)]
        o0 = n_in + len(c_ops)
        outs, cout = refs[o0:o0 + n_out], refs[o0 + n_out:o0 + n_out + len(c_outs)]
        s0 = o0 + n_out + len(c_outs)
        scr = refs[s0:s0 + n_scr]

        def each(method):
            sems = refs[s0 + n_scr]
            i_off = o_off = s_off = 0
            for p in pieces:
                getattr(p, method)(cin[i_off:i_off + len(p.operands)], cout[o_off:o_off + len(p.out_shapes)],
                                   functools.partial(lambda k, base: sems.at[base + k], base=s_off))
                i_off, o_off, s_off = i_off + len(p.operands), o_off + len(p.out_shapes), s_off + p.n_sems

        if pieces and grid:
            ids = [pl.program_id(a) for a in range(len(grid))]
            first = functools.reduce(jnp.logical_and, [i == 0 for i in ids])
            last = functools.reduce(jnp.logical_and, [i == g - 1 for i, g in zip(ids, grid)])
            pl.when(first)(lambda: each("start"))
        elif pieces:
            each("start")
        if body is not None:
            body(*pre, *ins, *outs, *scr)
        if pieces and grid:
            pl.when(last)(lambda: each("finish"))
        elif pieces:
            each("finish")

    scratch = list(scratch_shapes) + ([pltpu.SemaphoreType.DMA((n_sems,))] if pieces else [])
    all_in = list(in_specs) + [ANY] * len(c_ops)
    all_out = list(out_specs) + [ANY] * len(c_outs)
    shapes = list(out_shape) + c_outs
    kw = dict(name=name, out_shape=shapes, input_output_aliases=aliases, interpret=_INTERPRET)
    if grid:
        kw["compiler_params"] = pltpu.CompilerParams(dimension_semantics=("arbitrary",) * len(grid),
                                                     vmem_limit_bytes=VMEM_LIMIT)
    if prefetch:
        kw["grid_spec"] = pltpu.PrefetchScalarGridSpec(num_scalar_prefetch=prefetch, grid=grid, in_specs=all_in,
                                                       out_specs=all_out, scratch_shapes=scratch)
    else:
        kw.update(in_specs=all_in, out_specs=all_out, scratch_shapes=scratch)
        if grid:
            kw["grid"] = grid
    res = pl.pallas_call(wrapped, **kw)(*args, *c_ops)
    outs, rest = list(res[:n_out]), list(res[n_out:])
    couts = []
    for p in pieces:
        couts.append(rest[:len(p.out_shapes)])
        rest = rest[len(p.out_shapes):]
    return outs, couts


def comm_only(pieces, *, name):
    return _call(None, name=name, grid=(), in_specs=[], out_specs=[], out_shape=[], args=[], pieces=pieces)[1]


def rmsnorm_fwd(x, g, *, name, pieces=()):
    t, d = x.shape
    tm = min(512, t)

    def body(x_ref, g_ref, o_ref):
        xv = x_ref[...]
        rstd = lax.rsqrt(jnp.mean(xv * xv, axis=-1, keepdims=True) + RMS_EPS)
        o_ref[...] = (xv * rstd * g_ref[...]).astype(BF16)

    outs, couts = _call(
        body, name=name, grid=(t // tm,),
        in_specs=[pl.BlockSpec((tm, d), lambda i: (i, 0)), pl.BlockSpec((1, d), lambda i: (0, 0))],
        out_specs=[pl.BlockSpec((tm, d), lambda i: (i, 0))], out_shape=[SDS((t, d), BF16)],
        args=[x, g], pieces=pieces)
    return outs[0], couts


def loss_head(x, g, target, *, name):
    t, d = x.shape
    tm = min(512, t)

    def body(x_ref, g_ref, t_ref, loss_ref, dx_ref, dxb_ref, dg_ref):
        i = pl.program_id(0)
        xv = x_ref[...]
        gv = g_ref[...]
        rstd = lax.rsqrt(jnp.mean(xv * xv, axis=-1, keepdims=True) + RMS_EPS)
        xhat = xv * rstd
        err = xhat * gv - t_ref[...]
        dy = err * (1.0 / d)
        dxhat = dy * gv
        dx = rstd * (dxhat - xhat * jnp.mean(dxhat * xhat, axis=-1, keepdims=True))
        dx_ref[...] = dx
        dxb_ref[...] = dx.astype(BF16)

        @pl.when(i == 0)
        def _():
            dg_ref[...] = jnp.zeros_like(dg_ref)
            loss_ref[...] = jnp.zeros_like(loss_ref)

        dg_ref[...] += jnp.sum(dy * xhat, axis=0, keepdims=True)
        per_tok = jnp.sum(err * err, axis=-1, keepdims=True) * (0.5 / d)
        loss_ref[...] += jnp.sum(per_tok, axis=0, keepdims=True)

    row = pl.BlockSpec((tm, d), lambda i: (i, 0))
    vec = pl.BlockSpec((1, d), lambda i: (0, 0))
    one = pl.BlockSpec((1, 1), lambda i: (0, 0))
    outs, _ = _call(
        body, name=name, grid=(t // tm,), in_specs=[row, vec, row], out_specs=[one, row, row, vec],
        out_shape=[SDS((1, 1), F32), SDS((t, d), F32), SDS((t, d), BF16), SDS((1, d), F32)],
        args=[x, g, target])
    return outs


def mm_nn(a, b, *, name, tm, tn, out_dtype, residual=None, pieces=()):
    m, k = a.shape
    n = b.shape[1]
    tm, tn = min(tm, m), min(tn, n)
    has_res = residual is not None

    def body(a_ref, b_ref, *rest):
        o_ref = rest[-1]
        acc = _dot(a_ref[...], b_ref[...])
        if has_res:
            acc = acc + rest[0][...]
        o_ref[...] = acc.astype(o_ref.dtype)

    in_specs = [pl.BlockSpec((tm, k), lambda j, i: (i, 0)), pl.BlockSpec((k, tn), lambda j, i: (0, j))]
    args = [a, b]
    if has_res:
        in_specs.append(pl.BlockSpec((tm, tn), lambda j, i: (i, j)))
        args.append(residual)
    outs, couts = _call(
        body, name=name, grid=(n // tn, m // tm), in_specs=in_specs,
        out_specs=[pl.BlockSpec((tm, tn), lambda j, i: (i, j))], out_shape=[SDS((m, n), out_dtype)],
        args=args, pieces=pieces)
    return outs[0], couts


def mm_nt(a, b, *, name, tm, tn, out_dtype, pieces=()):
    m, k = a.shape
    n = b.shape[0]
    tm, tn = min(tm, m), min(tn, n)

    def body(a_ref, b_ref, o_ref):
        o_ref[...] = _dot_nt(a_ref[...], b_ref[...]).astype(o_ref.dtype)

    outs, couts = _call(
        body, name=name, grid=(n // tn, m // tm),
        in_specs=[pl.BlockSpec((tm, k), lambda j, i: (i, 0)), pl.BlockSpec((tn, k), lambda j, i: (j, 0))],
        out_specs=[pl.BlockSpec((tm, tn), lambda j, i: (i, j))], out_shape=[SDS((m, n), out_dtype)],
        args=[a, b], pieces=pieces)
    return outs[0], couts


def mm_tn(a, b, *, name, tk, tn, tt, out_dtype, pieces=()):
    t, k = a.shape
    n = b.shape[1]
    tk, tn, tt = min(tk, k), min(tn, n), min(tt, t)
    nt = t // tt

    def body(a_ref, b_ref, o_ref, acc_ref):
        s = pl.program_id(2)

        @pl.when(s == 0)
        def _():
            acc_ref[...] = jnp.zeros_like(acc_ref)

        acc_ref[...] += _dot_tn(a_ref[...], b_ref[...])

        @pl.when(s == nt - 1)
        def _():
            o_ref[...] = acc_ref[...].astype(o_ref.dtype)

    outs, couts = _call(
        body, name=name, grid=(k // tk, n // tn, nt),
        in_specs=[pl.BlockSpec((tt, tk), lambda i, j, s: (s, i)), pl.BlockSpec((tt, tn), lambda i, j, s: (s, j))],
        out_specs=[pl.BlockSpec((tk, tn), lambda i, j, s: (i, j))], out_shape=[SDS((k, n), out_dtype)],
        scratch_shapes=[pltpu.VMEM((tk, tn), F32)], args=[a, b], pieces=pieces)
    return outs[0], couts


def swiglu_fwd(h, wt, *, name, tm, tn, pieces=()):
    t, d = h.shape
    ff = wt.shape[0] // 2
    tm, tn = min(tm, t), min(tn, ff)
    nb = ff // tn

    def body(h_ref, wg_ref, wu_ref, gu_ref, act_ref):
        hv = h_ref[...]
        gate = _dot_nt(hv, wg_ref[...])
        up = _dot_nt(hv, wu_ref[...])
        gu_ref[0] = gate.astype(BF16)
        gu_ref[1] = up.astype(BF16)
        act_ref[...] = (gate * _sigmoid(gate) * up).astype(BF16)

    outs, couts = _call(
        body, name=name, grid=(nb, t // tm),
        in_specs=[pl.BlockSpec((tm, d), lambda j, i: (i, 0)),
                  pl.BlockSpec((tn, d), lambda j, i: (j, 0)),
                  pl.BlockSpec((tn, d), lambda j, i: (j + nb, 0))],
        out_specs=[pl.BlockSpec((2, tm, tn), lambda j, i: (0, i, j)), pl.BlockSpec((tm, tn), lambda j, i: (i, j))],
        out_shape=[SDS((2, t, ff), BF16), SDS((t, ff), BF16)], args=[h, wt, wt], pieces=pieces)
    return outs, couts


def swiglu_bwd(dx, w_down, gu, *, name, tm, tn, pieces=()):
    t, d = dx.shape
    ff = w_down.shape[0]
    tm, tn = min(tm, t), min(tn, ff)

    def body(dx_ref, w_ref, gu_ref, o_ref):
        dact = _dot_nt(dx_ref[...], w_ref[...])
        gate = gu_ref[0].astype(F32)
        up = gu_ref[1].astype(F32)
        sg = _sigmoid(gate)
        o_ref[0] = (dact * up * sg * (1.0 + gate * (1.0 - sg))).astype(BF16)
        o_ref[1] = (dact * gate * sg).astype(BF16)

    outs, couts = _call(
        body, name=name, grid=(ff // tn, t // tm),
        in_specs=[pl.BlockSpec((tm, d), lambda j, i: (i, 0)), pl.BlockSpec((tn, d), lambda j, i: (j, 0)),
                  pl.BlockSpec((2, tm, tn), lambda j, i: (0, i, j))],
        out_specs=[pl.BlockSpec((2, tm, tn), lambda j, i: (0, i, j))], out_shape=[SDS((2, t, ff), BF16)],
        args=[dx, w_down, gu], pieces=pieces)
    return outs[0], couts


def mm_norm_bwd(a, wt, x, g, dres, *, name, tm, pieces=()):
    parts = a.shape[0] if a.ndim == 3 else 1
    t, kp = a.shape[-2:]
    d = wt.shape[1]
    tm = min(tm, t)

    def body(a_ref, w_ref, x_ref, g_ref, dres_ref, dx_ref, dxb_ref, dg_ref):
        i = pl.program_id(0)
        if parts == 1:
            dh = _dot(a_ref[...], w_ref[...])
        else:
            dh = _dot(a_ref[0], w_ref[0:kp, :])
            for q in range(1, parts):
                dh = dh + _dot(a_ref[q], w_ref[q * kp:(q + 1) * kp, :])
        xv = x_ref[...]
        rstd = lax.rsqrt(jnp.mean(xv * xv, axis=-1, keepdims=True) + RMS_EPS)
        xhat = xv * rstd
        dxhat = dh * g_ref[...]
        dx = dres_ref[...] + rstd * (dxhat - xhat * jnp.mean(dxhat * xhat, axis=-1, keepdims=True))
        dx_ref[...] = dx
        dxb_ref[...] = dx.astype(BF16)

        @pl.when(i == 0)
        def _():
            dg_ref[...] = jnp.zeros_like(dg_ref)

        dg_ref[...] += jnp.sum(dh * xhat, axis=0, keepdims=True)

    a_spec = (pl.BlockSpec((tm, kp), lambda i: (i, 0)) if parts == 1
              else pl.BlockSpec((parts, tm, kp), lambda i: (0, i, 0)))
    row = pl.BlockSpec((tm, d), lambda i: (i, 0))
    vec = pl.BlockSpec((1, d), lambda i: (0, 0))
    outs, couts = _call(
        body, name=name, grid=(t // tm,),
        in_specs=[a_spec, pl.BlockSpec((parts * kp, d), lambda i: (0, 0)), row, vec, row],
        out_specs=[row, row, vec], out_shape=[SDS((t, d), F32), SDS((t, d), BF16), SDS((1, d), F32)],
        args=[a, wt, x, g, dres], pieces=pieces)
    return outs, couts


def mm_gu_tn(dgu, h, *, name, tn, tt, pieces=()):
    t, d = h.shape
    ff = dgu.shape[2]
    tn, tt = min(tn, ff), min(tt, t)
    nb = ff // tn
    nt = t // tt

    def body(a_ref, h_ref, o_ref, acc_ref):
        s = pl.program_id(1)

        @pl.when(s == 0)
        def _():
            acc_ref[...] = jnp.zeros_like(acc_ref)

        acc_ref[...] += _dot_tn(a_ref[...], h_ref[...])

        @pl.when(s == nt - 1)
        def _():
            o_ref[...] = acc_ref[...].astype(o_ref.dtype)

    outs, couts = _call(
        body, name=name, grid=(2 * nb, nt),
        in_specs=[pl.BlockSpec((None, tt, tn), lambda j, s: (j // nb, s, j % nb)),
                  pl.BlockSpec((tt, d), lambda j, s: (s, 0))],
        out_specs=[pl.BlockSpec((tn, d), lambda j, s: (j, 0))], out_shape=[SDS((2 * ff, d), BF16)],
        scratch_shapes=[pltpu.VMEM((tn, d), F32)], args=[dgu, h], pieces=pieces)
    return outs[0], couts


def _head_masks(shape):
    lane = lax.broadcasted_iota(jnp.int32, shape, 1)
    return [(lane >= h * HEAD_DIM_B) & (lane < (h + 1) * HEAD_DIM_B) for h in range(N_HEADS_B)]


def _causal_rows(ws):
    r = lax.broadcasted_iota(jnp.int32, ws.shape, 0) % CHUNK
    c = lax.broadcasted_iota(jnp.int32, ws.shape, 1)
    return jnp.where(c <= r, ws, jnp.zeros_like(ws))


def _pool_window(shape):
    lane = lax.broadcasted_iota(jnp.int32, shape, 1)
    return jnp.left_shift(2, lane // GROUP_DIM_C)


def _layer_norm_fwd(x, g, b):
    mu = jnp.mean(x, axis=-1, keepdims=True)
    xc = x - mu
    rstd = lax.rsqrt(jnp.mean(xc * xc, axis=-1, keepdims=True) + LN_EPS)
    xhat = xc * rstd
    return xhat * g + b, xhat, rstd


def _layer_norm_bwd(dy, xhat, rstd, g):
    dxhat = dy * g
    return rstd * (dxhat - jnp.mean(dxhat, axis=-1, keepdims=True)
                   - xhat * jnp.mean(dxhat * xhat, axis=-1, keepdims=True))


def _gate_mix(ws_masked, vl_chunk, masks):
    out = _dot(ws_masked, vl_chunk.astype(BF16))
    s = jnp.zeros((CHUNK, D_B), F32)
    for h in range(N_HEADS_B):
        s = s + jnp.where(masks[h], out[h * CHUNK:(h + 1) * CHUNK], 0.0)
    return s


SUB = 8


def _shifted_copies(ref, n):
    for b in range(1, SUB):
        ref[b, 0:n, :] = ref[0, pl.ds(b, n), :]


def _tap(ref, off, n):
    a, b = divmod(off, SUB)
    return ref[b, pl.ds(SUB * a, n), :]


def _pick_window(s2, s4, s8, s16):
    grp = lax.broadcasted_iota(jnp.int32, s2.shape, 1) // GROUP_DIM_C
    return jnp.where(grp == 0, s2, jnp.where(grp == 1, s4, jnp.where(grp == 2, s8, s16)))


def _trailing_window_sums(src_ref, l2_ref, l4_ref, l8_ref, tm):
    n = HALO + tm
    l2_ref[8:n, :] = src_ref[8:n, :] + src_ref[pl.ds(7, n - 8), :]
    l4_ref[16:n, :] = l2_ref[16:n, :] + l2_ref[pl.ds(14, n - 16), :]
    l8_ref[24:n, :] = l4_ref[24:n, :] + l4_ref[pl.ds(20, n - 24), :]
    s16 = l8_ref[HALO:n, :] + l8_ref[HALO - 8:n - 8, :]
    return _pick_window(l2_ref[HALO:n, :], l4_ref[HALO:n, :], l8_ref[HALO:n, :], s16)


def _leading_window_sums(src_ref, l2_ref, l4_ref, l8_ref, tm):
    n = HALO + tm
    l2_ref[0:n - 8, :] = src_ref[0:n - 8, :] + src_ref[pl.ds(1, n - 8), :]
    l4_ref[0:n - 16, :] = l2_ref[0:n - 16, :] + l2_ref[pl.ds(2, n - 16), :]
    l8_ref[0:n - 24, :] = l4_ref[0:n - 24, :] + l4_ref[pl.ds(4, n - 24), :]
    s16 = l8_ref[0:tm, :] + l8_ref[8:tm + 8, :]
    return _pick_window(l2_ref[0:tm, :], l4_ref[0:tm, :], l8_ref[0:tm, :], s16)


def _mixer_specs(tm, nt, seq):
    hb = tm // HALO

    def cur(c):
        return pl.BlockSpec((tm, c), lambda b, i: (b * nt + i, 0))

    def prev(c):
        return pl.BlockSpec((HALO, c), lambda b, i: (jnp.maximum((b * nt + i) * hb - 1, 0), 0))

    def nxt(c):
        last = (2 * seq) // HALO - 1
        return pl.BlockSpec((HALO, c), lambda b, i: (jnp.minimum((b * nt + i + 1) * hb, last), 0))

    def full(shape):
        return pl.BlockSpec(shape, lambda b, i: tuple(0 for _ in shape))

    return cur, prev, nxt, full


_MIX_PARAM_SHAPES = [(32, D_A), (1, D_A), (1, D_A), (1, D_A), (D_A, D_A), (1, D_B), (1, D_B),
                     (N_HEADS_B * CHUNK, CHUNK), (CHUNK, D_B), (D_C, D_C), (1, D_C)]


def mixer_fwd(z, mp, *, seq, name, tm=512, pieces=()):
    t = z.shape[0]
    tm = min(tm, seq)
    nt = seq // tm
    cur, prev, _, full = _mixer_specs(tm, nt, seq)

    def body(zc_ref, zp_ref, cw_ref, cb_ref, clg_ref, clb_ref, wpw_ref, slg_ref, slb_ref, ws_ref, bias_ref,
             wp_ref, ps_ref, o_ref, cv_ref, ys_ref, zs_ref, l2_ref, l4_ref, l8_ref):
        i = pl.program_id(1)
        has_prev = i > 0
        yp = zp_ref[:, 0:D_A].astype(F32) * _sigmoid(zp_ref[:, D_A:2 * D_A].astype(F32))
        ys_ref[0, 0:HALO, :] = jnp.where(has_prev, yp, 0.0)
        ys_ref[0, HALO:HALO + tm, :] = zc_ref[:, 0:D_A].astype(F32) * _sigmoid(zc_ref[:, D_A:2 * D_A].astype(F32))
        _shifted_copies(ys_ref, tm + HALO - SUB)
        acc = jnp.zeros((tm, D_A), F32) + cb_ref[...]
        for k in range(CONV_WIDTH):
            acc = acc + cw_ref[k:k + 1, :] * _tap(ys_ref, HALO - (CONV_WIDTH - 1) + k, tm)
        cv_ref[...] = acc
        ln, _, _ = _layer_norm_fwd(acc, clg_ref[...], clb_ref[...])
        sl = ln * _sigmoid(ln)
        o_ref[:, 0:D_A] = _dot(sl.astype(BF16), wpw_ref[...]).astype(BF16)
        gz = _gelu(zc_ref[:, 2 * D_A:2 * D_A + 2 * D_B].astype(F32))
        u = gz[:, :D_B]
        vl, _, _ = _layer_norm_fwd(gz[:, D_B:], slg_ref[...], slb_ref[...])
        wsm = _causal_rows(ws_ref[...])
        masks = _head_masks((CHUNK, D_B))
        for c in range(tm // CHUNK):
            rows = slice(c * CHUNK, (c + 1) * CHUNK)
            s = _gate_mix(wsm, vl[rows], masks) + bias_ref[...]
            o_ref[rows, D_A:D_A + D_B] = (u[rows] * s).astype(BF16)
        c0 = 2 * D_A + 2 * D_B
        zs_ref[0:HALO, :] = jnp.where(has_prev, zp_ref[:, c0:c0 + D_C].astype(F32), 0.0)
        zcur = zc_ref[:, c0:c0 + D_C].astype(F32)
        zs_ref[HALO:HALO + tm, :] = zcur
        win = _pool_window((tm, D_C))
        wsum = _trailing_window_sums(zs_ref, l2_ref, l4_ref, l8_ref, tm)
        pos = i * tm + lax.broadcasted_iota(jnp.int32, (tm, D_C), 0)
        cnt = jnp.minimum(pos + 1, win).astype(F32)
        p = wsum / cnt - zcur
        y = _dot(p.astype(BF16), wp_ref[...])
        o_ref[:, D_A + D_B:D_A + D_B + D_C] = (y * ps_ref[...]).astype(BF16)

    in_specs = [cur(D_IN), prev(D_IN)] + [full(s) for s in _MIX_PARAM_SHAPES]
    outs, couts = _call(
        body, name=name, grid=(2, nt), in_specs=in_specs, out_specs=[cur(D_MODEL), cur(D_A)],
        out_shape=[SDS((t, D_MODEL), BF16), SDS((t, D_A), F32)],
        scratch_shapes=[pltpu.VMEM((SUB, HALO + tm, D_A), F32)] + [pltpu.VMEM((HALO + tm, D_C), F32)] * 4,
        args=[z, z, *mp], pieces=pieces)
    return outs, couts


def mixer_bwd(z, conv, dm, mp, *, seq, name, tm=256, pieces=()):
    t = z.shape[0]
    tm = min(tm, seq)
    nt = seq // tm
    ext = tm + HALO
    cur, prev, nxt, full = _mixer_specs(tm, nt, seq)
    grad_shapes = [(32, D_A), (1, D_A), (1, D_A), (1, D_A), (D_A, D_A), (1, D_B), (1, D_B),
                   (N_HEADS_B * CHUNK, CHUNK), (CHUNK, CHUNK), (D_C, D_C), (1, D_C)]

    def body(zc_ref, zp_ref, cvc_ref, cvn_ref, dmc_ref, dmn_ref, cw_ref, cb_ref, clg_ref, clb_ref, wpw_ref, slg_ref,
             slb_ref, ws_ref, bias_ref, wp_ref, ps_ref,
             dz_ref, dcw_ref, dcb_ref, dclg_ref, dclb_ref, dwpw_ref, dslg_ref, dslb_ref, dws_ref, dbs_ref, dwp_ref,
             dps_ref, ys_ref, dcs_ref, zs_ref, qs_ref, l2_ref, l4_ref, l8_ref):
        b = pl.program_id(0)
        i = pl.program_id(1)
        has_prev = i > 0
        has_next = i < nt - 1
        grads = [dcw_ref, dcb_ref, dclg_ref, dclb_ref, dwpw_ref, dslg_ref, dslb_ref, dws_ref, dbs_ref, dwp_ref, dps_ref]

        @pl.when((b == 0) & (i == 0))
        def _():
            for r in grads:
                r[...] = jnp.zeros_like(r)

        ext_row = lax.broadcasted_iota(jnp.int32, (ext, 1), 0)
        live = (ext_row < tm) | has_next

        yp = zp_ref[:, 0:D_A].astype(F32) * _sigmoid(zp_ref[:, D_A:2 * D_A].astype(F32))
        ys_ref[0, 0:HALO, :] = jnp.where(has_prev, yp, 0.0)
        a_cur = zc_ref[:, 0:D_A].astype(F32)
        sig_cur = _sigmoid(zc_ref[:, D_A:2 * D_A].astype(F32))
        ys_ref[0, HALO:HALO + tm, :] = a_cur * sig_cur
        _shifted_copies(ys_ref, tm + HALO - SUB)
        acc = jnp.concatenate([cvc_ref[...], cvn_ref[...]], axis=0)
        ln, xhat, rstd = _layer_norm_fwd(acc, clg_ref[...], clb_ref[...])
        sg = _sigmoid(ln)
        sl = ln * sg
        dya = jnp.concatenate([dmc_ref[:, 0:D_A], dmn_ref[:, 0:D_A]], axis=0)
        dsl = _dot_nt(dya, wpw_ref[...])
        dln = dsl * sg * (1.0 + ln * (1.0 - sg))
        dc = _layer_norm_bwd(dln, xhat, rstd, clg_ref[...])
        dc = jnp.where(live, dc, 0.0)
        dcs_ref[0] = dc
        _shifted_copies(dcs_ref, ext - SUB)
        dwpw_ref[...] += _dot_tn(sl[:tm].astype(BF16), dya[:tm])
        dclg_ref[...] += jnp.sum(dln[:tm] * xhat[:tm], axis=0, keepdims=True)
        dclb_ref[...] += jnp.sum(dln[:tm], axis=0, keepdims=True)
        dcb_ref[...] += jnp.sum(dc[:tm], axis=0, keepdims=True)
        dy = jnp.zeros((tm, D_A), F32)
        for k in range(CONV_WIDTH):
            off = HALO - (CONV_WIDTH - 1) + k
            dcw_ref[k:k + 1, :] += jnp.sum(dc[:tm] * _tap(ys_ref, off, tm), axis=0, keepdims=True)
            dy = dy + cw_ref[k:k + 1, :] * _tap(dcs_ref, CONV_WIDTH - 1 - k, tm)
        dz_ref[:, 0:D_A] = (dy * sig_cur).astype(BF16)
        dz_ref[:, D_A:2 * D_A] = (dy * a_cur * sig_cur * (1.0 - sig_cur)).astype(BF16)

        zb = zc_ref[:, 2 * D_A:2 * D_A + 2 * D_B].astype(F32)
        gz = _gelu(zb)
        u = gz[:, :D_B]
        vl, vhat, vrstd = _layer_norm_fwd(gz[:, D_B:], slg_ref[...], slb_ref[...])
        dyb = dmc_ref[:, D_A:D_A + D_B].astype(F32)
        wsm = _causal_rows(ws_ref[...])
        masks = _head_masks((CHUNK, D_B))
        ds_all = dyb * u
        du_parts, dvl_parts = [], []
        for c in range(tm // CHUNK):
            rows = slice(c * CHUNK, (c + 1) * CHUNK)
            vlc = vl[rows].astype(BF16)
            s = _gate_mix(wsm, vl[rows], masks) + bias_ref[...]
            du_parts.append(dyb[rows] * s)
            ds = ds_all[rows]
            stack = jnp.concatenate([jnp.where(masks[h], ds, 0.0) for h in range(N_HEADS_B)], axis=0).astype(BF16)
            dvl_parts.append(_dot_tn(wsm, stack))
            dws_ref[...] += _dot_nt(stack, vlc)
        du = jnp.concatenate(du_parts, axis=0)
        dvl = jnp.concatenate(dvl_parts, axis=0)
        dbias = jnp.zeros((CHUNK, D_B), F32)
        for c in range(tm // CHUNK):
            dbias = dbias + ds_all[c * CHUNK:(c + 1) * CHUNK]
        lane = lax.broadcasted_iota(jnp.int32, (CHUNK, CHUNK), 1)
        dbs = jnp.zeros((CHUNK, CHUNK), F32)
        for h in range(N_HEADS_B):
            col = jnp.sum(jnp.where(masks[h], dbias, 0.0), axis=1, keepdims=True)
            dbs = dbs + jnp.where(lane == h, col, 0.0)
        dbs_ref[...] += dbs
        dslg_ref[...] += jnp.sum(dvl * vhat, axis=0, keepdims=True)
        dslb_ref[...] += jnp.sum(dvl, axis=0, keepdims=True)
        dv = _layer_norm_bwd(dvl, vhat, vrstd, slg_ref[...])
        gg = _gelu_grad(zb)
        dz_ref[:, 2 * D_A:2 * D_A + D_B] = (du * gg[:, :D_B]).astype(BF16)
        dz_ref[:, 2 * D_A + D_B:2 * D_A + 2 * D_B] = (dv * gg[:, D_B:]).astype(BF16)

        c0 = 2 * D_A + 2 * D_B
        m0 = D_A + D_B
        zs_ref[0:HALO, :] = jnp.where(has_prev, zp_ref[:, c0:c0 + D_C].astype(F32), 0.0)
        zcur = zc_ref[:, c0:c0 + D_C].astype(F32)
        zs_ref[HALO:HALO + tm, :] = zcur
        win = _pool_window((tm, D_C))
        wsum = _trailing_window_sums(zs_ref, l2_ref, l4_ref, l8_ref, tm)
        pos = i * tm + lax.broadcasted_iota(jnp.int32, (tm, D_C), 0)
        cnt = jnp.minimum(pos + 1, win).astype(F32)
        pb = (wsum / cnt - zcur).astype(BF16)
        y = _dot(pb, wp_ref[...])
        dyc = jnp.concatenate([dmc_ref[:, m0:m0 + D_C], dmn_ref[:, m0:m0 + D_C]], axis=0).astype(F32)
        dps_ref[...] += jnp.sum(dyc[:tm] * y, axis=0, keepdims=True)
        dyv = (dyc * ps_ref[...]).astype(BF16)
        dwp_ref[...] += _dot_tn(pb, dyv[:tm])
        dp = _dot_nt(dyv, wp_ref[...])
        win_e = _pool_window((ext, D_C))
        pos_e = i * tm + lax.broadcasted_iota(jnp.int32, (ext, D_C), 0)
        cnt_e = jnp.minimum(pos_e + 1, win_e).astype(F32)
        qs_ref[...] = jnp.where(live, dp / cnt_e, 0.0)
        dzc = _leading_window_sums(qs_ref, l2_ref, l4_ref, l8_ref, tm) - dp[:tm]
        dz_ref[:, c0:c0 + D_C] = dzc.astype(BF16)

        @pl.when((b == 1) & (i == nt - 1))
        def _():
            dws_ref[...] = _causal_rows(dws_ref[...])

    in_specs = ([cur(D_IN), prev(D_IN), cur(D_A), nxt(D_A), cur(D_MODEL), nxt(D_MODEL)]
                + [full(s) for s in _MIX_PARAM_SHAPES])
    out_specs = [cur(D_IN)] + [full(s) for s in grad_shapes]
    out_shape = [SDS((t, D_IN), BF16)] + [SDS(s, F32) for s in grad_shapes]
    outs, couts = _call(
        body, name=name, grid=(2, nt), in_specs=in_specs, out_specs=out_specs, out_shape=out_shape,
        scratch_shapes=[pltpu.VMEM((SUB, HALO + tm, D_A), F32), pltpu.VMEM((SUB, ext, D_A), F32)]
        + [pltpu.VMEM((ext, D_C), F32)] * 5,
        args=[z, z, conv, conv, dm, dm, *mp], pieces=pieces)
    return outs, couts


MIXER_SMALL = ["conv_w", "conv_b", "conv_ln_g", "conv_ln_b", "w_pw", "sg_ln_g", "sg_ln_b", "w_s", "b_s", "w_pool",
               "pool_scale"]


def _mixer_params(p, w_pw_bf16, l):
    wp_bd = jnp.zeros((D_C, D_C), F32)
    for g in range(D_C // GROUP_DIM_C):
        sl = slice(g * GROUP_DIM_C, (g + 1) * GROUP_DIM_C)
        wp_bd = wp_bd.at[sl, sl].set(p["w_pool"][l, g])
    return [
        jnp.pad(p["conv_w"][l], ((0, 32 - CONV_WIDTH), (0, 0))),
        p["conv_b"][l][None], p["conv_ln_g"][l][None], p["conv_ln_b"][l][None],
        w_pw_bf16,
        p["sg_ln_g"][l][None], p["sg_ln_b"][l][None],
        p["w_s"][l].reshape(N_HEADS_B * CHUNK, CHUNK).astype(BF16),
        jnp.repeat(p["b_s"][l].T, HEAD_DIM_B, axis=1),
        wp_bd.astype(BF16),
        p["pool_scale"][l][None],
    ]


def _mixer_grads(g):
    dcw, dcb, dclg, dclb, dwpw, dslg, dslb, dws, dbs, dwp, dps = g
    blocks = [dwp[i * GROUP_DIM_C:(i + 1) * GROUP_DIM_C, i * GROUP_DIM_C:(i + 1) * GROUP_DIM_C]
              for i in range(D_C // GROUP_DIM_C)]
    return [dcw[:CONV_WIDTH], dcb[0], dclg[0], dclb[0], dwpw, dslg[0], dslb[0],
            dws.reshape(N_HEADS_B, CHUNK, CHUNK), dbs[:, :N_HEADS_B].T, jnp.stack(blocks), dps[0]]


def _row_tile(r, cap=512):
    best = r
    for d in range(16, min(r, cap) + 1, 16):
        if r % d == 0:
            best = d
    return best if best <= cap else r


def add_core_halves(g, r1, core, *, name):
    _, _, r, c = g.shape
    tr = _row_tile(r)

    def body(core_ref, g_ref, r_ref, o_ref):
        o_ref[...] = (g_ref[...].astype(F32) + r_ref[...].astype(F32)).astype(o_ref.dtype)

    outs, _ = _call(
        body, name=name, grid=(N_CHIPS, r // tr), prefetch=1,
        in_specs=[pl.BlockSpec((None, None, tr, c), lambda j, i, s: (j, s[0], i, 0)),
                  pl.BlockSpec((None, tr, c), lambda j, i, s: (j, i, 0))],
        out_specs=[pl.BlockSpec((None, tr, c), lambda j, i, s: (j, i, 0))],
        out_shape=[SDS((N_CHIPS, r, c), g.dtype)], args=[core, g, r1])
    return outs[0]


def sum_chips(h, r2, place, *, name):
    _, r, c = h.shape
    tr = _row_tile(r, 256)

    def body(place_ref, h_ref, a_ref, b_ref, c_ref, o_ref):
        acc = h_ref[...].astype(F32) + a_ref[...].astype(F32)
        acc = acc + b_ref[...].astype(F32)
        o_ref[...] = acc + c_ref[...].astype(F32)

    def blk(k):
        return pl.BlockSpec((None, tr, c), lambda i, s: (jnp.bitwise_xor(s[0], k), i, 0))

    outs, _ = _call(
        body, name=name, grid=(r // tr,), prefetch=1, in_specs=[blk(0), blk(1), blk(2), blk(3)],
        out_specs=[pl.BlockSpec((None, tr, c), lambda i, s: (s[1], i, 0))],
        out_shape=[SDS((2, r, c), F32)], args=[place, h, r2, r2, r2])
    return outs[0]


def allreduce_small(p, *, name, pieces=()):
    _, n, _ = p.shape

    def body(p_ref, o_ref, land_ref, send1, recv1, send2, recv2):
        x, y, c = lax.axis_index("x"), lax.axis_index("y"), lax.axis_index("c")
        me = 4 * x + 2 * y + c

        def peer(r):
            return ((1 - x) if r & 4 else x, (1 - y) if r & 2 else y, (1 - c) if r & 1 else c)

        def index(r):
            px, py, pc = peer(r)
            return 4 * px + 2 * py + pc

        land_ref[me] = p_ref[me]
        first = [_remote(p_ref.at[index(r)], land_ref.at[me], send1.at[r - 1], recv1.at[r - 1], peer(r))
                 for r in range(1, N_DEV)]
        for cp in first:
            cp.start()
        for r in range(1, N_DEV):
            blk = land_ref.at[index(r)]
            _remote(blk, blk, send1.at[r - 1], recv1.at[r - 1], peer(r)).wait_recv()
        acc = land_ref[0]
        for d in range(1, N_DEV):
            acc = acc + land_ref[d]
        o_ref[me] = acc
        second = [_remote(o_ref.at[me], o_ref.at[me], send2.at[r - 1], recv2.at[r - 1], peer(r))
                  for r in range(1, N_DEV)]
        for cp in second:
            cp.start()
        for r in range(1, N_DEV):
            blk = o_ref.at[index(r)]
            _remote(blk, blk, send2.at[r - 1], recv2.at[r - 1], peer(r)).wait_recv()
        for cp in first + second:
            cp.wait_send()

    vm = pl.BlockSpec(memory_space=pltpu.VMEM)
    outs, couts = _call(
        body, name=name, grid=(), in_specs=[vm], out_specs=[vm], out_shape=[SDS(p.shape, F32)],
        scratch_shapes=[pltpu.VMEM(p.shape, F32)] + [pltpu.SemaphoreType.DMA((N_DEV - 1,))] * 4,
        args=[p], pieces=pieces)
    return outs[0], couts


def _adam_update(w, g, m, v):
    m_new = ADAM_B1 * m + (1.0 - ADAM_B1) * g
    v_new = ADAM_B2 * v + (1.0 - ADAM_B2) * (g * g)
    m_hat = m_new / (1.0 - ADAM_B1 ** ADAM_STEP)
    v_hat = v_new / (1.0 - ADAM_B2 ** ADAM_STEP)
    return -ADAM_LR * (m_hat / (jnp.sqrt(v_hat) + ADAM_EPS) + ADAM_WD * w), m_new, v_new


def adamw(w, g, m, v, *, name, pieces=()):
    nl, r, c = w.shape
    tr = _row_tile(r, 512)

    def body(w_ref, g_ref, m_ref, v_ref, d_ref, mo_ref, vo_ref):
        d_ref[...], mo_ref[...], vo_ref[...] = _adam_update(w_ref[...], g_ref[...], m_ref[...], v_ref[...])

    blk = pl.BlockSpec((None, tr, c), lambda l, i: (l, i, 0))
    return _call(body, name=name, grid=(nl, r // tr), in_specs=[blk] * 4, out_specs=[blk] * 3,
                 out_shape=[SDS(w.shape, F32)] * 3, args=[w, g, m, v], pieces=pieces)


def adamw_small(ws, gs, ms, vs, *, name, pieces=()):
    n = len(ws)

    def body(*refs):
        for i in range(n):
            w_ref, g_ref, m_ref, v_ref = (refs[k * n + i] for k in range(4))
            d, mn, vn = _adam_update(w_ref[...], g_ref[...], m_ref[...], v_ref[...])
            refs[4 * n + i][...] = d
            refs[5 * n + i][...] = mn
            refs[6 * n + i][...] = vn

    vm = pl.BlockSpec(memory_space=pltpu.VMEM)
    res, couts = _call(body, name=name, grid=(), in_specs=[vm] * (4 * n), out_specs=[vm] * (3 * n),
                       out_shape=[SDS(w.shape, F32) for w in ws] * 3, args=[*ws, *gs, *ms, *vs], pieces=pieces)
    return (res[:n], res[n:2 * n], res[2 * n:]), couts


WEIGHTS = ["norm1_g", "w_in", "conv_w", "conv_b", "conv_ln_g", "conv_ln_b", "w_pw", "sg_ln_g", "sg_ln_b", "w_s",
           "b_s", "w_pool", "pool_scale", "w_out", "norm2_g", "w_gate_up", "w_down", "final_g"]
BIG = ["w_in", "w_pw", "w_out", "w_gate_up", "w_down"]
TRANSPOSED = {"w_in": True, "w_pw": False, "w_out": False, "w_gate_up": True, "w_down": False}
SMALL = [k for k in WEIGHTS if k not in BIG]


def _wire(a, transposed):
    if transposed:
        a = a.transpose(0, 2, 1)
    return [a[l].reshape(2, a.shape[1] // 2, a.shape[2]) for l in range(a.shape[0])]


def _pack(arrays):
    flat = jnp.concatenate([a.reshape(-1) for a in arrays])
    n = -(-flat.shape[0] // (N_DEV * LANES * 8)) * 8
    return jnp.pad(flat, (0, N_DEV * n * LANES - flat.shape[0])).reshape(N_DEV, n, LANES)


def _unpack(packed, shapes):
    flat = packed.reshape(-1)
    out, off = [], 0
    for s in shapes:
        size = math.prod(s)
        out.append(flat[off:off + size].reshape(s))
        off += size
    return out


def kernel(x, norm1_g, w_in, conv_w, conv_b, conv_ln_g, conv_ln_b, w_pw, sg_ln_g, sg_ln_b, w_s, b_s, w_pool, pool_scale, w_out, norm2_g, w_gate_up, w_down, final_g, loss_target, m_norm1_g, m_w_in, m_conv_w, m_conv_b, m_conv_ln_g, m_conv_ln_b, m_w_pw, m_sg_ln_g, m_sg_ln_b, m_w_s, m_b_s, m_w_pool, m_pool_scale, m_w_out, m_norm2_g, m_w_gate_up, m_w_down, m_final_g, v_norm1_g, v_w_in, v_conv_w, v_conv_b, v_conv_ln_g, v_conv_ln_b, v_w_pw, v_sg_ln_g, v_sg_ln_b, v_w_s, v_b_s, v_w_pool, v_pool_scale, v_w_out, v_norm2_g, v_w_gate_up, v_w_down, v_final_g):
    w = dict(norm1_g=norm1_g, w_in=w_in, conv_w=conv_w, conv_b=conv_b, conv_ln_g=conv_ln_g, conv_ln_b=conv_ln_b,
             w_pw=w_pw, sg_ln_g=sg_ln_g, sg_ln_b=sg_ln_b, w_s=w_s, b_s=b_s, w_pool=w_pool, pool_scale=pool_scale,
             w_out=w_out, norm2_g=norm2_g, w_gate_up=w_gate_up, w_down=w_down, final_g=final_g)
    m = dict(norm1_g=m_norm1_g, w_in=m_w_in, conv_w=m_conv_w, conv_b=m_conv_b, conv_ln_g=m_conv_ln_g,
             conv_ln_b=m_conv_ln_b, w_pw=m_w_pw, sg_ln_g=m_sg_ln_g, sg_ln_b=m_sg_ln_b, w_s=m_w_s, b_s=m_b_s,
             w_pool=m_w_pool, pool_scale=m_pool_scale, w_out=m_w_out, norm2_g=m_norm2_g, w_gate_up=m_w_gate_up,
             w_down=m_w_down, final_g=m_final_g)
    v = dict(norm1_g=v_norm1_g, w_in=v_w_in, conv_w=v_conv_w, conv_b=v_conv_b, conv_ln_g=v_conv_ln_g,
             conv_ln_b=v_conv_ln_b, w_pw=v_w_pw, sg_ln_g=v_sg_ln_g, sg_ln_b=v_sg_ln_b, w_s=v_w_s, b_s=v_b_s,
             w_pool=v_w_pool, pool_scale=v_pool_scale, w_out=v_w_out, norm2_g=v_norm2_g, w_gate_up=v_w_gate_up,
             w_down=v_w_down, final_g=v_final_g)
    bsz, seq, d = x.shape
    t = bsz * seq
    chip = 2 * lax.axis_index("x") + lax.axis_index("y")
    core = lax.axis_index("c")
    core_arr = jnp.reshape(core, (1,)).astype(jnp.int32)
    place_arr = jnp.stack([chip, core]).astype(jnp.int32)

    own = {k: _wire(w[k].astype(BF16), TRANSPOSED[k]) for k in BIG}
    cw_cols = conv_w.shape[2]
    own["side"] = [jnp.pad(conv_w, ((0, 0), (0, 32 - CONV_WIDTH), (0, 0)))]
    parts = {"w_in": 1, "w_pw": 1, "w_out": 1, "w_gate_up": 4, "w_down": 2, "side": 1}
    plan = {
        "norm1_fwd_0": [("ici", "w_in", 0, 0), ("ici", "w_pw", 0, 0), ("ici", "w_pw", 1, 0), ("ici", "side", 0, 0)],
        "gather_first": [("d2d", "w_in", 0, 0), ("d2d", "w_pw", 0, 0), ("d2d", "w_pw", 1, 0), ("d2d", "side", 0, 0)],
        "in_proj_0": [("ici", "w_out", 0, 0), ("ici", "w_gate_up", 0, 0)],
        "mixer_fwd_0": [("d2d", "w_out", 0, 0), ("d2d", "w_gate_up", 0, 0), ("ici", "w_gate_up", 0, 1),
                        ("ici", "w_gate_up", 0, 2)],
        "out_proj_0": [("d2d", "w_gate_up", 0, 1), ("d2d", "w_gate_up", 0, 2), ("ici", "w_gate_up", 0, 3)],
        "norm2_fwd_0": [("d2d", "w_gate_up", 0, 3), ("ici", "w_down", 0, 0)],
        "swiglu_fwd_0": [("d2d", "w_down", 0, 0), ("both", "w_down", 0, 1), ("ici", "w_in", 1, 0),
                         ("ici", "w_out", 1, 0)],
        "down_proj_0": [("d2d", "w_in", 1, 0), ("d2d", "w_out", 1, 0), ("ici", "w_gate_up", 1, 0),
                        ("ici", "w_gate_up", 1, 1)],
        "norm1_fwd_1": [("d2d", "w_gate_up", 1, 0), ("d2d", "w_gate_up", 1, 1), ("ici", "w_gate_up", 1, 2)],
        "in_proj_1": [("d2d", "w_gate_up", 1, 2), ("ici", "w_gate_up", 1, 3)],
        "mixer_fwd_1": [("d2d", "w_gate_up", 1, 3), ("ici", "w_down", 1, 0), ("ici", "w_down", 1, 1)],
        "out_proj_1": [("d2d", "w_down", 1, 0), ("d2d", "w_down", 1, 1)],
    }
    bufs = {}

    def grouped(name):
        groups = {}
        for stage, k, l, q in plan.get(name, []):
            groups.setdefault((k, l), []).append((stage, q))
        return groups

    def riding(name):
        return [gather_stages(own[k][l], bufs.get((k, l)), stages, parts[k])
                for (k, l), stages in grouped(name).items()]

    def landed(name, couts):
        for key, co in zip(grouped(name), couts):
            bufs[key] = co[0]

    def whole(k, l):
        g = lax.dynamic_update_index_in_dim(bufs[k, l], own[k][l], chip, 0)
        return g.reshape(-1, g.shape[-1])

    xs = [x.reshape(t, d)]
    saved = []
    full = {}
    p = dict(w)
    for l in range(DEPTH):
        x0 = xs[-1]
        name = f"norm1_fwd_{l}"
        h1, co = rmsnorm_fwd(x0, w["norm1_g"][l][None], name=name, pieces=riding(name))
        landed(name, co)
        if l == 0:
            landed("gather_first", comm_only(riding("gather_first"), name="gather_first"))
            side_all = lax.dynamic_update_index_in_dim(bufs["side", 0], own["side"][0], chip, 0)
            p["conv_w"] = side_all[:, :, :CONV_WIDTH, :].transpose(1, 2, 0, 3).reshape(
                DEPTH, CONV_WIDTH, N_CHIPS * cw_cols)
        full["w_in", l] = whole("w_in", l)
        name = f"in_proj_{l}"
        z, co = mm_nt(h1, full["w_in", l], name=name, tm=512, tn=D_IN, out_dtype=BF16, pieces=riding(name))
        landed(name, co)
        mp = _mixer_params(p, whole("w_pw", l), l)
        name = f"mixer_fwd_{l}"
        (mc, cv), co = mixer_fwd(z, mp, seq=seq, name=name, pieces=riding(name))
        landed(name, co)
        full["w_out", l] = whole("w_out", l)
        name = f"out_proj_{l}"
        x1, co = mm_nn(mc, full["w_out", l], name=name, tm=512, tn=D_MODEL, out_dtype=F32, residual=x0,
                       pieces=riding(name))
        landed(name, co)
        name = f"norm2_fwd_{l}"
        h2, co = rmsnorm_fwd(x1, w["norm2_g"][l][None], name=name, pieces=riding(name))
        landed(name, co)
        full["w_gate_up", l] = whole("w_gate_up", l)
        name = f"swiglu_fwd_{l}"
        (gu, act), co = swiglu_fwd(h2, full["w_gate_up", l], name=name, tm=512, tn=1408, pieces=riding(name))
        landed(name, co)
        full["w_down", l] = whole("w_down", l)
        name = f"down_proj_{l}"
        x2, co = mm_nn(act, full["w_down", l], name=name, tm=512, tn=D_MODEL, out_dtype=F32, residual=x1,
                       pieces=riding(name))
        landed(name, co)
        saved.append((x0, h1, z, cv, mc, x1, h2, gu, act, mp))
        xs.append(x2)

    loss, dx, dxb, d_final_g = loss_head(xs[-1], p["final_g"][None], loss_target.reshape(t, d), name="loss_head")

    small = {k: [None] * DEPTH for k in SMALL if k != "final_g"}
    reduced = {}
    carry = None

    def halves(g):
        return g.reshape(N_CHIPS, 2, g.shape[0] // (2 * N_CHIPS), g.shape[1])

    for l in reversed(range(DEPTH)):
        x0, h1, z, cv, mc, x1, h2, gu, act, mp = saved[l]
        pieces = [exchange_d2d(carry[0]), reduce_ici(carry[1])] if carry else []
        g_down, co = mm_tn(act, dxb, name=f"down_proj_dw_{l}", tk=D_FF // 2, tn=D_MODEL, tt=TT, out_dtype=BF16,
                           pieces=pieces)
        g_down = halves(g_down)
        pieces = [reduce_d2d(g_down)]
        if carry:
            reduced["w_pw", l + 1] = co[0][0]
            s_in = sum_chips(carry[1], co[1][0], place_arr, name=f"sum_chips_w_in_{l + 1}")
            pieces.append(exchange_d2d(s_in))
        dgu, co = swiglu_bwd(dxb, full["w_down", l], gu, name=f"swiglu_bwd_{l}", tm=512, tn=1408, pieces=pieces)
        if carry:
            reduced["w_in", l + 1] = co[1][0]
        h_down = add_core_halves(g_down, co[0][0], core_arr, name=f"add_cores_w_down_{l}")
        g_gu, (r2,) = mm_gu_tn(dgu, h2, name=f"gate_up_dw_{l}", tn=1408, tt=TT, pieces=[reduce_ici(h_down)])
        s_down = sum_chips(h_down, r2[0], place_arr, name=f"sum_chips_w_down_{l}")
        g_gu = halves(g_gu)
        (dx, dxb, dn2), (e, r1) = mm_norm_bwd(dgu, full["w_gate_up", l], x1, p["norm2_g"][l][None], dx,
                                              name=f"gate_up_dx_{l}", tm=256,
                                              pieces=[exchange_d2d(s_down), reduce_d2d(g_gu)])
        reduced["w_down", l] = e[0]
        small["norm2_g"][l] = dn2[0]
        h_gu = add_core_halves(g_gu, r1[0], core_arr, name=f"add_cores_w_gate_up_{l}")
        g_out, _ = mm_tn(mc, dxb, name=f"out_proj_dw_{l}", tk=D_MODEL, tn=D_MODEL, tt=TT, out_dtype=BF16)
        g_out = halves(g_out)
        dmc, (r1,) = mm_nt(dxb, full["w_out", l], name=f"out_proj_dx_{l}", tm=512, tn=D_MODEL, out_dtype=BF16,
                           pieces=[reduce_d2d(g_out)])
        h_out = add_core_halves(g_out, r1[0], core_arr, name=f"add_cores_w_out_{l}")
        (dz, *mg), (r2a, r2b) = mixer_bwd(z, cv, dmc, mp, seq=seq, name=f"mixer_bwd_{l}",
                                          pieces=[reduce_ici(h_gu), reduce_ici(h_out)])
        s_gu = sum_chips(h_gu, r2a[0], place_arr, name=f"sum_chips_w_gate_up_{l}")
        s_out = sum_chips(h_out, r2b[0], place_arr, name=f"sum_chips_w_out_{l}")
        mgrads = dict(zip(MIXER_SMALL, _mixer_grads(mg)))
        for k in MIXER_SMALL:
            if k != "w_pw":
                small[k][l] = mgrads[k]
        g_pw = halves(mgrads["w_pw"].astype(BF16))
        g_in, (ea, eb, r1) = mm_tn(dz, h1, name=f"in_proj_dw_{l}", tk=D_IN // 2, tn=D_MODEL, tt=TT, out_dtype=BF16,
                                   pieces=[exchange_d2d(s_gu), exchange_d2d(s_out), reduce_d2d(g_pw)])
        reduced["w_gate_up", l], reduced["w_out", l] = ea[0], eb[0]
        h_pw = add_core_halves(g_pw, r1[0], core_arr, name=f"add_cores_w_pw_{l}")
        g_in = halves(g_in)
        (dx, dxb, dn1), (r2, r1) = mm_norm_bwd(dz, full["w_in", l], x0, p["norm1_g"][l][None], dx,
                                               name=f"in_proj_dx_{l}", tm=512,
                                               pieces=[reduce_ici(h_pw), reduce_d2d(g_in)])
        small["norm1_g"][l] = dn1[0]
        s_pw = sum_chips(h_pw, r2[0], place_arr, name=f"sum_chips_w_pw_{l}")
        h_in = add_core_halves(g_in, r1[0], core_arr, name=f"add_cores_w_in_{l}")
        carry = (s_pw, h_in)
    grad_x = dx.reshape(bsz, seq, d)

    g_small = [jnp.stack(small[k]) if k != "final_g" else d_final_g[0] for k in SMALL]
    small_shapes = [g.shape for g in g_small] + [(1,)]
    s_pw, h_in = carry
    summed, co = allreduce_small(_pack(g_small + [loss.reshape(1)]), name="allreduce_small",
                                 pieces=[exchange_d2d(s_pw), reduce_ici(h_in)])
    reduced["w_pw", 0] = co[0][0]
    s_in = sum_chips(h_in, co[1][0], place_arr, name="sum_chips_w_in_0")
    summed = _unpack(summed, small_shapes)
    loss = summed[-1][0]
    grad = dict(zip(SMALL, summed[:-1]))
    grad["conv_w"] = lax.dynamic_slice_in_dim(grad["conv_w"], chip * cw_cols, cw_cols, axis=2)

    delta, new_m, new_v = {}, {}, {}

    def flat2(a):
        return a.reshape(-1, a.shape[-1])

    res, co = adamw_small(*[[flat2(tt[k]) for k in SMALL] for tt in (w, grad, m, v)], name="adamw_small",
                          pieces=[exchange_d2d(s_in)])
    reduced["w_in", 0] = co[0][0]
    for out, arrs in zip((delta, new_m, new_v), res):
        out.update({k: a.reshape(w[k].shape) for k, a in zip(SMALL, arrs)})
    for k in BIG:
        g = jnp.stack([reduced[k, l].reshape(-1, reduced[k, l].shape[-1]) for l in range(DEPTH)])
        grad[k] = g.transpose(0, 2, 1) if TRANSPOSED[k] else g
        (delta[k], new_m[k], new_v[k]), _ = adamw(w[k], grad[k], m[k], v[k], name=f"adamw_{k}")
    return (loss, grad_x, *[grad[k] for k in WEIGHTS], *[delta[k] for k in WEIGHTS],
            *[new_m[k] for k in WEIGHTS], *[new_v[k] for k in WEIGHTS])
```

```python
import functools
import math

import jax
import jax.numpy as jnp
from jax import lax
from jax.experimental import pallas as pl
from jax.experimental.pallas import tpu as pltpu

F32 = jnp.float32
BF16 = jnp.bfloat16

D_MODEL = 1024
DEPTH = 2
D_A = 384
D_B = 384
D_C = 256
D_IN = 2 * D_A + 2 * D_B + D_C
N_HEADS_B = 4
HEAD_DIM_B = 96
GROUP_DIM_C = 64
CONV_WIDTH = 31
CHUNK = 128
D_FF = 2816
RMS_EPS = 1e-6
LN_EPS = 1e-5
HALO = 32
TT = 2048
N_CHIPS = 4
N_DEV = 8
LANES = 128

ADAM_LR = 0.001
ADAM_B1 = 0.9
ADAM_B2 = 0.999
ADAM_EPS = 1e-08
ADAM_WD = 0.01
ADAM_STEP = 10

VMEM_LIMIT = 56 * 1024 * 1024
_INTERPRET = False

MESH = pl.DeviceIdType.MESH
ANY = pl.BlockSpec(memory_space=pl.ANY)
SDS = jax.ShapeDtypeStruct


def _sigmoid(x):
    return 0.5 * jnp.tanh(0.5 * x) + 0.5


_GELU_C = math.sqrt(2.0 / math.pi)


def _gelu(x):
    return 0.5 * x * (1.0 + jnp.tanh(_GELU_C * (x + 0.044715 * x * x * x)))


def _gelu_grad(x):
    t = jnp.tanh(_GELU_C * (x + 0.044715 * x * x * x))
    return 0.5 * (1.0 + t) + 0.5 * x * (1.0 - t * t) * _GELU_C * (1.0 + 3 * 0.044715 * x * x)


def _dot(a, b):
    return jnp.dot(a, b, preferred_element_type=F32)


def _dot_nt(a, b):
    return lax.dot_general(a, b, (((1,), (1,)), ((), ())), preferred_element_type=F32)


def _dot_tn(a, b):
    return lax.dot_general(a, b, (((0,), (0,)), ((), ())), preferred_element_type=F32)


class Piece:
    def __init__(self, operands, out_shapes, aliases, n_sems, start, finish):
        self.operands, self.out_shapes, self.aliases, self.n_sems = operands, out_shapes, aliases, n_sems
        self.start, self.finish = start, finish


def _place():
    x, y, c = lax.axis_index("x"), lax.axis_index("y"), lax.axis_index("c")
    chips = [(1 - x, y), (x, 1 - y), (1 - x, 1 - y)]
    return x, y, c, chips


def _remote(src, dst, send_sem, recv_sem, to):
    return pltpu.make_async_remote_copy(src_ref=src, dst_ref=dst, send_sem=send_sem, recv_sem=recv_sem,
                                        device_id=to, device_id_type=MESH)


def _rows(r, q, nq):
    return pl.ds(q * (r // nq), r // nq)


def gather_ici(src, g=None, q=0, nq=1):
    rows = _rows(src.shape[1], q, nq)

    def copies(ins, outs, sem):
        x, y, c, chips = _place()
        j = 2 * x + y
        return [(_remote(ins[0].at[c, rows], outs[0].at[j, c, rows], sem(k), sem(3 + k), (px, py, c)),
                 outs[0].at[2 * px + py, c, rows], (px, py, c)) for k, (px, py) in enumerate(chips)]

    def start(ins, outs, sem):
        for cp, _, _ in copies(ins, outs, sem):
            cp.start()

    def finish(ins, outs, sem):
        cps = copies(ins, outs, sem)
        for k, (_, landed, frm) in enumerate(cps):
            _remote(landed, landed, sem(k), sem(3 + k), frm).wait_recv()
        for cp, _, _ in cps:
            cp.wait_send()

    shape = SDS((N_CHIPS,) + src.shape, src.dtype)
    if g is None:
        return Piece([src], [shape], {}, 6, start, finish)
    return Piece([src, g], [shape], {1: 0}, 6, start, finish)


def gather_d2d(g, q=0, nq=1, sem0=0):
    rows = _rows(g.shape[2], q, nq)

    def copies(outs, sem):
        x, y, c, chips = _place()
        return [(_remote(outs[0].at[2 * px + py, c, rows], outs[0].at[2 * px + py, c, rows],
                         sem(sem0 + k), sem(sem0 + 3 + k), (x, y, 1 - c)),
                 outs[0].at[2 * px + py, 1 - c, rows], (x, y, 1 - c)) for k, (px, py) in enumerate(chips)]

    def start(ins, outs, sem):
        for cp, _, _ in copies(outs, sem):
            cp.start()

    def finish(ins, outs, sem):
        cps = copies(outs, sem)
        for k, (_, landed, frm) in enumerate(cps):
            _remote(landed, landed, sem(sem0 + k), sem(sem0 + 3 + k), frm).wait_recv()
        for cp, _, _ in cps:
            cp.wait_send()

    return Piece([g], [SDS(g.shape, g.dtype)], {0: 0}, 6, start, finish)


def gather_both(src, g=None, q=0, nq=1):
    a = gather_ici(src, g, q, nq)
    b = gather_d2d(jax.ShapeDtypeStruct((N_CHIPS,) + src.shape, src.dtype), q, nq, sem0=6)

    def finish(ins, outs, sem):
        a.finish(ins, outs, sem)
        b.start(ins, outs, sem)
        b.finish(ins, outs, sem)

    return Piece(a.operands, a.out_shapes, a.aliases, 12, a.start, finish)


def gather_stages(src, g, stages, nq):
    shape = SDS((N_CHIPS,) + src.shape, src.dtype)
    make = {"ici": lambda q: gather_ici(src, None, q, nq), "d2d": lambda q: gather_d2d(shape, q, nq),
            "both": lambda q: gather_both(src, None, q, nq)}
    subs, n_sems = [], 0
    for stage, q in stages:
        sub = make[stage](q)
        subs.append((sub, n_sems))
        n_sems += sub.n_sems

    def run(method):
        def go(ins, outs, sem):
            for sub, base in subs:
                getattr(sub, method)(ins, outs, functools.partial(lambda k, base: sem(base + k), base=base))
        return go

    if g is None:
        return Piece([src], [shape], {}, n_sems, run("start"), run("finish"))
    return Piece([src, g], [shape], {1: 0}, n_sems, run("start"), run("finish"))


def reduce_d2d(g):
    def copies(ins, outs, sem):
        x, y, c, _ = _place()
        return [_remote(ins[0].at[j, 1 - c], outs[0].at[j], sem(j), sem(4 + j), (x, y, 1 - c)) for j in range(N_CHIPS)]

    def start(ins, outs, sem):
        for cp in copies(ins, outs, sem):
            cp.start()

    def finish(ins, outs, sem):
        cps = copies(ins, outs, sem)
        for cp in cps:
            cp.wait_recv()
        for cp in cps:
            cp.wait_send()

    return Piece([g], [SDS((N_CHIPS,) + g.shape[2:], g.dtype)], {}, 8, start, finish)


def reduce_ici(h):
    def copies(ins, outs, sem):
        x, y, c, chips = _place()
        j = 2 * x + y
        return [(_remote(ins[0].at[2 * px + py], outs[0].at[j], sem(k), sem(3 + k), (px, py, c)),
                 outs[0].at[2 * px + py], (px, py, c)) for k, (px, py) in enumerate(chips)]

    def start(ins, outs, sem):
        for cp, _, _ in copies(ins, outs, sem):
            cp.start()

    def finish(ins, outs, sem):
        cps = copies(ins, outs, sem)
        for k, (_, landed, frm) in enumerate(cps):
            _remote(landed, landed, sem(k), sem(3 + k), frm).wait_recv()
        for cp, _, _ in cps:
            cp.wait_send()

    return Piece([h], [SDS(h.shape, h.dtype)], {}, 6, start, finish)


def exchange_d2d(g):
    def copy(outs, sem):
        x, y, c, _ = _place()
        return _remote(outs[0].at[c], outs[0].at[c], sem(0), sem(1), (x, y, 1 - c)), outs[0].at[1 - c], (x, y, 1 - c)

    def start(ins, outs, sem):
        copy(outs, sem)[0].start()

    def finish(ins, outs, sem):
        cp, landed, frm = copy(outs, sem)
        _remote(landed, landed, sem(0), sem(1), frm).wait_recv()
        cp.wait_send()

    return Piece([g], [SDS(g.shape, g.dtype)], {0: 0}, 2, start, finish)


def _call(body, *, name, grid, in_specs, out_specs, out_shape, args, scratch_shapes=(), pieces=(), prefetch=0):
    n_in, n_out, n_scr = len(in_specs), len(out_specs), len(scratch_shapes)
    c_ops = [a for p in pieces for a in p.operands]
    c_outs = [s for p in pieces for s in p.out_shapes]
    n_sems = sum(p.n_sems for p in pieces)
    aliases = {}
    op_off, out_off = prefetch + n_in, n_out
    for p in pieces:
        for i, o in p.aliases.items():
            aliases[op_off + i] = out_off + o
        op_off += len(p.operands)
        out_off += len(p.out_shapes)

    def wrapped(*refs):
        pre, refs = refs[:prefetch], refs[prefetch:]
        ins, cin = refs[:n_in], refs[n_in:n_in + len(c_ops)]
        o0 = n_in + len(c_ops)
        outs, cout = refs[o0:o0 + n_out], refs[o0 + n_out:o0 + n_out + len(c_outs)]
        s0 = o0 + n_out + len(c_outs)
        scr = refs[s0:s0 + n_scr]

        def each(method):
            sems = refs[s0 + n_scr]
            i_off = o_off = s_off = 0
            for p in pieces:
                getattr(p, method)(cin[i_off:i_off + len(p.operands)], cout[o_off:o_off + len(p.out_shapes)],
                                   functools.partial(lambda k, base: sems.at[base + k], base=s_off))
                i_off, o_off, s_off = i_off + len(p.operands), o_off + len(p.out_shapes), s_off + p.n_sems

        if pieces and grid:
            ids = [pl.program_id(a) for a in range(len(grid))]
            first = functools.reduce(jnp.logical_and, [i == 0 for i in ids])
            last = functools.reduce(jnp.logical_and, [i == g - 1 for i, g in zip(ids, grid)])
            pl.when(first)(lambda: each("start"))
        elif pieces:
            each("start")
        if body is not None:
            body(*pre, *ins, *outs, *scr)
        if pieces and grid:
            pl.when(last)(lambda: each("finish"))
        elif pieces:
            each("finish")

    scratch = list(scratch_shapes) + ([pltpu.SemaphoreType.DMA((n_sems,))] if pieces else [])
    all_in = list(in_specs) + [ANY] * len(c_ops)
    all_out = list(out_specs) + [ANY] * len(c_outs)
    shapes = list(out_shape) + c_outs
    kw = dict(name=name, out_shape=shapes, input_output_aliases=aliases, interpret=_INTERPRET)
    if grid:
        kw["compiler_params"] = pltpu.CompilerParams(dimension_semantics=("arbitrary",) * len(grid),
                                                     vmem_limit_bytes=VMEM_LIMIT)
    if prefetch:
        kw["grid_spec"] = pltpu.PrefetchScalarGridSpec(num_scalar_prefetch=prefetch, grid=grid, in_specs=all_in,
                                                       out_specs=all_out, scratch_shapes=scratch)
    else:
        kw.update(in_specs=all_in, out_specs=all_out, scratch_shapes=scratch)
        if grid:
            kw["grid"] = grid
    res = pl.pallas_call(wrapped, **kw)(*args, *c_ops)
    outs, rest = list(res[:n_out]), list(res[n_out:])
    couts = []
    for p in pieces:
        couts.append(rest[:len(p.out_shapes)])
        rest = rest[len(p.out_shapes):]
    return outs, couts


def comm_only(pieces, *, name):
    return _call(None, name=name, grid=(), in_specs=[], out_specs=[], out_shape=[], args=[], pieces=pieces)[1]


def rmsnorm_fwd(x, g, *, name, pieces=()):
    t, d = x.shape
    tm = min(512, t)

    def body(x_ref, g_ref, o_ref):
        xv = x_ref[...]
        rstd = lax.rsqrt(jnp.mean(xv * xv, axis=-1, keepdims=True) + RMS_EPS)
        o_ref[...] = (xv * rstd * g_ref[...]).astype(BF16)

    outs, couts = _call(
        body, name=name, grid=(t // tm,),
        in_specs=[pl.BlockSpec((tm, d), lambda i: (i, 0)), pl.BlockSpec((1, d), lambda i: (0, 0))],
        out_specs=[pl.BlockSpec((tm, d), lambda i: (i, 0))], out_shape=[SDS((t, d), BF16)],
        args=[x, g], pieces=pieces)
    return outs[0], couts


def loss_head(x, g, target, *, name):
    t, d = x.shape
    tm = min(512, t)

    def body(x_ref, g_ref, t_ref, loss_ref, dx_ref, dxb_ref, dg_ref):
        i = pl.program_id(0)
        xv = x_ref[...]
        gv = g_ref[...]
        rstd = lax.rsqrt(jnp.mean(xv * xv, axis=-1, keepdims=True) + RMS_EPS)
        xhat = xv * rstd
        err = xhat * gv - t_ref[...]
        dy = err * (1.0 / d)
        dxhat = dy * gv
        dx = rstd * (dxhat - xhat * jnp.mean(dxhat * xhat, axis=-1, keepdims=True))
        dx_ref[...] = dx
        dxb_ref[...] = dx.astype(BF16)

        @pl.when(i == 0)
        def _():
            dg_ref[...] = jnp.zeros_like(dg_ref)
            loss_ref[...] = jnp.zeros_like(loss_ref)

        dg_ref[...] += jnp.sum(dy * xhat, axis=0, keepdims=True)
        per_tok = jnp.sum(err * err, axis=-1, keepdims=True) * (0.5 / d)
        loss_ref[...] += jnp.sum(per_tok, axis=0, keepdims=True)

    row = pl.BlockSpec((tm, d), lambda i: (i, 0))
    vec = pl.BlockSpec((1, d), lambda i: (0, 0))
    one = pl.BlockSpec((1, 1), lambda i: (0, 0))
    outs, _ = _call(
        body, name=name, grid=(t // tm,), in_specs=[row, vec, row], out_specs=[one, row, row, vec],
        out_shape=[SDS((1, 1), F32), SDS((t, d), F32), SDS((t, d), BF16), SDS((1, d), F32)],
        args=[x, g, target])
    return outs


def mm_nn(a, b, *, name, tm, tn, out_dtype, residual=None, pieces=()):
    m, k = a.shape
    n = b.shape[1]
    tm, tn = min(tm, m), min(tn, n)
    has_res = residual is not None

    def body(a_ref, b_ref, *rest):
        o_ref = rest[-1]
        acc = _dot(a_ref[...], b_ref[...])
        if has_res:
            acc = acc + rest[0][...]
        o_ref[...] = acc.astype(o_ref.dtype)

    in_specs = [pl.BlockSpec((tm, k), lambda j, i: (i, 0)), pl.BlockSpec((k, tn), lambda j, i: (0, j))]
    args = [a, b]
    if has_res:
        in_specs.append(pl.BlockSpec((tm, tn), lambda j, i: (i, j)))
        args.append(residual)
    outs, couts = _call(
        body, name=name, grid=(n // tn, m // tm), in_specs=in_specs,
        out_specs=[pl.BlockSpec((tm, tn), lambda j, i: (i, j))], out_shape=[SDS((m, n), out_dtype)],
        args=args, pieces=pieces)
    return outs[0], couts


def mm_nt(a, b, *, name, tm, tn, out_dtype, pieces=()):
    m, k = a.shape
    n = b.shape[0]
    tm, tn = min(tm, m), min(tn, n)

    def body(a_ref, b_ref, o_ref):
        o_ref[...] = _dot_nt(a_ref[...], b_ref[...]).astype(o_ref.dtype)

    outs, couts = _call(
        body, name=name, grid=(n // tn, m // tm),
        in_specs=[pl.BlockSpec((tm, k), lambda j, i: (i, 0)), pl.BlockSpec((tn, k), lambda j, i: (j, 0))],
        out_specs=[pl.BlockSpec((tm, tn), lambda j, i: (i, j))], out_shape=[SDS((m, n), out_dtype)],
        args=[a, b], pieces=pieces)
    return outs[0], couts


def _rms_norm(xv, gv):
    rstd = lax.rsqrt(jnp.mean(xv * xv, axis=-1, keepdims=True) + RMS_EPS)
    return (xv * rstd * gv).astype(BF16)


def norm_mm_nt(x, g, b, *, name, tm, out_dtype, pieces=()):
    m, k = x.shape
    n = b.shape[0]
    tm = min(tm, m)

    def body(x_ref, g_ref, b_ref, o_ref, h_ref):
        h = _rms_norm(x_ref[...], g_ref[...])
        h_ref[...] = h
        o_ref[...] = _dot_nt(h, b_ref[...]).astype(o_ref.dtype)

    outs, couts = _call(
        body, name=name, grid=(m // tm,),
        in_specs=[pl.BlockSpec((tm, k), lambda i: (i, 0)), pl.BlockSpec((1, k), lambda i: (0, 0)),
                  pl.BlockSpec((n, k), lambda i: (0, 0))],
        out_specs=[pl.BlockSpec((tm, n), lambda i: (i, 0)), pl.BlockSpec((tm, k), lambda i: (i, 0))],
        out_shape=[SDS((m, n), out_dtype), SDS((m, k), BF16)], args=[x, g, b], pieces=pieces)
    return outs, couts


def mm_tn(a, b, *, name, tk, tn, tt, out_dtype, pieces=()):
    t, k = a.shape
    n = b.shape[1]
    tk, tn, tt = min(tk, k), min(tn, n), min(tt, t)
    nt = t // tt

    def body(a_ref, b_ref, o_ref, acc_ref):
        s = pl.program_id(2)

        @pl.when(s == 0)
        def _():
            acc_ref[...] = jnp.zeros_like(acc_ref)

        acc_ref[...] += _dot_tn(a_ref[...], b_ref[...])

        @pl.when(s == nt - 1)
        def _():
            o_ref[...] = acc_ref[...].astype(o_ref.dtype)

    outs, couts = _call(
        body, name=name, grid=(k // tk, n // tn, nt),
        in_specs=[pl.BlockSpec((tt, tk), lambda i, j, s: (s, i)), pl.BlockSpec((tt, tn), lambda i, j, s: (s, j))],
        out_specs=[pl.BlockSpec((tk, tn), lambda i, j, s: (i, j))], out_shape=[SDS((k, n), out_dtype)],
        scratch_shapes=[pltpu.VMEM((tk, tn), F32)], args=[a, b], pieces=pieces)
    return outs[0], couts


def swiglu_fwd(x, g, wt, *, name, tm, tn, pieces=()):
    t, d = x.shape
    ff = wt.shape[0] // 2
    tm, tn = min(tm, t), min(tn, ff)
    nb = ff // tn

    nm = t // tm

    def body(x_ref, g_ref, wg_ref, wu_ref, gu_ref, act_ref, h_ref):
        hv = _rms_norm(x_ref[...], g_ref[...])

        @pl.when(pl.program_id(0) == 0)
        def _():
            h_ref[...] = hv

        gate = _dot_nt(hv, wg_ref[...])
        up = _dot_nt(hv, wu_ref[...])
        gu_ref[0] = gate.astype(BF16)
        gu_ref[1] = up.astype(BF16)
        act_ref[...] = (gate * _sigmoid(gate) * up).astype(BF16)

    outs, couts = _call(
        body, name=name, grid=(nb, t // tm),
        in_specs=[pl.BlockSpec((tm, d), lambda j, i: (i, 0)), pl.BlockSpec((1, d), lambda j, i: (0, 0)),
                  pl.BlockSpec((tn, d), lambda j, i: (j, 0)),
                  pl.BlockSpec((tn, d), lambda j, i: (j + nb, 0))],
        out_specs=[pl.BlockSpec((2, tm, tn), lambda j, i: (0, i, j)), pl.BlockSpec((tm, tn), lambda j, i: (i, j)),
                   pl.BlockSpec((tm, d), lambda j, i: (jnp.where(j == 0, i, nm - 1), 0))],
        out_shape=[SDS((2, t, ff), BF16), SDS((t, ff), BF16), SDS((t, d), BF16)], args=[x, g, wt, wt],
        pieces=pieces)
    return outs, couts


def swiglu_bwd(dx, w_down, gu, *, name, tm, tn, pieces=()):
    t, d = dx.shape
    ff = w_down.shape[0]
    tm, tn = min(tm, t), min(tn, ff)

    def body(dx_ref, w_ref, gu_ref, o_ref):
        dact = _dot_nt(dx_ref[...], w_ref[...])
        gate = gu_ref[0].astype(F32)
        up = gu_ref[1].astype(F32)
        sg = _sigmoid(gate)
        o_ref[0] = (dact * up * sg * (1.0 + gate * (1.0 - sg))).astype(BF16)
        o_ref[1] = (dact * gate * sg).astype(BF16)

    outs, couts = _call(
        body, name=name, grid=(ff // tn, t // tm),
        in_specs=[pl.BlockSpec((tm, d), lambda j, i: (i, 0)), pl.BlockSpec((tn, d), lambda j, i: (j, 0)),
                  pl.BlockSpec((2, tm, tn), lambda j, i: (0, i, j))],
        out_specs=[pl.BlockSpec((2, tm, tn), lambda j, i: (0, i, j))], out_shape=[SDS((2, t, ff), BF16)],
        args=[dx, w_down, gu], pieces=pieces)
    return outs[0], couts


def mm_norm_bwd(a, wt, x, g, dres, *, name, tm, pieces=()):
    parts = a.shape[0] if a.ndim == 3 else 1
    t, kp = a.shape[-2:]
    d = wt.shape[1]
    tm = min(tm, t)

    def body(a_ref, w_ref, x_ref, g_ref, dres_ref, dx_ref, dxb_ref, dg_ref):
        i = pl.program_id(0)
        if parts == 1:
            dh = _dot(a_ref[...], w_ref[...])
        else:
            dh = _dot(a_ref[0], w_ref[0:kp, :])
            for q in range(1, parts):
                dh = dh + _dot(a_ref[q], w_ref[q * kp:(q + 1) * kp, :])
        xv = x_ref[...]
        rstd = lax.rsqrt(jnp.mean(xv * xv, axis=-1, keepdims=True) + RMS_EPS)
        xhat = xv * rstd
        dxhat = dh * g_ref[...]
        dx = dres_ref[...] + rstd * (dxhat - xhat * jnp.mean(dxhat * xhat, axis=-1, keepdims=True))
        dx_ref[...] = dx
        dxb_ref[...] = dx.astype(BF16)

        @pl.when(i == 0)
        def _():
            dg_ref[...] = jnp.zeros_like(dg_ref)

        dg_ref[...] += jnp.sum(dh * xhat, axis=0, keepdims=True)

    a_spec = (pl.BlockSpec((tm, kp), lambda i: (i, 0)) if parts == 1
              else pl.BlockSpec((parts, tm, kp), lambda i: (0, i, 0)))
    row = pl.BlockSpec((tm, d), lambda i: (i, 0))
    vec = pl.BlockSpec((1, d), lambda i: (0, 0))
    outs, couts = _call(
        body, name=name, grid=(t // tm,),
        in_specs=[a_spec, pl.BlockSpec((parts * kp, d), lambda i: (0, 0)), row, vec, row],
        out_specs=[row, row, vec], out_shape=[SDS((t, d), F32), SDS((t, d), BF16), SDS((1, d), F32)],
        args=[a, wt, x, g, dres], pieces=pieces)
    return outs, couts


def mm_gu_tn(dgu, h, *, name, tn, tt, pieces=()):
    t, d = h.shape
    ff = dgu.shape[2]
    tn, tt = min(tn, ff), min(tt, t)
    nb = ff // tn
    nt = t // tt

    def body(a_ref, h_ref, o_ref, acc_ref):
        s = pl.program_id(1)

        @pl.when(s == 0)
        def _():
            acc_ref[...] = jnp.zeros_like(acc_ref)

        acc_ref[...] += _dot_tn(a_ref[...], h_ref[...])

        @pl.when(s == nt - 1)
        def _():
            o_ref[...] = acc_ref[...].astype(o_ref.dtype)

    outs, couts = _call(
        body, name=name, grid=(2 * nb, nt),
        in_specs=[pl.BlockSpec((None, tt, tn), lambda j, s: (j // nb, s, j % nb)),
                  pl.BlockSpec((tt, d), lambda j, s: (s, 0))],
        out_specs=[pl.BlockSpec((tn, d), lambda j, s: (j, 0))], out_shape=[SDS((2 * ff, d), BF16)],
        scratch_shapes=[pltpu.VMEM((tn, d), F32)], args=[dgu, h], pieces=pieces)
    return outs[0], couts


def _head_masks(shape):
    lane = lax.broadcasted_iota(jnp.int32, shape, 1)
    return [(lane >= h * HEAD_DIM_B) & (lane < (h + 1) * HEAD_DIM_B) for h in range(N_HEADS_B)]


def _causal_rows(ws):
    r = lax.broadcasted_iota(jnp.int32, ws.shape, 0) % CHUNK
    c = lax.broadcasted_iota(jnp.int32, ws.shape, 1)
    return jnp.where(c <= r, ws, jnp.zeros_like(ws))


def _pool_window(shape):
    lane = lax.broadcasted_iota(jnp.int32, shape, 1)
    return jnp.left_shift(2, lane // GROUP_DIM_C)


def _layer_norm_fwd(x, g, b):
    mu = jnp.mean(x, axis=-1, keepdims=True)
    xc = x - mu
    rstd = lax.rsqrt(jnp.mean(xc * xc, axis=-1, keepdims=True) + LN_EPS)
    xhat = xc * rstd
    return xhat * g + b, xhat, rstd


def _layer_norm_bwd(dy, xhat, rstd, g):
    dxhat = dy * g
    return rstd * (dxhat - jnp.mean(dxhat, axis=-1, keepdims=True)
                   - xhat * jnp.mean(dxhat * xhat, axis=-1, keepdims=True))


def _gate_mix(ws_masked, vl_chunk, masks):
    out = _dot(ws_masked, vl_chunk.astype(BF16))
    s = jnp.zeros((CHUNK, D_B), F32)
    for h in range(N_HEADS_B):
        s = s + jnp.where(masks[h], out[h * CHUNK:(h + 1) * CHUNK], 0.0)
    return s


SUB = 8


def _shifted_copies(ref, n):
    for b in range(1, SUB):
        ref[b, 0:n, :] = ref[0, pl.ds(b, n), :]


def _tap(ref, off, n):
    a, b = divmod(off, SUB)
    return ref[b, pl.ds(SUB * a, n), :]


def _pick_window(s2, s4, s8, s16):
    grp = lax.broadcasted_iota(jnp.int32, s2.shape, 1) // GROUP_DIM_C
    return jnp.where(grp == 0, s2, jnp.where(grp == 1, s4, jnp.where(grp == 2, s8, s16)))


def _trailing_window_sums(src_ref, l2_ref, l4_ref, l8_ref, tm):
    n = HALO + tm
    l2_ref[8:n, :] = src_ref[8:n, :] + src_ref[pl.ds(7, n - 8), :]
    l4_ref[16:n, :] = l2_ref[16:n, :] + l2_ref[pl.ds(14, n - 16), :]
    l8_ref[24:n, :] = l4_ref[24:n, :] + l4_ref[pl.ds(20, n - 24), :]
    s16 = l8_ref[HALO:n, :] + l8_ref[HALO - 8:n - 8, :]
    return _pick_window(l2_ref[HALO:n, :], l4_ref[HALO:n, :], l8_ref[HALO:n, :], s16)


def _leading_window_sums(src_ref, l2_ref, l4_ref, l8_ref, tm):
    n = HALO + tm
    l2_ref[0:n - 8, :] = src_ref[0:n - 8, :] + src_ref[pl.ds(1, n - 8), :]
    l4_ref[0:n - 16, :] = l2_ref[0:n - 16, :] + l2_ref[pl.ds(2, n - 16), :]
    l8_ref[0:n - 24, :] = l4_ref[0:n - 24, :] + l4_ref[pl.ds(4, n - 24), :]
    s16 = l8_ref[0:tm, :] + l8_ref[8:tm + 8, :]
    return _pick_window(l2_ref[0:tm, :], l4_ref[0:tm, :], l8_ref[0:tm, :], s16)


def _mixer_specs(tm, nt, seq):
    hb = tm // HALO

    def cur(c):
        return pl.BlockSpec((tm, c), lambda b, i: (b * nt + i, 0))

    def prev(c):
        return pl.BlockSpec((HALO, c), lambda b, i: (jnp.maximum((b * nt + i) * hb - 1, 0), 0))

    def nxt(c):
        last = (2 * seq) // HALO - 1
        return pl.BlockSpec((HALO, c), lambda b, i: (jnp.minimum((b * nt + i + 1) * hb, last), 0))

    def full(shape):
        return pl.BlockSpec(shape, lambda b, i: tuple(0 for _ in shape))

    return cur, prev, nxt, full


_MIX_PARAM_SHAPES = [(32, D_A), (1, D_A), (1, D_A), (1, D_A), (D_A, D_A), (1, D_B), (1, D_B),
                     (N_HEADS_B * CHUNK, CHUNK), (CHUNK, D_B), (D_C, D_C), (1, D_C)]


def mixer_fwd(z, mp, *, seq, name, tm=512, pieces=()):
    t = z.shape[0]
    tm = min(tm, seq)
    nt = seq // tm
    cur, prev, _, full = _mixer_specs(tm, nt, seq)

    def body(zc_ref, zp_ref, cw_ref, cb_ref, clg_ref, clb_ref, wpw_ref, slg_ref, slb_ref, ws_ref, bias_ref,
             wp_ref, ps_ref, o_ref, cv_ref, ys_ref, zs_ref, l2_ref, l4_ref, l8_ref):
        i = pl.program_id(1)
        has_prev = i > 0
        yp = zp_ref[:, 0:D_A].astype(F32) * _sigmoid(zp_ref[:, D_A:2 * D_A].astype(F32))
        ys_ref[0, 0:HALO, :] = jnp.where(has_prev, yp, 0.0)
        ys_ref[0, HALO:HALO + tm, :] = zc_ref[:, 0:D_A].astype(F32) * _sigmoid(zc_ref[:, D_A:2 * D_A].astype(F32))
        _shifted_copies(ys_ref, tm + HALO - SUB)
        acc = jnp.zeros((tm, D_A), F32) + cb_ref[...]
        for k in range(CONV_WIDTH):
            acc = acc + cw_ref[k:k + 1, :] * _tap(ys_ref, HALO - (CONV_WIDTH - 1) + k, tm)
        cv_ref[...] = acc
        ln, _, _ = _layer_norm_fwd(acc, clg_ref[...], clb_ref[...])
        sl = ln * _sigmoid(ln)
        o_ref[:, 0:D_A] = _dot(sl.astype(BF16), wpw_ref[...]).astype(BF16)
        gz = _gelu(zc_ref[:, 2 * D_A:2 * D_A + 2 * D_B].astype(F32))
        u = gz[:, :D_B]
        vl, _, _ = _layer_norm_fwd(gz[:, D_B:], slg_ref[...], slb_ref[...])
        wsm = _causal_rows(ws_ref[...])
        masks = _head_masks((CHUNK, D_B))
        for c in range(tm // CHUNK):
            rows = slice(c * CHUNK, (c + 1) * CHUNK)
            s = _gate_mix(wsm, vl[rows], masks) + bias_ref[...]
            o_ref[rows, D_A:D_A + D_B] = (u[rows] * s).astype(BF16)
        c0 = 2 * D_A + 2 * D_B
        zs_ref[0:HALO, :] = jnp.where(has_prev, zp_ref[:, c0:c0 + D_C].astype(F32), 0.0)
        zcur = zc_ref[:, c0:c0 + D_C].astype(F32)
        zs_ref[HALO:HALO + tm, :] = zcur
        win = _pool_window((tm, D_C))
        wsum = _trailing_window_sums(zs_ref, l2_ref, l4_ref, l8_ref, tm)
        pos = i * tm + lax.broadcasted_iota(jnp.int32, (tm, D_C), 0)
        cnt = jnp.minimum(pos + 1, win).astype(F32)
        p = wsum / cnt - zcur
        y = _dot(p.astype(BF16), wp_ref[...])
        o_ref[:, D_A + D_B:D_A + D_B + D_C] = (y * ps_ref[...]).astype(BF16)

    in_specs = [cur(D_IN), prev(D_IN)] + [full(s) for s in _MIX_PARAM_SHAPES]
    outs, couts = _call(
        body, name=name, grid=(2, nt), in_specs=in_specs, out_specs=[cur(D_MODEL), cur(D_A)],
        out_shape=[SDS((t, D_MODEL), BF16), SDS((t, D_A), F32)],
        scratch_shapes=[pltpu.VMEM((SUB, HALO + tm, D_A), F32)] + [pltpu.VMEM((HALO + tm, D_C), F32)] * 4,
        args=[z, z, *mp], pieces=pieces)
    return outs, couts


def mixer_bwd(z, conv, dm, mp, *, seq, name, tm=256, pieces=()):
    t = z.shape[0]
    tm = min(tm, seq)
    nt = seq // tm
    ext = tm + HALO
    cur, prev, nxt, full = _mixer_specs(tm, nt, seq)
    grad_shapes = [(32, D_A), (1, D_A), (1, D_A), (1, D_A), (D_A, D_A), (1, D_B), (1, D_B),
                   (N_HEADS_B * CHUNK, CHUNK), (CHUNK, CHUNK), (D_C, D_C), (1, D_C)]

    def body(zc_ref, zp_ref, cvc_ref, cvn_ref, dmc_ref, dmn_ref, cw_ref, cb_ref, clg_ref, clb_ref, wpw_ref, slg_ref,
             slb_ref, ws_ref, bias_ref, wp_ref, ps_ref,
             dz_ref, dcw_ref, dcb_ref, dclg_ref, dclb_ref, dwpw_ref, dslg_ref, dslb_ref, dws_ref, dbs_ref, dwp_ref,
             dps_ref, ys_ref, dcs_ref, zs_ref, qs_ref, l2_ref, l4_ref, l8_ref):
        b = pl.program_id(0)
        i = pl.program_id(1)
        has_prev = i > 0
        has_next = i < nt - 1
        grads = [dcw_ref, dcb_ref, dclg_ref, dclb_ref, dwpw_ref, dslg_ref, dslb_ref, dws_ref, dbs_ref, dwp_ref, dps_ref]

        @pl.when((b == 0) & (i == 0))
        def _():
            for r in grads:
                r[...] = jnp.zeros_like(r)

        ext_row = lax.broadcasted_iota(jnp.int32, (ext, 1), 0)
        live = (ext_row < tm) | has_next

        yp = zp_ref[:, 0:D_A].astype(F32) * _sigmoid(zp_ref[:, D_A:2 * D_A].astype(F32))
        ys_ref[0, 0:HALO, :] = jnp.where(has_prev, yp, 0.0)
        a_cur = zc_ref[:, 0:D_A].astype(F32)
        sig_cur = _sigmoid(zc_ref[:, D_A:2 * D_A].astype(F32))
        ys_ref[0, HALO:HALO + tm, :] = a_cur * sig_cur
        _shifted_copies(ys_ref, tm + HALO - SUB)
        acc = jnp.concatenate([cvc_ref[...], cvn_ref[...]], axis=0)
        ln, xhat, rstd = _layer_norm_fwd(acc, clg_ref[...], clb_ref[...])
        sg = _sigmoid(ln)
        sl = ln * sg
        dya = jnp.concatenate([dmc_ref[:, 0:D_A], dmn_ref[:, 0:D_A]], axis=0)
        dsl = _dot_nt(dya, wpw_ref[...])
        dln = dsl * sg * (1.0 + ln * (1.0 - sg))
        dc = _layer_norm_bwd(dln, xhat, rstd, clg_ref[...])
        dc = jnp.where(live, dc, 0.0)
        dcs_ref[0] = dc
        _shifted_copies(dcs_ref, ext - SUB)
        dwpw_ref[...] += _dot_tn(sl[:tm].astype(BF16), dya[:tm])
        dclg_ref[...] += jnp.sum(dln[:tm] * xhat[:tm], axis=0, keepdims=True)
        dclb_ref[...] += jnp.sum(dln[:tm], axis=0, keepdims=True)
        dcb_ref[...] += jnp.sum(dc[:tm], axis=0, keepdims=True)
        dy = jnp.zeros((tm, D_A), F32)
        for k in range(CONV_WIDTH):
            off = HALO - (CONV_WIDTH - 1) + k
            dcw_ref[k:k + 1, :] += jnp.sum(dc[:tm] * _tap(ys_ref, off, tm), axis=0, keepdims=True)
            dy = dy + cw_ref[k:k + 1, :] * _tap(dcs_ref, CONV_WIDTH - 1 - k, tm)
        dz_ref[:, 0:D_A] = (dy * sig_cur).astype(BF16)
        dz_ref[:, D_A:2 * D_A] = (dy * a_cur * sig_cur * (1.0 - sig_cur)).astype(BF16)

        zb = zc_ref[:, 2 * D_A:2 * D_A + 2 * D_B].astype(F32)
        gz = _gelu(zb)
        u = gz[:, :D_B]
        vl, vhat, vrstd = _layer_norm_fwd(gz[:, D_B:], slg_ref[...], slb_ref[...])
        dyb = dmc_ref[:, D_A:D_A + D_B].astype(F32)
        wsm = _causal_rows(ws_ref[...])
        masks = _head_masks((CHUNK, D_B))
        ds_all = dyb * u
        du_parts, dvl_parts = [], []
        for c in range(tm // CHUNK):
            rows = slice(c * CHUNK, (c + 1) * CHUNK)
            vlc = vl[rows].astype(BF16)
            s = _gate_mix(wsm, vl[rows], masks) + bias_ref[...]
            du_parts.append(dyb[rows] * s)
            ds = ds_all[rows]
            stack = jnp.concatenate([jnp.where(masks[h], ds, 0.0) for h in range(N_HEADS_B)], axis=0).astype(BF16)
            dvl_parts.append(_dot_tn(wsm, stack))
            dws_ref[...] += _dot_nt(stack, vlc)
        du = jnp.concatenate(du_parts, axis=0)
        dvl = jnp.concatenate(dvl_parts, axis=0)
        dbias = jnp.zeros((CHUNK, D_B), F32)
        for c in range(tm // CHUNK):
            dbias = dbias + ds_all[c * CHUNK:(c + 1) * CHUNK]
        lane = lax.broadcasted_iota(jnp.int32, (CHUNK, CHUNK), 1)
        dbs = jnp.zeros((CHUNK, CHUNK), F32)
        for h in range(N_HEADS_B):
            col = jnp.sum(jnp.where(masks[h], dbias, 0.0), axis=1, keepdims=True)
            dbs = dbs + jnp.where(lane == h, col, 0.0)
        dbs_ref[...] += dbs
        dslg_ref[...] += jnp.sum(dvl * vhat, axis=0, keepdims=True)
        dslb_ref[...] += jnp.sum(dvl, axis=0, keepdims=True)
        dv = _layer_norm_bwd(dvl, vhat, vrstd, slg_ref[...])
        gg = _gelu_grad(zb)
        dz_ref[:, 2 * D_A:2 * D_A + D_B] = (du * gg[:, :D_B]).astype(BF16)
        dz_ref[:, 2 * D_A + D_B:2 * D_A + 2 * D_B] = (dv * gg[:, D_B:]).astype(BF16)

        c0 = 2 * D_A + 2 * D_B
        m0 = D_A + D_B
        zs_ref[0:HALO, :] = jnp.where(has_prev, zp_ref[:, c0:c0 + D_C].astype(F32), 0.0)
        zcur = zc_ref[:, c0:c0 + D_C].astype(F32)
        zs_ref[HALO:HALO + tm, :] = zcur
        win = _pool_window((tm, D_C))
        wsum = _trailing_window_sums(zs_ref, l2_ref, l4_ref, l8_ref, tm)
        pos = i * tm + lax.broadcasted_iota(jnp.int32, (tm, D_C), 0)
        cnt = jnp.minimum(pos + 1, win).astype(F32)
        pb = (wsum / cnt - zcur).astype(BF16)
        y = _dot(pb, wp_ref[...])
        dyc = jnp.concatenate([dmc_ref[:, m0:m0 + D_C], dmn_ref[:, m0:m0 + D_C]], axis=0).astype(F32)
        dps_ref[...] += jnp.sum(dyc[:tm] * y, axis=0, keepdims=True)
        dyv = (dyc * ps_ref[...]).astype(BF16)
        dwp_ref[...] += _dot_tn(pb, dyv[:tm])
        dp = _dot_nt(dyv, wp_ref[...])
        win_e = _pool_window((ext, D_C))
        pos_e = i * tm + lax.broadcasted_iota(jnp.int32, (ext, D_C), 0)
        cnt_e = jnp.minimum(pos_e + 1, win_e).astype(F32)
        qs_ref[...] = jnp.where(live, dp / cnt_e, 0.0)
        dzc = _leading_window_sums(qs_ref, l2_ref, l4_ref, l8_ref, tm) - dp[:tm]
        dz_ref[:, c0:c0 + D_C] = dzc.astype(BF16)

        @pl.when((b == 1) & (i == nt - 1))
        def _():
            dws_ref[...] = _causal_rows(dws_ref[...])

    in_specs = ([cur(D_IN), prev(D_IN), cur(D_A), nxt(D_A), cur(D_MODEL), nxt(D_MODEL)]
                + [full(s) for s in _MIX_PARAM_SHAPES])
    out_specs = [cur(D_IN)] + [full(s) for s in grad_shapes]
    out_shape = [SDS((t, D_IN), BF16)] + [SDS(s, F32) for s in grad_shapes]
    outs, couts = _call(
        body, name=name, grid=(2, nt), in_specs=in_specs, out_specs=out_specs, out_shape=out_shape,
        scratch_shapes=[pltpu.VMEM((SUB, HALO + tm, D_A), F32), pltpu.VMEM((SUB, ext, D_A), F32)]
        + [pltpu.VMEM((ext, D_C), F32)] * 5,
        args=[z, z, conv, conv, dm, dm, *mp], pieces=pieces)
    return outs, couts


MIXER_SMALL = ["conv_w", "conv_b", "conv_ln_g", "conv_ln_b", "w_pw", "sg_ln_g", "sg_ln_b", "w_s", "b_s", "w_pool",
               "pool_scale"]


def _mixer_params(p, w_pw_bf16, l):
    wp_bd = jnp.zeros((D_C, D_C), F32)
    for g in range(D_C // GROUP_DIM_C):
        sl = slice(g * GROUP_DIM_C, (g + 1) * GROUP_DIM_C)
        wp_bd = wp_bd.at[sl, sl].set(p["w_pool"][l, g])
    return [
        jnp.pad(p["conv_w"][l], ((0, 32 - CONV_WIDTH), (0, 0))),
        p["conv_b"][l][None], p["conv_ln_g"][l][None], p["conv_ln_b"][l][None],
        w_pw_bf16,
        p["sg_ln_g"][l][None], p["sg_ln_b"][l][None],
        p["w_s"][l].reshape(N_HEADS_B * CHUNK, CHUNK).astype(BF16),
        jnp.repeat(p["b_s"][l].T, HEAD_DIM_B, axis=1),
        wp_bd.astype(BF16),
        p["pool_scale"][l][None],
    ]


def _mixer_grads(g):
    dcw, dcb, dclg, dclb, dwpw, dslg, dslb, dws, dbs, dwp, dps = g
    blocks = [dwp[i * GROUP_DIM_C:(i + 1) * GROUP_DIM_C, i * GROUP_DIM_C:(i + 1) * GROUP_DIM_C]
              for i in range(D_C // GROUP_DIM_C)]
    return [dcw[:CONV_WIDTH], dcb[0], dclg[0], dclb[0], dwpw, dslg[0], dslb[0],
            dws.reshape(N_HEADS_B, CHUNK, CHUNK), dbs[:, :N_HEADS_B].T, jnp.stack(blocks), dps[0]]


def _row_tile(r, cap=512):
    best = r
    for d in range(16, min(r, cap) + 1, 16):
        if r % d == 0:
            best = d
    return best if best <= cap else r


def add_core_halves(g, r1, core, *, name):
    _, _, r, c = g.shape
    tr = _row_tile(r)

    def body(core_ref, g_ref, r_ref, o_ref):
        o_ref[...] = (g_ref[...].astype(F32) + r_ref[...].astype(F32)).astype(o_ref.dtype)

    outs, _ = _call(
        body, name=name, grid=(N_CHIPS, r // tr), prefetch=1,
        in_specs=[pl.BlockSpec((None, None, tr, c), lambda j, i, s: (j, s[0], i, 0)),
                  pl.BlockSpec((None, tr, c), lambda j, i, s: (j, i, 0))],
        out_specs=[pl.BlockSpec((None, tr, c), lambda j, i, s: (j, i, 0))],
        out_shape=[SDS((N_CHIPS, r, c), g.dtype)], args=[core, g, r1])
    return outs[0]


def sum_chips(h, r2, place, *, name):
    _, r, c = h.shape
    tr = _row_tile(r, 256)

    def body(place_ref, h_ref, a_ref, b_ref, c_ref, o_ref):
        acc = h_ref[...].astype(F32) + a_ref[...].astype(F32)
        acc = acc + b_ref[...].astype(F32)
        o_ref[...] = acc + c_ref[...].astype(F32)

    def blk(k):
        return pl.BlockSpec((None, tr, c), lambda i, s: (jnp.bitwise_xor(s[0], k), i, 0))

    outs, _ = _call(
        body, name=name, grid=(r // tr,), prefetch=1, in_specs=[blk(0), blk(1), blk(2), blk(3)],
        out_specs=[pl.BlockSpec((None, tr, c), lambda i, s: (s[1], i, 0))],
        out_shape=[SDS((2, r, c), F32)], args=[place, h, r2, r2, r2])
    return outs[0]


def allreduce_small(p, *, name, pieces=()):
    _, n, _ = p.shape

    def body(p_ref, o_ref, land_ref, send1, recv1, send2, recv2):
        x, y, c = lax.axis_index("x"), lax.axis_index("y"), lax.axis_index("c")
        me = 4 * x + 2 * y + c

        def peer(r):
            return ((1 - x) if r & 4 else x, (1 - y) if r & 2 else y, (1 - c) if r & 1 else c)

        def index(r):
            px, py, pc = peer(r)
            return 4 * px + 2 * py + pc

        land_ref[me] = p_ref[me]
        first = [_remote(p_ref.at[index(r)], land_ref.at[me], send1.at[r - 1], recv1.at[r - 1], peer(r))
                 for r in range(1, N_DEV)]
        for cp in first:
            cp.start()
        for r in range(1, N_DEV):
            blk = land_ref.at[index(r)]
            _remote(blk, blk, send1.at[r - 1], recv1.at[r - 1], peer(r)).wait_recv()
        acc = land_ref[0]
        for d in range(1, N_DEV):
            acc = acc + land_ref[d]
        o_ref[me] = acc
        second = [_remote(o_ref.at[me], o_ref.at[me], send2.at[r - 1], recv2.at[r - 1], peer(r))
                  for r in range(1, N_DEV)]
        for cp in second:
            cp.start()
        for r in range(1, N_DEV):
            blk = o_ref.at[index(r)]
            _remote(blk, blk, send2.at[r - 1], recv2.at[r - 1], peer(r)).wait_recv()
        for cp in first + second:
            cp.wait_send()

    vm = pl.BlockSpec(memory_space=pltpu.VMEM)
    outs, couts = _call(
        body, name=name, grid=(), in_specs=[vm], out_specs=[vm], out_shape=[SDS(p.shape, F32)],
        scratch_shapes=[pltpu.VMEM(p.shape, F32)] + [pltpu.SemaphoreType.DMA((N_DEV - 1,))] * 4,
        args=[p], pieces=pieces)
    return outs[0], couts


def _adam_update(w, g, m, v):
    m_new = ADAM_B1 * m + (1.0 - ADAM_B1) * g
    v_new = ADAM_B2 * v + (1.0 - ADAM_B2) * (g * g)
    m_hat = m_new / (1.0 - ADAM_B1 ** ADAM_STEP)
    v_hat = v_new / (1.0 - ADAM_B2 ** ADAM_STEP)
    return -ADAM_LR * (m_hat / (jnp.sqrt(v_hat) + ADAM_EPS) + ADAM_WD * w), m_new, v_new


def adamw(w, g, m, v, *, name, pieces=()):
    nl, r, c = w.shape
    tr = _row_tile(r, 512)

    def body(w_ref, g_ref, m_ref, v_ref, d_ref, mo_ref, vo_ref):
        d_ref[...], mo_ref[...], vo_ref[...] = _adam_update(w_ref[...], g_ref[...], m_ref[...], v_ref[...])

    blk = pl.BlockSpec((None, tr, c), lambda l, i: (l, i, 0))
    return _call(body, name=name, grid=(nl, r // tr), in_specs=[blk] * 4, out_specs=[blk] * 3,
                 out_shape=[SDS(w.shape, F32)] * 3, args=[w, g, m, v], pieces=pieces)


def adamw_small(ws, gs, ms, vs, *, name, pieces=()):
    n = len(ws)

    def body(*refs):
        for i in range(n):
            w_ref, g_ref, m_ref, v_ref = (refs[k * n + i] for k in range(4))
            d, mn, vn = _adam_update(w_ref[...], g_ref[...], m_ref[...], v_ref[...])
            refs[4 * n + i][...] = d
            refs[5 * n + i][...] = mn
            refs[6 * n + i][...] = vn

    vm = pl.BlockSpec(memory_space=pltpu.VMEM)
    res, couts = _call(body, name=name, grid=(), in_specs=[vm] * (4 * n), out_specs=[vm] * (3 * n),
                       out_shape=[SDS(w.shape, F32) for w in ws] * 3, args=[*ws, *gs, *ms, *vs], pieces=pieces)
    return (res[:n], res[n:2 * n], res[2 * n:]), couts


WEIGHTS = ["norm1_g", "w_in", "conv_w", "conv_b", "conv_ln_g", "conv_ln_b", "w_pw", "sg_ln_g", "sg_ln_b", "w_s",
           "b_s", "w_pool", "pool_scale", "w_out", "norm2_g", "w_gate_up", "w_down", "final_g"]
BIG = ["w_in", "w_pw", "w_out", "w_gate_up", "w_down"]
TRANSPOSED = {"w_in": True, "w_pw": False, "w_out": False, "w_gate_up": True, "w_down": False}
SMALL = [k for k in WEIGHTS if k not in BIG]


def _wire(a, transposed):
    if transposed:
        a = a.transpose(0, 2, 1)
    return [a[l].reshape(2, a.shape[1] // 2, a.shape[2]) for l in range(a.shape[0])]


def _pack(arrays):
    flat = jnp.concatenate([a.reshape(-1) for a in arrays])
    n = -(-flat.shape[0] // (N_DEV * LANES * 8)) * 8
    return jnp.pad(flat, (0, N_DEV * n * LANES - flat.shape[0])).reshape(N_DEV, n, LANES)


def _unpack(packed, shapes):
    flat = packed.reshape(-1)
    out, off = [], 0
    for s in shapes:
        size = math.prod(s)
        out.append(flat[off:off + size].reshape(s))
        off += size
    return out


def kernel(x, norm1_g, w_in, conv_w, conv_b, conv_ln_g, conv_ln_b, w_pw, sg_ln_g, sg_ln_b, w_s, b_s, w_pool, pool_scale, w_out, norm2_g, w_gate_up, w_down, final_g, loss_target, m_norm1_g, m_w_in, m_conv_w, m_conv_b, m_conv_ln_g, m_conv_ln_b, m_w_pw, m_sg_ln_g, m_sg_ln_b, m_w_s, m_b_s, m_w_pool, m_pool_scale, m_w_out, m_norm2_g, m_w_gate_up, m_w_down, m_final_g, v_norm1_g, v_w_in, v_conv_w, v_conv_b, v_conv_ln_g, v_conv_ln_b, v_w_pw, v_sg_ln_g, v_sg_ln_b, v_w_s, v_b_s, v_w_pool, v_pool_scale, v_w_out, v_norm2_g, v_w_gate_up, v_w_down, v_final_g):
    w = dict(norm1_g=norm1_g, w_in=w_in, conv_w=conv_w, conv_b=conv_b, conv_ln_g=conv_ln_g, conv_ln_b=conv_ln_b,
             w_pw=w_pw, sg_ln_g=sg_ln_g, sg_ln_b=sg_ln_b, w_s=w_s, b_s=b_s, w_pool=w_pool, pool_scale=pool_scale,
             w_out=w_out, norm2_g=norm2_g, w_gate_up=w_gate_up, w_down=w_down, final_g=final_g)
    m = dict(norm1_g=m_norm1_g, w_in=m_w_in, conv_w=m_conv_w, conv_b=m_conv_b, conv_ln_g=m_conv_ln_g,
             conv_ln_b=m_conv_ln_b, w_pw=m_w_pw, sg_ln_g=m_sg_ln_g, sg_ln_b=m_sg_ln_b, w_s=m_w_s, b_s=m_b_s,
             w_pool=m_w_pool, pool_scale=m_pool_scale, w_out=m_w_out, norm2_g=m_norm2_g, w_gate_up=m_w_gate_up,
             w_down=m_w_down, final_g=m_final_g)
    v = dict(norm1_g=v_norm1_g, w_in=v_w_in, conv_w=v_conv_w, conv_b=v_conv_b, conv_ln_g=v_conv_ln_g,
             conv_ln_b=v_conv_ln_b, w_pw=v_w_pw, sg_ln_g=v_sg_ln_g, sg_ln_b=v_sg_ln_b, w_s=v_w_s, b_s=v_b_s,
             w_pool=v_w_pool, pool_scale=v_pool_scale, w_out=v_w_out, norm2_g=v_norm2_g, w_gate_up=v_w_gate_up,
             w_down=v_w_down, final_g=v_final_g)
    bsz, seq, d = x.shape
    t = bsz * seq
    chip = 2 * lax.axis_index("x") + lax.axis_index("y")
    core = lax.axis_index("c")
    core_arr = jnp.reshape(core, (1,)).astype(jnp.int32)
    place_arr = jnp.stack([chip, core]).astype(jnp.int32)

    own = {k: _wire(w[k].astype(BF16), TRANSPOSED[k]) for k in BIG}
    cw_cols = conv_w.shape[2]
    own["side"] = [jnp.pad(conv_w, ((0, 0), (0, 32 - CONV_WIDTH), (0, 0)))]
    parts = {"w_in": 1, "w_pw": 1, "w_out": 1, "w_gate_up": 4, "w_down": 2, "side": 1}
    plan = {
        "norm1_fwd_0": [("ici", "w_in", 0, 0), ("ici", "w_pw", 0, 0), ("ici", "w_pw", 1, 0), ("ici", "side", 0, 0)],
        "gather_first": [("d2d", "w_in", 0, 0), ("d2d", "w_pw", 0, 0), ("d2d", "w_pw", 1, 0), ("d2d", "side", 0, 0)],
        "in_proj_0": [("ici", "w_out", 0, 0), ("ici", "w_gate_up", 0, 0)],
        "mixer_fwd_0": [("d2d", "w_out", 0, 0), ("d2d", "w_gate_up", 0, 0), ("ici", "w_gate_up", 0, 1),
                        ("ici", "w_gate_up", 0, 2)],
        "out_proj_0": [("d2d", "w_gate_up", 0, 1), ("d2d", "w_gate_up", 0, 2), ("both", "w_gate_up", 0, 3)],
        "swiglu_fwd_0": [("both", "w_down", 0, 0), ("both", "w_down", 0, 1), ("ici", "w_in", 1, 0)],
        "down_proj_0": [("d2d", "w_in", 1, 0), ("ici", "w_out", 1, 0), ("ici", "w_gate_up", 1, 0)],
        "in_proj_1": [("d2d", "w_out", 1, 0), ("d2d", "w_gate_up", 1, 0), ("ici", "w_gate_up", 1, 1)],
        "mixer_fwd_1": [("d2d", "w_gate_up", 1, 1), ("ici", "w_gate_up", 1, 2), ("ici", "w_gate_up", 1, 3)],
        "out_proj_1": [("d2d", "w_gate_up", 1, 2), ("d2d", "w_gate_up", 1, 3)],
        "swiglu_fwd_1": [("both", "w_down", 1, 0), ("both", "w_down", 1, 1)],
    }
    bufs = {}

    def grouped(name):
        groups = {}
        for stage, k, l, q in plan.get(name, []):
            groups.setdefault((k, l), []).append((stage, q))
        return groups

    def riding(name):
        return [gather_stages(own[k][l], bufs.get((k, l)), stages, parts[k])
                for (k, l), stages in grouped(name).items()]

    def landed(name, couts):
        for key, co in zip(grouped(name), couts):
            bufs[key] = co[0]

    def whole(k, l):
        g = lax.dynamic_update_index_in_dim(bufs[k, l], own[k][l], chip, 0)
        return g.reshape(-1, g.shape[-1])

    xs = [x.reshape(t, d)]
    saved = []
    full = {}
    p = dict(w)
    for l in range(DEPTH):
        x0 = xs[-1]
        name = f"in_proj_{l}"
        if l == 0:
            h1, co = rmsnorm_fwd(x0, w["norm1_g"][l][None], name="norm1_fwd_0", pieces=riding("norm1_fwd_0"))
            landed("norm1_fwd_0", co)
            landed("gather_first", comm_only(riding("gather_first"), name="gather_first"))
            side_all = lax.dynamic_update_index_in_dim(bufs["side", 0], own["side"][0], chip, 0)
            p["conv_w"] = side_all[:, :, :CONV_WIDTH, :].transpose(1, 2, 0, 3).reshape(
                DEPTH, CONV_WIDTH, N_CHIPS * cw_cols)
            full["w_in", l] = whole("w_in", l)
            z, co = mm_nt(h1, full["w_in", l], name=name, tm=512, tn=D_IN, out_dtype=BF16, pieces=riding(name))
        else:
            full["w_in", l] = whole("w_in", l)
            (z, h1), co = norm_mm_nt(x0, w["norm1_g"][l][None], full["w_in", l], name=name, tm=512, out_dtype=BF16,
                                     pieces=riding(name))
        landed(name, co)
        mp = _mixer_params(p, whole("w_pw", l), l)
        name = f"mixer_fwd_{l}"
        (mc, cv), co = mixer_fwd(z, mp, seq=seq, name=name, pieces=riding(name))
        landed(name, co)
        full["w_out", l] = whole("w_out", l)
        name = f"out_proj_{l}"
        x1, co = mm_nn(mc, full["w_out", l], name=name, tm=512, tn=D_MODEL, out_dtype=F32, residual=x0,
                       pieces=riding(name))
        landed(name, co)
        full["w_gate_up", l] = whole("w_gate_up", l)
        name = f"swiglu_fwd_{l}"
        (gu, act, h2), co = swiglu_fwd(x1, w["norm2_g"][l][None], full["w_gate_up", l], name=name, tm=512, tn=1408,
                                       pieces=riding(name))
        landed(name, co)
        full["w_down", l] = whole("w_down", l)
        name = f"down_proj_{l}"
        x2, co = mm_nn(act, full["w_down", l], name=name, tm=512, tn=D_MODEL, out_dtype=F32, residual=x1,
                       pieces=riding(name))
        landed(name, co)
        saved.append((x0, h1, z, cv, mc, x1, h2, gu, act, mp))
        xs.append(x2)

    loss, dx, dxb, d_final_g = loss_head(xs[-1], p["final_g"][None], loss_target.reshape(t, d), name="loss_head")

    small = {k: [None] * DEPTH for k in SMALL if k != "final_g"}
    reduced = {}
    carry = None

    def halves(g):
        return g.reshape(N_CHIPS, 2, g.shape[0] // (2 * N_CHIPS), g.shape[1])

    for l in reversed(range(DEPTH)):
        x0, h1, z, cv, mc, x1, h2, gu, act, mp = saved[l]
        pieces = [exchange_d2d(carry[0]), reduce_ici(carry[1])] if carry else []
        g_down, co = mm_tn(act, dxb, name=f"down_proj_dw_{l}", tk=D_FF // 2, tn=D_MODEL, tt=TT, out_dtype=BF16,
                           pieces=pieces)
        g_down = halves(g_down)
        pieces = [reduce_d2d(g_down)]
        if carry:
            reduced["w_pw", l + 1] = co[0][0]
            s_in = sum_chips(carry[1], co[1][0], place_arr, name=f"sum_chips_w_in_{l + 1}")
            pieces.append(exchange_d2d(s_in))
        dgu, co = swiglu_bwd(dxb, full["w_down", l], gu, name=f"swiglu_bwd_{l}", tm=512, tn=1408, pieces=pieces)
        if carry:
            reduced["w_in", l + 1] = co[1][0]
        h_down = add_core_halves(g_down, co[0][0], core_arr, name=f"add_cores_w_down_{l}")
        g_gu, (r2,) = mm_gu_tn(dgu, h2, name=f"gate_up_dw_{l}", tn=1408, tt=TT, pieces=[reduce_ici(h_down)])
        s_down = sum_chips(h_down, r2[0], place_arr, name=f"sum_chips_w_down_{l}")
        g_gu = halves(g_gu)
        (dx, dxb, dn2), (e, r1) = mm_norm_bwd(dgu, full["w_gate_up", l], x1, p["norm2_g"][l][None], dx,
                                              name=f"gate_up_dx_{l}", tm=256,
                                              pieces=[exchange_d2d(s_down), reduce_d2d(g_gu)])
        reduced["w_down", l] = e[0]
        small["norm2_g"][l] = dn2[0]
        h_gu = add_core_halves(g_gu, r1[0], core_arr, name=f"add_cores_w_gate_up_{l}")
        g_out, _ = mm_tn(mc, dxb, name=f"out_proj_dw_{l}", tk=D_MODEL, tn=D_MODEL, tt=TT, out_dtype=BF16)
        g_out = halves(g_out)
        dmc, (r1,) = mm_nt(dxb, full["w_out", l], name=f"out_proj_dx_{l}", tm=512, tn=D_MODEL, out_dtype=BF16,
                           pieces=[reduce_d2d(g_out)])
        h_out = add_core_halves(g_out, r1[0], core_arr, name=f"add_cores_w_out_{l}")
        (dz, *mg), (r2a, r2b) = mixer_bwd(z, cv, dmc, mp, seq=seq, name=f"mixer_bwd_{l}",
                                          pieces=[reduce_ici(h_gu), reduce_ici(h_out)])
        s_gu = sum_chips(h_gu, r2a[0], place_arr, name=f"sum_chips_w_gate_up_{l}")
        s_out = sum_chips(h_out, r2b[0], place_arr, name=f"sum_chips_w_out_{l}")
        mgrads = dict(zip(MIXER_SMALL, _mixer_grads(mg)))
        for k in MIXER_SMALL:
            if k != "w_pw":
                small[k][l] = mgrads[k]
        g_pw = halves(mgrads["w_pw"].astype(BF16))
        g_in, (ea, eb, r1) = mm_tn(dz, h1, name=f"in_proj_dw_{l}", tk=D_IN // 2, tn=D_MODEL, tt=TT, out_dtype=BF16,
                                   pieces=[exchange_d2d(s_gu), exchange_d2d(s_out), reduce_d2d(g_pw)])
        reduced["w_gate_up", l], reduced["w_out", l] = ea[0], eb[0]
        h_pw = add_core_halves(g_pw, r1[0], core_arr, name=f"add_cores_w_pw_{l}")
        g_in = halves(g_in)
        (dx, dxb, dn1), (r2, r1) = mm_norm_bwd(dz, full["w_in", l], x0, p["norm1_g"][l][None], dx,
                                               name=f"in_proj_dx_{l}", tm=512,
                                               pieces=[reduce_ici(h_pw), reduce_d2d(g_in)])
        small["norm1_g"][l] = dn1[0]
        s_pw = sum_chips(h_pw, r2[0], place_arr, name=f"sum_chips_w_pw_{l}")
        h_in = add_core_halves(g_in, r1[0], core_arr, name=f"add_cores_w_in_{l}")
        carry = (s_pw, h_in)
    grad_x = dx.reshape(bsz, seq, d)

    g_small = [jnp.stack(small[k]) if k != "final_g" else d_final_g[0] for k in SMALL]
    small_shapes = [g.shape for g in g_small] + [(1,)]
    s_pw, h_in = carry
    summed, co = allreduce_small(_pack(g_small + [loss.reshape(1)]), name="allreduce_small",
                                 pieces=[exchange_d2d(s_pw), reduce_ici(h_in)])
    reduced["w_pw", 0] = co[0][0]
    s_in = sum_chips(h_in, co[1][0], place_arr, name="sum_chips_w_in_0")
    summed = _unpack(summed, small_shapes)
    loss = summed[-1][0]
    grad = dict(zip(SMALL, summed[:-1]))
    grad["conv_w"] = lax.dynamic_slice_in_dim(grad["conv_w"], chip * cw_cols, cw_cols, axis=2)

    delta, new_m, new_v = {}, {}, {}

    def flat2(a):
        return a.reshape(-1, a.shape[-1])

    res, co = adamw_small(*[[flat2(tt[k]) for k in SMALL] for tt in (w, grad, m, v)], name="adamw_small",
                          pieces=[exchange_d2d(s_in)])
    reduced["w_in", 0] = co[0][0]
    for out, arrs in zip((delta, new_m, new_v), res):
        out.update({k: a.reshape(w[k].shape) for k, a in zip(SMALL, arrs)})
    for k in BIG:
        g = jnp.stack([reduced[k, l].reshape(-1, reduced[k, l].shape[-1]) for l in range(DEPTH)])
        grad[k] = g.transpose(0, 2, 1) if TRANSPOSED[k] else g
        (delta[k], new_m[k], new_v[k]), _ = adamw(w[k], grad[k], m[k], v[k], name=f"adamw_{k}")
    return (loss, grad_x, *[grad[k] for k in WEIGHTS], *[delta[k] for k in WEIGHTS],
            *[new_m[k] for k in WEIGHTS], *[new_v[k] for k in WEIGHTS])
```

```python
import functools
import math

import jax
import jax.numpy as jnp
from jax import lax
from jax.experimental import pallas as pl
from jax.experimental.pallas import tpu as pltpu

F32 = jnp.float32
BF16 = jnp.bfloat16

D_MODEL = 1024
DEPTH = 2
D_A = 384
D_B = 384
D_C = 256
D_IN = 2 * D_A + 2 * D_B + D_C
N_HEADS_B = 4
HEAD_DIM_B = 96
GROUP_DIM_C = 64
CONV_WIDTH = 31
CHUNK = 128
D_FF = 2816
RMS_EPS = 1e-6
LN_EPS = 1e-5
HALO = 32
TT = 2048
N_CHIPS = 4
N_DEV = 8
LANES = 128

ADAM_LR = 0.001
ADAM_B1 = 0.9
ADAM_B2 = 0.999
ADAM_EPS = 1e-08
ADAM_WD = 0.01
ADAM_STEP = 10

VMEM_LIMIT = 56 * 1024 * 1024
_INTERPRET = False

MESH = pl.DeviceIdType.MESH
ANY = pl.BlockSpec(memory_space=pl.ANY)
SDS = jax.ShapeDtypeStruct


def _sigmoid(x):
    return 0.5 * jnp.tanh(0.5 * x) + 0.5


_GELU_C = math.sqrt(2.0 / math.pi)


def _gelu(x):
    return 0.5 * x * (1.0 + jnp.tanh(_GELU_C * (x + 0.044715 * x * x * x)))


def _gelu_grad(x):
    t = jnp.tanh(_GELU_C * (x + 0.044715 * x * x * x))
    return 0.5 * (1.0 + t) + 0.5 * x * (1.0 - t * t) * _GELU_C * (1.0 + 3 * 0.044715 * x * x)


def _dot(a, b):
    return jnp.dot(a, b, preferred_element_type=F32)


def _dot_nt(a, b):
    return lax.dot_general(a, b, (((1,), (1,)), ((), ())), preferred_element_type=F32)


def _dot_tn(a, b):
    return lax.dot_general(a, b, (((0,), (0,)), ((), ())), preferred_element_type=F32)


class Piece:
    def __init__(self, operands, out_shapes, aliases, n_sems, start, finish):
        self.operands, self.out_shapes, self.aliases, self.n_sems = operands, out_shapes, aliases, n_sems
        self.start, self.finish = start, finish


def _place():
    x, y, c = lax.axis_index("x"), lax.axis_index("y"), lax.axis_index("c")
    chips = [(1 - x, y), (x, 1 - y), (1 - x, 1 - y)]
    return x, y, c, chips


def _remote(src, dst, send_sem, recv_sem, to):
    return pltpu.make_async_remote_copy(src_ref=src, dst_ref=dst, send_sem=send_sem, recv_sem=recv_sem,
                                        device_id=to, device_id_type=MESH)


def _rows(r, q, nq):
    return pl.ds(q * (r // nq), r // nq)


def gather_ici(src, g=None, q=0, nq=1):
    rows = _rows(src.shape[1], q, nq)

    def copies(ins, outs, sem):
        x, y, c, chips = _place()
        j = 2 * x + y
        return [(_remote(ins[0].at[c, rows], outs[0].at[j, c, rows], sem(k), sem(3 + k), (px, py, c)),
                 outs[0].at[2 * px + py, c, rows], (px, py, c)) for k, (px, py) in enumerate(chips)]

    def start(ins, outs, sem):
        for cp, _, _ in copies(ins, outs, sem):
            cp.start()

    def finish(ins, outs, sem):
        cps = copies(ins, outs, sem)
        for k, (_, landed, frm) in enumerate(cps):
            _remote(landed, landed, sem(k), sem(3 + k), frm).wait_recv()
        for cp, _, _ in cps:
            cp.wait_send()

    shape = SDS((N_CHIPS,) + src.shape, src.dtype)
    if g is None:
        return Piece([src], [shape], {}, 6, start, finish)
    return Piece([src, g], [shape], {1: 0}, 6, start, finish)


def gather_d2d(g, q=0, nq=1, sem0=0):
    rows = _rows(g.shape[2], q, nq)

    def copies(outs, sem):
        x, y, c, chips = _place()
        return [(_remote(outs[0].at[2 * px + py, c, rows], outs[0].at[2 * px + py, c, rows],
                         sem(sem0 + k), sem(sem0 + 3 + k), (x, y, 1 - c)),
                 outs[0].at[2 * px + py, 1 - c, rows], (x, y, 1 - c)) for k, (px, py) in enumerate(chips)]

    def start(ins, outs, sem):
        for cp, _, _ in copies(outs, sem):
            cp.start()

    def finish(ins, outs, sem):
        cps = copies(outs, sem)
        for k, (_, landed, frm) in enumerate(cps):
            _remote(landed, landed, sem(sem0 + k), sem(sem0 + 3 + k), frm).wait_recv()
        for cp, _, _ in cps:
            cp.wait_send()

    return Piece([g], [SDS(g.shape, g.dtype)], {0: 0}, 6, start, finish)


def gather_both(src, g=None, q=0, nq=1):
    a = gather_ici(src, g, q, nq)
    b = gather_d2d(jax.ShapeDtypeStruct((N_CHIPS,) + src.shape, src.dtype), q, nq, sem0=6)

    def finish(ins, outs, sem):
        a.finish(ins, outs, sem)
        b.start(ins, outs, sem)
        b.finish(ins, outs, sem)

    return Piece(a.operands, a.out_shapes, a.aliases, 12, a.start, finish)


def gather_stages(src, g, stages, nq):
    shape = SDS((N_CHIPS,) + src.shape, src.dtype)
    make = {"ici": lambda q: gather_ici(src, None, q, nq), "d2d": lambda q: gather_d2d(shape, q, nq),
            "both": lambda q: gather_both(src, None, q, nq)}
    subs, n_sems = [], 0
    for stage, q in stages:
        sub = make[stage](q)
        subs.append((sub, n_sems))
        n_sems += sub.n_sems

    def run(method):
        def go(ins, outs, sem):
            for sub, base in subs:
                getattr(sub, method)(ins, outs, functools.partial(lambda k, base: sem(base + k), base=base))
        return go

    if g is None:
        return Piece([src], [shape], {}, n_sems, run("start"), run("finish"))
    return Piece([src, g], [shape], {1: 0}, n_sems, run("start"), run("finish"))


def reduce_d2d(g):
    def copies(ins, outs, sem):
        x, y, c, _ = _place()
        return [_remote(ins[0].at[j, 1 - c], outs[0].at[j], sem(j), sem(4 + j), (x, y, 1 - c)) for j in range(N_CHIPS)]

    def start(ins, outs, sem):
        for cp in copies(ins, outs, sem):
            cp.start()

    def finish(ins, outs, sem):
        cps = copies(ins, outs, sem)
        for cp in cps:
            cp.wait_recv()
        for cp in cps:
            cp.wait_send()

    return Piece([g], [SDS((N_CHIPS,) + g.shape[2:], g.dtype)], {}, 8, start, finish)


def reduce_ici(h):
    def copies(ins, outs, sem):
        x, y, c, chips = _place()
        j = 2 * x + y
        return [(_remote(ins[0].at[2 * px + py], outs[0].at[j], sem(k), sem(3 + k), (px, py, c)),
                 outs[0].at[2 * px + py], (px, py, c)) for k, (px, py) in enumerate(chips)]

    def start(ins, outs, sem):
        for cp, _, _ in copies(ins, outs, sem):
            cp.start()

    def finish(ins, outs, sem):
        cps = copies(ins, outs, sem)
        for k, (_, landed, frm) in enumerate(cps):
            _remote(landed, landed, sem(k), sem(3 + k), frm).wait_recv()
        for cp, _, _ in cps:
            cp.wait_send()

    return Piece([h], [SDS(h.shape, h.dtype)], {}, 6, start, finish)


def exchange_d2d(g):
    def copy(outs, sem):
        x, y, c, _ = _place()
        return _remote(outs[0].at[c], outs[0].at[c], sem(0), sem(1), (x, y, 1 - c)), outs[0].at[1 - c], (x, y, 1 - c)

    def start(ins, outs, sem):
        copy(outs, sem)[0].start()

    def finish(ins, outs, sem):
        cp, landed, frm = copy(outs, sem)
        _remote(landed, landed, sem(0), sem(1), frm).wait_recv()
        cp.wait_send()

    return Piece([g], [SDS(g.shape, g.dtype)], {0: 0}, 2, start, finish)


def _call(body, *, name, grid, in_specs, out_specs, out_shape, args, scratch_shapes=(), pieces=(), prefetch=0):
    n_in, n_out, n_scr = len(in_specs), len(out_specs), len(scratch_shapes)
    c_ops = [a for p in pieces for a in p.operands]
    c_outs = [s for p in pieces for s in p.out_shapes]
    n_sems = sum(p.n_sems for p in pieces)
    aliases = {}
    op_off, out_off = prefetch + n_in, n_out
    for p in pieces:
        for i, o in p.aliases.items():
            aliases[op_off + i] = out_off + o
        op_off += len(p.operands)
        out_off += len(p.out_shapes)

    def wrapped(*refs):
        pre, refs = refs[:prefetch], refs[prefetch:]
        ins, cin = refs[:n_in], refs[n_in:n_in + len(c_ops)]
        o0 = n_in + len(c_ops)
        outs, cout = refs[o0:o0 + n_out], refs[o0 + n_out:o0 + n_out + len(c_outs)]
        s0 = o0 + n_out + len(c_outs)
        scr = refs[s0:s0 + n_scr]

        def each(method):
            sems = refs[s0 + n_scr]
            i_off = o_off = s_off = 0
            for p in pieces:
                getattr(p, method)(cin[i_off:i_off + len(p.operands)], cout[o_off:o_off + len(p.out_shapes)],
                                   functools.partial(lambda k, base: sems.at[base + k], base=s_off))
                i_off, o_off, s_off = i_off + len(p.operands), o_off + len(p.out_shapes), s_off + p.n_sems

        if pieces and grid:
            ids = [pl.program_id(a) for a in range(len(grid))]
            first = functools.reduce(jnp.logical_and, [i == 0 for i in ids])
            last = functools.reduce(jnp.logical_and, [i == g - 1 for i, g in zip(ids, grid)])
            pl.when(first)(lambda: each("start"))
        elif pieces:
            each("start")
        if body is not None:
            body(*pre, *ins, *outs, *scr)
        if pieces and grid:
            pl.when(last)(lambda: each("finish"))
        elif pieces:
            each("finish")

    scratch = list(scratch_shapes) + ([pltpu.SemaphoreType.DMA((n_sems,))] if pieces else [])
    all_in = list(in_specs) + [ANY] * len(c_ops)
    all_out = list(out_specs) + [ANY] * len(c_outs)
    shapes = list(out_shape) + c_outs
    kw = dict(name=name, out_shape=shapes, input_output_aliases=aliases, interpret=_INTERPRET)
    if grid:
        kw["compiler_params"] = pltpu.CompilerParams(dimension_semantics=("arbitrary",) * len(grid),
                                                     vmem_limit_bytes=VMEM_LIMIT)
    if prefetch:
        kw["grid_spec"] = pltpu.PrefetchScalarGridSpec(num_scalar_prefetch=prefetch, grid=grid, in_specs=all_in,
                                                       out_specs=all_out, scratch_shapes=scratch)
    else:
        kw.update(in_specs=all_in, out_specs=all_out, scratch_shapes=scratch)
        if grid:
            kw["grid"] = grid
    res = pl.pallas_call(wrapped, **kw)(*args, *c_ops)
    outs, rest = list(res[:n_out]), list(res[n_out:])
    couts = []
    for p in pieces:
        couts.append(rest[:len(p.out_shapes)])
        rest = rest[len(p.out_shapes):]
    return outs, couts


def comm_only(pieces, *, name):
    return _call(None, name=name, grid=(), in_specs=[], out_specs=[], out_shape=[], args=[], pieces=pieces)[1]


def rmsnorm_fwd(x, g, *, name, pieces=()):
    t, d = x.shape
    tm = min(512, t)

    def body(x_ref, g_ref, o_ref):
        xv = x_ref[...]
        rstd = lax.rsqrt(jnp.mean(xv * xv, axis=-1, keepdims=True) + RMS_EPS)
        o_ref[...] = (xv * rstd * g_ref[...]).astype(BF16)

    outs, couts = _call(
        body, name=name, grid=(t // tm,),
        in_specs=[pl.BlockSpec((tm, d), lambda i: (i, 0)), pl.BlockSpec((1, d), lambda i: (0, 0))],
        out_specs=[pl.BlockSpec((tm, d), lambda i: (i, 0))], out_shape=[SDS((t, d), BF16)],
        args=[x, g], pieces=pieces)
    return outs[0], couts


def loss_head(x, g, target, *, name):
    t, d = x.shape
    tm = min(512, t)

    def body(x_ref, g_ref, t_ref, loss_ref, dx_ref, dxb_ref, dg_ref):
        i = pl.program_id(0)
        xv = x_ref[...]
        gv = g_ref[...]
        rstd = lax.rsqrt(jnp.mean(xv * xv, axis=-1, keepdims=True) + RMS_EPS)
        xhat = xv * rstd
        err = xhat * gv - t_ref[...]
        dy = err * (1.0 / d)
        dxhat = dy * gv
        dx = rstd * (dxhat - xhat * jnp.mean(dxhat * xhat, axis=-1, keepdims=True))
        dx_ref[...] = dx
        dxb_ref[...] = dx.astype(BF16)

        @pl.when(i == 0)
        def _():
            dg_ref[...] = jnp.zeros_like(dg_ref)
            loss_ref[...] = jnp.zeros_like(loss_ref)

        dg_ref[...] += jnp.sum(dy * xhat, axis=0, keepdims=True)
        per_tok = jnp.sum(err * err, axis=-1, keepdims=True) * (0.5 / d)
        loss_ref[...] += jnp.sum(per_tok, axis=0, keepdims=True)

    row = pl.BlockSpec((tm, d), lambda i: (i, 0))
    vec = pl.BlockSpec((1, d), lambda i: (0, 0))
    one = pl.BlockSpec((1, 1), lambda i: (0, 0))
    outs, _ = _call(
        body, name=name, grid=(t // tm,), in_specs=[row, vec, row], out_specs=[one, row, row, vec],
        out_shape=[SDS((1, 1), F32), SDS((t, d), F32), SDS((t, d), BF16), SDS((1, d), F32)],
        args=[x, g, target])
    return outs


def mm_nn(a, b, *, name, tm, tn, out_dtype, residual=None, pieces=()):
    m, k = a.shape
    n = b.shape[1]
    tm, tn = min(tm, m), min(tn, n)
    has_res = residual is not None

    def body(a_ref, b_ref, *rest):
        o_ref = rest[-1]
        acc = _dot(a_ref[...], b_ref[...])
        if has_res:
            acc = acc + rest[0][...]
        o_ref[...] = acc.astype(o_ref.dtype)

    in_specs = [pl.BlockSpec((tm, k), lambda j, i: (i, 0)), pl.BlockSpec((k, tn), lambda j, i: (0, j))]
    args = [a, b]
    if has_res:
        in_specs.append(pl.BlockSpec((tm, tn), lambda j, i: (i, j)))
        args.append(residual)
    outs, couts = _call(
        body, name=name, grid=(n // tn, m // tm), in_specs=in_specs,
        out_specs=[pl.BlockSpec((tm, tn), lambda j, i: (i, j))], out_shape=[SDS((m, n), out_dtype)],
        args=args, pieces=pieces)
    return outs[0], couts


def mm_nt(a, b, *, name, tm, tn, out_dtype, pieces=()):
    m, k = a.shape
    n = b.shape[0]
    tm, tn = min(tm, m), min(tn, n)

    def body(a_ref, b_ref, o_ref):
        o_ref[...] = _dot_nt(a_ref[...], b_ref[...]).astype(o_ref.dtype)

    outs, couts = _call(
        body, name=name, grid=(n // tn, m // tm),
        in_specs=[pl.BlockSpec((tm, k), lambda j, i: (i, 0)), pl.BlockSpec((tn, k), lambda j, i: (j, 0))],
        out_specs=[pl.BlockSpec((tm, tn), lambda j, i: (i, j))], out_shape=[SDS((m, n), out_dtype)],
        args=[a, b], pieces=pieces)
    return outs[0], couts


def _rms_norm(xv, gv):
    rstd = lax.rsqrt(jnp.mean(xv * xv, axis=-1, keepdims=True) + RMS_EPS)
    return (xv * rstd * gv).astype(BF16)


def norm_mm_nt(x, g, b, *, name, tm, out_dtype, pieces=()):
    m, k = x.shape
    n = b.shape[0]
    tm = min(tm, m)

    def body(x_ref, g_ref, b_ref, o_ref, h_ref):
        h = _rms_norm(x_ref[...], g_ref[...])
        h_ref[...] = h
        o_ref[...] = _dot_nt(h, b_ref[...]).astype(o_ref.dtype)

    outs, couts = _call(
        body, name=name, grid=(m // tm,),
        in_specs=[pl.BlockSpec((tm, k), lambda i: (i, 0)), pl.BlockSpec((1, k), lambda i: (0, 0)),
                  pl.BlockSpec((n, k), lambda i: (0, 0))],
        out_specs=[pl.BlockSpec((tm, n), lambda i: (i, 0)), pl.BlockSpec((tm, k), lambda i: (i, 0))],
        out_shape=[SDS((m, n), out_dtype), SDS((m, k), BF16)], args=[x, g, b], pieces=pieces)
    return outs, couts


def mm_tn(a, b, *, name, tk, tn, tt, out_dtype, pieces=()):
    t, k = a.shape
    n = b.shape[1]
    tk, tn, tt = min(tk, k), min(tn, n), min(tt, t)
    nt = t // tt

    def body(a_ref, b_ref, o_ref, acc_ref):
        s = pl.program_id(2)

        @pl.when(s == 0)
        def _():
            acc_ref[...] = jnp.zeros_like(acc_ref)

        acc_ref[...] += _dot_tn(a_ref[...], b_ref[...])

        @pl.when(s == nt - 1)
        def _():
            o_ref[...] = acc_ref[...].astype(o_ref.dtype)

    outs, couts = _call(
        body, name=name, grid=(k // tk, n // tn, nt),
        in_specs=[pl.BlockSpec((tt, tk), lambda i, j, s: (s, i)), pl.BlockSpec((tt, tn), lambda i, j, s: (s, j))],
        out_specs=[pl.BlockSpec((tk, tn), lambda i, j, s: (i, j))], out_shape=[SDS((k, n), out_dtype)],
        scratch_shapes=[pltpu.VMEM((tk, tn), F32)], args=[a, b], pieces=pieces)
    return outs[0], couts


def swiglu_fwd(x, g, wt, *, name, tm, tn, pieces=()):
    t, d = x.shape
    ff = wt.shape[0] // 2
    tm, tn = min(tm, t), min(tn, ff)
    nb = ff // tn

    nm = t // tm

    def body(x_ref, g_ref, wg_ref, wu_ref, gu_ref, act_ref, h_ref):
        hv = _rms_norm(x_ref[...], g_ref[...])

        @pl.when(pl.program_id(0) == 0)
        def _():
            h_ref[...] = hv

        gate = _dot_nt(hv, wg_ref[...])
        up = _dot_nt(hv, wu_ref[...])
        gu_ref[0] = gate.astype(BF16)
        gu_ref[1] = up.astype(BF16)
        act_ref[...] = (gate * _sigmoid(gate) * up).astype(BF16)

    outs, couts = _call(
        body, name=name, grid=(nb, t // tm),
        in_specs=[pl.BlockSpec((tm, d), lambda j, i: (i, 0)), pl.BlockSpec((1, d), lambda j, i: (0, 0)),
                  pl.BlockSpec((tn, d), lambda j, i: (j, 0)),
                  pl.BlockSpec((tn, d), lambda j, i: (j + nb, 0))],
        out_specs=[pl.BlockSpec((2, tm, tn), lambda j, i: (0, i, j)), pl.BlockSpec((tm, tn), lambda j, i: (i, j)),
                   pl.BlockSpec((tm, d), lambda j, i: (jnp.where(j == 0, i, nm - 1), 0))],
        out_shape=[SDS((2, t, ff), BF16), SDS((t, ff), BF16), SDS((t, d), BF16)], args=[x, g, wt, wt],
        pieces=pieces)
    return outs, couts


def swiglu_bwd(dx, w_down, gu, *, name, tm, tn, pieces=()):
    t, d = dx.shape
    ff = w_down.shape[0]
    tm, tn = min(tm, t), min(tn, ff)

    def body(dx_ref, w_ref, gu_ref, o_ref):
        dact = _dot_nt(dx_ref[...], w_ref[...])
        gate = gu_ref[0].astype(F32)
        up = gu_ref[1].astype(F32)
        sg = _sigmoid(gate)
        o_ref[0] = (dact * up * sg * (1.0 + gate * (1.0 - sg))).astype(BF16)
        o_ref[1] = (dact * gate * sg).astype(BF16)

    outs, couts = _call(
        body, name=name, grid=(ff // tn, t // tm),
        in_specs=[pl.BlockSpec((tm, d), lambda j, i: (i, 0)), pl.BlockSpec((tn, d), lambda j, i: (j, 0)),
                  pl.BlockSpec((2, tm, tn), lambda j, i: (0, i, j))],
        out_specs=[pl.BlockSpec((2, tm, tn), lambda j, i: (0, i, j))], out_shape=[SDS((2, t, ff), BF16)],
        args=[dx, w_down, gu], pieces=pieces)
    return outs[0], couts


def mm_norm_bwd(a, wt, x, g, dres, *, name, tm, pieces=()):
    parts = a.shape[0] if a.ndim == 3 else 1
    t, kp = a.shape[-2:]
    d = wt.shape[1]
    tm = min(tm, t)

    def body(a_ref, w_ref, x_ref, g_ref, dres_ref, dx_ref, dxb_ref, dg_ref):
        i = pl.program_id(0)
        if parts == 1:
            dh = _dot(a_ref[...], w_ref[...])
        else:
            dh = _dot(a_ref[0], w_ref[0:kp, :])
            for q in range(1, parts):
                dh = dh + _dot(a_ref[q], w_ref[q * kp:(q + 1) * kp, :])
        xv = x_ref[...]
        rstd = lax.rsqrt(jnp.mean(xv * xv, axis=-1, keepdims=True) + RMS_EPS)
        xhat = xv * rstd
        dxhat = dh * g_ref[...]
        dx = dres_ref[...] + rstd * (dxhat - xhat * jnp.mean(dxhat * xhat, axis=-1, keepdims=True))
        dx_ref[...] = dx
        dxb_ref[...] = dx.astype(BF16)

        @pl.when(i == 0)
        def _():
            dg_ref[...] = jnp.zeros_like(dg_ref)

        dg_ref[...] += jnp.sum(dh * xhat, axis=0, keepdims=True)

    a_spec = (pl.BlockSpec((tm, kp), lambda i: (i, 0)) if parts == 1
              else pl.BlockSpec((parts, tm, kp), lambda i: (0, i, 0)))
    row = pl.BlockSpec((tm, d), lambda i: (i, 0))
    vec = pl.BlockSpec((1, d), lambda i: (0, 0))
    outs, couts = _call(
        body, name=name, grid=(t // tm,),
        in_specs=[a_spec, pl.BlockSpec((parts * kp, d), lambda i: (0, 0)), row, vec, row],
        out_specs=[row, row, vec], out_shape=[SDS((t, d), F32), SDS((t, d), BF16), SDS((1, d), F32)],
        args=[a, wt, x, g, dres], pieces=pieces)
    return outs, couts


def mm_gu_tn(dgu, h, *, name, tn, tt, pieces=()):
    t, d = h.shape
    ff = dgu.shape[2]
    tn, tt = min(tn, ff), min(tt, t)
    nb = ff // tn
    nt = t // tt

    def body(a_ref, h_ref, o_ref, acc_ref):
        s = pl.program_id(1)

        @pl.when(s == 0)
        def _():
            acc_ref[...] = jnp.zeros_like(acc_ref)

        acc_ref[...] += _dot_tn(a_ref[...], h_ref[...])

        @pl.when(s == nt - 1)
        def _():
            o_ref[...] = acc_ref[...].astype(o_ref.dtype)

    outs, couts = _call(
        body, name=name, grid=(2 * nb, nt),
        in_specs=[pl.BlockSpec((None, tt, tn), lambda j, s: (j // nb, s, j % nb)),
                  pl.BlockSpec((tt, d), lambda j, s: (s, 0))],
        out_specs=[pl.BlockSpec((tn, d), lambda j, s: (j, 0))], out_shape=[SDS((2 * ff, d), BF16)],
        scratch_shapes=[pltpu.VMEM((tn, d), F32)], args=[dgu, h], pieces=pieces)
    return outs[0], couts


def _head_masks(shape):
    lane = lax.broadcasted_iota(jnp.int32, shape, 1)
    return [(lane >= h * HEAD_DIM_B) & (lane < (h + 1) * HEAD_DIM_B) for h in range(N_HEADS_B)]


def _causal_rows(ws):
    r = lax.broadcasted_iota(jnp.int32, ws.shape, 0) % CHUNK
    c = lax.broadcasted_iota(jnp.int32, ws.shape, 1)
    return jnp.where(c <= r, ws, jnp.zeros_like(ws))


def _pool_window(shape):
    lane = lax.broadcasted_iota(jnp.int32, shape, 1)
    return jnp.left_shift(2, lane // GROUP_DIM_C)


def _layer_norm_fwd(x, g, b):
    mu = jnp.mean(x, axis=-1, keepdims=True)
    xc = x - mu
    rstd = lax.rsqrt(jnp.mean(xc * xc, axis=-1, keepdims=True) + LN_EPS)
    xhat = xc * rstd
    return xhat * g + b, xhat, rstd


def _layer_norm_bwd(dy, xhat, rstd, g):
    dxhat = dy * g
    return rstd * (dxhat - jnp.mean(dxhat, axis=-1, keepdims=True)
                   - xhat * jnp.mean(dxhat * xhat, axis=-1, keepdims=True))


def _gate_mix(ws_masked, vl_chunk, masks):
    out = _dot(ws_masked, vl_chunk.astype(BF16))
    s = jnp.zeros((CHUNK, D_B), F32)
    for h in range(N_HEADS_B):
        s = s + jnp.where(masks[h], out[h * CHUNK:(h + 1) * CHUNK], 0.0)
    return s


SUB = 8
CONV_ROWS = 64


def _shifted_copies(ref, n):
    for b in range(1, SUB):
        ref[b, 0:n, :] = ref[0, pl.ds(b, n), :]


def _tap(ref, off, n):
    a, b = divmod(off, SUB)
    return ref[b, pl.ds(SUB * a, n), :]


def _pick_window(s2, s4, s8, s16):
    grp = lax.broadcasted_iota(jnp.int32, s2.shape, 1) // GROUP_DIM_C
    return jnp.where(grp == 0, s2, jnp.where(grp == 1, s4, jnp.where(grp == 2, s8, s16)))


def _trailing_window_sums(src_ref, l2_ref, l4_ref, l8_ref, tm):
    n = HALO + tm
    l2_ref[8:n, :] = src_ref[8:n, :] + src_ref[pl.ds(7, n - 8), :]
    l4_ref[16:n, :] = l2_ref[16:n, :] + l2_ref[pl.ds(14, n - 16), :]
    l8_ref[24:n, :] = l4_ref[24:n, :] + l4_ref[pl.ds(20, n - 24), :]
    s16 = l8_ref[HALO:n, :] + l8_ref[HALO - 8:n - 8, :]
    return _pick_window(l2_ref[HALO:n, :], l4_ref[HALO:n, :], l8_ref[HALO:n, :], s16)


def _leading_window_sums(src_ref, l2_ref, l4_ref, l8_ref, tm):
    n = HALO + tm
    l2_ref[0:n - 8, :] = src_ref[0:n - 8, :] + src_ref[pl.ds(1, n - 8), :]
    l4_ref[0:n - 16, :] = l2_ref[0:n - 16, :] + l2_ref[pl.ds(2, n - 16), :]
    l8_ref[0:n - 24, :] = l4_ref[0:n - 24, :] + l4_ref[pl.ds(4, n - 24), :]
    s16 = l8_ref[0:tm, :] + l8_ref[8:tm + 8, :]
    return _pick_window(l2_ref[0:tm, :], l4_ref[0:tm, :], l8_ref[0:tm, :], s16)


def _mixer_specs(tm, nt, seq):
    hb = tm // HALO

    def cur(c):
        return pl.BlockSpec((tm, c), lambda b, i: (b * nt + i, 0))

    def prev(c):
        return pl.BlockSpec((HALO, c), lambda b, i: (jnp.maximum((b * nt + i) * hb - 1, 0), 0))

    def nxt(c):
        last = (2 * seq) // HALO - 1
        return pl.BlockSpec((HALO, c), lambda b, i: (jnp.minimum((b * nt + i + 1) * hb, last), 0))

    def full(shape, layer=None):
        if layer is None:
            return pl.BlockSpec(shape, lambda b, i: tuple(0 for _ in shape))
        return pl.BlockSpec((None,) + shape, lambda b, i: (layer,) + tuple(0 for _ in shape))

    return cur, prev, nxt, full


_MIX_PARAM_SHAPES = [(32, D_A), (1, D_A), (1, D_A), (1, D_A), (D_A, D_A), (1, D_B), (1, D_B),
                     (N_HEADS_B * CHUNK, CHUNK), (CHUNK, D_B), (D_C, D_C), (1, D_C)]


def mixer_fwd(z, mp, layer, *, seq, name, tm=512, pieces=()):
    t = z.shape[0]
    tm = min(tm, seq)
    nt = seq // tm
    cur, prev, _, full = _mixer_specs(tm, nt, seq)

    def body(zc_ref, zp_ref, cw_ref, cb_ref, clg_ref, clb_ref, wpw_ref, slg_ref, slb_ref, ws_ref, bias_ref,
             wp_ref, ps_ref, o_ref, cv_ref, ys_ref, zs_ref, l2_ref, l4_ref, l8_ref):
        i = pl.program_id(1)
        has_prev = i > 0
        yp = zp_ref[:, 0:D_A].astype(F32) * _sigmoid(zp_ref[:, D_A:2 * D_A].astype(F32))
        ys_ref[0, 0:HALO, :] = jnp.where(has_prev, yp, 0.0)
        ys_ref[0, HALO:HALO + tm, :] = zc_ref[:, 0:D_A].astype(F32) * _sigmoid(zc_ref[:, D_A:2 * D_A].astype(F32))
        _shifted_copies(ys_ref, tm + HALO - SUB)
        acc = jnp.zeros((tm, D_A), F32) + cb_ref[...]
        for k in range(CONV_WIDTH):
            acc = acc + cw_ref[k:k + 1, :] * _tap(ys_ref, HALO - (CONV_WIDTH - 1) + k, tm)
        cv_ref[...] = acc
        ln, _, _ = _layer_norm_fwd(acc, clg_ref[...], clb_ref[...])
        sl = ln * _sigmoid(ln)
        o_ref[:, 0:D_A] = _dot(sl.astype(BF16), wpw_ref[...]).astype(BF16)
        gz = _gelu(zc_ref[:, 2 * D_A:2 * D_A + 2 * D_B].astype(F32))
        u = gz[:, :D_B]
        vl, _, _ = _layer_norm_fwd(gz[:, D_B:], slg_ref[...], slb_ref[...])
        wsm = _causal_rows(ws_ref[...])
        masks = _head_masks((CHUNK, D_B))
        for c in range(tm // CHUNK):
            rows = slice(c * CHUNK, (c + 1) * CHUNK)
            s = _gate_mix(wsm, vl[rows], masks) + bias_ref[...]
            o_ref[rows, D_A:D_A + D_B] = (u[rows] * s).astype(BF16)
        c0 = 2 * D_A + 2 * D_B
        zs_ref[0:HALO, :] = jnp.where(has_prev, zp_ref[:, c0:c0 + D_C].astype(F32), 0.0)
        zcur = zc_ref[:, c0:c0 + D_C].astype(F32)
        zs_ref[HALO:HALO + tm, :] = zcur
        win = _pool_window((tm, D_C))
        wsum = _trailing_window_sums(zs_ref, l2_ref, l4_ref, l8_ref, tm)
        pos = i * tm + lax.broadcasted_iota(jnp.int32, (tm, D_C), 0)
        cnt = jnp.minimum(pos + 1, win).astype(F32)
        p = wsum / cnt - zcur
        y = _dot(p.astype(BF16), wp_ref[...])
        o_ref[:, D_A + D_B:D_A + D_B + D_C] = (y * ps_ref[...]).astype(BF16)

    in_specs = [cur(D_IN), prev(D_IN)] + [full(s, layer) for s in _MIX_PARAM_SHAPES]
    outs, couts = _call(
        body, name=name, grid=(2, nt), in_specs=in_specs, out_specs=[cur(D_MODEL), cur(D_A)],
        out_shape=[SDS((t, D_MODEL), BF16), SDS((t, D_A), F32)],
        scratch_shapes=[pltpu.VMEM((SUB, HALO + tm, D_A), F32)] + [pltpu.VMEM((HALO + tm, D_C), F32)] * 4,
        args=[z, z, *mp], pieces=pieces)
    return outs, couts


def mixer_bwd(z, conv, dm, mp, layer, *, seq, name, tm=256, pieces=()):
    t = z.shape[0]
    tm = min(tm, seq)
    nt = seq // tm
    ext = tm + HALO
    cur, prev, nxt, full = _mixer_specs(tm, nt, seq)
    grad_shapes = [(32, D_A), (1, D_A), (1, D_A), (1, D_A), (D_A, D_A), (1, D_B), (1, D_B),
                   (N_HEADS_B * CHUNK, CHUNK), (CHUNK, CHUNK), (D_C, D_C), (1, D_C)]

    def body(zc_ref, zp_ref, cvc_ref, cvn_ref, dmc_ref, dmn_ref, cw_ref, cb_ref, clg_ref, clb_ref, wpw_ref, slg_ref,
             slb_ref, ws_ref, bias_ref, wp_ref, ps_ref,
             dz_ref, dcw_ref, dcb_ref, dclg_ref, dclb_ref, dwpw_ref, dslg_ref, dslb_ref, dws_ref, dbs_ref, dwp_ref,
             dps_ref, ys_ref, dcs_ref, zs_ref, qs_ref, l2_ref, l4_ref, l8_ref):
        b = pl.program_id(0)
        i = pl.program_id(1)
        has_prev = i > 0
        has_next = i < nt - 1
        grads = [dcw_ref, dcb_ref, dclg_ref, dclb_ref, dwpw_ref, dslg_ref, dslb_ref, dws_ref, dbs_ref, dwp_ref, dps_ref]

        @pl.when((b == 0) & (i == 0))
        def _():
            for r in grads:
                r[...] = jnp.zeros_like(r)

        ext_row = lax.broadcasted_iota(jnp.int32, (ext, 1), 0)
        live = (ext_row < tm) | has_next

        yp = zp_ref[:, 0:D_A].astype(F32) * _sigmoid(zp_ref[:, D_A:2 * D_A].astype(F32))
        ys_ref[0, 0:HALO, :] = jnp.where(has_prev, yp, 0.0)
        a_cur = zc_ref[:, 0:D_A].astype(F32)
        sig_cur = _sigmoid(zc_ref[:, D_A:2 * D_A].astype(F32))
        ys_ref[0, HALO:HALO + tm, :] = a_cur * sig_cur
        _shifted_copies(ys_ref, tm + HALO - SUB)
        acc = jnp.concatenate([cvc_ref[...], cvn_ref[...]], axis=0)
        ln, xhat, rstd = _layer_norm_fwd(acc, clg_ref[...], clb_ref[...])
        sg = _sigmoid(ln)
        sl = ln * sg
        dya = jnp.concatenate([dmc_ref[:, 0:D_A], dmn_ref[:, 0:D_A]], axis=0)
        dsl = _dot_nt(dya, wpw_ref[...])
        dln = dsl * sg * (1.0 + ln * (1.0 - sg))
        dc = _layer_norm_bwd(dln, xhat, rstd, clg_ref[...])
        dc = jnp.where(live, dc, 0.0)
        dcs_ref[0] = dc
        _shifted_copies(dcs_ref, ext - SUB)
        dwpw_ref[...] += _dot_tn(sl[:tm].astype(BF16), dya[:tm])
        dclg_ref[...] += jnp.sum(dln[:tm] * xhat[:tm], axis=0, keepdims=True)
        dclb_ref[...] += jnp.sum(dln[:tm], axis=0, keepdims=True)
        dcb_ref[...] += jnp.sum(dc[:tm], axis=0, keepdims=True)
        for k in range(CONV_WIDTH):
            off = HALO - (CONV_WIDTH - 1) + k
            dcw_ref[k:k + 1, :] += jnp.sum(dc[:tm] * _tap(ys_ref, off, tm), axis=0, keepdims=True)
        for r0 in range(0, tm, CONV_ROWS):
            rows = slice(r0, r0 + CONV_ROWS)
            dy = jnp.zeros((CONV_ROWS, D_A), F32)
            for k in range(CONV_WIDTH):
                dy = dy + cw_ref[k:k + 1, :] * _tap(dcs_ref, CONV_WIDTH - 1 - k + r0, CONV_ROWS)
            dz_ref[rows, 0:D_A] = (dy * sig_cur[rows]).astype(BF16)
            dz_ref[rows, D_A:2 * D_A] = (dy * a_cur[rows] * sig_cur[rows] * (1.0 - sig_cur[rows])).astype(BF16)

        zb = zc_ref[:, 2 * D_A:2 * D_A + 2 * D_B].astype(F32)
        gz = _gelu(zb)
        u = gz[:, :D_B]
        vl, vhat, vrstd = _layer_norm_fwd(gz[:, D_B:], slg_ref[...], slb_ref[...])
        dyb = dmc_ref[:, D_A:D_A + D_B].astype(F32)
        wsm = _causal_rows(ws_ref[...])
        masks = _head_masks((CHUNK, D_B))
        ds_all = dyb * u
        du_parts, dvl_parts = [], []
        for c in range(tm // CHUNK):
            rows = slice(c * CHUNK, (c + 1) * CHUNK)
            vlc = vl[rows].astype(BF16)
            s = _gate_mix(wsm, vl[rows], masks) + bias_ref[...]
            du_parts.append(dyb[rows] * s)
            ds = ds_all[rows]
            stack = jnp.concatenate([jnp.where(masks[h], ds, 0.0) for h in range(N_HEADS_B)], axis=0).astype(BF16)
            dvl_parts.append(_dot_tn(wsm, stack))
            dws_ref[...] += _dot_nt(stack, vlc)
        du = jnp.concatenate(du_parts, axis=0)
        dvl = jnp.concatenate(dvl_parts, axis=0)
        dbias = jnp.zeros((CHUNK, D_B), F32)
        for c in range(tm // CHUNK):
            dbias = dbias + ds_all[c * CHUNK:(c + 1) * CHUNK]
        lane = lax.broadcasted_iota(jnp.int32, (CHUNK, CHUNK), 1)
        dbs = jnp.zeros((CHUNK, CHUNK), F32)
        for h in range(N_HEADS_B):
            col = jnp.sum(jnp.where(masks[h], dbias, 0.0), axis=1, keepdims=True)
            dbs = dbs + jnp.where(lane == h, col, 0.0)
        dbs_ref[...] += dbs
        dslg_ref[...] += jnp.sum(dvl * vhat, axis=0, keepdims=True)
        dslb_ref[...] += jnp.sum(dvl, axis=0, keepdims=True)
        dv = _layer_norm_bwd(dvl, vhat, vrstd, slg_ref[...])
        gg = _gelu_grad(zb)
        dz_ref[:, 2 * D_A:2 * D_A + D_B] = (du * gg[:, :D_B]).astype(BF16)
        dz_ref[:, 2 * D_A + D_B:2 * D_A + 2 * D_B] = (dv * gg[:, D_B:]).astype(BF16)

        c0 = 2 * D_A + 2 * D_B
        m0 = D_A + D_B
        zs_ref[0:HALO, :] = jnp.where(has_prev, zp_ref[:, c0:c0 + D_C].astype(F32), 0.0)
        zcur = zc_ref[:, c0:c0 + D_C].astype(F32)
        zs_ref[HALO:HALO + tm, :] = zcur
        win = _pool_window((tm, D_C))
        wsum = _trailing_window_sums(zs_ref, l2_ref, l4_ref, l8_ref, tm)
        pos = i * tm + lax.broadcasted_iota(jnp.int32, (tm, D_C), 0)
        cnt = jnp.minimum(pos + 1, win).astype(F32)
        pb = (wsum / cnt - zcur).astype(BF16)
        y = _dot(pb, wp_ref[...])
        dyc = jnp.concatenate([dmc_ref[:, m0:m0 + D_C], dmn_ref[:, m0:m0 + D_C]], axis=0).astype(F32)
        dps_ref[...] += jnp.sum(dyc[:tm] * y, axis=0, keepdims=True)
        dyv = (dyc * ps_ref[...]).astype(BF16)
        dwp_ref[...] += _dot_tn(pb, dyv[:tm])
        dp = _dot_nt(dyv, wp_ref[...])
        win_e = _pool_window((ext, D_C))
        pos_e = i * tm + lax.broadcasted_iota(jnp.int32, (ext, D_C), 0)
        cnt_e = jnp.minimum(pos_e + 1, win_e).astype(F32)
        qs_ref[...] = jnp.where(live, dp / cnt_e, 0.0)
        dzc = _leading_window_sums(qs_ref, l2_ref, l4_ref, l8_ref, tm) - dp[:tm]
        dz_ref[:, c0:c0 + D_C] = dzc.astype(BF16)

        @pl.when((b == 1) & (i == nt - 1))
        def _():
            dws_ref[...] = _causal_rows(dws_ref[...])

    in_specs = ([cur(D_IN), prev(D_IN), cur(D_A), nxt(D_A), cur(D_MODEL), nxt(D_MODEL)]
                + [full(s, layer) for s in _MIX_PARAM_SHAPES])
    out_specs = [cur(D_IN)] + [full(s) for s in grad_shapes]
    out_shape = [SDS((t, D_IN), BF16)] + [SDS(s, F32) for s in grad_shapes]
    outs, couts = _call(
        body, name=name, grid=(2, nt), in_specs=in_specs, out_specs=out_specs, out_shape=out_shape,
        scratch_shapes=[pltpu.VMEM((SUB, HALO + tm, D_A), F32), pltpu.VMEM((SUB, ext, D_A), F32)]
        + [pltpu.VMEM((ext, D_C), F32)] * 5,
        args=[z, z, conv, conv, dm, dm, *mp], pieces=pieces)
    return outs, couts


N_GROUPS_C = D_C // GROUP_DIM_C


def _mixer_params(p, w_pw_bf16):
    eye = jnp.eye(N_GROUPS_C, dtype=F32)
    wp_bd = jnp.einsum("lgio,gh->lgiho", p["w_pool"], eye).reshape(DEPTH, D_C, D_C)
    return [
        jnp.pad(p["conv_w"], ((0, 0), (0, 32 - CONV_WIDTH), (0, 0))),
        p["conv_b"][:, None], p["conv_ln_g"][:, None], p["conv_ln_b"][:, None],
        w_pw_bf16,
        p["sg_ln_g"][:, None], p["sg_ln_b"][:, None],
        p["w_s"].reshape(DEPTH, N_HEADS_B * CHUNK, CHUNK).astype(BF16),
        jnp.repeat(p["b_s"].transpose(0, 2, 1), HEAD_DIM_B, axis=2),
        wp_bd.astype(BF16),
        p["pool_scale"][:, None],
    ]


RAW_SMALL = [("conv_w", (32, D_A)), ("conv_b", (1, D_A)), ("conv_ln_g", (1, D_A)), ("conv_ln_b", (1, D_A)),
             ("sg_ln_g", (1, D_B)), ("sg_ln_b", (1, D_B)), ("w_s", (N_HEADS_B * CHUNK, CHUNK)), ("b_s", (CHUNK, CHUNK)),
             ("w_pool", (D_C, D_C)), ("pool_scale", (1, D_C)), ("norm1_g", (1, D_MODEL)), ("norm2_g", (1, D_MODEL))]


def _small_grads(flat):
    per_layer = sum(math.prod(s) for _, s in RAW_SMALL)
    layers = flat[:DEPTH * per_layer].reshape(DEPTH, per_layer)
    raw, off = {}, 0
    for k, s in RAW_SMALL:
        raw[k] = layers[:, off:off + math.prod(s)].reshape((DEPTH,) + s)
        off += math.prod(s)
    eye = jnp.eye(N_GROUPS_C, dtype=F32)
    g = {k: raw[k][:, 0] for k in ["conv_b", "conv_ln_g", "conv_ln_b", "sg_ln_g", "sg_ln_b", "pool_scale", "norm1_g",
                                  "norm2_g"]}
    g["conv_w"] = raw["conv_w"][:, :CONV_WIDTH]
    g["w_s"] = raw["w_s"].reshape(DEPTH, N_HEADS_B, CHUNK, CHUNK)
    g["b_s"] = raw["b_s"][:, :, :N_HEADS_B].transpose(0, 2, 1)
    blocks = raw["w_pool"].reshape(DEPTH, N_GROUPS_C, GROUP_DIM_C, N_GROUPS_C, GROUP_DIM_C)
    g["w_pool"] = jnp.sum(blocks * eye[None, :, None, :, None], axis=3)
    end = DEPTH * per_layer
    g["final_g"] = flat[end:end + D_MODEL]
    return g, flat[end + D_MODEL]


def _row_tile(r, cap=512):
    best = r
    for d in range(16, min(r, cap) + 1, 16):
        if r % d == 0:
            best = d
    return best if best <= cap else r


def add_core_halves(g, r1, core, *, name):
    _, _, r, c = g.shape
    tr = _row_tile(r)

    def body(core_ref, g_ref, r_ref, o_ref):
        o_ref[...] = (g_ref[...].astype(F32) + r_ref[...].astype(F32)).astype(o_ref.dtype)

    outs, _ = _call(
        body, name=name, grid=(N_CHIPS, r // tr), prefetch=1,
        in_specs=[pl.BlockSpec((None, None, tr, c), lambda j, i, s: (j, s[0], i, 0)),
                  pl.BlockSpec((None, tr, c), lambda j, i, s: (j, i, 0))],
        out_specs=[pl.BlockSpec((None, tr, c), lambda j, i, s: (j, i, 0))],
        out_shape=[SDS((N_CHIPS, r, c), g.dtype)], args=[core, g, r1])
    return outs[0]


def sum_chips(h, r2, place, *, name):
    _, r, c = h.shape
    tr = _row_tile(r, 256)

    def body(place_ref, h_ref, a_ref, b_ref, c_ref, o_ref):
        acc = h_ref[...].astype(F32) + a_ref[...].astype(F32)
        acc = acc + b_ref[...].astype(F32)
        o_ref[...] = acc + c_ref[...].astype(F32)

    def blk(k):
        return pl.BlockSpec((None, tr, c), lambda i, s: (jnp.bitwise_xor(s[0], k), i, 0))

    outs, _ = _call(
        body, name=name, grid=(r // tr,), prefetch=1, in_specs=[blk(0), blk(1), blk(2), blk(3)],
        out_specs=[pl.BlockSpec((None, tr, c), lambda i, s: (s[1], i, 0))],
        out_shape=[SDS((2, r, c), F32)], args=[place, h, r2, r2, r2])
    return outs[0]


def allreduce_small(p, *, name, pieces=()):
    _, n, _ = p.shape

    def body(p_ref, o_ref, land_ref, send1, recv1, send2, recv2):
        x, y, c = lax.axis_index("x"), lax.axis_index("y"), lax.axis_index("c")
        me = 4 * x + 2 * y + c

        def peer(r):
            return ((1 - x) if r & 4 else x, (1 - y) if r & 2 else y, (1 - c) if r & 1 else c)

        def index(r):
            px, py, pc = peer(r)
            return 4 * px + 2 * py + pc

        land_ref[me] = p_ref[me]
        first = [_remote(p_ref.at[index(r)], land_ref.at[me], send1.at[r - 1], recv1.at[r - 1], peer(r))
                 for r in range(1, N_DEV)]
        for cp in first:
            cp.start()
        for r in range(1, N_DEV):
            blk = land_ref.at[index(r)]
            _remote(blk, blk, send1.at[r - 1], recv1.at[r - 1], peer(r)).wait_recv()
        acc = land_ref[0]
        for d in range(1, N_DEV):
            acc = acc + land_ref[d]
        o_ref[me] = acc
        second = [_remote(o_ref.at[me], o_ref.at[me], send2.at[r - 1], recv2.at[r - 1], peer(r))
                  for r in range(1, N_DEV)]
        for cp in second:
            cp.start()
        for r in range(1, N_DEV):
            blk = o_ref.at[index(r)]
            _remote(blk, blk, send2.at[r - 1], recv2.at[r - 1], peer(r)).wait_recv()
        for cp in first + second:
            cp.wait_send()

    vm = pl.BlockSpec(memory_space=pltpu.VMEM)
    outs, couts = _call(
        body, name=name, grid=(), in_specs=[vm], out_specs=[vm], out_shape=[SDS(p.shape, F32)],
        scratch_shapes=[pltpu.VMEM(p.shape, F32)] + [pltpu.SemaphoreType.DMA((N_DEV - 1,))] * 4,
        args=[p], pieces=pieces)
    return outs[0], couts


def _adam_update(w, g, m, v):
    m_new = ADAM_B1 * m + (1.0 - ADAM_B1) * g
    v_new = ADAM_B2 * v + (1.0 - ADAM_B2) * (g * g)
    m_hat = m_new / (1.0 - ADAM_B1 ** ADAM_STEP)
    v_hat = v_new / (1.0 - ADAM_B2 ** ADAM_STEP)
    return -ADAM_LR * (m_hat / (jnp.sqrt(v_hat) + ADAM_EPS) + ADAM_WD * w), m_new, v_new


def adamw(w, g, m, v, *, name, pieces=()):
    nl, r, c = w.shape
    tr = _row_tile(r, 512)

    def body(w_ref, g_ref, m_ref, v_ref, d_ref, mo_ref, vo_ref):
        d_ref[...], mo_ref[...], vo_ref[...] = _adam_update(w_ref[...], g_ref[...], m_ref[...], v_ref[...])

    blk = pl.BlockSpec((None, tr, c), lambda l, i: (l, i, 0))
    return _call(body, name=name, grid=(nl, r // tr), in_specs=[blk] * 4, out_specs=[blk] * 3,
                 out_shape=[SDS(w.shape, F32)] * 3, args=[w, g, m, v], pieces=pieces)


def adamw_small(ws, gs, ms, vs, *, name, pieces=()):
    n = len(ws)

    def body(*refs):
        for i in range(n):
            w_ref, g_ref, m_ref, v_ref = (refs[k * n + i] for k in range(4))
            d, mn, vn = _adam_update(w_ref[...], g_ref[...], m_ref[...], v_ref[...])
            refs[4 * n + i][...] = d
            refs[5 * n + i][...] = mn
            refs[6 * n + i][...] = vn

    vm = pl.BlockSpec(memory_space=pltpu.VMEM)
    res, couts = _call(body, name=name, grid=(), in_specs=[vm] * (4 * n), out_specs=[vm] * (3 * n),
                       out_shape=[SDS(w.shape, F32) for w in ws] * 3, args=[*ws, *gs, *ms, *vs], pieces=pieces)
    return (res[:n], res[n:2 * n], res[2 * n:]), couts


WEIGHTS = ["norm1_g", "w_in", "conv_w", "conv_b", "conv_ln_g", "conv_ln_b", "w_pw", "sg_ln_g", "sg_ln_b", "w_s",
           "b_s", "w_pool", "pool_scale", "w_out", "norm2_g", "w_gate_up", "w_down", "final_g"]
BIG = ["w_in", "w_pw", "w_out", "w_gate_up", "w_down"]
TRANSPOSED = {"w_in": True, "w_pw": False, "w_out": False, "w_gate_up": True, "w_down": False}
SMALL = [k for k in WEIGHTS if k not in BIG]


def _wire(a, transposed):
    if transposed:
        a = a.transpose(0, 2, 1)
    return [a[l].reshape(2, a.shape[1] // 2, a.shape[2]) for l in range(a.shape[0])]


def _pack(arrays):
    flat = jnp.concatenate([a.reshape(-1) for a in arrays])
    n = -(-flat.shape[0] // (N_DEV * LANES * 8)) * 8
    return jnp.pad(flat, (0, N_DEV * n * LANES - flat.shape[0])).reshape(N_DEV, n, LANES)


def kernel(x, norm1_g, w_in, conv_w, conv_b, conv_ln_g, conv_ln_b, w_pw, sg_ln_g, sg_ln_b, w_s, b_s, w_pool, pool_scale, w_out, norm2_g, w_gate_up, w_down, final_g, loss_target, m_norm1_g, m_w_in, m_conv_w, m_conv_b, m_conv_ln_g, m_conv_ln_b, m_w_pw, m_sg_ln_g, m_sg_ln_b, m_w_s, m_b_s, m_w_pool, m_pool_scale, m_w_out, m_norm2_g, m_w_gate_up, m_w_down, m_final_g, v_norm1_g, v_w_in, v_conv_w, v_conv_b, v_conv_ln_g, v_conv_ln_b, v_w_pw, v_sg_ln_g, v_sg_ln_b, v_w_s, v_b_s, v_w_pool, v_pool_scale, v_w_out, v_norm2_g, v_w_gate_up, v_w_down, v_final_g):
    w = dict(norm1_g=norm1_g, w_in=w_in, conv_w=conv_w, conv_b=conv_b, conv_ln_g=conv_ln_g, conv_ln_b=conv_ln_b,
             w_pw=w_pw, sg_ln_g=sg_ln_g, sg_ln_b=sg_ln_b, w_s=w_s, b_s=b_s, w_pool=w_pool, pool_scale=pool_scale,
             w_out=w_out, norm2_g=norm2_g, w_gate_up=w_gate_up, w_down=w_down, final_g=final_g)
    m = dict(norm1_g=m_norm1_g, w_in=m_w_in, conv_w=m_conv_w, conv_b=m_conv_b, conv_ln_g=m_conv_ln_g,
             conv_ln_b=m_conv_ln_b, w_pw=m_w_pw, sg_ln_g=m_sg_ln_g, sg_ln_b=m_sg_ln_b, w_s=m_w_s, b_s=m_b_s,
             w_pool=m_w_pool, pool_scale=m_pool_scale, w_out=m_w_out, norm2_g=m_norm2_g, w_gate_up=m_w_gate_up,
             w_down=m_w_down, final_g=m_final_g)
    v = dict(norm1_g=v_norm1_g, w_in=v_w_in, conv_w=v_conv_w, conv_b=v_conv_b, conv_ln_g=v_conv_ln_g,
             conv_ln_b=v_conv_ln_b, w_pw=v_w_pw, sg_ln_g=v_sg_ln_g, sg_ln_b=v_sg_ln_b, w_s=v_w_s, b_s=v_b_s,
             w_pool=v_w_pool, pool_scale=v_pool_scale, w_out=v_w_out, norm2_g=v_norm2_g, w_gate_up=v_w_gate_up,
             w_down=v_w_down, final_g=v_final_g)
    bsz, seq, d = x.shape
    t = bsz * seq
    chip = 2 * lax.axis_index("x") + lax.axis_index("y")
    core = lax.axis_index("c")
    core_arr = jnp.reshape(core, (1,)).astype(jnp.int32)
    place_arr = jnp.stack([chip, core]).astype(jnp.int32)

    own = {k: _wire(w[k].astype(BF16), TRANSPOSED[k]) for k in BIG}
    cw_cols = conv_w.shape[2]
    own["side"] = [jnp.pad(conv_w, ((0, 0), (0, 32 - CONV_WIDTH), (0, 0)))]
    parts = {"w_in": 1, "w_pw": 1, "w_out": 1, "w_gate_up": 4, "w_down": 2, "side": 1}
    plan = {
        "norm1_fwd_0": [("ici", "w_in", 0, 0), ("ici", "w_pw", 0, 0), ("ici", "w_pw", 1, 0), ("ici", "side", 0, 0)],
        "gather_first": [("d2d", "w_in", 0, 0), ("d2d", "w_pw", 0, 0), ("d2d", "w_pw", 1, 0), ("d2d", "side", 0, 0)],
        "in_proj_0": [("ici", "w_out", 0, 0), ("ici", "w_gate_up", 0, 0)],
        "mixer_fwd_0": [("d2d", "w_out", 0, 0), ("d2d", "w_gate_up", 0, 0), ("ici", "w_gate_up", 0, 1),
                        ("ici", "w_gate_up", 0, 2)],
        "out_proj_0": [("d2d", "w_gate_up", 0, 1), ("d2d", "w_gate_up", 0, 2), ("both", "w_gate_up", 0, 3)],
        "swiglu_fwd_0": [("both", "w_down", 0, 0), ("both", "w_down", 0, 1), ("ici", "w_in", 1, 0)],
        "down_proj_0": [("d2d", "w_in", 1, 0), ("ici", "w_out", 1, 0), ("ici", "w_gate_up", 1, 0)],
        "in_proj_1": [("d2d", "w_out", 1, 0), ("d2d", "w_gate_up", 1, 0), ("ici", "w_gate_up", 1, 1)],
        "mixer_fwd_1": [("d2d", "w_gate_up", 1, 1), ("ici", "w_gate_up", 1, 2), ("ici", "w_gate_up", 1, 3)],
        "out_proj_1": [("d2d", "w_gate_up", 1, 2), ("d2d", "w_gate_up", 1, 3)],
        "swiglu_fwd_1": [("both", "w_down", 1, 0), ("both", "w_down", 1, 1)],
    }
    bufs = {}

    def grouped(name):
        groups = {}
        for stage, k, l, q in plan.get(name, []):
            groups.setdefault((k, l), []).append((stage, q))
        return groups

    def riding(name):
        return [gather_stages(own[k][l], bufs.get((k, l)), stages, parts[k])
                for (k, l), stages in grouped(name).items()]

    def landed(name, couts):
        for key, co in zip(grouped(name), couts):
            bufs[key] = co[0]

    def whole(k, l):
        g = lax.dynamic_update_index_in_dim(bufs[k, l], own[k][l], chip, 0)
        return g.reshape(-1, g.shape[-1])

    xs = [x.reshape(t, d)]
    saved = []
    full = {}
    p = dict(w)
    for l in range(DEPTH):
        x0 = xs[-1]
        name = f"in_proj_{l}"
        if l == 0:
            h1, co = rmsnorm_fwd(x0, w["norm1_g"][l][None], name="norm1_fwd_0", pieces=riding("norm1_fwd_0"))
            landed("norm1_fwd_0", co)
            landed("gather_first", comm_only(riding("gather_first"), name="gather_first"))
            side_all = lax.dynamic_update_index_in_dim(bufs["side", 0], own["side"][0], chip, 0)
            p["conv_w"] = side_all[:, :, :CONV_WIDTH, :].transpose(1, 2, 0, 3).reshape(
                DEPTH, CONV_WIDTH, N_CHIPS * cw_cols)
            full["w_in", l] = whole("w_in", l)
            z, co = mm_nt(h1, full["w_in", l], name=name, tm=512, tn=D_IN, out_dtype=BF16, pieces=riding(name))
        else:
            full["w_in", l] = whole("w_in", l)
            (z, h1), co = norm_mm_nt(x0, w["norm1_g"][l][None], full["w_in", l], name=name, tm=512, out_dtype=BF16,
                                     pieces=riding(name))
        landed(name, co)
        if l == 0:
            mp = _mixer_params(p, jnp.stack([whole("w_pw", 0), whole("w_pw", 1)]))
        name = f"mixer_fwd_{l}"
        (mc, cv), co = mixer_fwd(z, mp, l, seq=seq, name=name, pieces=riding(name))
        landed(name, co)
        full["w_out", l] = whole("w_out", l)
        name = f"out_proj_{l}"
        x1, co = mm_nn(mc, full["w_out", l], name=name, tm=512, tn=D_MODEL, out_dtype=F32, residual=x0,
                       pieces=riding(name))
        landed(name, co)
        full["w_gate_up", l] = whole("w_gate_up", l)
        name = f"swiglu_fwd_{l}"
        (gu, act, h2), co = swiglu_fwd(x1, w["norm2_g"][l][None], full["w_gate_up", l], name=name, tm=512, tn=1408,
                                       pieces=riding(name))
        landed(name, co)
        full["w_down", l] = whole("w_down", l)
        name = f"down_proj_{l}"
        x2, co = mm_nn(act, full["w_down", l], name=name, tm=512, tn=D_MODEL, out_dtype=F32, residual=x1,
                       pieces=riding(name))
        landed(name, co)
        saved.append((x0, h1, z, cv, mc, x1, h2, gu, act, mp))
        xs.append(x2)

    loss, dx, dxb, d_final_g = loss_head(xs[-1], p["final_g"][None], loss_target.reshape(t, d), name="loss_head")

    raw_small = [None] * DEPTH
    reduced = {}
    carry = None

    def halves(g):
        return g.reshape(N_CHIPS, 2, g.shape[0] // (2 * N_CHIPS), g.shape[1])

    for l in reversed(range(DEPTH)):
        x0, h1, z, cv, mc, x1, h2, gu, act, mp = saved[l]
        pieces = [exchange_d2d(carry[0]), reduce_ici(carry[1])] if carry else []
        g_down, co = mm_tn(act, dxb, name=f"down_proj_dw_{l}", tk=D_FF // 2, tn=D_MODEL, tt=TT, out_dtype=BF16,
                           pieces=pieces)
        g_down = halves(g_down)
        pieces = [reduce_d2d(g_down)]
        if carry:
            reduced["w_pw", l + 1] = co[0][0]
            s_in = sum_chips(carry[1], co[1][0], place_arr, name=f"sum_chips_w_in_{l + 1}")
            pieces.append(exchange_d2d(s_in))
        dgu, co = swiglu_bwd(dxb, full["w_down", l], gu, name=f"swiglu_bwd_{l}", tm=512, tn=1408, pieces=pieces)
        if carry:
            reduced["w_in", l + 1] = co[1][0]
        h_down = add_core_halves(g_down, co[0][0], core_arr, name=f"add_cores_w_down_{l}")
        g_gu, (r2,) = mm_gu_tn(dgu, h2, name=f"gate_up_dw_{l}", tn=1408, tt=TT, pieces=[reduce_ici(h_down)])
        s_down = sum_chips(h_down, r2[0], place_arr, name=f"sum_chips_w_down_{l}")
        g_gu = halves(g_gu)
        (dx, dxb, dn2), (e, r1) = mm_norm_bwd(dgu, full["w_gate_up", l], x1, p["norm2_g"][l][None], dx,
                                              name=f"gate_up_dx_{l}", tm=256,
                                              pieces=[exchange_d2d(s_down), reduce_d2d(g_gu)])
        reduced["w_down", l] = e[0]
        h_gu = add_core_halves(g_gu, r1[0], core_arr, name=f"add_cores_w_gate_up_{l}")
        g_out, _ = mm_tn(mc, dxb, name=f"out_proj_dw_{l}", tk=D_MODEL, tn=D_MODEL, tt=TT, out_dtype=BF16)
        g_out = halves(g_out)
        dmc, (r1,) = mm_nt(dxb, full["w_out", l], name=f"out_proj_dx_{l}", tm=512, tn=D_MODEL, out_dtype=BF16,
                           pieces=[reduce_d2d(g_out)])
        h_out = add_core_halves(g_out, r1[0], core_arr, name=f"add_cores_w_out_{l}")
        (dz, *mg), (r2a, r2b) = mixer_bwd(z, cv, dmc, mp, l, seq=seq, name=f"mixer_bwd_{l}",
                                          pieces=[reduce_ici(h_gu), reduce_ici(h_out)])
        s_gu = sum_chips(h_gu, r2a[0], place_arr, name=f"sum_chips_w_gate_up_{l}")
        s_out = sum_chips(h_out, r2b[0], place_arr, name=f"sum_chips_w_out_{l}")
        dcw, dcb, dclg, dclb, dwpw, dslg, dslb, dws, dbs, dwp, dps = mg
        g_pw = halves(dwpw.astype(BF16))
        g_in, (ea, eb, r1) = mm_tn(dz, h1, name=f"in_proj_dw_{l}", tk=D_IN // 2, tn=D_MODEL, tt=TT, out_dtype=BF16,
                                   pieces=[exchange_d2d(s_gu), exchange_d2d(s_out), reduce_d2d(g_pw)])
        reduced["w_gate_up", l], reduced["w_out", l] = ea[0], eb[0]
        h_pw = add_core_halves(g_pw, r1[0], core_arr, name=f"add_cores_w_pw_{l}")
        g_in = halves(g_in)
        (dx, dxb, dn1), (r2, r1) = mm_norm_bwd(dz, full["w_in", l], x0, p["norm1_g"][l][None], dx,
                                               name=f"in_proj_dx_{l}", tm=512,
                                               pieces=[reduce_ici(h_pw), reduce_d2d(g_in)])
        raw_small[l] = [dcw, dcb, dclg, dclb, dslg, dslb, dws, dbs, dwp, dps, dn1, dn2]
        s_pw = sum_chips(h_pw, r2[0], place_arr, name=f"sum_chips_w_pw_{l}")
        h_in = add_core_halves(g_in, r1[0], core_arr, name=f"add_cores_w_in_{l}")
        carry = (s_pw, h_in)
    grad_x = dx.reshape(bsz, seq, d)

    s_pw, h_in = carry
    summed, co = allreduce_small(_pack(raw_small[0] + raw_small[1] + [d_final_g, loss]), name="allreduce_small",
                                 pieces=[exchange_d2d(s_pw), reduce_ici(h_in)])
    reduced["w_pw", 0] = co[0][0]
    s_in = sum_chips(h_in, co[1][0], place_arr, name="sum_chips_w_in_0")
    grad, loss = _small_grads(summed.reshape(-1))
    grad["conv_w"] = lax.dynamic_slice_in_dim(grad["conv_w"], chip * cw_cols, cw_cols, axis=2)

    delta, new_m, new_v = {}, {}, {}

    def flat2(a):
        return a.reshape(-1, a.shape[-1])

    res, co = adamw_small(*[[flat2(tt[k]) for k in SMALL] for tt in (w, grad, m, v)], name="adamw_small",
                          pieces=[exchange_d2d(s_in)])
    reduced["w_in", 0] = co[0][0]
    for out, arrs in zip((delta, new_m, new_v), res):
        out.update({k: a.reshape(w[k].shape) for k, a in zip(SMALL, arrs)})
    for k in BIG:
        g = jnp.stack([reduced[k, l].reshape(-1, reduced[k, l].shape[-1]) for l in range(DEPTH)])
        grad[k] = g.transpose(0, 2, 1) if TRANSPOSED[k] else g
        (delta[k], new_m[k], new_v[k]), _ = adamw(w[k], grad[k], m[k], v[k], name=f"adamw_{k}")
    return (loss, grad_x, *[grad[k] for k in WEIGHTS], *[delta[k] for k in WEIGHTS],
            *[new_m[k] for k in WEIGHTS], *[new_v[k] for k in WEIGHTS])
```

```python
import functools
import math

import jax
import jax.numpy as jnp
from jax import lax
from jax.experimental import pallas as pl
from jax.experimental.pallas import tpu as pltpu

F32 = jnp.float32
BF16 = jnp.bfloat16

D_MODEL = 1024
DEPTH = 2
D_A = 384
D_B = 384
D_C = 256
D_IN = 2 * D_A + 2 * D_B + D_C
N_HEADS_B = 4
HEAD_DIM_B = 96
GROUP_DIM_C = 64
CONV_WIDTH = 31
CHUNK = 128
D_FF = 2816
RMS_EPS = 1e-6
LN_EPS = 1e-5
HALO = 32
TT = 2048
N_CHIPS = 4
N_DEV = 8
LANES = 128

ADAM_LR = 0.001
ADAM_B1 = 0.9
ADAM_B2 = 0.999
ADAM_EPS = 1e-08
ADAM_WD = 0.01
ADAM_STEP = 10

VMEM_LIMIT = 56 * 1024 * 1024
_INTERPRET = False

MESH = pl.DeviceIdType.MESH
ANY = pl.BlockSpec(memory_space=pl.ANY)
SDS = jax.ShapeDtypeStruct


def _sigmoid(x):
    return 0.5 * jnp.tanh(0.5 * x) + 0.5


_GELU_C = math.sqrt(2.0 / math.pi)


def _gelu(x):
    return 0.5 * x * (1.0 + jnp.tanh(_GELU_C * (x + 0.044715 * x * x * x)))


def _gelu_grad(x):
    t = jnp.tanh(_GELU_C * (x + 0.044715 * x * x * x))
    return 0.5 * (1.0 + t) + 0.5 * x * (1.0 - t * t) * _GELU_C * (1.0 + 3 * 0.044715 * x * x)


def _dot(a, b):
    return jnp.dot(a, b, preferred_element_type=F32)


def _dot_nt(a, b):
    return lax.dot_general(a, b, (((1,), (1,)), ((), ())), preferred_element_type=F32)


def _dot_tn(a, b):
    return lax.dot_general(a, b, (((0,), (0,)), ((), ())), preferred_element_type=F32)


class Piece:
    def __init__(self, operands, out_shapes, aliases, n_sems, start, finish):
        self.operands, self.out_shapes, self.aliases, self.n_sems = operands, out_shapes, aliases, n_sems
        self.start, self.finish = start, finish


def _place():
    x, y, c = lax.axis_index("x"), lax.axis_index("y"), lax.axis_index("c")
    chips = [(1 - x, y), (x, 1 - y), (1 - x, 1 - y)]
    return x, y, c, chips


def _remote(src, dst, send_sem, recv_sem, to):
    return pltpu.make_async_remote_copy(src_ref=src, dst_ref=dst, send_sem=send_sem, recv_sem=recv_sem,
                                        device_id=to, device_id_type=MESH)


def _rows(r, q, nq):
    return pl.ds(q * (r // nq), r // nq)


def gather_ici(src, g=None, q=0, nq=1):
    rows = _rows(src.shape[1], q, nq)

    def copies(ins, outs, sem):
        x, y, c, chips = _place()
        j = 2 * x + y
        return [(_remote(ins[0].at[c, rows], outs[0].at[j, c, rows], sem(k), sem(3 + k), (px, py, c)),
                 outs[0].at[2 * px + py, c, rows], (px, py, c)) for k, (px, py) in enumerate(chips)]

    def start(ins, outs, sem):
        for cp, _, _ in copies(ins, outs, sem):
            cp.start()

    def finish(ins, outs, sem):
        cps = copies(ins, outs, sem)
        for k, (_, landed, frm) in enumerate(cps):
            _remote(landed, landed, sem(k), sem(3 + k), frm).wait_recv()
        for cp, _, _ in cps:
            cp.wait_send()

    shape = SDS((N_CHIPS,) + src.shape, src.dtype)
    if g is None:
        return Piece([src], [shape], {}, 6, start, finish)
    return Piece([src, g], [shape], {1: 0}, 6, start, finish)


def gather_d2d(g, q=0, nq=1, sem0=0):
    rows = _rows(g.shape[2], q, nq)

    def copies(outs, sem):
        x, y, c, chips = _place()
        return [(_remote(outs[0].at[2 * px + py, c, rows], outs[0].at[2 * px + py, c, rows],
                         sem(sem0 + k), sem(sem0 + 3 + k), (x, y, 1 - c)),
                 outs[0].at[2 * px + py, 1 - c, rows], (x, y, 1 - c)) for k, (px, py) in enumerate(chips)]

    def start(ins, outs, sem):
        for cp, _, _ in copies(outs, sem):
            cp.start()

    def finish(ins, outs, sem):
        cps = copies(outs, sem)
        for k, (_, landed, frm) in enumerate(cps):
            _remote(landed, landed, sem(sem0 + k), sem(sem0 + 3 + k), frm).wait_recv()
        for cp, _, _ in cps:
            cp.wait_send()

    return Piece([g], [SDS(g.shape, g.dtype)], {0: 0}, 6, start, finish)


def gather_both(src, g=None, q=0, nq=1):
    a = gather_ici(src, g, q, nq)
    b = gather_d2d(jax.ShapeDtypeStruct((N_CHIPS,) + src.shape, src.dtype), q, nq, sem0=6)

    def finish(ins, outs, sem):
        a.finish(ins, outs, sem)
        b.start(ins, outs, sem)
        b.finish(ins, outs, sem)

    return Piece(a.operands, a.out_shapes, a.aliases, 12, a.start, finish)


def gather_stages(src, g, stages, nq):
    shape = SDS((N_CHIPS,) + src.shape, src.dtype)
    make = {"ici": lambda q: gather_ici(src, None, q, nq), "d2d": lambda q: gather_d2d(shape, q, nq),
            "both": lambda q: gather_both(src, None, q, nq)}
    subs, n_sems = [], 0
    for stage, q in stages:
        sub = make[stage](q)
        subs.append((sub, n_sems))
        n_sems += sub.n_sems

    def run(method):
        def go(ins, outs, sem):
            for sub, base in subs:
                getattr(sub, method)(ins, outs, functools.partial(lambda k, base: sem(base + k), base=base))
        return go

    if g is None:
        return Piece([src], [shape], {}, n_sems, run("start"), run("finish"))
    return Piece([src, g], [shape], {1: 0}, n_sems, run("start"), run("finish"))


def reduce_d2d(g):
    def copies(ins, outs, sem):
        x, y, c, _ = _place()
        return [_remote(ins[0].at[j, 1 - c], outs[0].at[j], sem(j), sem(4 + j), (x, y, 1 - c)) for j in range(N_CHIPS)]

    def start(ins, outs, sem):
        for cp in copies(ins, outs, sem):
            cp.start()

    def finish(ins, outs, sem):
        cps = copies(ins, outs, sem)
        for cp in cps:
            cp.wait_recv()
        for cp in cps:
            cp.wait_send()

    return Piece([g], [SDS((N_CHIPS,) + g.shape[2:], g.dtype)], {}, 8, start, finish)


def reduce_ici(h):
    def copies(ins, outs, sem):
        x, y, c, chips = _place()
        j = 2 * x + y
        return [(_remote(ins[0].at[2 * px + py], outs[0].at[j], sem(k), sem(3 + k), (px, py, c)),
                 outs[0].at[2 * px + py], (px, py, c)) for k, (px, py) in enumerate(chips)]

    def start(ins, outs, sem):
        for cp, _, _ in copies(ins, outs, sem):
            cp.start()

    def finish(ins, outs, sem):
        cps = copies(ins, outs, sem)
        for k, (_, landed, frm) in enumerate(cps):
            _remote(landed, landed, sem(k), sem(3 + k), frm).wait_recv()
        for cp, _, _ in cps:
            cp.wait_send()

    return Piece([h], [SDS(h.shape, h.dtype)], {}, 6, start, finish)


def exchange_d2d(g):
    def copy(outs, sem):
        x, y, c, _ = _place()
        return _remote(outs[0].at[c], outs[0].at[c], sem(0), sem(1), (x, y, 1 - c)), outs[0].at[1 - c], (x, y, 1 - c)

    def start(ins, outs, sem):
        copy(outs, sem)[0].start()

    def finish(ins, outs, sem):
        cp, landed, frm = copy(outs, sem)
        _remote(landed, landed, sem(0), sem(1), frm).wait_recv()
        cp.wait_send()

    return Piece([g], [SDS(g.shape, g.dtype)], {0: 0}, 2, start, finish)


def _call(body, *, name, grid, in_specs, out_specs, out_shape, args, scratch_shapes=(), pieces=(), prefetch=0,
          in_place=None):
    n_in, n_out, n_scr = len(in_specs), len(out_specs), len(scratch_shapes)
    c_ops = [a for p in pieces for a in p.operands]
    c_outs = [s for p in pieces for s in p.out_shapes]
    n_sems = sum(p.n_sems for p in pieces)
    aliases = {prefetch + i: o for i, o in (in_place or {}).items()}
    op_off, out_off = prefetch + n_in, n_out
    for p in pieces:
        for i, o in p.aliases.items():
            aliases[op_off + i] = out_off + o
        op_off += len(p.operands)
        out_off += len(p.out_shapes)

    def wrapped(*refs):
        pre, refs = refs[:prefetch], refs[prefetch:]
        ins, cin = refs[:n_in], refs[n_in:n_in + len(c_ops)]
        o0 = n_in + len(c_ops)
        outs, cout = refs[o0:o0 + n_out], refs[o0 + n_out:o0 + n_out + len(c_outs)]
        s0 = o0 + n_out + len(c_outs)
        scr = refs[s0:s0 + n_scr]

        def each(method):
            sems = refs[s0 + n_scr]
            i_off = o_off = s_off = 0
            for p in pieces:
                getattr(p, method)(cin[i_off:i_off + len(p.operands)], cout[o_off:o_off + len(p.out_shapes)],
                                   functools.partial(lambda k, base: sems.at[base + k], base=s_off))
                i_off, o_off, s_off = i_off + len(p.operands), o_off + len(p.out_shapes), s_off + p.n_sems

        if pieces and grid:
            ids = [pl.program_id(a) for a in range(len(grid))]
            first = functools.reduce(jnp.logical_and, [i == 0 for i in ids])
            last = functools.reduce(jnp.logical_and, [i == g - 1 for i, g in zip(ids, grid)])
            pl.when(first)(lambda: each("start"))
        elif pieces:
            each("start")
        if body is not None:
            body(*pre, *ins, *outs, *scr)
        if pieces and grid:
            pl.when(last)(lambda: each("finish"))
        elif pieces:
            each("finish")

    scratch = list(scratch_shapes) + ([pltpu.SemaphoreType.DMA((n_sems,))] if pieces else [])
    all_in = list(in_specs) + [ANY] * len(c_ops)
    all_out = list(out_specs) + [ANY] * len(c_outs)
    shapes = list(out_shape) + c_outs
    kw = dict(name=name, out_shape=shapes, input_output_aliases=aliases, interpret=_INTERPRET)
    if grid:
        kw["compiler_params"] = pltpu.CompilerParams(dimension_semantics=("arbitrary",) * len(grid),
                                                     vmem_limit_bytes=VMEM_LIMIT)
    if prefetch:
        kw["grid_spec"] = pltpu.PrefetchScalarGridSpec(num_scalar_prefetch=prefetch, grid=grid, in_specs=all_in,
                                                       out_specs=all_out, scratch_shapes=scratch)
    else:
        kw.update(in_specs=all_in, out_specs=all_out, scratch_shapes=scratch)
        if grid:
            kw["grid"] = grid
    res = pl.pallas_call(wrapped, **kw)(*args, *c_ops)
    outs, rest = list(res[:n_out]), list(res[n_out:])
    couts = []
    for p in pieces:
        couts.append(rest[:len(p.out_shapes)])
        rest = rest[len(p.out_shapes):]
    return outs, couts


def comm_only(pieces, *, name):
    return _call(None, name=name, grid=(), in_specs=[], out_specs=[], out_shape=[], args=[], pieces=pieces)[1]


def rmsnorm_fwd(x, g, *, name, pieces=()):
    t, d = x.shape
    tm = min(512, t)

    def body(x_ref, g_ref, o_ref):
        xv = x_ref[...]
        rstd = lax.rsqrt(jnp.mean(xv * xv, axis=-1, keepdims=True) + RMS_EPS)
        o_ref[...] = (xv * rstd * g_ref[...]).astype(BF16)

    outs, couts = _call(
        body, name=name, grid=(t // tm,),
        in_specs=[pl.BlockSpec((tm, d), lambda i: (i, 0)), pl.BlockSpec((1, d), lambda i: (0, 0))],
        out_specs=[pl.BlockSpec((tm, d), lambda i: (i, 0))], out_shape=[SDS((t, d), BF16)],
        args=[x, g], pieces=pieces)
    return outs[0], couts


def loss_head(x, g, target, *, name):
    t, d = x.shape
    tm = min(512, t)

    def body(x_ref, g_ref, t_ref, loss_ref, dx_ref, dxb_ref, dg_ref):
        i = pl.program_id(0)
        xv = x_ref[...]
        gv = g_ref[...]
        rstd = lax.rsqrt(jnp.mean(xv * xv, axis=-1, keepdims=True) + RMS_EPS)
        xhat = xv * rstd
        err = xhat * gv - t_ref[...]
        dy = err * (1.0 / d)
        dxhat = dy * gv
        dx = rstd * (dxhat - xhat * jnp.mean(dxhat * xhat, axis=-1, keepdims=True))
        dx_ref[...] = dx
        dxb_ref[...] = dx.astype(BF16)

        @pl.when(i == 0)
        def _():
            dg_ref[...] = jnp.zeros_like(dg_ref)
            loss_ref[...] = jnp.zeros_like(loss_ref)

        dg_ref[...] += jnp.sum(dy * xhat, axis=0, keepdims=True)
        per_tok = jnp.sum(err * err, axis=-1, keepdims=True) * (0.5 / d)
        loss_ref[...] += jnp.sum(per_tok, axis=0, keepdims=True)

    row = pl.BlockSpec((tm, d), lambda i: (i, 0))
    vec = pl.BlockSpec((1, d), lambda i: (0, 0))
    one = pl.BlockSpec((1, 1), lambda i: (0, 0))
    outs, _ = _call(
        body, name=name, grid=(t // tm,), in_specs=[row, vec, row], out_specs=[one, row, row, vec],
        out_shape=[SDS((1, 1), F32), SDS((t, d), F32), SDS((t, d), BF16), SDS((1, d), F32)],
        args=[x, g, target])
    return outs


def mm_nn(a, b, *, name, tm, tn, out_dtype, residual=None, pieces=()):
    m, k = a.shape
    n = b.shape[1]
    tm, tn = min(tm, m), min(tn, n)
    has_res = residual is not None

    def body(a_ref, b_ref, *rest):
        o_ref = rest[-1]
        acc = _dot(a_ref[...], b_ref[...])
        if has_res:
            acc = acc + rest[0][...]
        o_ref[...] = acc.astype(o_ref.dtype)

    in_specs = [pl.BlockSpec((tm, k), lambda j, i: (i, 0)), pl.BlockSpec((k, tn), lambda j, i: (0, j))]
    args = [a, b]
    if has_res:
        in_specs.append(pl.BlockSpec((tm, tn), lambda j, i: (i, j)))
        args.append(residual)
    outs, couts = _call(
        body, name=name, grid=(n // tn, m // tm), in_specs=in_specs,
        out_specs=[pl.BlockSpec((tm, tn), lambda j, i: (i, j))], out_shape=[SDS((m, n), out_dtype)],
        args=args, pieces=pieces)
    return outs[0], couts


def mm_nt(a, b, *, name, tm, tn, out_dtype, pieces=()):
    m, k = a.shape
    n = b.shape[0]
    tm, tn = min(tm, m), min(tn, n)

    def body(a_ref, b_ref, o_ref):
        o_ref[...] = _dot_nt(a_ref[...], b_ref[...]).astype(o_ref.dtype)

    outs, couts = _call(
        body, name=name, grid=(n // tn, m // tm),
        in_specs=[pl.BlockSpec((tm, k), lambda j, i: (i, 0)), pl.BlockSpec((tn, k), lambda j, i: (j, 0))],
        out_specs=[pl.BlockSpec((tm, tn), lambda j, i: (i, j))], out_shape=[SDS((m, n), out_dtype)],
        args=[a, b], pieces=pieces)
    return outs[0], couts


def _rms_norm(xv, gv):
    rstd = lax.rsqrt(jnp.mean(xv * xv, axis=-1, keepdims=True) + RMS_EPS)
    return (xv * rstd * gv).astype(BF16)


def norm_mm_nt(x, g, b, *, name, tm, out_dtype, pieces=()):
    m, k = x.shape
    n = b.shape[0]
    tm = min(tm, m)

    def body(x_ref, g_ref, b_ref, o_ref, h_ref):
        h = _rms_norm(x_ref[...], g_ref[...])
        h_ref[...] = h
        o_ref[...] = _dot_nt(h, b_ref[...]).astype(o_ref.dtype)

    outs, couts = _call(
        body, name=name, grid=(m // tm,),
        in_specs=[pl.BlockSpec((tm, k), lambda i: (i, 0)), pl.BlockSpec((1, k), lambda i: (0, 0)),
                  pl.BlockSpec((n, k), lambda i: (0, 0))],
        out_specs=[pl.BlockSpec((tm, n), lambda i: (i, 0)), pl.BlockSpec((tm, k), lambda i: (i, 0))],
        out_shape=[SDS((m, n), out_dtype), SDS((m, k), BF16)], args=[x, g, b], pieces=pieces)
    return outs, couts


def mm_tn(a, b, *, name, tk, tn, tt, out_dtype, pieces=()):
    t, k = a.shape
    n = b.shape[1]
    tk, tn, tt = min(tk, k), min(tn, n), min(tt, t)
    nt = t // tt

    def body(a_ref, b_ref, o_ref, acc_ref):
        s = pl.program_id(2)

        @pl.when(s == 0)
        def _():
            acc_ref[...] = jnp.zeros_like(acc_ref)

        acc_ref[...] += _dot_tn(a_ref[...], b_ref[...])

        @pl.when(s == nt - 1)
        def _():
            o_ref[...] = acc_ref[...].astype(o_ref.dtype)

    outs, couts = _call(
        body, name=name, grid=(k // tk, n // tn, nt),
        in_specs=[pl.BlockSpec((tt, tk), lambda i, j, s: (s, i)), pl.BlockSpec((tt, tn), lambda i, j, s: (s, j))],
        out_specs=[pl.BlockSpec((tk, tn), lambda i, j, s: (i, j))], out_shape=[SDS((k, n), out_dtype)],
        scratch_shapes=[pltpu.VMEM((tk, tn), F32)], args=[a, b], pieces=pieces)
    return outs[0], couts


def swiglu_fwd(x, g, wt, *, name, tm, tn, pieces=()):
    t, d = x.shape
    ff = wt.shape[0] // 2
    tm, tn = min(tm, t), min(tn, ff)
    nb = ff // tn

    nm = t // tm

    def body(x_ref, g_ref, wg_ref, wu_ref, gu_ref, act_ref, h_ref):
        hv = _rms_norm(x_ref[...], g_ref[...])

        @pl.when(pl.program_id(0) == 0)
        def _():
            h_ref[...] = hv

        gate = _dot_nt(hv, wg_ref[...])
        up = _dot_nt(hv, wu_ref[...])
        gu_ref[0] = gate.astype(BF16)
        gu_ref[1] = up.astype(BF16)
        act_ref[...] = (gate * _sigmoid(gate) * up).astype(BF16)

    outs, couts = _call(
        body, name=name, grid=(nb, t // tm),
        in_specs=[pl.BlockSpec((tm, d), lambda j, i: (i, 0)), pl.BlockSpec((1, d), lambda j, i: (0, 0)),
                  pl.BlockSpec((tn, d), lambda j, i: (j, 0)),
                  pl.BlockSpec((tn, d), lambda j, i: (j + nb, 0))],
        out_specs=[pl.BlockSpec((2, tm, tn), lambda j, i: (0, i, j)), pl.BlockSpec((tm, tn), lambda j, i: (i, j)),
                   pl.BlockSpec((tm, d), lambda j, i: (jnp.where(j == 0, i, nm - 1), 0))],
        out_shape=[SDS((2, t, ff), BF16), SDS((t, ff), BF16), SDS((t, d), BF16)], args=[x, g, wt, wt],
        pieces=pieces)
    return outs, couts


def swiglu_bwd(dx, w_down, gu, *, name, tm, tn, pieces=()):
    t, d = dx.shape
    ff = w_down.shape[0]
    tm, tn = min(tm, t), min(tn, ff)

    def body(dx_ref, w_ref, gu_ref, o_ref):
        dact = _dot_nt(dx_ref[...], w_ref[...])
        gate = gu_ref[0].astype(F32)
        up = gu_ref[1].astype(F32)
        sg = _sigmoid(gate)
        o_ref[0] = (dact * up * sg * (1.0 + gate * (1.0 - sg))).astype(BF16)
        o_ref[1] = (dact * gate * sg).astype(BF16)

    outs, couts = _call(
        body, name=name, grid=(ff // tn, t // tm),
        in_specs=[pl.BlockSpec((tm, d), lambda j, i: (i, 0)), pl.BlockSpec((tn, d), lambda j, i: (j, 0)),
                  pl.BlockSpec((2, tm, tn), lambda j, i: (0, i, j))],
        out_specs=[pl.BlockSpec((2, tm, tn), lambda j, i: (0, i, j))], out_shape=[SDS((2, t, ff), BF16)],
        args=[dx, w_down, gu], pieces=pieces)
    return outs[0], couts


def mm_norm_bwd(a, wt, x, g, dres, *, name, tm, pieces=(), blocks=None, before=None):
    parts = a.shape[0] if a.ndim == 3 else 1
    t, kp = a.shape[-2:]
    d = wt.shape[1]
    tm = min(tm, t)
    first, count = blocks if blocks else (0, t // tm)

    def body(a_ref, w_ref, x_ref, g_ref, dres_ref, *rest):
        dx_ref, dxb_ref, dg_ref = rest[-3:]
        i = pl.program_id(0)
        if parts == 1:
            dh = _dot(a_ref[...], w_ref[...])
        else:
            dh = _dot(a_ref[0], w_ref[0:kp, :])
            for q in range(1, parts):
                dh = dh + _dot(a_ref[q], w_ref[q * kp:(q + 1) * kp, :])
        xv = x_ref[...]
        rstd = lax.rsqrt(jnp.mean(xv * xv, axis=-1, keepdims=True) + RMS_EPS)
        xhat = xv * rstd
        dxhat = dh * g_ref[...]
        dx = dres_ref[...] + rstd * (dxhat - xhat * jnp.mean(dxhat * xhat, axis=-1, keepdims=True))
        dx_ref[...] = dx
        dxb_ref[...] = dx.astype(BF16)

        @pl.when(i == 0)
        def _():
            dg_ref[...] = rest[2][...] if before else jnp.zeros_like(dg_ref)

        dg_ref[...] += jnp.sum(dh * xhat, axis=0, keepdims=True)

    a_spec = (pl.BlockSpec((tm, kp), lambda i: (i + first, 0)) if parts == 1
              else pl.BlockSpec((parts, tm, kp), lambda i: (0, i + first, 0)))
    row = pl.BlockSpec((tm, d), lambda i: (i + first, 0))
    vec = pl.BlockSpec((1, d), lambda i: (0, 0))
    in_specs = [a_spec, pl.BlockSpec((parts * kp, d), lambda i: (0, 0)), row, vec, row]
    args = [a, wt, x, g, dres]
    if before:
        in_specs += [ANY, ANY, vec]
        args += list(before)
    outs, couts = _call(
        body, name=name, grid=(count,), in_specs=in_specs,
        out_specs=[row, row, vec], out_shape=[SDS((t, d), F32), SDS((t, d), BF16), SDS((1, d), F32)],
        args=args, pieces=pieces, in_place={5: 0, 6: 1} if before else None)
    return outs, couts


def mm_gu_tn(dgu, h, *, name, tn, tt, pieces=()):
    t, d = h.shape
    ff = dgu.shape[2]
    tn, tt = min(tn, ff), min(tt, t)
    nb = ff // tn
    nt = t // tt

    def body(a_ref, h_ref, o_ref, acc_ref):
        s = pl.program_id(1)

        @pl.when(s == 0)
        def _():
            acc_ref[...] = jnp.zeros_like(acc_ref)

        acc_ref[...] += _dot_tn(a_ref[...], h_ref[...])

        @pl.when(s == nt - 1)
        def _():
            o_ref[...] = acc_ref[...].astype(o_ref.dtype)

    outs, couts = _call(
        body, name=name, grid=(2 * nb, nt),
        in_specs=[pl.BlockSpec((None, tt, tn), lambda j, s: (j // nb, s, j % nb)),
                  pl.BlockSpec((tt, d), lambda j, s: (s, 0))],
        out_specs=[pl.BlockSpec((tn, d), lambda j, s: (j, 0))], out_shape=[SDS((2 * ff, d), BF16)],
        scratch_shapes=[pltpu.VMEM((tn, d), F32)], args=[dgu, h], pieces=pieces)
    return outs[0], couts


def _head_masks(shape):
    lane = lax.broadcasted_iota(jnp.int32, shape, 1)
    return [(lane >= h * HEAD_DIM_B) & (lane < (h + 1) * HEAD_DIM_B) for h in range(N_HEADS_B)]


def _causal_rows(ws):
    r = lax.broadcasted_iota(jnp.int32, ws.shape, 0) % CHUNK
    c = lax.broadcasted_iota(jnp.int32, ws.shape, 1)
    return jnp.where(c <= r, ws, jnp.zeros_like(ws))


def _pool_window(shape):
    lane = lax.broadcasted_iota(jnp.int32, shape, 1)
    return jnp.left_shift(2, lane // GROUP_DIM_C)


def _layer_norm_fwd(x, g, b):
    mu = jnp.mean(x, axis=-1, keepdims=True)
    xc = x - mu
    rstd = lax.rsqrt(jnp.mean(xc * xc, axis=-1, keepdims=True) + LN_EPS)
    xhat = xc * rstd
    return xhat * g + b, xhat, rstd


def _layer_norm_bwd(dy, xhat, rstd, g):
    dxhat = dy * g
    return rstd * (dxhat - jnp.mean(dxhat, axis=-1, keepdims=True)
                   - xhat * jnp.mean(dxhat * xhat, axis=-1, keepdims=True))


def _gate_mix(ws_masked, vl_chunk, masks):
    out = _dot(ws_masked, vl_chunk.astype(BF16))
    s = jnp.zeros((CHUNK, D_B), F32)
    for h in range(N_HEADS_B):
        s = s + jnp.where(masks[h], out[h * CHUNK:(h + 1) * CHUNK], 0.0)
    return s


SUB = 8
CONV_ROWS = 64


def _shifted_copies(ref, n):
    for b in range(1, SUB):
        ref[b, 0:n, :] = ref[0, pl.ds(b, n), :]


def _tap(ref, off, n):
    a, b = divmod(off, SUB)
    return ref[b, pl.ds(SUB * a, n), :]


def _pick_window(s2, s4, s8, s16):
    grp = lax.broadcasted_iota(jnp.int32, s2.shape, 1) // GROUP_DIM_C
    return jnp.where(grp == 0, s2, jnp.where(grp == 1, s4, jnp.where(grp == 2, s8, s16)))


def _trailing_window_sums(src_ref, l2_ref, l4_ref, l8_ref, tm):
    n = HALO + tm
    l2_ref[8:n, :] = src_ref[8:n, :] + src_ref[pl.ds(7, n - 8), :]
    l4_ref[16:n, :] = l2_ref[16:n, :] + l2_ref[pl.ds(14, n - 16), :]
    l8_ref[24:n, :] = l4_ref[24:n, :] + l4_ref[pl.ds(20, n - 24), :]
    s16 = l8_ref[HALO:n, :] + l8_ref[HALO - 8:n - 8, :]
    return _pick_window(l2_ref[HALO:n, :], l4_ref[HALO:n, :], l8_ref[HALO:n, :], s16)


def _leading_window_sums(src_ref, l2_ref, l4_ref, l8_ref, tm):
    n = HALO + tm
    l2_ref[0:n - 8, :] = src_ref[0:n - 8, :] + src_ref[pl.ds(1, n - 8), :]
    l4_ref[0:n - 16, :] = l2_ref[0:n - 16, :] + l2_ref[pl.ds(2, n - 16), :]
    l8_ref[0:n - 24, :] = l4_ref[0:n - 24, :] + l4_ref[pl.ds(4, n - 24), :]
    s16 = l8_ref[0:tm, :] + l8_ref[8:tm + 8, :]
    return _pick_window(l2_ref[0:tm, :], l4_ref[0:tm, :], l8_ref[0:tm, :], s16)


def _mixer_specs(tm, nt, seq):
    hb = tm // HALO

    def cur(c):
        return pl.BlockSpec((tm, c), lambda b, i: (b * nt + i, 0))

    def prev(c):
        return pl.BlockSpec((HALO, c), lambda b, i: (jnp.maximum((b * nt + i) * hb - 1, 0), 0))

    def nxt(c):
        last = (2 * seq) // HALO - 1
        return pl.BlockSpec((HALO, c), lambda b, i: (jnp.minimum((b * nt + i + 1) * hb, last), 0))

    def full(shape, layer=None):
        if layer is None:
            return pl.BlockSpec(shape, lambda b, i: tuple(0 for _ in shape))
        return pl.BlockSpec((None,) + shape, lambda b, i: (layer,) + tuple(0 for _ in shape))

    return cur, prev, nxt, full


_MIX_PARAM_SHAPES = [(32, D_A), (1, D_A), (1, D_A), (1, D_A), (D_A, D_A), (1, D_B), (1, D_B),
                     (N_HEADS_B * CHUNK, CHUNK), (CHUNK, D_B), (D_C, D_C), (1, D_C)]


def mixer_fwd(z, mp, layer, *, seq, name, tm=512, pieces=()):
    t = z.shape[0]
    tm = min(tm, seq)
    nt = seq // tm
    cur, prev, _, full = _mixer_specs(tm, nt, seq)

    def body(zc_ref, zp_ref, cw_ref, cb_ref, clg_ref, clb_ref, wpw_ref, slg_ref, slb_ref, ws_ref, bias_ref,
             wp_ref, ps_ref, o_ref, cv_ref, ys_ref, zs_ref, l2_ref, l4_ref, l8_ref):
        i = pl.program_id(1)
        has_prev = i > 0
        yp = zp_ref[:, 0:D_A].astype(F32) * _sigmoid(zp_ref[:, D_A:2 * D_A].astype(F32))
        ys_ref[0, 0:HALO, :] = jnp.where(has_prev, yp, 0.0)
        ys_ref[0, HALO:HALO + tm, :] = zc_ref[:, 0:D_A].astype(F32) * _sigmoid(zc_ref[:, D_A:2 * D_A].astype(F32))
        _shifted_copies(ys_ref, tm + HALO - SUB)
        acc = jnp.zeros((tm, D_A), F32) + cb_ref[...]
        for k in range(CONV_WIDTH):
            acc = acc + cw_ref[k:k + 1, :] * _tap(ys_ref, HALO - (CONV_WIDTH - 1) + k, tm)
        cv_ref[...] = acc
        ln, _, _ = _layer_norm_fwd(acc, clg_ref[...], clb_ref[...])
        sl = ln * _sigmoid(ln)
        o_ref[:, 0:D_A] = _dot(sl.astype(BF16), wpw_ref[...]).astype(BF16)
        gz = _gelu(zc_ref[:, 2 * D_A:2 * D_A + 2 * D_B].astype(F32))
        u = gz[:, :D_B]
        vl, _, _ = _layer_norm_fwd(gz[:, D_B:], slg_ref[...], slb_ref[...])
        wsm = _causal_rows(ws_ref[...])
        masks = _head_masks((CHUNK, D_B))
        for c in range(tm // CHUNK):
            rows = slice(c * CHUNK, (c + 1) * CHUNK)
            s = _gate_mix(wsm, vl[rows], masks) + bias_ref[...]
            o_ref[rows, D_A:D_A + D_B] = (u[rows] * s).astype(BF16)
        c0 = 2 * D_A + 2 * D_B
        zs_ref[0:HALO, :] = jnp.where(has_prev, zp_ref[:, c0:c0 + D_C].astype(F32), 0.0)
        zcur = zc_ref[:, c0:c0 + D_C].astype(F32)
        zs_ref[HALO:HALO + tm, :] = zcur
        win = _pool_window((tm, D_C))
        wsum = _trailing_window_sums(zs_ref, l2_ref, l4_ref, l8_ref, tm)
        pos = i * tm + lax.broadcasted_iota(jnp.int32, (tm, D_C), 0)
        cnt = jnp.minimum(pos + 1, win).astype(F32)
        p = wsum / cnt - zcur
        y = _dot(p.astype(BF16), wp_ref[...])
        o_ref[:, D_A + D_B:D_A + D_B + D_C] = (y * ps_ref[...]).astype(BF16)

    in_specs = [cur(D_IN), prev(D_IN)] + [full(s, layer) for s in _MIX_PARAM_SHAPES]
    outs, couts = _call(
        body, name=name, grid=(2, nt), in_specs=in_specs, out_specs=[cur(D_MODEL), cur(D_A)],
        out_shape=[SDS((t, D_MODEL), BF16), SDS((t, D_A), F32)],
        scratch_shapes=[pltpu.VMEM((SUB, HALO + tm, D_A), F32)] + [pltpu.VMEM((HALO + tm, D_C), F32)] * 4,
        args=[z, z, *mp], pieces=pieces)
    return outs, couts


def mixer_bwd(z, conv, dm, mp, layer, *, seq, name, tm=256, pieces=()):
    t = z.shape[0]
    tm = min(tm, seq)
    nt = seq // tm
    ext = tm + HALO
    cur, prev, nxt, full = _mixer_specs(tm, nt, seq)
    grad_shapes = [(32, D_A), (1, D_A), (1, D_A), (1, D_A), (D_A, D_A), (1, D_B), (1, D_B),
                   (N_HEADS_B * CHUNK, CHUNK), (CHUNK, CHUNK), (D_C, D_C), (1, D_C)]

    def body(zc_ref, zp_ref, cvc_ref, cvn_ref, dmc_ref, dmn_ref, cw_ref, cb_ref, clg_ref, clb_ref, wpw_ref, slg_ref,
             slb_ref, ws_ref, bias_ref, wp_ref, ps_ref,
             dz_ref, dcw_ref, dcb_ref, dclg_ref, dclb_ref, dwpw_ref, dslg_ref, dslb_ref, dws_ref, dbs_ref, dwp_ref,
             dps_ref, ys_ref, dcs_ref, zs_ref, qs_ref, l2_ref, l4_ref, l8_ref):
        b = pl.program_id(0)
        i = pl.program_id(1)
        has_prev = i > 0
        has_next = i < nt - 1
        grads = [dcw_ref, dcb_ref, dclg_ref, dclb_ref, dwpw_ref, dslg_ref, dslb_ref, dws_ref, dbs_ref, dwp_ref, dps_ref]

        @pl.when((b == 0) & (i == 0))
        def _():
            for r in grads:
                r[...] = jnp.zeros_like(r)

        ext_row = lax.broadcasted_iota(jnp.int32, (ext, 1), 0)
        live = (ext_row < tm) | has_next

        yp = zp_ref[:, 0:D_A].astype(F32) * _sigmoid(zp_ref[:, D_A:2 * D_A].astype(F32))
        ys_ref[0, 0:HALO, :] = jnp.where(has_prev, yp, 0.0)
        a_cur = zc_ref[:, 0:D_A].astype(F32)
        sig_cur = _sigmoid(zc_ref[:, D_A:2 * D_A].astype(F32))
        ys_ref[0, HALO:HALO + tm, :] = a_cur * sig_cur
        _shifted_copies(ys_ref, tm + HALO - SUB)
        acc = jnp.concatenate([cvc_ref[...], cvn_ref[...]], axis=0)
        ln, xhat, rstd = _layer_norm_fwd(acc, clg_ref[...], clb_ref[...])
        sg = _sigmoid(ln)
        sl = ln * sg
        dya = jnp.concatenate([dmc_ref[:, 0:D_A], dmn_ref[:, 0:D_A]], axis=0)
        dsl = _dot_nt(dya, wpw_ref[...])
        dln = dsl * sg * (1.0 + ln * (1.0 - sg))
        dc = _layer_norm_bwd(dln, xhat, rstd, clg_ref[...])
        dc = jnp.where(live, dc, 0.0)
        dcs_ref[0] = dc
        _shifted_copies(dcs_ref, ext - SUB)
        dwpw_ref[...] += _dot_tn(sl[:tm].astype(BF16), dya[:tm])
        dclg_ref[...] += jnp.sum(dln[:tm] * xhat[:tm], axis=0, keepdims=True)
        dclb_ref[...] += jnp.sum(dln[:tm], axis=0, keepdims=True)
        dcb_ref[...] += jnp.sum(dc[:tm], axis=0, keepdims=True)
        for k in range(CONV_WIDTH):
            off = HALO - (CONV_WIDTH - 1) + k
            dcw_ref[k:k + 1, :] += jnp.sum(dc[:tm] * _tap(ys_ref, off, tm), axis=0, keepdims=True)
        for r0 in range(0, tm, CONV_ROWS):
            rows = slice(r0, r0 + CONV_ROWS)
            dy = jnp.zeros((CONV_ROWS, D_A), F32)
            for k in range(CONV_WIDTH):
                dy = dy + cw_ref[k:k + 1, :] * _tap(dcs_ref, CONV_WIDTH - 1 - k + r0, CONV_ROWS)
            dz_ref[rows, 0:D_A] = (dy * sig_cur[rows]).astype(BF16)
            dz_ref[rows, D_A:2 * D_A] = (dy * a_cur[rows] * sig_cur[rows] * (1.0 - sig_cur[rows])).astype(BF16)

        zb = zc_ref[:, 2 * D_A:2 * D_A + 2 * D_B].astype(F32)
        gz = _gelu(zb)
        u = gz[:, :D_B]
        vl, vhat, vrstd = _layer_norm_fwd(gz[:, D_B:], slg_ref[...], slb_ref[...])
        dyb = dmc_ref[:, D_A:D_A + D_B].astype(F32)
        wsm = _causal_rows(ws_ref[...])
        masks = _head_masks((CHUNK, D_B))
        ds_all = dyb * u
        du_parts, dvl_parts = [], []
        for c in range(tm // CHUNK):
            rows = slice(c * CHUNK, (c + 1) * CHUNK)
            vlc = vl[rows].astype(BF16)
            s = _gate_mix(wsm, vl[rows], masks) + bias_ref[...]
            du_parts.append(dyb[rows] * s)
            ds = ds_all[rows]
            stack = jnp.concatenate([jnp.where(masks[h], ds, 0.0) for h in range(N_HEADS_B)], axis=0).astype(BF16)
            dvl_parts.append(_dot_tn(wsm, stack))
            dws_ref[...] += _dot_nt(stack, vlc)
        du = jnp.concatenate(du_parts, axis=0)
        dvl = jnp.concatenate(dvl_parts, axis=0)
        dbias = jnp.zeros((CHUNK, D_B), F32)
        for c in range(tm // CHUNK):
            dbias = dbias + ds_all[c * CHUNK:(c + 1) * CHUNK]
        lane = lax.broadcasted_iota(jnp.int32, (CHUNK, CHUNK), 1)
        dbs = jnp.zeros((CHUNK, CHUNK), F32)
        for h in range(N_HEADS_B):
            col = jnp.sum(jnp.where(masks[h], dbias, 0.0), axis=1, keepdims=True)
            dbs = dbs + jnp.where(lane == h, col, 0.0)
        dbs_ref[...] += dbs
        dslg_ref[...] += jnp.sum(dvl * vhat, axis=0, keepdims=True)
        dslb_ref[...] += jnp.sum(dvl, axis=0, keepdims=True)
        dv = _layer_norm_bwd(dvl, vhat, vrstd, slg_ref[...])
        gg = _gelu_grad(zb)
        dz_ref[:, 2 * D_A:2 * D_A + D_B] = (du * gg[:, :D_B]).astype(BF16)
        dz_ref[:, 2 * D_A + D_B:2 * D_A + 2 * D_B] = (dv * gg[:, D_B:]).astype(BF16)

        c0 = 2 * D_A + 2 * D_B
        m0 = D_A + D_B
        zs_ref[0:HALO, :] = jnp.where(has_prev, zp_ref[:, c0:c0 + D_C].astype(F32), 0.0)
        zcur = zc_ref[:, c0:c0 + D_C].astype(F32)
        zs_ref[HALO:HALO + tm, :] = zcur
        win = _pool_window((tm, D_C))
        wsum = _trailing_window_sums(zs_ref, l2_ref, l4_ref, l8_ref, tm)
        pos = i * tm + lax.broadcasted_iota(jnp.int32, (tm, D_C), 0)
        cnt = jnp.minimum(pos + 1, win).astype(F32)
        pb = (wsum / cnt - zcur).astype(BF16)
        y = _dot(pb, wp_ref[...])
        dyc = jnp.concatenate([dmc_ref[:, m0:m0 + D_C], dmn_ref[:, m0:m0 + D_C]], axis=0).astype(F32)
        dps_ref[...] += jnp.sum(dyc[:tm] * y, axis=0, keepdims=True)
        dyv = (dyc * ps_ref[...]).astype(BF16)
        dwp_ref[...] += _dot_tn(pb, dyv[:tm])
        dp = _dot_nt(dyv, wp_ref[...])
        win_e = _pool_window((ext, D_C))
        pos_e = i * tm + lax.broadcasted_iota(jnp.int32, (ext, D_C), 0)
        cnt_e = jnp.minimum(pos_e + 1, win_e).astype(F32)
        qs_ref[...] = jnp.where(live, dp / cnt_e, 0.0)
        dzc = _leading_window_sums(qs_ref, l2_ref, l4_ref, l8_ref, tm) - dp[:tm]
        dz_ref[:, c0:c0 + D_C] = dzc.astype(BF16)

        @pl.when((b == 1) & (i == nt - 1))
        def _():
            dws_ref[...] = _causal_rows(dws_ref[...])

    in_specs = ([cur(D_IN), prev(D_IN), cur(D_A), nxt(D_A), cur(D_MODEL), nxt(D_MODEL)]
                + [full(s, layer) for s in _MIX_PARAM_SHAPES])
    out_specs = [cur(D_IN)] + [full(s) for s in grad_shapes]
    out_shape = [SDS((t, D_IN), BF16)] + [SDS(s, F32) for s in grad_shapes]
    outs, couts = _call(
        body, name=name, grid=(2, nt), in_specs=in_specs, out_specs=out_specs, out_shape=out_shape,
        scratch_shapes=[pltpu.VMEM((SUB, HALO + tm, D_A), F32), pltpu.VMEM((SUB, ext, D_A), F32)]
        + [pltpu.VMEM((ext, D_C), F32)] * 5,
        args=[z, z, conv, conv, dm, dm, *mp], pieces=pieces)
    return outs, couts


N_GROUPS_C = D_C // GROUP_DIM_C


def _mixer_params(p, w_pw_bf16):
    eye = jnp.eye(N_GROUPS_C, dtype=F32)
    wp_bd = jnp.einsum("lgio,gh->lgiho", p["w_pool"], eye).reshape(DEPTH, D_C, D_C)
    return [
        jnp.pad(p["conv_w"], ((0, 0), (0, 32 - CONV_WIDTH), (0, 0))),
        p["conv_b"][:, None], p["conv_ln_g"][:, None], p["conv_ln_b"][:, None],
        w_pw_bf16,
        p["sg_ln_g"][:, None], p["sg_ln_b"][:, None],
        p["w_s"].reshape(DEPTH, N_HEADS_B * CHUNK, CHUNK).astype(BF16),
        jnp.repeat(p["b_s"].transpose(0, 2, 1), HEAD_DIM_B, axis=2),
        wp_bd.astype(BF16),
        p["pool_scale"][:, None],
    ]


RAW_SMALL = [("conv_w", (32, D_A)), ("conv_b", (1, D_A)), ("conv_ln_g", (1, D_A)), ("conv_ln_b", (1, D_A)),
             ("sg_ln_g", (1, D_B)), ("sg_ln_b", (1, D_B)), ("w_s", (N_HEADS_B * CHUNK, CHUNK)), ("b_s", (CHUNK, CHUNK)),
             ("w_pool", (D_C, D_C)), ("pool_scale", (1, D_C)), ("norm1_g", (1, D_MODEL)), ("norm2_g", (1, D_MODEL))]


def _small_grads(flat):
    per_layer = sum(math.prod(s) for _, s in RAW_SMALL)
    layers = flat[:DEPTH * per_layer].reshape(DEPTH, per_layer)
    raw, off = {}, 0
    for k, s in RAW_SMALL:
        raw[k] = layers[:, off:off + math.prod(s)].reshape((DEPTH,) + s)
        off += math.prod(s)
    eye = jnp.eye(N_GROUPS_C, dtype=F32)
    g = {k: raw[k][:, 0] for k in ["conv_b", "conv_ln_g", "conv_ln_b", "sg_ln_g", "sg_ln_b", "pool_scale", "norm1_g",
                                  "norm2_g"]}
    g["conv_w"] = raw["conv_w"][:, :CONV_WIDTH]
    g["w_s"] = raw["w_s"].reshape(DEPTH, N_HEADS_B, CHUNK, CHUNK)
    g["b_s"] = raw["b_s"][:, :, :N_HEADS_B].transpose(0, 2, 1)
    blocks = raw["w_pool"].reshape(DEPTH, N_GROUPS_C, GROUP_DIM_C, N_GROUPS_C, GROUP_DIM_C)
    g["w_pool"] = jnp.sum(blocks * eye[None, :, None, :, None], axis=3)
    end = DEPTH * per_layer
    g["final_g"] = flat[end:end + D_MODEL]
    return g, flat[end + D_MODEL]


def _row_tile(r, cap=512):
    best = r
    for d in range(16, min(r, cap) + 1, 16):
        if r % d == 0:
            best = d
    return best if best <= cap else r


def add_core_halves(g, r1, core, *, name):
    _, _, r, c = g.shape
    tr = _row_tile(r)

    def body(core_ref, g_ref, r_ref, o_ref):
        o_ref[...] = (g_ref[...].astype(F32) + r_ref[...].astype(F32)).astype(o_ref.dtype)

    outs, _ = _call(
        body, name=name, grid=(N_CHIPS, r // tr), prefetch=1,
        in_specs=[pl.BlockSpec((None, None, tr, c), lambda j, i, s: (j, s[0], i, 0)),
                  pl.BlockSpec((None, tr, c), lambda j, i, s: (j, i, 0))],
        out_specs=[pl.BlockSpec((None, tr, c), lambda j, i, s: (j, i, 0))],
        out_shape=[SDS((N_CHIPS, r, c), g.dtype)], args=[core, g, r1])
    return outs[0]


def sum_chips(h, r2, place, *, name):
    _, r, c = h.shape
    tr = _row_tile(r, 256)

    def body(place_ref, h_ref, a_ref, b_ref, c_ref, o_ref):
        acc = h_ref[...].astype(F32) + a_ref[...].astype(F32)
        acc = acc + b_ref[...].astype(F32)
        o_ref[...] = acc + c_ref[...].astype(F32)

    def blk(k):
        return pl.BlockSpec((None, tr, c), lambda i, s: (jnp.bitwise_xor(s[0], k), i, 0))

    outs, _ = _call(
        body, name=name, grid=(r // tr,), prefetch=1, in_specs=[blk(0), blk(1), blk(2), blk(3)],
        out_specs=[pl.BlockSpec((None, tr, c), lambda i, s: (s[1], i, 0))],
        out_shape=[SDS((2, r, c), F32)], args=[place, h, r2, r2, r2])
    return outs[0]


def allreduce_small(p, *, name, pieces=()):
    _, n, _ = p.shape

    def body(p_ref, o_ref, land_ref, send1, recv1, send2, recv2):
        x, y, c = lax.axis_index("x"), lax.axis_index("y"), lax.axis_index("c")
        me = 4 * x + 2 * y + c

        def peer(r):
            return ((1 - x) if r & 4 else x, (1 - y) if r & 2 else y, (1 - c) if r & 1 else c)

        def index(r):
            px, py, pc = peer(r)
            return 4 * px + 2 * py + pc

        land_ref[me] = p_ref[me]
        first = [_remote(p_ref.at[index(r)], land_ref.at[me], send1.at[r - 1], recv1.at[r - 1], peer(r))
                 for r in range(1, N_DEV)]
        for cp in first:
            cp.start()
        for r in range(1, N_DEV):
            blk = land_ref.at[index(r)]
            _remote(blk, blk, send1.at[r - 1], recv1.at[r - 1], peer(r)).wait_recv()
        acc = land_ref[0]
        for d in range(1, N_DEV):
            acc = acc + land_ref[d]
        o_ref[me] = acc
        second = [_remote(o_ref.at[me], o_ref.at[me], send2.at[r - 1], recv2.at[r - 1], peer(r))
                  for r in range(1, N_DEV)]
        for cp in second:
            cp.start()
        for r in range(1, N_DEV):
            blk = o_ref.at[index(r)]
            _remote(blk, blk, send2.at[r - 1], recv2.at[r - 1], peer(r)).wait_recv()
        for cp in first + second:
            cp.wait_send()

    vm = pl.BlockSpec(memory_space=pltpu.VMEM)
    outs, couts = _call(
        body, name=name, grid=(), in_specs=[vm], out_specs=[vm], out_shape=[SDS(p.shape, F32)],
        scratch_shapes=[pltpu.VMEM(p.shape, F32)] + [pltpu.SemaphoreType.DMA((N_DEV - 1,))] * 4,
        args=[p], pieces=pieces)
    return outs[0], couts


def _adam_update(w, g, m, v):
    m_new = ADAM_B1 * m + (1.0 - ADAM_B1) * g
    v_new = ADAM_B2 * v + (1.0 - ADAM_B2) * (g * g)
    m_hat = m_new / (1.0 - ADAM_B1 ** ADAM_STEP)
    v_hat = v_new / (1.0 - ADAM_B2 ** ADAM_STEP)
    return -ADAM_LR * (m_hat / (jnp.sqrt(v_hat) + ADAM_EPS) + ADAM_WD * w), m_new, v_new


def adamw(w, g, m, v, *, name, pieces=()):
    nl, r, c = w.shape
    tr = _row_tile(r, 512)

    def body(w_ref, g_ref, m_ref, v_ref, d_ref, mo_ref, vo_ref):
        d_ref[...], mo_ref[...], vo_ref[...] = _adam_update(w_ref[...], g_ref[...], m_ref[...], v_ref[...])

    blk = pl.BlockSpec((None, tr, c), lambda l, i: (l, i, 0))
    return _call(body, name=name, grid=(nl, r // tr), in_specs=[blk] * 4, out_specs=[blk] * 3,
                 out_shape=[SDS(w.shape, F32)] * 3, args=[w, g, m, v], pieces=pieces)


def adamw_small(ws, gs, ms, vs, *, name, pieces=()):
    n = len(ws)

    def body(*refs):
        for i in range(n):
            w_ref, g_ref, m_ref, v_ref = (refs[k * n + i] for k in range(4))
            d, mn, vn = _adam_update(w_ref[...], g_ref[...], m_ref[...], v_ref[...])
            refs[4 * n + i][...] = d
            refs[5 * n + i][...] = mn
            refs[6 * n + i][...] = vn

    vm = pl.BlockSpec(memory_space=pltpu.VMEM)
    res, couts = _call(body, name=name, grid=(), in_specs=[vm] * (4 * n), out_specs=[vm] * (3 * n),
                       out_shape=[SDS(w.shape, F32) for w in ws] * 3, args=[*ws, *gs, *ms, *vs], pieces=pieces)
    return (res[:n], res[n:2 * n], res[2 * n:]), couts


WEIGHTS = ["norm1_g", "w_in", "conv_w", "conv_b", "conv_ln_g", "conv_ln_b", "w_pw", "sg_ln_g", "sg_ln_b", "w_s",
           "b_s", "w_pool", "pool_scale", "w_out", "norm2_g", "w_gate_up", "w_down", "final_g"]
BIG = ["w_in", "w_pw", "w_out", "w_gate_up", "w_down"]
TRANSPOSED = {"w_in": True, "w_pw": False, "w_out": False, "w_gate_up": True, "w_down": False}
SMALL = [k for k in WEIGHTS if k not in BIG]


def _wire(a, transposed):
    if transposed:
        a = a.transpose(0, 2, 1)
    return [a[l].reshape(2, a.shape[1] // 2, a.shape[2]) for l in range(a.shape[0])]


def _pack(arrays):
    flat = jnp.concatenate([a.reshape(-1) for a in arrays])
    n = -(-flat.shape[0] // (N_DEV * LANES * 8)) * 8
    return jnp.pad(flat, (0, N_DEV * n * LANES - flat.shape[0])).reshape(N_DEV, n, LANES)


def kernel(x, norm1_g, w_in, conv_w, conv_b, conv_ln_g, conv_ln_b, w_pw, sg_ln_g, sg_ln_b, w_s, b_s, w_pool, pool_scale, w_out, norm2_g, w_gate_up, w_down, final_g, loss_target, m_norm1_g, m_w_in, m_conv_w, m_conv_b, m_conv_ln_g, m_conv_ln_b, m_w_pw, m_sg_ln_g, m_sg_ln_b, m_w_s, m_b_s, m_w_pool, m_pool_scale, m_w_out, m_norm2_g, m_w_gate_up, m_w_down, m_final_g, v_norm1_g, v_w_in, v_conv_w, v_conv_b, v_conv_ln_g, v_conv_ln_b, v_w_pw, v_sg_ln_g, v_sg_ln_b, v_w_s, v_b_s, v_w_pool, v_pool_scale, v_w_out, v_norm2_g, v_w_gate_up, v_w_down, v_final_g):
    w = dict(norm1_g=norm1_g, w_in=w_in, conv_w=conv_w, conv_b=conv_b, conv_ln_g=conv_ln_g, conv_ln_b=conv_ln_b,
             w_pw=w_pw, sg_ln_g=sg_ln_g, sg_ln_b=sg_ln_b, w_s=w_s, b_s=b_s, w_pool=w_pool, pool_scale=pool_scale,
             w_out=w_out, norm2_g=norm2_g, w_gate_up=w_gate_up, w_down=w_down, final_g=final_g)
    m = dict(norm1_g=m_norm1_g, w_in=m_w_in, conv_w=m_conv_w, conv_b=m_conv_b, conv_ln_g=m_conv_ln_g,
             conv_ln_b=m_conv_ln_b, w_pw=m_w_pw, sg_ln_g=m_sg_ln_g, sg_ln_b=m_sg_ln_b, w_s=m_w_s, b_s=m_b_s,
             w_pool=m_w_pool, pool_scale=m_pool_scale, w_out=m_w_out, norm2_g=m_norm2_g, w_gate_up=m_w_gate_up,
             w_down=m_w_down, final_g=m_final_g)
    v = dict(norm1_g=v_norm1_g, w_in=v_w_in, conv_w=v_conv_w, conv_b=v_conv_b, conv_ln_g=v_conv_ln_g,
             conv_ln_b=v_conv_ln_b, w_pw=v_w_pw, sg_ln_g=v_sg_ln_g, sg_ln_b=v_sg_ln_b, w_s=v_w_s, b_s=v_b_s,
             w_pool=v_w_pool, pool_scale=v_pool_scale, w_out=v_w_out, norm2_g=v_norm2_g, w_gate_up=v_w_gate_up,
             w_down=v_w_down, final_g=v_final_g)
    bsz, seq, d = x.shape
    t = bsz * seq
    chip = 2 * lax.axis_index("x") + lax.axis_index("y")
    core = lax.axis_index("c")
    core_arr = jnp.reshape(core, (1,)).astype(jnp.int32)
    place_arr = jnp.stack([chip, core]).astype(jnp.int32)

    own = {k: _wire(w[k].astype(BF16), TRANSPOSED[k]) for k in BIG}
    cw_cols = conv_w.shape[2]
    own["side"] = [jnp.pad(conv_w, ((0, 0), (0, 32 - CONV_WIDTH), (0, 0)))]
    parts = {"w_in": 1, "w_pw": 1, "w_out": 1, "w_gate_up": 4, "w_down": 2, "side": 1}
    plan = {
        "norm1_fwd_0": [("ici", "w_in", 0, 0), ("ici", "w_pw", 0, 0), ("ici", "w_pw", 1, 0), ("ici", "side", 0, 0)],
        "gather_first": [("d2d", "w_in", 0, 0), ("d2d", "w_pw", 0, 0), ("d2d", "w_pw", 1, 0), ("d2d", "side", 0, 0)],
        "in_proj_0": [("ici", "w_out", 0, 0), ("ici", "w_gate_up", 0, 0)],
        "mixer_fwd_0": [("d2d", "w_out", 0, 0), ("d2d", "w_gate_up", 0, 0), ("ici", "w_gate_up", 0, 1),
                        ("ici", "w_gate_up", 0, 2)],
        "out_proj_0": [("d2d", "w_gate_up", 0, 1), ("d2d", "w_gate_up", 0, 2), ("both", "w_gate_up", 0, 3)],
        "swiglu_fwd_0": [("both", "w_down", 0, 0), ("both", "w_down", 0, 1), ("ici", "w_in", 1, 0)],
        "down_proj_0": [("d2d", "w_in", 1, 0), ("ici", "w_out", 1, 0), ("ici", "w_gate_up", 1, 0)],
        "in_proj_1": [("d2d", "w_out", 1, 0), ("d2d", "w_gate_up", 1, 0), ("ici", "w_gate_up", 1, 1)],
        "mixer_fwd_1": [("d2d", "w_gate_up", 1, 1), ("ici", "w_gate_up", 1, 2), ("ici", "w_gate_up", 1, 3)],
        "out_proj_1": [("d2d", "w_gate_up", 1, 2), ("d2d", "w_gate_up", 1, 3)],
        "swiglu_fwd_1": [("both", "w_down", 1, 0), ("both", "w_down", 1, 1)],
    }
    bufs = {}

    def grouped(name):
        groups = {}
        for stage, k, l, q in plan.get(name, []):
            groups.setdefault((k, l), []).append((stage, q))
        return groups

    def riding(name):
        return [gather_stages(own[k][l], bufs.get((k, l)), stages, parts[k])
                for (k, l), stages in grouped(name).items()]

    def landed(name, couts):
        for key, co in zip(grouped(name), couts):
            bufs[key] = co[0]

    def whole(k, l):
        g = lax.dynamic_update_index_in_dim(bufs[k, l], own[k][l], chip, 0)
        return g.reshape(-1, g.shape[-1])

    xs = [x.reshape(t, d)]
    saved = []
    full = {}
    p = dict(w)
    for l in range(DEPTH):
        x0 = xs[-1]
        name = f"in_proj_{l}"
        if l == 0:
            h1, co = rmsnorm_fwd(x0, w["norm1_g"][l][None], name="norm1_fwd_0", pieces=riding("norm1_fwd_0"))
            landed("norm1_fwd_0", co)
            landed("gather_first", comm_only(riding("gather_first"), name="gather_first"))
            side_all = lax.dynamic_update_index_in_dim(bufs["side", 0], own["side"][0], chip, 0)
            p["conv_w"] = side_all[:, :, :CONV_WIDTH, :].transpose(1, 2, 0, 3).reshape(
                DEPTH, CONV_WIDTH, N_CHIPS * cw_cols)
            full["w_in", l] = whole("w_in", l)
            z, co = mm_nt(h1, full["w_in", l], name=name, tm=512, tn=D_IN, out_dtype=BF16, pieces=riding(name))
        else:
            full["w_in", l] = whole("w_in", l)
            (z, h1), co = norm_mm_nt(x0, w["norm1_g"][l][None], full["w_in", l], name=name, tm=512, out_dtype=BF16,
                                     pieces=riding(name))
        landed(name, co)
        if l == 0:
            mp = _mixer_params(p, jnp.stack([whole("w_pw", 0), whole("w_pw", 1)]))
        name = f"mixer_fwd_{l}"
        (mc, cv), co = mixer_fwd(z, mp, l, seq=seq, name=name, pieces=riding(name))
        landed(name, co)
        full["w_out", l] = whole("w_out", l)
        name = f"out_proj_{l}"
        x1, co = mm_nn(mc, full["w_out", l], name=name, tm=512, tn=D_MODEL, out_dtype=F32, residual=x0,
                       pieces=riding(name))
        landed(name, co)
        full["w_gate_up", l] = whole("w_gate_up", l)
        name = f"swiglu_fwd_{l}"
        (gu, act, h2), co = swiglu_fwd(x1, w["norm2_g"][l][None], full["w_gate_up", l], name=name, tm=512, tn=1408,
                                       pieces=riding(name))
        landed(name, co)
        full["w_down", l] = whole("w_down", l)
        name = f"down_proj_{l}"
        x2, co = mm_nn(act, full["w_down", l], name=name, tm=512, tn=D_MODEL, out_dtype=F32, residual=x1,
                       pieces=riding(name))
        landed(name, co)
        saved.append((x0, h1, z, cv, mc, x1, h2, gu, act, mp))
        xs.append(x2)

    loss, dx, dxb, d_final_g = loss_head(xs[-1], p["final_g"][None], loss_target.reshape(t, d), name="loss_head")

    raw_small = [None] * DEPTH
    reduced = {}
    carry = None

    def halves(g):
        return g.reshape(N_CHIPS, 2, g.shape[0] // (2 * N_CHIPS), g.shape[1])

    for l in reversed(range(DEPTH)):
        x0, h1, z, cv, mc, x1, h2, gu, act, mp = saved[l]
        pieces = [exchange_d2d(carry[0]), reduce_ici(carry[1])] if carry else []
        g_down, co = mm_tn(act, dxb, name=f"down_proj_dw_{l}", tk=D_FF // 2, tn=D_MODEL, tt=TT, out_dtype=BF16,
                           pieces=pieces)
        g_down = halves(g_down)
        pieces = [reduce_d2d(g_down)]
        if carry:
            reduced["w_pw", l + 1] = co[0][0]
            s_in = sum_chips(carry[1], co[1][0], place_arr, name=f"sum_chips_w_in_{l + 1}")
            pieces.append(exchange_d2d(s_in))
        dgu, co = swiglu_bwd(dxb, full["w_down", l], gu, name=f"swiglu_bwd_{l}", tm=512, tn=1408, pieces=pieces)
        if carry:
            reduced["w_in", l + 1] = co[1][0]
        h_down = add_core_halves(g_down, co[0][0], core_arr, name=f"add_cores_w_down_{l}")
        g_gu, (r2,) = mm_gu_tn(dgu, h2, name=f"gate_up_dw_{l}", tn=1408, tt=TT, pieces=[reduce_ici(h_down)])
        s_down = sum_chips(h_down, r2[0], place_arr, name=f"sum_chips_w_down_{l}")
        g_gu = halves(g_gu)
        (dx, dxb, dn2), (e, r1) = mm_norm_bwd(dgu, full["w_gate_up", l], x1, p["norm2_g"][l][None], dx,
                                              name=f"gate_up_dx_{l}", tm=256,
                                              pieces=[exchange_d2d(s_down), reduce_d2d(g_gu)])
        reduced["w_down", l] = e[0]
        h_gu = add_core_halves(g_gu, r1[0], core_arr, name=f"add_cores_w_gate_up_{l}")
        g_out, _ = mm_tn(mc, dxb, name=f"out_proj_dw_{l}", tk=D_MODEL, tn=D_MODEL, tt=TT, out_dtype=BF16)
        g_out = halves(g_out)
        dmc, (r1,) = mm_nt(dxb, full["w_out", l], name=f"out_proj_dx_{l}", tm=512, tn=D_MODEL, out_dtype=BF16,
                           pieces=[reduce_d2d(g_out)])
        h_out = add_core_halves(g_out, r1[0], core_arr, name=f"add_cores_w_out_{l}")
        (dz, *mg), (r2a, r2b) = mixer_bwd(z, cv, dmc, mp, l, seq=seq, name=f"mixer_bwd_{l}",
                                          pieces=[reduce_ici(h_gu), reduce_ici(h_out)])
        s_gu = sum_chips(h_gu, r2a[0], place_arr, name=f"sum_chips_w_gate_up_{l}")
        s_out = sum_chips(h_out, r2b[0], place_arr, name=f"sum_chips_w_out_{l}")
        dcw, dcb, dclg, dclb, dwpw, dslg, dslb, dws, dbs, dwp, dps = mg
        g_pw = halves(dwpw.astype(BF16))
        g_in, (ea, eb, r1) = mm_tn(dz, h1, name=f"in_proj_dw_{l}", tk=D_IN // 2, tn=D_MODEL, tt=TT, out_dtype=BF16,
                                   pieces=[exchange_d2d(s_gu), exchange_d2d(s_out), reduce_d2d(g_pw)])
        reduced["w_gate_up", l], reduced["w_out", l] = ea[0], eb[0]
        h_pw = add_core_halves(g_pw, r1[0], core_arr, name=f"add_cores_w_pw_{l}")
        g_in = halves(g_in)
        norm_g = p["norm1_g"][l][None]
        if l > 0:
            (dx, dxb, dn1), (r2, r1) = mm_norm_bwd(dz, full["w_in", l], x0, norm_g, dx, name=f"in_proj_dx_{l}", tm=512,
                                                   pieces=[reduce_ici(h_pw), reduce_d2d(g_in)])
            s_pw = sum_chips(h_pw, r2[0], place_arr, name=f"sum_chips_w_pw_{l}")
            h_in = add_core_halves(g_in, r1[0], core_arr, name=f"add_cores_w_in_{l}")
            carry = (s_pw, h_in)
        else:
            tm = min(512, t // 2)
            half = t // tm // 2
            before, (r2, r1) = mm_norm_bwd(dz, full["w_in", l], x0, norm_g, dx, name="in_proj_dx_0", tm=tm,
                                           blocks=(0, half), pieces=[reduce_ici(h_pw), reduce_d2d(g_in)])
            s_pw = sum_chips(h_pw, r2[0], place_arr, name="sum_chips_w_pw_0")
            h_in = add_core_halves(g_in, r1[0], core_arr, name="add_cores_w_in_0")
            (dx, dxb, dn1), (e, r2) = mm_norm_bwd(dz, full["w_in", l], x0, norm_g, dx, name="in_proj_dx_0_rest", tm=tm,
                                                  blocks=(half, t // tm - half), before=before,
                                                  pieces=[exchange_d2d(s_pw), reduce_ici(h_in)])
            reduced["w_pw", 0] = e[0]
            s_in = sum_chips(h_in, r2[0], place_arr, name="sum_chips_w_in_0")
        raw_small[l] = [dcw, dcb, dclg, dclb, dslg, dslb, dws, dbs, dwp, dps, dn1, dn2]
    grad_x = dx.reshape(bsz, seq, d)

    summed, co = allreduce_small(_pack(raw_small[0] + raw_small[1] + [d_final_g, loss]), name="allreduce_small",
                                 pieces=[exchange_d2d(s_in)])
    reduced["w_in", 0] = co[0][0]
    grad, loss = _small_grads(summed.reshape(-1))
    grad["conv_w"] = lax.dynamic_slice_in_dim(grad["conv_w"], chip * cw_cols, cw_cols, axis=2)

    delta, new_m, new_v = {}, {}, {}

    def flat2(a):
        return a.reshape(-1, a.shape[-1])

    res, _ = adamw_small(*[[flat2(tt[k]) for k in SMALL] for tt in (w, grad, m, v)], name="adamw_small")
    for out, arrs in zip((delta, new_m, new_v), res):
        out.update({k: a.reshape(w[k].shape) for k, a in zip(SMALL, arrs)})
    for k in BIG:
        g = jnp.stack([reduced[k, l].reshape(-1, reduced[k, l].shape[-1]) for l in range(DEPTH)])
        grad[k] = g.transpose(0, 2, 1) if TRANSPOSED[k] else g
        (delta[k], new_m[k], new_v[k]), _ = adamw(w[k], grad[k], m[k], v[k], name=f"adamw_{k}")
    return (loss, grad_x, *[grad[k] for k in WEIGHTS], *[delta[k] for k in WEIGHTS],
            *[new_m[k] for k in WEIGHTS], *[new_v[k] for k in WEIGHTS])
```

```python
import functools
import math

import jax
import jax.numpy as jnp
from jax import lax
from jax.experimental import pallas as pl
from jax.experimental.pallas import tpu as pltpu

F32 = jnp.float32
BF16 = jnp.bfloat16

D_MODEL = 1024
DEPTH = 2
D_A = 384
D_B = 384
D_C = 256
D_IN = 2 * D_A + 2 * D_B + D_C
N_HEADS_B = 4
HEAD_DIM_B = 96
GROUP_DIM_C = 64
CONV_WIDTH = 31
CHUNK = 128
D_FF = 2816
RMS_EPS = 1e-6
LN_EPS = 1e-5
HALO = 32
TT = 2048
N_CHIPS = 4
N_DEV = 8
LANES = 128

ADAM_LR = 0.001
ADAM_B1 = 0.9
ADAM_B2 = 0.999
ADAM_EPS = 1e-08
ADAM_WD = 0.01
ADAM_STEP = 10

VMEM_LIMIT = 56 * 1024 * 1024

MESH = pl.DeviceIdType.MESH
ANY = pl.BlockSpec(memory_space=pl.ANY)
SDS = jax.ShapeDtypeStruct


def _sigmoid(x):
    return 0.5 * jnp.tanh(0.5 * x) + 0.5


_GELU_C = math.sqrt(2.0 / math.pi)


def _gelu(x):
    return 0.5 * x * (1.0 + jnp.tanh(_GELU_C * (x + 0.044715 * x * x * x)))


def _gelu_grad(x):
    t = jnp.tanh(_GELU_C * (x + 0.044715 * x * x * x))
    return 0.5 * (1.0 + t) + 0.5 * x * (1.0 - t * t) * _GELU_C * (1.0 + 3 * 0.044715 * x * x)


def _dot(a, b):
    return jnp.dot(a, b, preferred_element_type=F32)


def _dot_nt(a, b):
    return lax.dot_general(a, b, (((1,), (1,)), ((), ())), preferred_element_type=F32)


def _dot_tn(a, b):
    return lax.dot_general(a, b, (((0,), (0,)), ((), ())), preferred_element_type=F32)


class Piece:
    def __init__(self, operands, out_shapes, aliases, n_sems, start, finish):
        self.operands, self.out_shapes, self.aliases, self.n_sems = operands, out_shapes, aliases, n_sems
        self.start, self.finish = start, finish


def _place():
    x, y, c = lax.axis_index("x"), lax.axis_index("y"), lax.axis_index("c")
    chips = [(1 - x, y), (x, 1 - y), (1 - x, 1 - y)]
    return x, y, c, chips


def _remote(src, dst, send_sem, recv_sem, to):
    return pltpu.make_async_remote_copy(src_ref=src, dst_ref=dst, send_sem=send_sem, recv_sem=recv_sem,
                                        device_id=to, device_id_type=MESH)


def _rows(r, q, nq):
    return pl.ds(q * (r // nq), r // nq)


def gather_ici(src, g=None, q=0, nq=1):
    rows = _rows(src.shape[1], q, nq)

    def copies(ins, outs, sem):
        x, y, c, chips = _place()
        j = 2 * x + y
        return [(_remote(ins[0].at[c, rows], outs[0].at[j, c, rows], sem(k), sem(3 + k), (px, py, c)),
                 outs[0].at[2 * px + py, c, rows], (px, py, c)) for k, (px, py) in enumerate(chips)]

    def start(ins, outs, sem):
        for cp, _, _ in copies(ins, outs, sem):
            cp.start()

    def finish(ins, outs, sem):
        cps = copies(ins, outs, sem)
        for k, (_, landed, frm) in enumerate(cps):
            _remote(landed, landed, sem(k), sem(3 + k), frm).wait_recv()
        for cp, _, _ in cps:
            cp.wait_send()

    shape = SDS((N_CHIPS,) + src.shape, src.dtype)
    if g is None:
        return Piece([src], [shape], {}, 6, start, finish)
    return Piece([src, g], [shape], {1: 0}, 6, start, finish)


def gather_d2d(g, q=0, nq=1, sem0=0):
    rows = _rows(g.shape[2], q, nq)

    def copies(outs, sem):
        x, y, c, chips = _place()
        return [(_remote(outs[0].at[2 * px + py, c, rows], outs[0].at[2 * px + py, c, rows],
                         sem(sem0 + k), sem(sem0 + 3 + k), (x, y, 1 - c)),
                 outs[0].at[2 * px + py, 1 - c, rows], (x, y, 1 - c)) for k, (px, py) in enumerate(chips)]

    def start(ins, outs, sem):
        for cp, _, _ in copies(outs, sem):
            cp.start()

    def finish(ins, outs, sem):
        cps = copies(outs, sem)
        for k, (_, landed, frm) in enumerate(cps):
            _remote(landed, landed, sem(sem0 + k), sem(sem0 + 3 + k), frm).wait_recv()
        for cp, _, _ in cps:
            cp.wait_send()

    return Piece([g], [SDS(g.shape, g.dtype)], {0: 0}, 6, start, finish)


def gather_both(src, g=None, q=0, nq=1):
    a = gather_ici(src, g, q, nq)
    b = gather_d2d(jax.ShapeDtypeStruct((N_CHIPS,) + src.shape, src.dtype), q, nq, sem0=6)

    def finish(ins, outs, sem):
        a.finish(ins, outs, sem)
        b.start(ins, outs, sem)
        b.finish(ins, outs, sem)

    return Piece(a.operands, a.out_shapes, a.aliases, 12, a.start, finish)


def gather_stages(src, g, stages, nq):
    shape = SDS((N_CHIPS,) + src.shape, src.dtype)
    make = {"ici": lambda q: gather_ici(src, None, q, nq), "d2d": lambda q: gather_d2d(shape, q, nq),
            "both": lambda q: gather_both(src, None, q, nq)}
    subs, n_sems = [], 0
    for stage, q in stages:
        sub = make[stage](q)
        subs.append((sub, n_sems))
        n_sems += sub.n_sems

    def run(method):
        def go(ins, outs, sem):
            for sub, base in subs:
                getattr(sub, method)(ins, outs, functools.partial(lambda k, base: sem(base + k), base=base))
        return go

    if g is None:
        return Piece([src], [shape], {}, n_sems, run("start"), run("finish"))
    return Piece([src, g], [shape], {1: 0}, n_sems, run("start"), run("finish"))


def reduce_d2d(g):
    def copies(ins, outs, sem):
        x, y, c, _ = _place()
        return [_remote(ins[0].at[j, 1 - c], outs[0].at[j], sem(j), sem(4 + j), (x, y, 1 - c)) for j in range(N_CHIPS)]

    def start(ins, outs, sem):
        for cp in copies(ins, outs, sem):
            cp.start()

    def finish(ins, outs, sem):
        cps = copies(ins, outs, sem)
        for cp in cps:
            cp.wait_recv()
        for cp in cps:
            cp.wait_send()

    return Piece([g], [SDS((N_CHIPS,) + g.shape[2:], g.dtype)], {}, 8, start, finish)


def reduce_ici(h):
    def copies(ins, outs, sem):
        x, y, c, chips = _place()
        j = 2 * x + y
        return [(_remote(ins[0].at[2 * px + py], outs[0].at[j], sem(k), sem(3 + k), (px, py, c)),
                 outs[0].at[2 * px + py], (px, py, c)) for k, (px, py) in enumerate(chips)]

    def start(ins, outs, sem):
        for cp, _, _ in copies(ins, outs, sem):
            cp.start()

    def finish(ins, outs, sem):
        cps = copies(ins, outs, sem)
        for k, (_, landed, frm) in enumerate(cps):
            _remote(landed, landed, sem(k), sem(3 + k), frm).wait_recv()
        for cp, _, _ in cps:
            cp.wait_send()

    return Piece([h], [SDS(h.shape, h.dtype)], {}, 6, start, finish)


def exchange_d2d(g):
    def copy(outs, sem):
        x, y, c, _ = _place()
        return _remote(outs[0].at[c], outs[0].at[c], sem(0), sem(1), (x, y, 1 - c)), outs[0].at[1 - c], (x, y, 1 - c)

    def start(ins, outs, sem):
        copy(outs, sem)[0].start()

    def finish(ins, outs, sem):
        cp, landed, frm = copy(outs, sem)
        _remote(landed, landed, sem(0), sem(1), frm).wait_recv()
        cp.wait_send()

    return Piece([g], [SDS(g.shape, g.dtype)], {0: 0}, 2, start, finish)


def _call(body, *, name, grid, in_specs, out_specs, out_shape, args, scratch_shapes=(), pieces=(), prefetch=0,
          in_place=None):
    n_in, n_out, n_scr = len(in_specs), len(out_specs), len(scratch_shapes)
    c_ops = [a for p in pieces for a in p.operands]
    c_outs = [s for p in pieces for s in p.out_shapes]
    n_sems = sum(p.n_sems for p in pieces)
    aliases = {prefetch + i: o for i, o in (in_place or {}).items()}
    op_off, out_off = prefetch + n_in, n_out
    for p in pieces:
        for i, o in p.aliases.items():
            aliases[op_off + i] = out_off + o
        op_off += len(p.operands)
        out_off += len(p.out_shapes)

    def wrapped(*refs):
        pre, refs = refs[:prefetch], refs[prefetch:]
        ins, cin = refs[:n_in], refs[n_in:n_in + len(c_ops)]
        o0 = n_in + len(c_ops)
        outs, cout = refs[o0:o0 + n_out], refs[o0 + n_out:o0 + n_out + len(c_outs)]
        s0 = o0 + n_out + len(c_outs)
        scr = refs[s0:s0 + n_scr]

        def each(method):
            sems = refs[s0 + n_scr]
            i_off = o_off = s_off = 0
            for p in pieces:
                getattr(p, method)(cin[i_off:i_off + len(p.operands)], cout[o_off:o_off + len(p.out_shapes)],
                                   functools.partial(lambda k, base: sems.at[base + k], base=s_off))
                i_off, o_off, s_off = i_off + len(p.operands), o_off + len(p.out_shapes), s_off + p.n_sems

        if pieces and grid:
            ids = [pl.program_id(a) for a in range(len(grid))]
            first = functools.reduce(jnp.logical_and, [i == 0 for i in ids])
            last = functools.reduce(jnp.logical_and, [i == g - 1 for i, g in zip(ids, grid)])
            pl.when(first)(lambda: each("start"))
        elif pieces:
            each("start")
        if body is not None:
            body(*pre, *ins, *outs, *scr)
        if pieces and grid:
            pl.when(last)(lambda: each("finish"))
        elif pieces:
            each("finish")

    scratch = list(scratch_shapes) + ([pltpu.SemaphoreType.DMA((n_sems,))] if pieces else [])
    all_in = list(in_specs) + [ANY] * len(c_ops)
    all_out = list(out_specs) + [ANY] * len(c_outs)
    shapes = list(out_shape) + c_outs
    kw = dict(name=name, out_shape=shapes, input_output_aliases=aliases)
    if grid:
        kw["compiler_params"] = pltpu.CompilerParams(dimension_semantics=("arbitrary",) * len(grid),
                                                     vmem_limit_bytes=VMEM_LIMIT)
    if prefetch:
        kw["grid_spec"] = pltpu.PrefetchScalarGridSpec(num_scalar_prefetch=prefetch, grid=grid, in_specs=all_in,
                                                       out_specs=all_out, scratch_shapes=scratch)
    else:
        kw.update(in_specs=all_in, out_specs=all_out, scratch_shapes=scratch)
        if grid:
            kw["grid"] = grid
    res = pl.pallas_call(wrapped, **kw)(*args, *c_ops)
    outs, rest = list(res[:n_out]), list(res[n_out:])
    couts = []
    for p in pieces:
        couts.append(rest[:len(p.out_shapes)])
        rest = rest[len(p.out_shapes):]
    return outs, couts


def comm_only(pieces, *, name):
    return _call(None, name=name, grid=(), in_specs=[], out_specs=[], out_shape=[], args=[], pieces=pieces)[1]


def rmsnorm_fwd(x, g, *, name, pieces=()):
    t, d = x.shape
    tm = min(512, t)

    def body(x_ref, g_ref, o_ref):
        xv = x_ref[...]
        rstd = lax.rsqrt(jnp.mean(xv * xv, axis=-1, keepdims=True) + RMS_EPS)
        o_ref[...] = (xv * rstd * g_ref[...]).astype(BF16)

    outs, couts = _call(
        body, name=name, grid=(t // tm,),
        in_specs=[pl.BlockSpec((tm, d), lambda i: (i, 0)), pl.BlockSpec((1, d), lambda i: (0, 0))],
        out_specs=[pl.BlockSpec((tm, d), lambda i: (i, 0))], out_shape=[SDS((t, d), BF16)],
        args=[x, g], pieces=pieces)
    return outs[0], couts


def loss_head(x, g, target, *, name):
    t, d = x.shape
    tm = min(512, t)

    def body(x_ref, g_ref, t_ref, loss_ref, dx_ref, dxb_ref, dg_ref):
        i = pl.program_id(0)
        xv = x_ref[...]
        gv = g_ref[...]
        rstd = lax.rsqrt(jnp.mean(xv * xv, axis=-1, keepdims=True) + RMS_EPS)
        xhat = xv * rstd
        err = xhat * gv - t_ref[...]
        dy = err * (1.0 / d)
        dxhat = dy * gv
        dx = rstd * (dxhat - xhat * jnp.mean(dxhat * xhat, axis=-1, keepdims=True))
        dx_ref[...] = dx
        dxb_ref[...] = dx.astype(BF16)

        @pl.when(i == 0)
        def _():
            dg_ref[...] = jnp.zeros_like(dg_ref)
            loss_ref[...] = jnp.zeros_like(loss_ref)

        dg_ref[...] += jnp.sum(dy * xhat, axis=0, keepdims=True)
        per_tok = jnp.sum(err * err, axis=-1, keepdims=True) * (0.5 / d)
        loss_ref[...] += jnp.sum(per_tok, axis=0, keepdims=True)

    row = pl.BlockSpec((tm, d), lambda i: (i, 0))
    vec = pl.BlockSpec((1, d), lambda i: (0, 0))
    one = pl.BlockSpec((1, 1), lambda i: (0, 0))
    outs, _ = _call(
        body, name=name, grid=(t // tm,), in_specs=[row, vec, row], out_specs=[one, row, row, vec],
        out_shape=[SDS((1, 1), F32), SDS((t, d), F32), SDS((t, d), BF16), SDS((1, d), F32)],
        args=[x, g, target])
    return outs


def mm_nn(a, b, *, name, tm, tn, out_dtype, residual=None, pieces=()):
    m, k = a.shape
    n = b.shape[1]
    tm, tn = min(tm, m), min(tn, n)
    has_res = residual is not None

    def body(a_ref, b_ref, *rest):
        o_ref = rest[-1]
        acc = _dot(a_ref[...], b_ref[...])
        if has_res:
            acc = acc + rest[0][...]
        o_ref[...] = acc.astype(o_ref.dtype)

    in_specs = [pl.BlockSpec((tm, k), lambda j, i: (i, 0)), pl.BlockSpec((k, tn), lambda j, i: (0, j))]
    args = [a, b]
    if has_res:
        in_specs.append(pl.BlockSpec((tm, tn), lambda j, i: (i, j)))
        args.append(residual)
    outs, couts = _call(
        body, name=name, grid=(n // tn, m // tm), in_specs=in_specs,
        out_specs=[pl.BlockSpec((tm, tn), lambda j, i: (i, j))], out_shape=[SDS((m, n), out_dtype)],
        args=args, pieces=pieces)
    return outs[0], couts


def mm_nt(a, b, *, name, tm, tn, out_dtype, pieces=()):
    m, k = a.shape
    n = b.shape[0]
    tm, tn = min(tm, m), min(tn, n)

    def body(a_ref, b_ref, o_ref):
        o_ref[...] = _dot_nt(a_ref[...], b_ref[...]).astype(o_ref.dtype)

    outs, couts = _call(
        body, name=name, grid=(n // tn, m // tm),
        in_specs=[pl.BlockSpec((tm, k), lambda j, i: (i, 0)), pl.BlockSpec((tn, k), lambda j, i: (j, 0))],
        out_specs=[pl.BlockSpec((tm, tn), lambda j, i: (i, j))], out_shape=[SDS((m, n), out_dtype)],
        args=[a, b], pieces=pieces)
    return outs[0], couts


def _rms_norm(xv, gv):
    rstd = lax.rsqrt(jnp.mean(xv * xv, axis=-1, keepdims=True) + RMS_EPS)
    return (xv * rstd * gv).astype(BF16)


def norm_mm_nt(x, g, b, *, name, tm, out_dtype, pieces=()):
    m, k = x.shape
    n = b.shape[0]
    tm = min(tm, m)

    def body(x_ref, g_ref, b_ref, o_ref, h_ref):
        h = _rms_norm(x_ref[...], g_ref[...])
        h_ref[...] = h
        o_ref[...] = _dot_nt(h, b_ref[...]).astype(o_ref.dtype)

    outs, couts = _call(
        body, name=name, grid=(m // tm,),
        in_specs=[pl.BlockSpec((tm, k), lambda i: (i, 0)), pl.BlockSpec((1, k), lambda i: (0, 0)),
                  pl.BlockSpec((n, k), lambda i: (0, 0))],
        out_specs=[pl.BlockSpec((tm, n), lambda i: (i, 0)), pl.BlockSpec((tm, k), lambda i: (i, 0))],
        out_shape=[SDS((m, n), out_dtype), SDS((m, k), BF16)], args=[x, g, b], pieces=pieces)
    return outs, couts


def mm_tn(a, b, *, name, tk, tn, tt, out_dtype, pieces=()):
    t, k = a.shape
    n = b.shape[1]
    tk, tn, tt = min(tk, k), min(tn, n), min(tt, t)
    nt = t // tt

    def body(a_ref, b_ref, o_ref, acc_ref):
        s = pl.program_id(2)

        @pl.when(s == 0)
        def _():
            acc_ref[...] = jnp.zeros_like(acc_ref)

        acc_ref[...] += _dot_tn(a_ref[...], b_ref[...])

        @pl.when(s == nt - 1)
        def _():
            o_ref[...] = acc_ref[...].astype(o_ref.dtype)

    outs, couts = _call(
        body, name=name, grid=(k // tk, n // tn, nt),
        in_specs=[pl.BlockSpec((tt, tk), lambda i, j, s: (s, i)), pl.BlockSpec((tt, tn), lambda i, j, s: (s, j))],
        out_specs=[pl.BlockSpec((tk, tn), lambda i, j, s: (i, j))], out_shape=[SDS((k, n), out_dtype)],
        scratch_shapes=[pltpu.VMEM((tk, tn), F32)], args=[a, b], pieces=pieces)
    return outs[0], couts


def swiglu_fwd(x, g, wt, *, name, tm, tn, pieces=()):
    t, d = x.shape
    ff = wt.shape[0] // 2
    tm, tn = min(tm, t), min(tn, ff)
    nb = ff // tn

    nm = t // tm

    def body(x_ref, g_ref, wg_ref, wu_ref, gu_ref, act_ref, h_ref):
        hv = _rms_norm(x_ref[...], g_ref[...])

        @pl.when(pl.program_id(0) == 0)
        def _():
            h_ref[...] = hv

        gate = _dot_nt(hv, wg_ref[...])
        up = _dot_nt(hv, wu_ref[...])
        gu_ref[0] = gate.astype(BF16)
        gu_ref[1] = up.astype(BF16)
        act_ref[...] = (gate * _sigmoid(gate) * up).astype(BF16)

    outs, couts = _call(
        body, name=name, grid=(nb, t // tm),
        in_specs=[pl.BlockSpec((tm, d), lambda j, i: (i, 0)), pl.BlockSpec((1, d), lambda j, i: (0, 0)),
                  pl.BlockSpec((tn, d), lambda j, i: (j, 0)),
                  pl.BlockSpec((tn, d), lambda j, i: (j + nb, 0))],
        out_specs=[pl.BlockSpec((2, tm, tn), lambda j, i: (0, i, j)), pl.BlockSpec((tm, tn), lambda j, i: (i, j)),
                   pl.BlockSpec((tm, d), lambda j, i: (jnp.where(j == 0, i, nm - 1), 0))],
        out_shape=[SDS((2, t, ff), BF16), SDS((t, ff), BF16), SDS((t, d), BF16)], args=[x, g, wt, wt],
        pieces=pieces)
    return outs, couts


def swiglu_bwd(dx, w_down, gu, *, name, tm, tn, pieces=()):
    t, d = dx.shape
    ff = w_down.shape[0]
    tm, tn = min(tm, t), min(tn, ff)

    rb = 16

    def body(dx_ref, w_ref, gu_ref, o_ref, dact_ref):
        dact_ref[...] = _dot_nt(dx_ref[...], w_ref[...])

        def tail(r, carry):
            rows = pl.ds(pl.multiple_of(r * rb, rb), rb)
            dact = dact_ref[rows, :]
            gate = gu_ref[0, rows, :].astype(F32)
            up = gu_ref[1, rows, :].astype(F32)
            sg = _sigmoid(gate)
            o_ref[0, rows, :] = (dact * up * sg * (1.0 + gate * (1.0 - sg))).astype(BF16)
            o_ref[1, rows, :] = (dact * gate * sg).astype(BF16)
            return carry

        lax.fori_loop(0, tm // rb, tail, 0)

    outs, couts = _call(
        body, name=name, grid=(ff // tn, t // tm),
        in_specs=[pl.BlockSpec((tm, d), lambda j, i: (i, 0)), pl.BlockSpec((tn, d), lambda j, i: (j, 0)),
                  pl.BlockSpec((2, tm, tn), lambda j, i: (0, i, j))],
        out_specs=[pl.BlockSpec((2, tm, tn), lambda j, i: (0, i, j))], out_shape=[SDS((2, t, ff), BF16)],
        scratch_shapes=[pltpu.VMEM((tm, tn), F32)], args=[dx, w_down, gu], pieces=pieces)
    return outs[0], couts


def mm_norm_bwd(a, wt, x, g, dres, *, name, tm, pieces=(), blocks=None, before=None):
    parts = a.shape[0] if a.ndim == 3 else 1
    t, kp = a.shape[-2:]
    d = wt.shape[1]
    tm = min(tm, t)
    first, count = blocks if blocks else (0, t // tm)

    def body(a_ref, w_ref, x_ref, g_ref, dres_ref, *rest):
        dx_ref, dxb_ref, dg_ref = rest[-3:]
        i = pl.program_id(0)
        if parts == 1:
            dh = _dot(a_ref[...], w_ref[...])
        else:
            dh = _dot(a_ref[0], w_ref[0:kp, :])
            for q in range(1, parts):
                dh = dh + _dot(a_ref[q], w_ref[q * kp:(q + 1) * kp, :])
        xv = x_ref[...]
        rstd = lax.rsqrt(jnp.mean(xv * xv, axis=-1, keepdims=True) + RMS_EPS)
        xhat = xv * rstd
        dxhat = dh * g_ref[...]
        dx = dres_ref[...] + rstd * (dxhat - xhat * jnp.mean(dxhat * xhat, axis=-1, keepdims=True))
        dx_ref[...] = dx
        dxb_ref[...] = dx.astype(BF16)

        @pl.when(i == 0)
        def _():
            dg_ref[...] = rest[2][...] if before else jnp.zeros_like(dg_ref)

        dg_ref[...] += jnp.sum(dh * xhat, axis=0, keepdims=True)

    a_spec = (pl.BlockSpec((tm, kp), lambda i: (i + first, 0)) if parts == 1
              else pl.BlockSpec((parts, tm, kp), lambda i: (0, i + first, 0)))
    row = pl.BlockSpec((tm, d), lambda i: (i + first, 0))
    vec = pl.BlockSpec((1, d), lambda i: (0, 0))
    in_specs = [a_spec, pl.BlockSpec((parts * kp, d), lambda i: (0, 0)), row, vec, row]
    args = [a, wt, x, g, dres]
    if before:
        in_specs += [ANY, ANY, vec]
        args += list(before)
    outs, couts = _call(
        body, name=name, grid=(count,), in_specs=in_specs,
        out_specs=[row, row, vec], out_shape=[SDS((t, d), F32), SDS((t, d), BF16), SDS((1, d), F32)],
        args=args, pieces=pieces, in_place={5: 0, 6: 1} if before else None)
    return outs, couts


def mm_gu_tn(dgu, h, *, name, tn, tt, pieces=()):
    t, d = h.shape
    ff = dgu.shape[2]
    tn, tt = min(tn, ff), min(tt, t)
    nb = ff // tn
    nt = t // tt

    def body(a_ref, h_ref, o_ref, acc_ref):
        s = pl.program_id(1)

        @pl.when(s == 0)
        def _():
            acc_ref[...] = jnp.zeros_like(acc_ref)

        acc_ref[...] += _dot_tn(a_ref[...], h_ref[...])

        @pl.when(s == nt - 1)
        def _():
            o_ref[...] = acc_ref[...].astype(o_ref.dtype)

    outs, couts = _call(
        body, name=name, grid=(2 * nb, nt),
        in_specs=[pl.BlockSpec((None, tt, tn), lambda j, s: (j // nb, s, j % nb)),
                  pl.BlockSpec((tt, d), lambda j, s: (s, 0))],
        out_specs=[pl.BlockSpec((tn, d), lambda j, s: (j, 0))], out_shape=[SDS((2 * ff, d), BF16)],
        scratch_shapes=[pltpu.VMEM((tn, d), F32)], args=[dgu, h], pieces=pieces)
    return outs[0], couts


def _head_masks(shape):
    lane = lax.broadcasted_iota(jnp.int32, shape, 1)
    return [(lane >= h * HEAD_DIM_B) & (lane < (h + 1) * HEAD_DIM_B) for h in range(N_HEADS_B)]


def _causal_rows(ws):
    r = lax.broadcasted_iota(jnp.int32, ws.shape, 0) % CHUNK
    c = lax.broadcasted_iota(jnp.int32, ws.shape, 1)
    return jnp.where(c <= r, ws, jnp.zeros_like(ws))


def _pool_window(shape):
    lane = lax.broadcasted_iota(jnp.int32, shape, 1)
    return jnp.left_shift(2, lane // GROUP_DIM_C)


def _layer_norm_fwd(x, g, b):
    mu = jnp.mean(x, axis=-1, keepdims=True)
    xc = x - mu
    rstd = lax.rsqrt(jnp.mean(xc * xc, axis=-1, keepdims=True) + LN_EPS)
    xhat = xc * rstd
    return xhat * g + b, xhat, rstd


def _layer_norm_bwd(dy, xhat, rstd, g):
    dxhat = dy * g
    return rstd * (dxhat - jnp.mean(dxhat, axis=-1, keepdims=True)
                   - xhat * jnp.mean(dxhat * xhat, axis=-1, keepdims=True))


def _gate_mix(ws_masked, vl_chunk, masks):
    out = _dot(ws_masked, vl_chunk.astype(BF16))
    s = jnp.zeros((CHUNK, D_B), F32)
    for h in range(N_HEADS_B):
        s = s + jnp.where(masks[h], out[h * CHUNK:(h + 1) * CHUNK], 0.0)
    return s


SUB = 8
CONV_ROWS = 64


def _shifted_copies(ref, n):
    for b in range(1, SUB):
        ref[b, 0:n, :] = ref[0, pl.ds(b, n), :]


def _tap(ref, off, n):
    a, b = divmod(off, SUB)
    return ref[b, pl.ds(SUB * a, n), :]


def _pick_window(s2, s4, s8, s16):
    grp = lax.broadcasted_iota(jnp.int32, s2.shape, 1) // GROUP_DIM_C
    return jnp.where(grp == 0, s2, jnp.where(grp == 1, s4, jnp.where(grp == 2, s8, s16)))


def _trailing_window_sums(src_ref, l2_ref, l4_ref, l8_ref, tm):
    n = HALO + tm
    l2_ref[8:n, :] = src_ref[8:n, :] + src_ref[pl.ds(7, n - 8), :]
    l4_ref[16:n, :] = l2_ref[16:n, :] + l2_ref[pl.ds(14, n - 16), :]
    l8_ref[24:n, :] = l4_ref[24:n, :] + l4_ref[pl.ds(20, n - 24), :]
    s16 = l8_ref[HALO:n, :] + l8_ref[HALO - 8:n - 8, :]
    return _pick_window(l2_ref[HALO:n, :], l4_ref[HALO:n, :], l8_ref[HALO:n, :], s16)


def _leading_window_sums(src_ref, l2_ref, l4_ref, l8_ref, tm):
    n = HALO + tm
    l2_ref[0:n - 8, :] = src_ref[0:n - 8, :] + src_ref[pl.ds(1, n - 8), :]
    l4_ref[0:n - 16, :] = l2_ref[0:n - 16, :] + l2_ref[pl.ds(2, n - 16), :]
    l8_ref[0:n - 24, :] = l4_ref[0:n - 24, :] + l4_ref[pl.ds(4, n - 24), :]
    s16 = l8_ref[0:tm, :] + l8_ref[8:tm + 8, :]
    return _pick_window(l2_ref[0:tm, :], l4_ref[0:tm, :], l8_ref[0:tm, :], s16)


def _mixer_specs(tm, nt, seq):
    hb = tm // HALO

    def cur(c):
        return pl.BlockSpec((tm, c), lambda b, i: (b * nt + i, 0))

    def prev(c):
        return pl.BlockSpec((HALO, c), lambda b, i: (jnp.maximum((b * nt + i) * hb - 1, 0), 0))

    def nxt(c):
        last = (2 * seq) // HALO - 1
        return pl.BlockSpec((HALO, c), lambda b, i: (jnp.minimum((b * nt + i + 1) * hb, last), 0))

    def full(shape, layer=None):
        if layer is None:
            return pl.BlockSpec(shape, lambda b, i: tuple(0 for _ in shape))
        return pl.BlockSpec((None,) + shape, lambda b, i: (layer,) + tuple(0 for _ in shape))

    return cur, prev, nxt, full


_MIX_PARAM_SHAPES = [(32, D_A), (1, D_A), (1, D_A), (1, D_A), (D_A, D_A), (1, D_B), (1, D_B),
                     (N_HEADS_B * CHUNK, CHUNK), (CHUNK, D_B), (D_C, D_C), (1, D_C)]


def mixer_fwd(z, mp, layer, *, seq, name, tm=512, pieces=()):
    t = z.shape[0]
    tm = min(tm, seq)
    nt = seq // tm
    cur, prev, _, full = _mixer_specs(tm, nt, seq)

    def body(zc_ref, zp_ref, cw_ref, cb_ref, clg_ref, clb_ref, wpw_ref, slg_ref, slb_ref, ws_ref, bias_ref,
             wp_ref, ps_ref, o_ref, cv_ref, ys_ref, zs_ref, l2_ref, l4_ref, l8_ref):
        i = pl.program_id(1)
        has_prev = i > 0
        yp = zp_ref[:, 0:D_A].astype(F32) * _sigmoid(zp_ref[:, D_A:2 * D_A].astype(F32))
        ys_ref[0, 0:HALO, :] = jnp.where(has_prev, yp, 0.0)
        ys_ref[0, HALO:HALO + tm, :] = zc_ref[:, 0:D_A].astype(F32) * _sigmoid(zc_ref[:, D_A:2 * D_A].astype(F32))
        _shifted_copies(ys_ref, tm + HALO - SUB)
        acc = jnp.zeros((tm, D_A), F32) + cb_ref[...]
        for k in range(CONV_WIDTH):
            acc = acc + cw_ref[k:k + 1, :] * _tap(ys_ref, HALO - (CONV_WIDTH - 1) + k, tm)
        cv_ref[...] = acc
        ln, _, _ = _layer_norm_fwd(acc, clg_ref[...], clb_ref[...])
        sl = ln * _sigmoid(ln)
        o_ref[:, 0:D_A] = _dot(sl.astype(BF16), wpw_ref[...]).astype(BF16)
        gz = _gelu(zc_ref[:, 2 * D_A:2 * D_A + 2 * D_B].astype(F32))
        u = gz[:, :D_B]
        vl, _, _ = _layer_norm_fwd(gz[:, D_B:], slg_ref[...], slb_ref[...])
        wsm = _causal_rows(ws_ref[...])
        masks = _head_masks((CHUNK, D_B))
        for c in range(tm // CHUNK):
            rows = slice(c * CHUNK, (c + 1) * CHUNK)
            s = _gate_mix(wsm, vl[rows], masks) + bias_ref[...]
            o_ref[rows, D_A:D_A + D_B] = (u[rows] * s).astype(BF16)
        c0 = 2 * D_A + 2 * D_B
        zs_ref[0:HALO, :] = jnp.where(has_prev, zp_ref[:, c0:c0 + D_C].astype(F32), 0.0)
        zcur = zc_ref[:, c0:c0 + D_C].astype(F32)
        zs_ref[HALO:HALO + tm, :] = zcur
        win = _pool_window((tm, D_C))
        wsum = _trailing_window_sums(zs_ref, l2_ref, l4_ref, l8_ref, tm)
        pos = i * tm + lax.broadcasted_iota(jnp.int32, (tm, D_C), 0)
        cnt = jnp.minimum(pos + 1, win).astype(F32)
        p = wsum / cnt - zcur
        y = _dot(p.astype(BF16), wp_ref[...])
        o_ref[:, D_A + D_B:D_A + D_B + D_C] = (y * ps_ref[...]).astype(BF16)

    in_specs = [cur(D_IN), prev(D_IN)] + [full(s, layer) for s in _MIX_PARAM_SHAPES]
    outs, couts = _call(
        body, name=name, grid=(2, nt), in_specs=in_specs, out_specs=[cur(D_MODEL), cur(D_A)],
        out_shape=[SDS((t, D_MODEL), BF16), SDS((t, D_A), F32)],
        scratch_shapes=[pltpu.VMEM((SUB, HALO + tm, D_A), F32)] + [pltpu.VMEM((HALO + tm, D_C), F32)] * 4,
        args=[z, z, *mp], pieces=pieces)
    return outs, couts


def mixer_bwd(z, conv, dm, mp, layer, *, seq, name, tm=256, pieces=()):
    t = z.shape[0]
    tm = min(tm, seq)
    nt = seq // tm
    ext = tm + HALO
    cur, prev, nxt, full = _mixer_specs(tm, nt, seq)
    grad_shapes = [(32, D_A), (1, D_A), (1, D_A), (1, D_A), (D_A, D_A), (1, D_B), (1, D_B),
                   (N_HEADS_B * CHUNK, CHUNK), (CHUNK, CHUNK), (D_C, D_C), (1, D_C)]

    def body(zc_ref, zp_ref, cvc_ref, cvn_ref, dmc_ref, dmn_ref, cw_ref, cb_ref, clg_ref, clb_ref, wpw_ref, slg_ref,
             slb_ref, ws_ref, bias_ref, wp_ref, ps_ref,
             dz_ref, dcw_ref, dcb_ref, dclg_ref, dclb_ref, dwpw_ref, dslg_ref, dslb_ref, dws_ref, dbs_ref, dwp_ref,
             dps_ref, ys_ref, dcs_ref, zs_ref, qs_ref, l2_ref, l4_ref, l8_ref):
        b = pl.program_id(0)
        i = pl.program_id(1)
        has_prev = i > 0
        has_next = i < nt - 1
        grads = [dcw_ref, dcb_ref, dclg_ref, dclb_ref, dwpw_ref, dslg_ref, dslb_ref, dws_ref, dbs_ref, dwp_ref, dps_ref]

        @pl.when((b == 0) & (i == 0))
        def _():
            for r in grads:
                r[...] = jnp.zeros_like(r)

        ext_row = lax.broadcasted_iota(jnp.int32, (ext, 1), 0)
        live = (ext_row < tm) | has_next

        yp = zp_ref[:, 0:D_A].astype(F32) * _sigmoid(zp_ref[:, D_A:2 * D_A].astype(F32))
        ys_ref[0, 0:HALO, :] = jnp.where(has_prev, yp, 0.0)
        a_cur = zc_ref[:, 0:D_A].astype(F32)
        sig_cur = _sigmoid(zc_ref[:, D_A:2 * D_A].astype(F32))
        ys_ref[0, HALO:HALO + tm, :] = a_cur * sig_cur
        _shifted_copies(ys_ref, tm + HALO - SUB)
        acc = jnp.concatenate([cvc_ref[...], cvn_ref[...]], axis=0)
        ln, xhat, rstd = _layer_norm_fwd(acc, clg_ref[...], clb_ref[...])
        sg = _sigmoid(ln)
        sl = ln * sg
        dya = jnp.concatenate([dmc_ref[:, 0:D_A], dmn_ref[:, 0:D_A]], axis=0)
        dsl = _dot_nt(dya, wpw_ref[...])
        dln = dsl * sg * (1.0 + ln * (1.0 - sg))
        dc = _layer_norm_bwd(dln, xhat, rstd, clg_ref[...])
        dc = jnp.where(live, dc, 0.0)
        dcs_ref[0] = dc
        _shifted_copies(dcs_ref, ext - SUB)
        dwpw_ref[...] += _dot_tn(sl[:tm].astype(BF16), dya[:tm])
        dclg_ref[...] += jnp.sum(dln[:tm] * xhat[:tm], axis=0, keepdims=True)
        dclb_ref[...] += jnp.sum(dln[:tm], axis=0, keepdims=True)
        dcb_ref[...] += jnp.sum(dc[:tm], axis=0, keepdims=True)
        for k in range(CONV_WIDTH):
            off = HALO - (CONV_WIDTH - 1) + k
            dcw_ref[k:k + 1, :] += jnp.sum(dc[:tm] * _tap(ys_ref, off, tm), axis=0, keepdims=True)
        for r0 in range(0, tm, CONV_ROWS):
            rows = slice(r0, r0 + CONV_ROWS)
            dy = jnp.zeros((CONV_ROWS, D_A), F32)
            for k in range(CONV_WIDTH):
                dy = dy + cw_ref[k:k + 1, :] * _tap(dcs_ref, CONV_WIDTH - 1 - k + r0, CONV_ROWS)
            dz_ref[rows, 0:D_A] = (dy * sig_cur[rows]).astype(BF16)
            dz_ref[rows, D_A:2 * D_A] = (dy * a_cur[rows] * sig_cur[rows] * (1.0 - sig_cur[rows])).astype(BF16)

        zb = zc_ref[:, 2 * D_A:2 * D_A + 2 * D_B].astype(F32)
        gz = _gelu(zb)
        u = gz[:, :D_B]
        vl, vhat, vrstd = _layer_norm_fwd(gz[:, D_B:], slg_ref[...], slb_ref[...])
        dyb = dmc_ref[:, D_A:D_A + D_B].astype(F32)
        wsm = _causal_rows(ws_ref[...])
        masks = _head_masks((CHUNK, D_B))
        ds_all = dyb * u
        du_parts, dvl_parts = [], []
        for c in range(tm // CHUNK):
            rows = slice(c * CHUNK, (c + 1) * CHUNK)
            vlc = vl[rows].astype(BF16)
            s = _gate_mix(wsm, vl[rows], masks) + bias_ref[...]
            du_parts.append(dyb[rows] * s)
            ds = ds_all[rows]
            stack = jnp.concatenate([jnp.where(masks[h], ds, 0.0) for h in range(N_HEADS_B)], axis=0).astype(BF16)
            dvl_parts.append(_dot_tn(wsm, stack))
            dws_ref[...] += _dot_nt(stack, vlc)
        du = jnp.concatenate(du_parts, axis=0)
        dvl = jnp.concatenate(dvl_parts, axis=0)
        dbias = jnp.zeros((CHUNK, D_B), F32)
        for c in range(tm // CHUNK):
            dbias = dbias + ds_all[c * CHUNK:(c + 1) * CHUNK]
        lane = lax.broadcasted_iota(jnp.int32, (CHUNK, CHUNK), 1)
        dbs = jnp.zeros((CHUNK, CHUNK), F32)
        for h in range(N_HEADS_B):
            col = jnp.sum(jnp.where(masks[h], dbias, 0.0), axis=1, keepdims=True)
            dbs = dbs + jnp.where(lane == h, col, 0.0)
        dbs_ref[...] += dbs
        dslg_ref[...] += jnp.sum(dvl * vhat, axis=0, keepdims=True)
        dslb_ref[...] += jnp.sum(dvl, axis=0, keepdims=True)
        dv = _layer_norm_bwd(dvl, vhat, vrstd, slg_ref[...])
        gg = _gelu_grad(zb)
        dz_ref[:, 2 * D_A:2 * D_A + D_B] = (du * gg[:, :D_B]).astype(BF16)
        dz_ref[:, 2 * D_A + D_B:2 * D_A + 2 * D_B] = (dv * gg[:, D_B:]).astype(BF16)

        c0 = 2 * D_A + 2 * D_B
        m0 = D_A + D_B
        zs_ref[0:HALO, :] = jnp.where(has_prev, zp_ref[:, c0:c0 + D_C].astype(F32), 0.0)
        zcur = zc_ref[:, c0:c0 + D_C].astype(F32)
        zs_ref[HALO:HALO + tm, :] = zcur
        win = _pool_window((tm, D_C))
        wsum = _trailing_window_sums(zs_ref, l2_ref, l4_ref, l8_ref, tm)
        pos = i * tm + lax.broadcasted_iota(jnp.int32, (tm, D_C), 0)
        cnt = jnp.minimum(pos + 1, win).astype(F32)
        pb = (wsum / cnt - zcur).astype(BF16)
        y = _dot(pb, wp_ref[...])
        dyc = jnp.concatenate([dmc_ref[:, m0:m0 + D_C], dmn_ref[:, m0:m0 + D_C]], axis=0).astype(F32)
        dps_ref[...] += jnp.sum(dyc[:tm] * y, axis=0, keepdims=True)
        dyv = (dyc * ps_ref[...]).astype(BF16)
        dwp_ref[...] += _dot_tn(pb, dyv[:tm])
        dp = _dot_nt(dyv, wp_ref[...])
        win_e = _pool_window((ext, D_C))
        pos_e = i * tm + lax.broadcasted_iota(jnp.int32, (ext, D_C), 0)
        cnt_e = jnp.minimum(pos_e + 1, win_e).astype(F32)
        qs_ref[...] = jnp.where(live, dp / cnt_e, 0.0)
        dzc = _leading_window_sums(qs_ref, l2_ref, l4_ref, l8_ref, tm) - dp[:tm]
        dz_ref[:, c0:c0 + D_C] = dzc.astype(BF16)

        @pl.when((b == 1) & (i == nt - 1))
        def _():
            dws_ref[...] = _causal_rows(dws_ref[...])

    in_specs = ([cur(D_IN), prev(D_IN), cur(D_A), nxt(D_A), cur(D_MODEL), nxt(D_MODEL)]
                + [full(s, layer) for s in _MIX_PARAM_SHAPES])
    out_specs = [cur(D_IN)] + [full(s) for s in grad_shapes]
    out_shape = [SDS((t, D_IN), BF16)] + [SDS(s, F32) for s in grad_shapes]
    outs, couts = _call(
        body, name=name, grid=(2, nt), in_specs=in_specs, out_specs=out_specs, out_shape=out_shape,
        scratch_shapes=[pltpu.VMEM((SUB, HALO + tm, D_A), F32), pltpu.VMEM((SUB, ext, D_A), F32)]
        + [pltpu.VMEM((ext, D_C), F32)] * 5,
        args=[z, z, conv, conv, dm, dm, *mp], pieces=pieces)
    return outs, couts


N_GROUPS_C = D_C // GROUP_DIM_C


def _mixer_params(p, w_pw_bf16):
    eye = jnp.eye(N_GROUPS_C, dtype=F32)
    wp_bd = jnp.einsum("lgio,gh->lgiho", p["w_pool"], eye).reshape(DEPTH, D_C, D_C)
    return [
        jnp.pad(p["conv_w"], ((0, 0), (0, 32 - CONV_WIDTH), (0, 0))),
        p["conv_b"][:, None], p["conv_ln_g"][:, None], p["conv_ln_b"][:, None],
        w_pw_bf16,
        p["sg_ln_g"][:, None], p["sg_ln_b"][:, None],
        p["w_s"].reshape(DEPTH, N_HEADS_B * CHUNK, CHUNK).astype(BF16),
        jnp.repeat(p["b_s"].transpose(0, 2, 1), HEAD_DIM_B, axis=2),
        wp_bd.astype(BF16),
        p["pool_scale"][:, None],
    ]


RAW_SMALL = [("conv_w", (32, D_A)), ("conv_b", (1, D_A)), ("conv_ln_g", (1, D_A)), ("conv_ln_b", (1, D_A)),
             ("sg_ln_g", (1, D_B)), ("sg_ln_b", (1, D_B)), ("w_s", (N_HEADS_B * CHUNK, CHUNK)), ("b_s", (CHUNK, CHUNK)),
             ("w_pool", (D_C, D_C)), ("pool_scale", (1, D_C)), ("norm1_g", (1, D_MODEL)), ("norm2_g", (1, D_MODEL))]


def _small_grads(flat):
    per_layer = sum(math.prod(s) for _, s in RAW_SMALL)
    layers = flat[:DEPTH * per_layer].reshape(DEPTH, per_layer)
    raw, off = {}, 0
    for k, s in RAW_SMALL:
        raw[k] = layers[:, off:off + math.prod(s)].reshape((DEPTH,) + s)
        off += math.prod(s)
    eye = jnp.eye(N_GROUPS_C, dtype=F32)
    g = {k: raw[k][:, 0] for k in ["conv_b", "conv_ln_g", "conv_ln_b", "sg_ln_g", "sg_ln_b", "pool_scale", "norm1_g",
                                  "norm2_g"]}
    g["conv_w"] = raw["conv_w"][:, :CONV_WIDTH]
    g["w_s"] = raw["w_s"].reshape(DEPTH, N_HEADS_B, CHUNK, CHUNK)
    g["b_s"] = raw["b_s"][:, :, :N_HEADS_B].transpose(0, 2, 1)
    blocks = raw["w_pool"].reshape(DEPTH, N_GROUPS_C, GROUP_DIM_C, N_GROUPS_C, GROUP_DIM_C)
    g["w_pool"] = jnp.sum(blocks * eye[None, :, None, :, None], axis=3)
    end = DEPTH * per_layer
    g["final_g"] = flat[end:end + D_MODEL]
    return g, flat[end + D_MODEL]


def _row_tile(r, cap=512):
    best = r
    for d in range(16, min(r, cap) + 1, 16):
        if r % d == 0:
            best = d
    return best if best <= cap else r


def add_core_halves(g, r1, core, *, name):
    _, _, r, c = g.shape
    tr = _row_tile(r)

    def body(core_ref, g_ref, r_ref, o_ref):
        o_ref[...] = (g_ref[...].astype(F32) + r_ref[...].astype(F32)).astype(o_ref.dtype)

    outs, _ = _call(
        body, name=name, grid=(N_CHIPS, r // tr), prefetch=1,
        in_specs=[pl.BlockSpec((None, None, tr, c), lambda j, i, s: (j, s[0], i, 0)),
                  pl.BlockSpec((None, tr, c), lambda j, i, s: (j, i, 0))],
        out_specs=[pl.BlockSpec((None, tr, c), lambda j, i, s: (j, i, 0))],
        out_shape=[SDS((N_CHIPS, r, c), g.dtype)], args=[core, g, r1])
    return outs[0]


def sum_chips(h, r2, place, *, name):
    _, r, c = h.shape
    tr = _row_tile(r, 256)

    def body(place_ref, h_ref, a_ref, b_ref, c_ref, o_ref):
        acc = h_ref[...].astype(F32) + a_ref[...].astype(F32)
        acc = acc + b_ref[...].astype(F32)
        o_ref[...] = acc + c_ref[...].astype(F32)

    def blk(k):
        return pl.BlockSpec((None, tr, c), lambda i, s: (jnp.bitwise_xor(s[0], k), i, 0))

    outs, _ = _call(
        body, name=name, grid=(r // tr,), prefetch=1, in_specs=[blk(0), blk(1), blk(2), blk(3)],
        out_specs=[pl.BlockSpec((None, tr, c), lambda i, s: (s[1], i, 0))],
        out_shape=[SDS((2, r, c), F32)], args=[place, h, r2, r2, r2])
    return outs[0]


def allreduce_small(p, *, name, pieces=()):
    _, n, _ = p.shape

    def body(p_ref, o_ref, land_ref, send1, recv1, send2, recv2):
        x, y, c = lax.axis_index("x"), lax.axis_index("y"), lax.axis_index("c")
        me = 4 * x + 2 * y + c

        def peer(r):
            return ((1 - x) if r & 4 else x, (1 - y) if r & 2 else y, (1 - c) if r & 1 else c)

        def index(r):
            px, py, pc = peer(r)
            return 4 * px + 2 * py + pc

        land_ref[me] = p_ref[me]
        first = [_remote(p_ref.at[index(r)], land_ref.at[me], send1.at[r - 1], recv1.at[r - 1], peer(r))
                 for r in range(1, N_DEV)]
        for cp in first:
            cp.start()
        for r in range(1, N_DEV):
            blk = land_ref.at[index(r)]
            _remote(blk, blk, send1.at[r - 1], recv1.at[r - 1], peer(r)).wait_recv()
        acc = land_ref[0]
        for d in range(1, N_DEV):
            acc = acc + land_ref[d]
        o_ref[me] = acc
        second = [_remote(o_ref.at[me], o_ref.at[me], send2.at[r - 1], recv2.at[r - 1], peer(r))
                  for r in range(1, N_DEV)]
        for cp in second:
            cp.start()
        for r in range(1, N_DEV):
            blk = o_ref.at[index(r)]
            _remote(blk, blk, send2.at[r - 1], recv2.at[r - 1], peer(r)).wait_recv()
        for cp in first + second:
            cp.wait_send()

    vm = pl.BlockSpec(memory_space=pltpu.VMEM)
    outs, couts = _call(
        body, name=name, grid=(), in_specs=[vm], out_specs=[vm], out_shape=[SDS(p.shape, F32)],
        scratch_shapes=[pltpu.VMEM(p.shape, F32)] + [pltpu.SemaphoreType.DMA((N_DEV - 1,))] * 4,
        args=[p], pieces=pieces)
    return outs[0], couts


def _adam_update(w, g, m, v):
    m_new = ADAM_B1 * m + (1.0 - ADAM_B1) * g
    v_new = ADAM_B2 * v + (1.0 - ADAM_B2) * (g * g)
    m_hat = m_new / (1.0 - ADAM_B1 ** ADAM_STEP)
    v_hat = v_new / (1.0 - ADAM_B2 ** ADAM_STEP)
    return -ADAM_LR * (m_hat / (jnp.sqrt(v_hat) + ADAM_EPS) + ADAM_WD * w), m_new, v_new


def adamw(w, g, m, v, *, name, pieces=()):
    nl, r, c = w.shape
    tr = _row_tile(r, 512)

    def body(w_ref, g_ref, m_ref, v_ref, d_ref, mo_ref, vo_ref):
        d_ref[...], mo_ref[...], vo_ref[...] = _adam_update(w_ref[...], g_ref[...], m_ref[...], v_ref[...])

    blk = pl.BlockSpec((None, tr, c), lambda l, i: (l, i, 0))
    return _call(body, name=name, grid=(nl, r // tr), in_specs=[blk] * 4, out_specs=[blk] * 3,
                 out_shape=[SDS(w.shape, F32)] * 3, args=[w, g, m, v], pieces=pieces)


def adamw_small(ws, gs, ms, vs, *, name, pieces=()):
    n = len(ws)

    def body(*refs):
        for i in range(n):
            w_ref, g_ref, m_ref, v_ref = (refs[k * n + i] for k in range(4))
            d, mn, vn = _adam_update(w_ref[...], g_ref[...], m_ref[...], v_ref[...])
            refs[4 * n + i][...] = d
            refs[5 * n + i][...] = mn
            refs[6 * n + i][...] = vn

    vm = pl.BlockSpec(memory_space=pltpu.VMEM)
    res, couts = _call(body, name=name, grid=(), in_specs=[vm] * (4 * n), out_specs=[vm] * (3 * n),
                       out_shape=[SDS(w.shape, F32) for w in ws] * 3, args=[*ws, *gs, *ms, *vs], pieces=pieces)
    return (res[:n], res[n:2 * n], res[2 * n:]), couts


WEIGHTS = ["norm1_g", "w_in", "conv_w", "conv_b", "conv_ln_g", "conv_ln_b", "w_pw", "sg_ln_g", "sg_ln_b", "w_s",
           "b_s", "w_pool", "pool_scale", "w_out", "norm2_g", "w_gate_up", "w_down", "final_g"]
BIG = ["w_in", "w_pw", "w_out", "w_gate_up", "w_down"]
TRANSPOSED = {"w_in": True, "w_pw": False, "w_out": False, "w_gate_up": True, "w_down": False}
SMALL = [k for k in WEIGHTS if k not in BIG]


def _wire(a, transposed):
    if transposed:
        a = a.transpose(0, 2, 1)
    return [a[l].reshape(2, a.shape[1] // 2, a.shape[2]) for l in range(a.shape[0])]


def _pack(arrays):
    flat = jnp.concatenate([a.reshape(-1) for a in arrays])
    n = -(-flat.shape[0] // (N_DEV * LANES * 8)) * 8
    return jnp.pad(flat, (0, N_DEV * n * LANES - flat.shape[0])).reshape(N_DEV, n, LANES)


def kernel(x, norm1_g, w_in, conv_w, conv_b, conv_ln_g, conv_ln_b, w_pw, sg_ln_g, sg_ln_b, w_s, b_s, w_pool, pool_scale, w_out, norm2_g, w_gate_up, w_down, final_g, loss_target, m_norm1_g, m_w_in, m_conv_w, m_conv_b, m_conv_ln_g, m_conv_ln_b, m_w_pw, m_sg_ln_g, m_sg_ln_b, m_w_s, m_b_s, m_w_pool, m_pool_scale, m_w_out, m_norm2_g, m_w_gate_up, m_w_down, m_final_g, v_norm1_g, v_w_in, v_conv_w, v_conv_b, v_conv_ln_g, v_conv_ln_b, v_w_pw, v_sg_ln_g, v_sg_ln_b, v_w_s, v_b_s, v_w_pool, v_pool_scale, v_w_out, v_norm2_g, v_w_gate_up, v_w_down, v_final_g):
    w = dict(norm1_g=norm1_g, w_in=w_in, conv_w=conv_w, conv_b=conv_b, conv_ln_g=conv_ln_g, conv_ln_b=conv_ln_b,
             w_pw=w_pw, sg_ln_g=sg_ln_g, sg_ln_b=sg_ln_b, w_s=w_s, b_s=b_s, w_pool=w_pool, pool_scale=pool_scale,
             w_out=w_out, norm2_g=norm2_g, w_gate_up=w_gate_up, w_down=w_down, final_g=final_g)
    m = dict(norm1_g=m_norm1_g, w_in=m_w_in, conv_w=m_conv_w, conv_b=m_conv_b, conv_ln_g=m_conv_ln_g,
             conv_ln_b=m_conv_ln_b, w_pw=m_w_pw, sg_ln_g=m_sg_ln_g, sg_ln_b=m_sg_ln_b, w_s=m_w_s, b_s=m_b_s,
             w_pool=m_w_pool, pool_scale=m_pool_scale, w_out=m_w_out, norm2_g=m_norm2_g, w_gate_up=m_w_gate_up,
             w_down=m_w_down, final_g=m_final_g)
    v = dict(norm1_g=v_norm1_g, w_in=v_w_in, conv_w=v_conv_w, conv_b=v_conv_b, conv_ln_g=v_conv_ln_g,
             conv_ln_b=v_conv_ln_b, w_pw=v_w_pw, sg_ln_g=v_sg_ln_g, sg_ln_b=v_sg_ln_b, w_s=v_w_s, b_s=v_b_s,
             w_pool=v_w_pool, pool_scale=v_pool_scale, w_out=v_w_out, norm2_g=v_norm2_g, w_gate_up=v_w_gate_up,
             w_down=v_w_down, final_g=v_final_g)
    bsz, seq, d = x.shape
    t = bsz * seq
    chip = 2 * lax.axis_index("x") + lax.axis_index("y")
    core = lax.axis_index("c")
    core_arr = jnp.reshape(core, (1,)).astype(jnp.int32)
    place_arr = jnp.stack([chip, core]).astype(jnp.int32)

    own = {k: _wire(w[k].astype(BF16), TRANSPOSED[k]) for k in BIG}
    cw_cols = conv_w.shape[2]
    own["side"] = [jnp.pad(conv_w, ((0, 0), (0, 32 - CONV_WIDTH), (0, 0)))]
    parts = {"w_in": 1, "w_pw": 1, "w_out": 1, "w_gate_up": 4, "w_down": 2, "side": 1}
    plan = {
        "norm1_fwd_0": [("ici", "w_in", 0, 0), ("ici", "w_pw", 0, 0), ("ici", "w_pw", 1, 0), ("ici", "side", 0, 0)],
        "gather_first": [("d2d", "w_in", 0, 0), ("d2d", "w_pw", 0, 0), ("d2d", "w_pw", 1, 0), ("d2d", "side", 0, 0)],
        "in_proj_0": [("ici", "w_out", 0, 0), ("ici", "w_gate_up", 0, 0)],
        "mixer_fwd_0": [("d2d", "w_out", 0, 0), ("d2d", "w_gate_up", 0, 0), ("ici", "w_gate_up", 0, 1),
                        ("ici", "w_gate_up", 0, 2)],
        "out_proj_0": [("d2d", "w_gate_up", 0, 1), ("d2d", "w_gate_up", 0, 2), ("both", "w_gate_up", 0, 3)],
        "swiglu_fwd_0": [("both", "w_down", 0, 0), ("both", "w_down", 0, 1), ("ici", "w_in", 1, 0)],
        "down_proj_0": [("d2d", "w_in", 1, 0), ("ici", "w_out", 1, 0), ("ici", "w_gate_up", 1, 0)],
        "in_proj_1": [("d2d", "w_out", 1, 0), ("d2d", "w_gate_up", 1, 0), ("ici", "w_gate_up", 1, 1)],
        "mixer_fwd_1": [("d2d", "w_gate_up", 1, 1), ("ici", "w_gate_up", 1, 2), ("ici", "w_gate_up", 1, 3)],
        "out_proj_1": [("d2d", "w_gate_up", 1, 2), ("d2d", "w_gate_up", 1, 3)],
        "swiglu_fwd_1": [("both", "w_down", 1, 0), ("both", "w_down", 1, 1)],
    }
    bufs = {}

    def grouped(name):
        groups = {}
        for stage, k, l, q in plan.get(name, []):
            groups.setdefault((k, l), []).append((stage, q))
        return groups

    def riding(name):
        return [gather_stages(own[k][l], bufs.get((k, l)), stages, parts[k])
                for (k, l), stages in grouped(name).items()]

    def landed(name, couts):
        for key, co in zip(grouped(name), couts):
            bufs[key] = co[0]

    def whole(k, l):
        g = lax.dynamic_update_index_in_dim(bufs[k, l], own[k][l], chip, 0)
        return g.reshape(-1, g.shape[-1])

    xs = [x.reshape(t, d)]
    saved = []
    full = {}
    p = dict(w)
    for l in range(DEPTH):
        x0 = xs[-1]
        name = f"in_proj_{l}"
        if l == 0:
            h1, co = rmsnorm_fwd(x0, w["norm1_g"][l][None], name="norm1_fwd_0", pieces=riding("norm1_fwd_0"))
            landed("norm1_fwd_0", co)
            landed("gather_first", comm_only(riding("gather_first"), name="gather_first"))
            side_all = lax.dynamic_update_index_in_dim(bufs["side", 0], own["side"][0], chip, 0)
            p["conv_w"] = side_all[:, :, :CONV_WIDTH, :].transpose(1, 2, 0, 3).reshape(
                DEPTH, CONV_WIDTH, N_CHIPS * cw_cols)
            full["w_in", l] = whole("w_in", l)
            z, co = mm_nt(h1, full["w_in", l], name=name, tm=512, tn=D_IN, out_dtype=BF16, pieces=riding(name))
        else:
            full["w_in", l] = whole("w_in", l)
            (z, h1), co = norm_mm_nt(x0, w["norm1_g"][l][None], full["w_in", l], name=name, tm=512, out_dtype=BF16,
                                     pieces=riding(name))
        landed(name, co)
        if l == 0:
            mp = _mixer_params(p, jnp.stack([whole("w_pw", 0), whole("w_pw", 1)]))
        name = f"mixer_fwd_{l}"
        (mc, cv), co = mixer_fwd(z, mp, l, seq=seq, name=name, pieces=riding(name))
        landed(name, co)
        full["w_out", l] = whole("w_out", l)
        name = f"out_proj_{l}"
        x1, co = mm_nn(mc, full["w_out", l], name=name, tm=512, tn=D_MODEL, out_dtype=F32, residual=x0,
                       pieces=riding(name))
        landed(name, co)
        full["w_gate_up", l] = whole("w_gate_up", l)
        name = f"swiglu_fwd_{l}"
        (gu, act, h2), co = swiglu_fwd(x1, w["norm2_g"][l][None], full["w_gate_up", l], name=name, tm=512, tn=1408,
                                       pieces=riding(name))
        landed(name, co)
        full["w_down", l] = whole("w_down", l)
        name = f"down_proj_{l}"
        x2, co = mm_nn(act, full["w_down", l], name=name, tm=512, tn=D_MODEL, out_dtype=F32, residual=x1,
                       pieces=riding(name))
        landed(name, co)
        saved.append((x0, h1, z, cv, mc, x1, h2, gu, act, mp))
        xs.append(x2)

    loss, dx, dxb, d_final_g = loss_head(xs[-1], p["final_g"][None], loss_target.reshape(t, d), name="loss_head")

    raw_small = [None] * DEPTH
    reduced = {}
    carry = None

    def halves(g):
        return g.reshape(N_CHIPS, 2, g.shape[0] // (2 * N_CHIPS), g.shape[1])

    for l in reversed(range(DEPTH)):
        x0, h1, z, cv, mc, x1, h2, gu, act, mp = saved[l]
        pieces = [exchange_d2d(carry[0]), reduce_ici(carry[1])] if carry else []
        g_down, co = mm_tn(act, dxb, name=f"down_proj_dw_{l}", tk=D_FF // 2, tn=D_MODEL, tt=TT, out_dtype=BF16,
                           pieces=pieces)
        g_down = halves(g_down)
        pieces = [reduce_d2d(g_down)]
        if carry:
            reduced["w_pw", l + 1] = co[0][0]
            s_in = sum_chips(carry[1], co[1][0], place_arr, name=f"sum_chips_w_in_{l + 1}")
            pieces.append(exchange_d2d(s_in))
        dgu, co = swiglu_bwd(dxb, full["w_down", l], gu, name=f"swiglu_bwd_{l}", tm=512, tn=1408, pieces=pieces)
        if carry:
            reduced["w_in", l + 1] = co[1][0]
        h_down = add_core_halves(g_down, co[0][0], core_arr, name=f"add_cores_w_down_{l}")
        g_gu, (r2,) = mm_gu_tn(dgu, h2, name=f"gate_up_dw_{l}", tn=1408, tt=TT, pieces=[reduce_ici(h_down)])
        s_down = sum_chips(h_down, r2[0], place_arr, name=f"sum_chips_w_down_{l}")
        g_gu = halves(g_gu)
        (dx, dxb, dn2), (e, r1) = mm_norm_bwd(dgu, full["w_gate_up", l], x1, p["norm2_g"][l][None], dx,
                                              name=f"gate_up_dx_{l}", tm=256,
                                              pieces=[exchange_d2d(s_down), reduce_d2d(g_gu)])
        reduced["w_down", l] = e[0]
        h_gu = add_core_halves(g_gu, r1[0], core_arr, name=f"add_cores_w_gate_up_{l}")
        g_out, _ = mm_tn(mc, dxb, name=f"out_proj_dw_{l}", tk=D_MODEL, tn=D_MODEL, tt=TT, out_dtype=BF16)
        g_out = halves(g_out)
        dmc, (r1,) = mm_nt(dxb, full["w_out", l], name=f"out_proj_dx_{l}", tm=512, tn=D_MODEL, out_dtype=BF16,
                           pieces=[reduce_d2d(g_out)])
        h_out = add_core_halves(g_out, r1[0], core_arr, name=f"add_cores_w_out_{l}")
        (dz, *mg), (r2a, r2b) = mixer_bwd(z, cv, dmc, mp, l, seq=seq, name=f"mixer_bwd_{l}",
                                          pieces=[reduce_ici(h_gu), reduce_ici(h_out)])
        s_gu = sum_chips(h_gu, r2a[0], place_arr, name=f"sum_chips_w_gate_up_{l}")
        s_out = sum_chips(h_out, r2b[0], place_arr, name=f"sum_chips_w_out_{l}")
        dcw, dcb, dclg, dclb, dwpw, dslg, dslb, dws, dbs, dwp, dps = mg
        g_pw = halves(dwpw.astype(BF16))
        g_in, (ea, eb, r1) = mm_tn(dz, h1, name=f"in_proj_dw_{l}", tk=D_IN // 2, tn=D_MODEL, tt=TT, out_dtype=BF16,
                                   pieces=[exchange_d2d(s_gu), exchange_d2d(s_out), reduce_d2d(g_pw)])
        reduced["w_gate_up", l], reduced["w_out", l] = ea[0], eb[0]
        h_pw = add_core_halves(g_pw, r1[0], core_arr, name=f"add_cores_w_pw_{l}")
        g_in = halves(g_in)
        norm_g = p["norm1_g"][l][None]
        if l > 0:
            (dx, dxb, dn1), (r2, r1) = mm_norm_bwd(dz, full["w_in", l], x0, norm_g, dx, name=f"in_proj_dx_{l}", tm=512,
                                                   pieces=[reduce_ici(h_pw), reduce_d2d(g_in)])
            s_pw = sum_chips(h_pw, r2[0], place_arr, name=f"sum_chips_w_pw_{l}")
            h_in = add_core_halves(g_in, r1[0], core_arr, name=f"add_cores_w_in_{l}")
            carry = (s_pw, h_in)
        else:
            tm = min(512, t // 2)
            half = t // tm // 2
            before, (r2, r1) = mm_norm_bwd(dz, full["w_in", l], x0, norm_g, dx, name="in_proj_dx_0", tm=tm,
                                           blocks=(0, half), pieces=[reduce_ici(h_pw), reduce_d2d(g_in)])
            s_pw = sum_chips(h_pw, r2[0], place_arr, name="sum_chips_w_pw_0")
            h_in = add_core_halves(g_in, r1[0], core_arr, name="add_cores_w_in_0")
            (dx, dxb, dn1), (e, r2) = mm_norm_bwd(dz, full["w_in", l], x0, norm_g, dx, name="in_proj_dx_0_rest", tm=tm,
                                                  blocks=(half, t // tm - half), before=before,
                                                  pieces=[exchange_d2d(s_pw), reduce_ici(h_in)])
            reduced["w_pw", 0] = e[0]
            s_in = sum_chips(h_in, r2[0], place_arr, name="sum_chips_w_in_0")
        raw_small[l] = [dcw, dcb, dclg, dclb, dslg, dslb, dws, dbs, dwp, dps, dn1, dn2]
    grad_x = dx.reshape(bsz, seq, d)

    summed, co = allreduce_small(_pack(raw_small[0] + raw_small[1] + [d_final_g, loss]), name="allreduce_small",
                                 pieces=[exchange_d2d(s_in)])
    reduced["w_in", 0] = co[0][0]
    grad, loss = _small_grads(summed.reshape(-1))
    grad["conv_w"] = lax.dynamic_slice_in_dim(grad["conv_w"], chip * cw_cols, cw_cols, axis=2)

    delta, new_m, new_v = {}, {}, {}

    def flat2(a):
        return a.reshape(-1, a.shape[-1])

    res, _ = adamw_small(*[[flat2(tt[k]) for k in SMALL] for tt in (w, grad, m, v)], name="adamw_small")
    for out, arrs in zip((delta, new_m, new_v), res):
        out.update({k: a.reshape(w[k].shape) for k, a in zip(SMALL, arrs)})
    for k in BIG:
        g = jnp.stack([reduced[k, l].reshape(-1, reduced[k, l].shape[-1]) for l in range(DEPTH)])
        grad[k] = g.transpose(0, 2, 1) if TRANSPOSED[k] else g
        (delta[k], new_m[k], new_v[k]), _ = adamw(w[k], grad[k], m[k], v[k], name=f"adamw_{k}")
    return (loss, grad_x, *[grad[k] for k in WEIGHTS], *[delta[k] for k in WEIGHTS],
            *[new_m[k] for k in WEIGHTS], *[new_v[k] for k in WEIGHTS])
```
